```python
import math
import jax
import jax.numpy as jnp
from jax import lax
import numpy as np

D_MODEL = 1024
BATCH = 16
SEQ = 2048
DEPTH = 1

HEAD_DIM = 64
ATTN_GROUPS = ((128, 1), (512, 4), (2048, 16))
HEADS_PER_GROUP = 8
N_ATTN_HEADS = HEADS_PER_GROUP * len(ATTN_GROUPS)
ATTN_WIDTH = N_ATTN_HEADS * HEAD_DIM
ATTN_OUT_WIDTH = HEADS_PER_GROUP * HEAD_DIM
BLOCK = 128
NUM_BUCKETS = 32
MAX_DISTANCE = 2048

D_INNER = 2 * D_MODEL
SSM_HEAD_DIM = 64
N_SSM_HEADS = D_INNER // SSM_HEAD_DIM
N_SSM_GROUPS = 4
D_STATE = 128
CONV_WIDTH = 4
CHUNK = 128
CONV_DIM = D_INNER + 2 * N_SSM_GROUPS * D_STATE

N_BRANCHES = 2
SPLIT_POINTS = (ATTN_WIDTH, 2 * ATTN_WIDTH, 3 * ATTN_WIDTH,
                3 * ATTN_WIDTH + D_INNER,
                3 * ATTN_WIDTH + D_INNER + CONV_DIM,
                3 * ATTN_WIDTH + D_INNER + CONV_DIM + N_SSM_HEADS)
IN_PROJ_WIDTH = 3 * ATTN_WIDTH + D_INNER + CONV_DIM + N_SSM_HEADS + N_BRANCHES * D_MODEL

N_EXPERTS = 32
TOP_K = 4
D_EXPERT = D_MODEL
SWIGLU_LIMIT = 7.0
SWIGLU_ALPHA = 1.702
MOE_BLOCK = 128

EPS = 1e-5

kernel_name = 'hybrid_dilated_attn_ssd_moe'


def rmsnorm(x, w):
    xf = x.astype(jnp.float32)
    y = xf * lax.rsqrt(jnp.mean(xf * xf, axis=-1, keepdims=True) + EPS)
    return (y * w.astype(jnp.float32)).astype(x.dtype)


def t5_bucket(dist):
    max_exact = NUM_BUCKETS // 2
    nf = jnp.maximum(dist, max_exact).astype(jnp.float32)
    large = max_exact + (jnp.log(nf / max_exact) / math.log(MAX_DISTANCE / max_exact)
                         * (NUM_BUCKETS - max_exact)).astype(jnp.int32)
    large = jnp.minimum(large, NUM_BUCKETS - 1)
    return jnp.where(dist < max_exact, dist, large)


def dilated_window_attention(q, k, v, bias_table, window, dilation):
    b, s, h, hd = q.shape
    span = dilation * BLOCK
    s_pad = -(-s // span) * span
    sub_len = s_pad // dilation
    nb = sub_len // BLOCK
    w_sub = window // dilation

    def to_sub(t):
        t = jnp.pad(t.astype(jnp.float32), ((0, 0), (0, s_pad - s), (0, 0), (0, 0)))
        t = t.reshape(b, sub_len, dilation, h, hd).transpose(0, 2, 1, 3, 4)
        return t.reshape(b * dilation, nb, BLOCK, h, hd)

    def with_prev(t):
        prev = jnp.pad(t, ((0, 0), (1, 0), (0, 0), (0, 0), (0, 0)))[:, :-1]
        return jnp.concatenate([prev, t], axis=2)

    qb = to_sub(q)
    kc = with_prev(to_sub(k))
    vc = with_prev(to_sub(v))
    logits = jnp.einsum('znqhd,znkhd->znhqk', qb, kc) * (hd ** -0.5)
    q_idx = jnp.arange(BLOCK)[:, None]
    k_idx = jnp.arange(2 * BLOCK)[None, :]
    delta = q_idx + BLOCK - k_idx
    in_band = (delta >= 0) & (delta <= w_sub)
    has_prev = (jnp.arange(nb)[:, None, None] > 0) | (k_idx >= BLOCK)[None]
    valid = in_band[None] & has_prev
    bucket = t5_bucket(jnp.clip(delta, 0, w_sub) * dilation)
    bias = jnp.transpose(bias_table.astype(jnp.float32)[bucket], (2, 0, 1))
    logits = jnp.where(valid[None, :, None], logits + bias[None, None], -jnp.inf)
    m = jnp.max(logits, axis=-1, keepdims=True)
    p = jnp.exp(logits - m)
    denom = jnp.sum(p, axis=-1, keepdims=True)
    lse = (m + jnp.log(denom))[..., 0]
    out = jnp.einsum('znhqk,znkhd->znqhd', p / denom, vc)
    out = out.reshape(b, dilation, sub_len, h, hd).transpose(0, 2, 1, 3, 4).reshape(b, s_pad, h, hd)[:, :s]
    lse = lse.transpose(0, 1, 3, 2).reshape(b, dilation, sub_len, h).transpose(0, 2, 1, 3).reshape(b, s_pad, h)[:, :s]
    return out, lse


def causal_depthwise_conv(u, w, bias):
    c = u.shape[-1]
    out = lax.conv_general_dilated(u, w[:, None, :].astype(u.dtype), window_strides=(1,),
                                   padding=[(CONV_WIDTH - 1, 0)],
                                   dimension_numbers=('NWC', 'WIO', 'NWC'),
                                   feature_group_count=c)
    return out + bias.astype(u.dtype)


def ssd_chunked(xs, dt, a, bm, cm):
    b, s, h, p = xs.shape
    g, n = bm.shape[2], bm.shape[3]
    r = h // g
    nc = s // CHUNK
    x_d = (xs * dt[..., None]).reshape(b, nc, CHUNK, g, r, p)
    a_d = jnp.moveaxis((dt * a).reshape(b, nc, CHUNK, g, r), 2, -1)
    a_cs = jnp.cumsum(a_d, axis=-1)
    bc = bm.reshape(b, nc, CHUNK, g, n)
    cc = cm.reshape(b, nc, CHUNK, g, n)
    tril = jnp.tril(jnp.ones((CHUNK, CHUNK), dtype=bool))
    seg = a_cs[..., :, None] - a_cs[..., None, :]
    decay_in = jnp.exp(jnp.where(tril, seg, -jnp.inf))
    cb = jnp.einsum('bclgn,bcsgn->bcgls', cc, bc)
    y_diag = jnp.einsum('bcgrls,bcsgrp->bclgrp', cb[:, :, :, None] * decay_in, x_d)
    decay_states = jnp.exp(a_cs[..., -1:] - a_cs)
    states = jnp.einsum('bclgn,bcgrl,bclgrp->bcgrpn', bc, decay_states, x_d)
    chunk_decay = jnp.exp(a_cs[..., -1])

    def step(carry, inp):
        st, dec = inp
        return carry * dec[..., None, None] + st, carry

    init = jnp.zeros((b, g, r, p, n), jnp.float32)
    _, prev = lax.scan(step, init, (jnp.moveaxis(states, 1, 0), jnp.moveaxis(chunk_decay, 1, 0)))
    prev = jnp.moveaxis(prev, 0, 1)
    y_off = jnp.einsum('bclgn,bcgrpn,bcgrl->bclgrp', cc, prev, jnp.exp(a_cs))
    return (y_diag + y_off).reshape(b, s, h, p)


def hybrid_mixer(h, w_in, rel_bias, w_branch_attn, conv_w, conv_b, dt_bias, a_log, d_skip,
                 ssm_norm_w, w_branch_ssm, gate_bias, w_out):
    b, s, _ = h.shape
    f32 = jnp.float32
    proj = h @ w_in
    q, k, v, z, xbc, dt_raw, gate_logits = jnp.split(proj, SPLIT_POINTS, axis=-1)

    q = q.reshape(b, s, N_ATTN_HEADS, HEAD_DIM)
    k = k.reshape(b, s, N_ATTN_HEADS, HEAD_DIM)
    v = v.reshape(b, s, N_ATTN_HEADS, HEAD_DIM)
    outs, lses = [], []
    for gi, (window, dilation) in enumerate(ATTN_GROUPS):
        hs = slice(gi * HEADS_PER_GROUP, (gi + 1) * HEADS_PER_GROUP)
        o, l = dilated_window_attention(q[:, :, hs], k[:, :, hs], v[:, :, hs], rel_bias[:, hs], window, dilation)
        outs.append(o)
        lses.append(l)
    mix_w = jax.nn.softmax(jnp.stack(lses, axis=0), axis=0)
    attn = jnp.sum(mix_w[..., None] * jnp.stack(outs, axis=0), axis=0).reshape(b, s, ATTN_OUT_WIDTH)
    y_attn = attn.astype(h.dtype) @ w_branch_attn

    xbc = jax.nn.silu(causal_depthwise_conv(xbc, conv_w, conv_b))
    xs, bm, cm = jnp.split(xbc, [D_INNER, D_INNER + N_SSM_GROUPS * D_STATE], axis=-1)
    xs = xs.reshape(b, s, N_SSM_HEADS, SSM_HEAD_DIM).astype(f32)
    bm = bm.reshape(b, s, N_SSM_GROUPS, D_STATE).astype(f32)
    cm = cm.reshape(b, s, N_SSM_GROUPS, D_STATE).astype(f32)
    dt = jax.nn.softplus(dt_raw.astype(f32) + dt_bias.astype(f32))
    a = -jnp.exp(a_log.astype(f32))
    y = ssd_chunked(xs, dt, a, bm, cm) + d_skip.astype(f32)[:, None] * xs
    y = y.reshape(b, s, D_INNER) * jax.nn.silu(z.astype(f32))
    yg = y.reshape(b, s, N_SSM_GROUPS, D_INNER // N_SSM_GROUPS)
    yg = yg * lax.rsqrt(jnp.mean(yg * yg, axis=-1, keepdims=True) + EPS)
    y = yg.reshape(b, s, D_INNER) * ssm_norm_w.astype(f32)
    y_ssm = y.astype(h.dtype) @ w_branch_ssm

    gates = jax.nn.sigmoid((gate_logits + gate_bias).astype(f32)).reshape(b, s, N_BRANCHES, D_MODEL)
    merged = gates[..., 0, :] * y_attn.astype(f32) + gates[..., 1, :] * y_ssm.astype(f32)
    return merged.astype(h.dtype) @ w_out


def clamped_swiglu(hid):
    x_glu = jnp.minimum(hid[..., ::2], SWIGLU_LIMIT)
    x_lin = jnp.clip(hid[..., 1::2], -SWIGLU_LIMIT, SWIGLU_LIMIT)
    return x_glu * jax.nn.sigmoid(SWIGLU_ALPHA * x_glu) * (x_lin + 1.0)


def moe_ffn(h, router_w, router_b, w1, b1, w2, b2):
    b, s, d = h.shape
    t_n = b * s
    tok = h.reshape(t_n, d)
    logits = (tok @ router_w + router_b).astype(jnp.float32)
    top_val, top_idx = lax.top_k(logits, TOP_K)
    weights = jax.nn.softmax(top_val, axis=-1)
    n_assign = t_n * TOP_K
    flat_e = top_idx.reshape(n_assign)
    flat_tok = jnp.arange(n_assign, dtype=jnp.int32) // TOP_K
    flat_w = weights.reshape(n_assign)
    order = jnp.argsort(flat_e)
    se, stok, sw = flat_e[order], flat_tok[order], flat_w[order]
    counts = jnp.bincount(flat_e, length=N_EXPERTS)
    padded = (counts + MOE_BLOCK - 1) // MOE_BLOCK * MOE_BLOCK
    start = jnp.cumsum(counts) - counts
    pend = jnp.cumsum(padded)
    pstart = pend - padded
    dest = pstart[se] + jnp.arange(n_assign, dtype=jnp.int32) - start[se]
    cap = n_assign + N_EXPERTS * MOE_BLOCK
    n_blocks = cap // MOE_BLOCK
    buf = jnp.zeros((cap, d), h.dtype).at[dest].set(tok[stok])
    block_e = jnp.minimum(jnp.searchsorted(pend, jnp.arange(n_blocks) * MOE_BLOCK, side='right'), N_EXPERTS - 1)

    def expert_block(args):
        xb, e = args
        hid = xb @ w1[e] + b1[e]
        return clamped_swiglu(hid) @ w2[e] + b2[e]

    ybuf = lax.map(expert_block, (buf.reshape(n_blocks, MOE_BLOCK, d), block_e)).reshape(cap, d)
    y = ybuf[dest] * sw[:, None].astype(ybuf.dtype)
    out = jax.ops.segment_sum(y, stok, num_segments=t_n)
    return out.reshape(b, s, d)


def setup_inputs(seed: int = 0) -> dict:
    key = jax.random.key(seed)
    ks = jax.random.split(key, 24)
    f32 = jnp.float32

    def nrm(k, shape, scale):
        return jax.random.normal(k, shape, f32) * scale

    dt0 = jnp.exp(jax.random.uniform(ks[6], (DEPTH, N_SSM_HEADS), f32, math.log(1e-3), math.log(1e-1)))
    return {
        'x': nrm(ks[0], (BATCH, SEQ, D_MODEL), 1.0),
        'w_in': nrm(ks[1], (DEPTH, D_MODEL, IN_PROJ_WIDTH), D_MODEL ** -0.5),
        'rel_bias': nrm(ks[2], (NUM_BUCKETS, N_ATTN_HEADS), 0.5),
        'w_branch_attn': nrm(ks[3], (DEPTH, ATTN_OUT_WIDTH, D_MODEL), ATTN_OUT_WIDTH ** -0.5),
        'conv_w': nrm(ks[4], (DEPTH, CONV_WIDTH, CONV_DIM), CONV_WIDTH ** -0.5),
        'conv_b': nrm(ks[5], (DEPTH, CONV_DIM), 0.02),
        'dt_bias': dt0 + jnp.log(-jnp.expm1(-dt0)),
        'a_log': jnp.log(jax.random.uniform(ks[7], (DEPTH, N_SSM_HEADS), f32, 1.0, 16.0)),
        'd_skip': 1.0 + nrm(ks[8], (DEPTH, N_SSM_HEADS), 0.1),
        'ssm_norm_w': 1.0 + nrm(ks[9], (DEPTH, D_INNER), 0.1),
        'w_branch_ssm': nrm(ks[10], (DEPTH, D_INNER, D_MODEL), D_INNER ** -0.5),
        'gate_bias': nrm(ks[11], (DEPTH, N_BRANCHES * D_MODEL), 0.1),
        'w_out': nrm(ks[12], (DEPTH, D_MODEL, D_MODEL), D_MODEL ** -0.5),
        'norm_mix': 1.0 + nrm(ks[13], (DEPTH, D_MODEL), 0.1),
        'norm_ffn': 1.0 + nrm(ks[14], (DEPTH, D_MODEL), 0.1),
        'router_w': nrm(ks[15], (DEPTH, D_MODEL, N_EXPERTS), D_MODEL ** -0.5),
        'router_b': nrm(ks[16], (DEPTH, N_EXPERTS), 0.01),
        'w1': nrm(ks[17], (DEPTH, N_EXPERTS, D_MODEL, 2 * D_EXPERT), D_MODEL ** -0.5),
        'b1': nrm(ks[18], (DEPTH, N_EXPERTS, 2 * D_EXPERT), 0.01),
        'w2': nrm(ks[19], (DEPTH, N_EXPERTS, D_EXPERT, D_MODEL), D_EXPERT ** -0.5),
        'b2': nrm(ks[20], (DEPTH, N_EXPERTS, D_MODEL), 0.01),
        'norm_final': 1.0 + nrm(ks[21], (D_MODEL,), 0.1),
    }


def reference(x, w_in, rel_bias, w_branch_attn, conv_w, conv_b, dt_bias, a_log, d_skip, ssm_norm_w,
              w_branch_ssm, gate_bias, w_out, norm_mix, norm_ffn, router_w, router_b, w1, b1, w2, b2,
              norm_final):
    for l in range(DEPTH):
        h = rmsnorm(x, norm_mix[l])
        x = x + hybrid_mixer(h, w_in[l], rel_bias, w_branch_attn[l], conv_w[l], conv_b[l], dt_bias[l],
                             a_log[l], d_skip[l], ssm_norm_w[l], w_branch_ssm[l], gate_bias[l], w_out[l])
        h = rmsnorm(x, norm_ffn[l])
        x = x + moe_ffn(h, router_w[l], router_b[l], w1[l], b1[l], w2[l], b2[l])
    return rmsnorm(x, norm_final)
```

```python
import functools
import math

import jax
import jax.numpy as jnp
import numpy as np
from jax import lax
from jax.experimental import pallas as pl
from jax.experimental.pallas import tpu as pltpu

F32 = jnp.float32
BF16 = jnp.bfloat16

EPS = 1e-5
NEG_BIG = -1e30

HEAD_DIM = 64
ATTN_GROUPS = ((128, 1), (512, 4), (2048, 16))
HEADS_PER_GROUP = 8
GROUP_WIDTH = HEADS_PER_GROUP * HEAD_DIM
ATTN_BLOCK = 128
NUM_BUCKETS = 32
MAX_DISTANCE = 2048
SSM_HEAD_DIM = 64
N_SSM_GROUPS = 4
SSM_HEADS_PER_GROUP = 8
D_STATE = 128
CONV_WIDTH = 4
CHUNK = 128
N_EXPERTS = 32
TOP_K = 4
SWIGLU_LIMIT = 7.0
SWIGLU_ALPHA = 1.702

LANES = 128
V7X_VMEM_BYTES = 64 * 1024 * 1024
VMEM_LIMIT = 48 * 1024 * 1024

ROW_TILE = 1024
EXPERT_ROWS = 256


def _params(semantics):
    return pltpu.CompilerParams(dimension_semantics=semantics, vmem_limit_bytes=VMEM_LIMIT)


def _rms_matmul_kernel(x_ref, g_ref, w_ref, ws_ref, o_ref, os_ref, h_ref):
    @pl.when(pl.program_id(1) == 0)
    def _():
        x = x_ref[...]
        ms = jnp.mean(x * x, axis=-1, keepdims=True)
        h_ref[...] = (x * lax.rsqrt(ms + EPS) * g_ref[...]).astype(BF16)
        os_ref[...] = jnp.dot(h_ref[...], ws_ref[...], preferred_element_type=F32)

    o_ref[...] = jnp.dot(h_ref[...], w_ref[...], preferred_element_type=F32).astype(o_ref.dtype)


def rms_matmul(x, g, w, w_side, tn):
    t, d = x.shape
    n = w.shape[1]
    ns = w_side.shape[1]
    tm = ROW_TILE
    return pl.pallas_call(
        _rms_matmul_kernel,
        grid=(t // tm, n // tn),
        in_specs=[
            pl.BlockSpec((tm, d), lambda i, j: (i, 0)),
            pl.BlockSpec((1, d), lambda i, j: (0, 0)),
            pl.BlockSpec((d, tn), lambda i, j: (0, j)),
            pl.BlockSpec((d, ns), lambda i, j: (0, 0)),
        ],
        out_specs=[
            pl.BlockSpec((tm, tn), lambda i, j: (i, j)),
            pl.BlockSpec((tm, ns), lambda i, j: (i, 0)),
        ],
        out_shape=[jax.ShapeDtypeStruct((t, n), BF16), jax.ShapeDtypeStruct((t, ns), F32)],
        scratch_shapes=[pltpu.VMEM((tm, d), BF16)],
        compiler_params=_params(("parallel", "arbitrary")),
        name="rms_matmul",
    )(x, g, w, w_side)


def _attn_kernel(*refs, has_prev):
    if has_prev:
        q_ref, kc_ref, vc_ref, kp_ref, vp_ref, bias_ref, o_ref, lse_ref = refs
    else:
        q_ref, kc_ref, vc_ref, bias_ref, o_ref, lse_ref = refs
    first = pl.program_id(2) == 0
    lane = lax.broadcasted_iota(jnp.int32, (ATTN_BLOCK, LANES), 1)
    lse_all = jnp.zeros((ATTN_BLOCK, LANES), F32)
    nt = (((1,), (1,)), ((), ()))
    for h in range(HEADS_PER_GROUP):
        cols = slice(h * HEAD_DIM, (h + 1) * HEAD_DIM)
        q = q_ref[:, cols] * jnp.asarray(HEAD_DIM ** -0.5, BF16)
        s_cur = lax.dot_general(q, kc_ref[:, cols], nt, preferred_element_type=F32)
        s_cur = s_cur + bias_ref[h, :, ATTN_BLOCK:]
        m = jnp.max(s_cur, axis=-1, keepdims=True)
        if has_prev:
            s_prev = lax.dot_general(q, kp_ref[:, cols], nt, preferred_element_type=F32)
            s_prev = jnp.where(first, NEG_BIG, s_prev + bias_ref[h, :, :ATTN_BLOCK])
            m = jnp.maximum(m, jnp.max(s_prev, axis=-1, keepdims=True))
        p_cur = jnp.exp(s_cur - m)
        denom = jnp.sum(p_cur, axis=-1, keepdims=True)
        acc = jnp.dot(p_cur.astype(BF16), vc_ref[:, cols], preferred_element_type=F32)
        if has_prev:
            p_prev = jnp.exp(s_prev - m)
            denom = denom + jnp.sum(p_prev, axis=-1, keepdims=True)
            acc = acc + jnp.dot(p_prev.astype(BF16), vp_ref[:, cols], preferred_element_type=F32)
        o_ref[:, cols] = (acc / denom).astype(o_ref.dtype)
        lse_all = jnp.where(lane == h, m + jnp.log(denom), lse_all)
    lse_ref[...] = lse_all


def dilated_attention(qkv, bias, gi, batch, seq):
    _, dil = ATTN_GROUPS[gi]
    t, width = qkv.shape
    n_groups = len(ATTN_GROUPS)
    blocks_per_token = width // GROUP_WIDTH
    sub_len = seq // dil
    nb = sub_len // ATTN_BLOCK
    has_prev = nb > 1
    view = qkv.reshape(t // dil, dil * width)

    def col(which):
        return lambda b, r, n: (b * nb + n, r * blocks_per_token + which * n_groups + gi)

    def col_prev(which):
        return lambda b, r, n: (b * nb + jnp.maximum(n - 1, 0),
                                r * blocks_per_token + which * n_groups + gi)

    blk = (ATTN_BLOCK, GROUP_WIDTH)
    in_specs = [pl.BlockSpec(blk, col(0)), pl.BlockSpec(blk, col(1)), pl.BlockSpec(blk, col(2))]
    args = [view, view, view]
    if has_prev:
        in_specs += [pl.BlockSpec(blk, col_prev(1)), pl.BlockSpec(blk, col_prev(2))]
        args += [view, view]
    in_specs.append(pl.BlockSpec((HEADS_PER_GROUP, ATTN_BLOCK, 2 * ATTN_BLOCK), lambda b, r, n: (0, 0, 0)))
    args.append(bias)
    out, lse = pl.pallas_call(
        functools.partial(_attn_kernel, has_prev=has_prev),
        grid=(batch, dil, nb),
        in_specs=in_specs,
        out_specs=[
            pl.BlockSpec(blk, lambda b, r, n: (b * nb + n, r)),
            pl.BlockSpec((ATTN_BLOCK, LANES), lambda b, r, n: (b * nb + n, r)),
        ],
        out_shape=[
            jax.ShapeDtypeStruct((t // dil, dil * GROUP_WIDTH), BF16),
            jax.ShapeDtypeStruct((t // dil, dil * LANES), F32),
        ],
        compiler_params=_params(("parallel", "parallel", "arbitrary")),
        name=f"dilated_attn_g{gi}",
    )(*args)
    return out.reshape(t, GROUP_WIDTH), lse.reshape(t, LANES)


def attention_bias(rel_bias, gi):
    window, dil = ATTN_GROUPS[gi]
    w_sub = window // dil
    q_idx = np.arange(ATTN_BLOCK)[:, None]
    k_idx = np.arange(2 * ATTN_BLOCK)[None, :]
    delta = q_idx + ATTN_BLOCK - k_idx
    in_band = (delta >= 0) & (delta <= w_sub)
    dist = np.clip(delta, 0, w_sub) * dil
    max_exact = NUM_BUCKETS // 2
    nf = np.maximum(dist, max_exact).astype(np.float32)
    large = max_exact + (np.log(nf / max_exact) / math.log(MAX_DISTANCE / max_exact)
                         * (NUM_BUCKETS - max_exact)).astype(np.int32)
    large = np.minimum(large, NUM_BUCKETS - 1)
    bucket = np.where(dist < max_exact, dist, large)
    table = rel_bias[:, gi * HEADS_PER_GROUP:(gi + 1) * HEADS_PER_GROUP].astype(F32)
    bias = jnp.transpose(table[bucket], (2, 0, 1))
    return jnp.where(in_band[None], bias, NEG_BIG)


def _silu(v):
    return v * (1.0 / (1.0 + jnp.exp(-v)))


def _conv_silu(ext_ref, u, w_ref, b_ref, first):
    rows = u.shape[0]

    @pl.when(first)
    def _():
        ext_ref[0:8, :] = jnp.zeros((8, u.shape[1]), F32)

    ext_ref[8:8 + rows, :] = u
    acc = u * w_ref[CONV_WIDTH - 1:CONV_WIDTH, :] + b_ref[...]
    for k in range(1, CONV_WIDTH):
        acc = acc + ext_ref[8 - k:8 - k + rows, :] * w_ref[CONV_WIDTH - 1 - k:CONV_WIDTH - k, :]
    ext_ref[0:8, :] = u[rows - 8:, :]
    return _silu(acc)


def _ssd_kernel(x_ref, bc_ref, z_ref, dt_ref, cwx_ref, cbx_ref, cwbc_ref, cbbc_ref, dtb_ref, a_ref,
                dskip_ref, nw_ref, expand_ref, o_ref, extx_ref, extbc_ref, state_ref):
    first = pl.program_id(1) == 0
    gw = SSM_HEADS_PER_GROUP * SSM_HEAD_DIM

    @pl.when(first)
    def _():
        state_ref[...] = jnp.zeros(state_ref.shape, F32)

    xs_all = _conv_silu(extx_ref, x_ref[...].astype(F32), cwx_ref, cbx_ref, first)
    bc_all = _conv_silu(extbc_ref, bc_ref[...].astype(F32), cwbc_ref, cbbc_ref, first)

    row = lax.broadcasted_iota(jnp.int32, (CHUNK, CHUNK), 0)
    colm = lax.broadcasted_iota(jnp.int32, (CHUNK, CHUNK), 1)
    tril = row >= colm
    tril_f = tril.astype(F32)
    eye = (row == colm).astype(F32)
    hi = lax.Precision.HIGHEST
    nt = (((1,), (1,)), ((), ()))
    tn = (((0,), (0,)), ((), ()))

    for g in range(N_SSM_GROUPS):
        lanes = slice(g * LANES, (g + 1) * LANES)
        ch = slice(g * gw, (g + 1) * gw)
        v = dt_ref[:, lanes] + dtb_ref[:, lanes]
        dt = jnp.maximum(v, 0.0) + jnp.log1p(jnp.exp(-jnp.abs(v)))
        a_d = dt * a_ref[:, lanes]
        acs = jnp.dot(tril_f, a_d, precision=hi, preferred_element_type=F32)
        acs_t = lax.dot_general(eye, acs, nt, precision=hi, preferred_element_type=F32)
        expand = expand_ref[...]
        dt_e = jnp.dot(dt, expand, precision=hi, preferred_element_type=F32)
        acs_e = jnp.dot(acs, expand, precision=hi, preferred_element_type=F32)
        last_e = acs_e[CHUNK - 1:CHUNK, :]

        xs = xs_all[:, ch]
        bm = bc_all[:, lanes].astype(BF16)
        cm = bc_all[:, N_SSM_GROUPS * D_STATE + g * D_STATE:N_SSM_GROUPS * D_STATE + (g + 1) * D_STATE].astype(BF16)
        x_d = xs * dt_e
        cb = lax.dot_general(cm, bm, nt, preferred_element_type=F32)

        prev = state_ref[g]
        y = jnp.dot(cm, prev.astype(BF16), preferred_element_type=F32) * jnp.exp(acs_e)
        x_d16 = x_d.astype(BF16)
        parts = []
        for j in range(SSM_HEADS_PER_GROUP):
            seg = acs[:, j:j + 1] - acs_t[j:j + 1, :]
            m_h = (cb * jnp.exp(jnp.where(tril, seg, NEG_BIG))).astype(BF16)
            parts.append(jnp.dot(m_h, x_d16[:, j * SSM_HEAD_DIM:(j + 1) * SSM_HEAD_DIM],
                                 preferred_element_type=F32))
        y = y + jnp.concatenate(parts, axis=1)

        xw = (x_d * jnp.exp(last_e - acs_e)).astype(BF16)
        state_ref[g] = prev * jnp.exp(last_e) + lax.dot_general(bm, xw, tn, preferred_element_type=F32)

        y = y + dskip_ref[:, ch] * xs
        y = y * _silu(z_ref[:, ch].astype(F32))
        y = y * lax.rsqrt(jnp.mean(y * y, axis=-1, keepdims=True) + EPS)
        o_ref[:, ch] = (y * nw_ref[:, ch]).astype(o_ref.dtype)


def ssd_mixer(rest, dt_raw, conv_w, conv_b, dt_bias_p, a_p, d_skip_e, norm_w, batch, seq):
    t = rest.shape[0]
    d_inner = N_SSM_GROUPS * SSM_HEADS_PER_GROUP * SSM_HEAD_DIM
    bc_w = 2 * N_SSM_GROUPS * D_STATE
    nc = seq // CHUNK
    gw = SSM_HEADS_PER_GROUP * SSM_HEAD_DIM
    expand = (np.arange(LANES)[:, None] == (np.arange(gw)[None, :] // SSM_HEAD_DIM)).astype(np.float32)
    rowmap = lambda b, c: (b * nc + c, 0)
    const = lambda b, c: (0, 0)
    return pl.pallas_call(
        _ssd_kernel,
        grid=(batch, nc),
        in_specs=[
            pl.BlockSpec((CHUNK, d_inner), lambda b, c: (b * nc + c, 2)),
            pl.BlockSpec((CHUNK, bc_w), lambda b, c: (b * nc + c, 6)),
            pl.BlockSpec((CHUNK, d_inner), rowmap),
            pl.BlockSpec((CHUNK, N_SSM_GROUPS * LANES), rowmap),
            pl.BlockSpec((CONV_WIDTH, d_inner), const),
            pl.BlockSpec((1, d_inner), const),
            pl.BlockSpec((CONV_WIDTH, bc_w), const),
            pl.BlockSpec((1, bc_w), const),
            pl.BlockSpec((1, N_SSM_GROUPS * LANES), const),
            pl.BlockSpec((1, N_SSM_GROUPS * LANES), const),
            pl.BlockSpec((1, d_inner), const),
            pl.BlockSpec((1, d_inner), const),
            pl.BlockSpec((LANES, gw), const),
        ],
        out_specs=pl.BlockSpec((CHUNK, d_inner), rowmap),
        out_shape=jax.ShapeDtypeStruct((t, d_inner), BF16),
        scratch_shapes=[
            pltpu.VMEM((CHUNK + 8, d_inner), F32),
            pltpu.VMEM((CHUNK + 8, bc_w), F32),
            pltpu.VMEM((N_SSM_GROUPS, D_STATE, gw), F32),
        ],
        compiler_params=_params(("parallel", "arbitrary")),
        name="ssd_mixer",
    )(rest, rest, rest, dt_raw, conv_w[:, :d_inner], conv_b[:, :d_inner], conv_w[:, d_inner:],
      conv_b[:, d_inner:], dt_bias_p, a_p, d_skip_e, norm_w, jnp.asarray(expand))


def _merge_kernel(o0_ref, o1_ref, o2_ref, l0_ref, l1_ref, l2_ref, ys_ref, gl_ref, x_ref, wa_ref, ws_ref,
                  wo_ref, gb_ref, nf_ref, rw_ref, rb_ref, expand_ref, x1_ref, h2_ref, ti_ref, tw_ref):
    hi = lax.Precision.HIGHEST
    d = x_ref.shape[1]
    l0, l1, l2 = l0_ref[...], l1_ref[...], l2_ref[...]
    lm = jnp.maximum(jnp.maximum(l0, l1), l2)
    e0, e1, e2 = jnp.exp(l0 - lm), jnp.exp(l1 - lm), jnp.exp(l2 - lm)
    inv = 1.0 / (e0 + e1 + e2)
    expand = expand_ref[...]
    attn = jnp.zeros(o0_ref.shape, F32)
    for e, o_ref in ((e0, o0_ref), (e1, o1_ref), (e2, o2_ref)):
        w_e = jnp.dot(e * inv, expand, precision=hi, preferred_element_type=F32)
        attn = attn + w_e * o_ref[...].astype(F32)
    y_attn = jnp.dot(attn.astype(BF16), wa_ref[...], preferred_element_type=F32)
    y_ssm = jnp.dot(ys_ref[...], ws_ref[...], preferred_element_type=F32)
    gv = gl_ref[...].astype(F32) + gb_ref[...]
    gates = 1.0 / (1.0 + jnp.exp(-gv))
    merged = gates[:, :d] * y_attn + gates[:, d:] * y_ssm
    x1 = x_ref[...] + jnp.dot(merged.astype(BF16), wo_ref[...], preferred_element_type=F32)
    x1_ref[...] = x1
    h2 = x1 * lax.rsqrt(jnp.mean(x1 * x1, axis=-1, keepdims=True) + EPS) * nf_ref[...]
    h2_ref[...] = h2
    logits = jnp.dot(h2, rw_ref[...], precision=hi, preferred_element_type=F32) + rb_ref[...]
    lane = lax.broadcasted_iota(jnp.int32, logits.shape, 1)
    top_i = jnp.zeros(logits.shape, jnp.int32)
    top_v = jnp.full(logits.shape, NEG_BIG, F32)
    work = logits
    for k in range(TOP_K):
        m = jnp.max(work, axis=-1, keepdims=True)
        idx = jnp.min(jnp.where(work == m, lane, LANES), axis=-1, keepdims=True)
        top_i = jnp.where(lane == k, idx, top_i)
        top_v = jnp.where(lane == k, m, top_v)
        work = jnp.where(lane == idx, NEG_BIG * 2.0, work)
    ev = jnp.exp(top_v - jnp.max(top_v, axis=-1, keepdims=True))
    ti_ref[...] = top_i
    tw_ref[...] = ev / jnp.sum(ev, axis=-1, keepdims=True)


def merge_project(outs, lses, y_ssm, rest, x, wa, ws, wo, gate_bias, norm_ffn, router_w_p, router_b_p):
    t, d = x.shape
    tm = 256
    d_inner = y_ssm.shape[1]
    expand = (np.arange(LANES)[:, None] == (np.arange(GROUP_WIDTH)[None, :] // HEAD_DIM)).astype(np.float32)
    rowmap = lambda i: (i, 0)
    const = lambda i: (0, 0)
    full = lambda a: pl.BlockSpec(a.shape, const)
    args = [*outs, *lses, y_ssm, rest, x, wa, ws, wo, gate_bias, norm_ffn, router_w_p, router_b_p,
            jnp.asarray(expand)]
    in_specs = (
        [pl.BlockSpec((tm, GROUP_WIDTH), rowmap)] * 3 + [pl.BlockSpec((tm, LANES), rowmap)] * 3
        + [pl.BlockSpec((tm, d_inner), rowmap),
           pl.BlockSpec((tm, 2 * d), lambda i: (i, 1)),
           pl.BlockSpec((tm, d), rowmap)]
        + [full(a) for a in args[9:]]
    )
    return pl.pallas_call(
        _merge_kernel,
        grid=(t // tm,),
        in_specs=in_specs,
        out_specs=[pl.BlockSpec((tm, d), rowmap), pl.BlockSpec((tm, d), rowmap),
                   pl.BlockSpec((tm, LANES), rowmap), pl.BlockSpec((tm, LANES), rowmap)],
        out_shape=[jax.ShapeDtypeStruct((t, d), F32), jax.ShapeDtypeStruct((t, d), F32),
                   jax.ShapeDtypeStruct((t, LANES), jnp.int32), jax.ShapeDtypeStruct((t, LANES), F32)],
        compiler_params=_params(("parallel",)),
        name="merge_project",
    )(*args)


def _expert_kernel(be_ref, nu_ref, x_ref, w1g_ref, w1l_ref, w2_ref, b1g_ref, b1l_ref, b2_ref, o_ref):
    used = pl.program_id(0) < nu_ref[0]

    @pl.when(used)
    def _():
        xb = x_ref[...]
        glu = jnp.dot(xb, w1g_ref[0], preferred_element_type=F32) + b1g_ref[0]
        lin = jnp.dot(xb, w1l_ref[0], preferred_element_type=F32) + b1l_ref[0]
        glu = jnp.minimum(glu, SWIGLU_LIMIT)
        lin = jnp.clip(lin, -SWIGLU_LIMIT, SWIGLU_LIMIT)
        act = glu * (1.0 / (1.0 + jnp.exp(-SWIGLU_ALPHA * glu))) * (lin + 1.0)
        o_ref[...] = (jnp.dot(act.astype(BF16), w2_ref[0], preferred_element_type=F32)
                      + b2_ref[0]).astype(o_ref.dtype)

    @pl.when(jnp.logical_not(used))
    def _():
        o_ref[...] = jnp.zeros(o_ref.shape, o_ref.dtype)


def expert_ffn(xs, block_e, n_used, w1g, w1l, w2, b1g, b1l, b2):
    cap, d = xs.shape
    de = w1g.shape[2]
    bm = EXPERT_ROWS
    wmap = lambda i, be, nu: (be[i], 0, 0)
    grid_spec = pltpu.PrefetchScalarGridSpec(
        num_scalar_prefetch=2,
        grid=(cap // bm,),
        in_specs=[
            pl.BlockSpec((bm, d), lambda i, be, nu: (i, 0)),
            pl.BlockSpec((1, d, de), wmap),
            pl.BlockSpec((1, d, de), wmap),
            pl.BlockSpec((1, de, d), wmap),
            pl.BlockSpec((1, 1, de), wmap),
            pl.BlockSpec((1, 1, de), wmap),
            pl.BlockSpec((1, 1, d), wmap),
        ],
        out_specs=pl.BlockSpec((bm, d), lambda i, be, nu: (i, 0)),
    )
    return pl.pallas_call(
        _expert_kernel,
        grid_spec=grid_spec,
        out_shape=jax.ShapeDtypeStruct((cap, d), BF16),
        compiler_params=_params(("arbitrary",)),
        name="expert_ffn",
    )(block_e, n_used, xs, w1g, w1l, w2, b1g, b1l, b2)


def _combine_kernel(x1_ref, y_ref, tw_ref, nw_ref, o_ref, *, normalize):
    d = x1_ref.shape[1]
    acc = x1_ref[...]
    for k in range(TOP_K):
        acc = acc + tw_ref[:, k:k + 1] * y_ref[:, k * d:(k + 1) * d].astype(F32)
    if normalize:
        acc = acc * lax.rsqrt(jnp.mean(acc * acc, axis=-1, keepdims=True) + EPS) * nw_ref[...]
    o_ref[...] = acc


def combine_norm(x1, y4, top_w, norm_w, normalize):
    t, d = x1.shape
    tm = 256
    return pl.pallas_call(
        functools.partial(_combine_kernel, normalize=normalize),
        grid=(t // tm,),
        in_specs=[pl.BlockSpec((tm, d), lambda i: (i, 0)), pl.BlockSpec((tm, TOP_K * d), lambda i: (i, 0)),
                  pl.BlockSpec((tm, LANES), lambda i: (i, 0)), pl.BlockSpec((1, d), lambda i: (0, 0))],
        out_specs=pl.BlockSpec((tm, d), lambda i: (i, 0)),
        out_shape=jax.ShapeDtypeStruct((t, d), F32),
        compiler_params=_params(("parallel",)),
        name="combine_norm",
    )(x1, y4, top_w, norm_w)


def routing_layout(top_i, n_tokens):
    n_assign = n_tokens * TOP_K
    bm = EXPERT_ROWS
    cap = n_assign + N_EXPERTS * bm
    n_blocks = cap // bm
    flat_e = top_i[:, :TOP_K].reshape(n_assign)
    order = jnp.argsort(flat_e, stable=True).astype(jnp.int32)
    counts = jnp.zeros((N_EXPERTS,), jnp.int32).at[flat_e].add(1)
    padded = (counts + bm - 1) // bm * bm
    start = jnp.cumsum(counts) - counts
    pend = jnp.cumsum(padded)
    pstart = pend - padded
    n_used = (pend[-1] // bm).astype(jnp.int32)
    block_e = jnp.minimum(jnp.searchsorted(pend, jnp.arange(n_blocks, dtype=jnp.int32) * bm, side='right'),
                          N_EXPERTS - 1).astype(jnp.int32)
    rows = jnp.arange(cap, dtype=jnp.int32)
    row_e = block_e[rows // bm]
    within = rows - pstart[row_e]
    src = start[row_e] + jnp.minimum(within, jnp.maximum(counts[row_e] - 1, 0))
    row_tok = order[jnp.clip(src, 0, n_assign - 1)] // TOP_K
    rank = jnp.zeros((n_assign,), jnp.int32).at[order].set(jnp.arange(n_assign, dtype=jnp.int32) - start[flat_e[order]])
    dest = pstart[flat_e] + rank
    return block_e, n_used.reshape(1), row_tok, dest


def kernel(x, w_in, rel_bias, w_branch_attn, conv_w, conv_b, dt_bias, a_log, d_skip, ssm_norm_w,
           w_branch_ssm, gate_bias, w_out, norm_mix, norm_ffn, router_w, router_b, w1, b1, w2, b2,
           norm_final):
    batch, seq, d = x.shape
    t = batch * seq
    depth = w_in.shape[0]
    n_groups = len(ATTN_GROUPS)
    attn_w = n_groups * GROUP_WIDTH
    d_inner = N_SSM_GROUPS * SSM_HEADS_PER_GROUP * SSM_HEAD_DIM
    n_heads = N_SSM_GROUPS * SSM_HEADS_PER_GROUP
    bc_w = 2 * N_SSM_GROUPS * D_STATE
    xf = x.reshape(t, d)
    for l in range(depth):
        wl = w_in[l]
        o_z = 3 * attn_w
        o_xbc = o_z + d_inner
        o_dt = o_xbc + d_inner + bc_w
        o_gate = o_dt + n_heads
        w_qkv = wl[:, :o_z].astype(BF16)
        w_rest = jnp.concatenate([wl[:, o_z:o_xbc], wl[:, o_gate:], wl[:, o_xbc:o_dt]], axis=1).astype(BF16)
        lane_of_head = (np.arange(n_heads) // SSM_HEADS_PER_GROUP) * LANES + np.arange(n_heads) % SSM_HEADS_PER_GROUP
        w_dt = jnp.zeros((d, N_SSM_GROUPS * LANES), F32).at[:, lane_of_head].set(wl[:, o_dt:o_gate]).astype(BF16)
        dt_bias_p = jnp.zeros((1, N_SSM_GROUPS * LANES), F32).at[0, lane_of_head].set(dt_bias[l].astype(F32))
        a_p = jnp.zeros((1, N_SSM_GROUPS * LANES), F32).at[0, lane_of_head].set(-jnp.exp(a_log[l].astype(F32)))
        d_skip_e = jnp.repeat(d_skip[l].astype(F32), SSM_HEAD_DIM)[None, :]
        zero_side = jnp.zeros((d, LANES), BF16)

        g_mix = norm_mix[l].astype(F32)[None, :]
        qkv, _ = rms_matmul(xf, g_mix, w_qkv, zero_side, tn=attn_w)
        rest, dt_raw = rms_matmul(xf, g_mix, w_rest, w_dt, tn=1792)
        outs, lses = [], []
        for gi in range(n_groups):
            o, lse = dilated_attention(qkv, attention_bias(rel_bias, gi), gi, batch, seq)
            outs.append(o)
            lses.append(lse)
        y_ssm = ssd_mixer(rest, dt_raw, conv_w[l].astype(F32), conv_b[l].astype(F32)[None, :], dt_bias_p, a_p,
                          d_skip_e, ssm_norm_w[l].astype(F32)[None, :], batch, seq)
        router_w_p = jnp.zeros((d, LANES), F32).at[:, :N_EXPERTS].set(router_w[l].astype(F32))
        router_b_p = jnp.full((1, LANES), NEG_BIG, F32).at[0, :N_EXPERTS].set(router_b[l].astype(F32))
        x1, h2, top_i, top_w = merge_project(
            outs, lses, y_ssm, rest, xf, w_branch_attn[l].astype(BF16), w_branch_ssm[l].astype(BF16),
            w_out[l].astype(BF16), gate_bias[l].astype(F32)[None, :], norm_ffn[l].astype(F32)[None, :],
            router_w_p, router_b_p)

        block_e, n_used, row_tok, dest = routing_layout(top_i, t)
        xs = h2.astype(BF16)[row_tok]
        w1l_ = w1[l]
        ybuf = expert_ffn(xs, block_e, n_used,
                          w1l_[:, :, 0::2].astype(BF16), w1l_[:, :, 1::2].astype(BF16), w2[l].astype(BF16),
                          b1[l][:, None, 0::2].astype(F32), b1[l][:, None, 1::2].astype(F32),
                          b2[l][:, None, :].astype(F32))
        y4 = ybuf[dest].reshape(t, TOP_K * d)
        xf = combine_norm(x1, y4, top_w, norm_final.astype(F32)[None, :], normalize=(l == depth - 1))
    return xf.reshape(batch, seq, d)
```

```python
import functools
import math

import jax
import jax.numpy as jnp
import numpy as np
from jax import lax
from jax.experimental import pallas as pl
from jax.experimental.pallas import tpu as pltpu

F32 = jnp.float32
BF16 = jnp.bfloat16

EPS = 1e-5
NEG_BIG = -1e30

HEAD_DIM = 64
ATTN_GROUPS = ((128, 1), (512, 4), (2048, 16))
HEADS_PER_GROUP = 8
GROUP_WIDTH = HEADS_PER_GROUP * HEAD_DIM
ATTN_BLOCK = 128
NUM_BUCKETS = 32
MAX_DISTANCE = 2048
SSM_HEAD_DIM = 64
N_SSM_GROUPS = 4
SSM_HEADS_PER_GROUP = 8
D_STATE = 128
CONV_WIDTH = 4
CHUNK = 128
N_EXPERTS = 32
TOP_K = 4
SWIGLU_LIMIT = 7.0
SWIGLU_ALPHA = 1.702

LANES = 128
V7X_VMEM_BYTES = 64 * 1024 * 1024
VMEM_LIMIT = 48 * 1024 * 1024

ROW_TILE = 1024
EXPERT_ROWS = 256


def _params(semantics):
    return pltpu.CompilerParams(dimension_semantics=semantics, vmem_limit_bytes=VMEM_LIMIT)


def _rms_matmul_kernel(x_ref, g_ref, w_ref, ws_ref, o_ref, os_ref, h_ref):
    @pl.when(pl.program_id(1) == 0)
    def _():
        x = x_ref[...]
        ms = jnp.mean(x * x, axis=-1, keepdims=True)
        h_ref[...] = (x * lax.rsqrt(ms + EPS) * g_ref[...]).astype(BF16)
        os_ref[...] = jnp.dot(h_ref[...], ws_ref[...], preferred_element_type=F32)

    o_ref[...] = jnp.dot(h_ref[...], w_ref[...], preferred_element_type=F32).astype(o_ref.dtype)


def rms_matmul(x, g, w, w_side, tn):
    t, d = x.shape
    n = w.shape[1]
    ns = w_side.shape[1]
    tm = ROW_TILE
    return pl.pallas_call(
        _rms_matmul_kernel,
        grid=(t // tm, n // tn),
        in_specs=[
            pl.BlockSpec((tm, d), lambda i, j: (i, 0)),
            pl.BlockSpec((1, d), lambda i, j: (0, 0)),
            pl.BlockSpec((d, tn), lambda i, j: (0, j)),
            pl.BlockSpec((d, ns), lambda i, j: (0, 0)),
        ],
        out_specs=[
            pl.BlockSpec((tm, tn), lambda i, j: (i, j)),
            pl.BlockSpec((tm, ns), lambda i, j: (i, 0)),
        ],
        out_shape=[jax.ShapeDtypeStruct((t, n), BF16), jax.ShapeDtypeStruct((t, ns), F32)],
        scratch_shapes=[pltpu.VMEM((tm, d), BF16)],
        compiler_params=_params(("parallel", "arbitrary")),
        name="rms_matmul",
    )(x, g, w, w_side)


def _attn_kernel(*refs, has_prev):
    if has_prev:
        q_ref, kc_ref, vc_ref, kp_ref, vp_ref, bias_ref, o_ref, lse_ref = refs
    else:
        q_ref, kc_ref, vc_ref, bias_ref, o_ref, lse_ref = refs
    first = pl.program_id(2) == 0
    lane = lax.broadcasted_iota(jnp.int32, (ATTN_BLOCK, LANES), 1)
    lse_all = jnp.zeros((ATTN_BLOCK, LANES), F32)
    nt = (((1,), (1,)), ((), ()))
    for h in range(HEADS_PER_GROUP):
        cols = slice(h * HEAD_DIM, (h + 1) * HEAD_DIM)
        q = q_ref[:, cols] * jnp.asarray(HEAD_DIM ** -0.5, BF16)
        s_cur = lax.dot_general(q, kc_ref[:, cols], nt, preferred_element_type=F32)
        s_cur = s_cur + bias_ref[h, :, ATTN_BLOCK:]
        m = jnp.max(s_cur, axis=-1, keepdims=True)
        if has_prev:
            s_prev = lax.dot_general(q, kp_ref[:, cols], nt, preferred_element_type=F32)
            s_prev = jnp.where(first, NEG_BIG, s_prev + bias_ref[h, :, :ATTN_BLOCK])
            m = jnp.maximum(m, jnp.max(s_prev, axis=-1, keepdims=True))
        p_cur = jnp.exp(s_cur - m)
        denom = jnp.sum(p_cur, axis=-1, keepdims=True)
        acc = jnp.dot(p_cur.astype(BF16), vc_ref[:, cols], preferred_element_type=F32)
        if has_prev:
            p_prev = jnp.exp(s_prev - m)
            denom = denom + jnp.sum(p_prev, axis=-1, keepdims=True)
            acc = acc + jnp.dot(p_prev.astype(BF16), vp_ref[:, cols], preferred_element_type=F32)
        o_ref[:, cols] = (acc / denom).astype(o_ref.dtype)
        lse_all = jnp.where(lane == h, m + jnp.log(denom), lse_all)
    lse_ref[...] = lse_all


def dilated_attention(qkv, bias, gi, batch, seq):
    _, dil = ATTN_GROUPS[gi]
    t, width = qkv.shape
    n_groups = len(ATTN_GROUPS)
    blocks_per_token = width // GROUP_WIDTH
    sub_len = seq // dil
    nb = sub_len // ATTN_BLOCK
    has_prev = nb > 1
    view = qkv.reshape(t // dil, dil * width)

    def col(which):
        return lambda b, r, n: (b * nb + n, r * blocks_per_token + which * n_groups + gi)

    def col_prev(which):
        return lambda b, r, n: (b * nb + jnp.maximum(n - 1, 0),
                                r * blocks_per_token + which * n_groups + gi)

    blk = (ATTN_BLOCK, GROUP_WIDTH)
    in_specs = [pl.BlockSpec(blk, col(0)), pl.BlockSpec(blk, col(1)), pl.BlockSpec(blk, col(2))]
    args = [view, view, view]
    if has_prev:
        in_specs += [pl.BlockSpec(blk, col_prev(1)), pl.BlockSpec(blk, col_prev(2))]
        args += [view, view]
    in_specs.append(pl.BlockSpec((HEADS_PER_GROUP, ATTN_BLOCK, 2 * ATTN_BLOCK), lambda b, r, n: (0, 0, 0)))
    args.append(bias)
    out, lse = pl.pallas_call(
        functools.partial(_attn_kernel, has_prev=has_prev),
        grid=(batch, dil, nb),
        in_specs=in_specs,
        out_specs=[
            pl.BlockSpec(blk, lambda b, r, n: (b * nb + n, r)),
            pl.BlockSpec((ATTN_BLOCK, LANES), lambda b, r, n: (b * nb + n, r)),
        ],
        out_shape=[
            jax.ShapeDtypeStruct((t // dil, dil * GROUP_WIDTH), BF16),
            jax.ShapeDtypeStruct((t // dil, dil * LANES), F32),
        ],
        compiler_params=_params(("parallel", "parallel", "arbitrary")),
        name=f"dilated_attn_g{gi}",
    )(*args)
    return out.reshape(t, GROUP_WIDTH), lse.reshape(t, LANES)


def attention_bias(rel_bias, gi):
    window, dil = ATTN_GROUPS[gi]
    w_sub = window // dil
    q_idx = np.arange(ATTN_BLOCK)[:, None]
    k_idx = np.arange(2 * ATTN_BLOCK)[None, :]
    delta = q_idx + ATTN_BLOCK - k_idx
    in_band = (delta >= 0) & (delta <= w_sub)
    dist = np.clip(delta, 0, w_sub) * dil
    max_exact = NUM_BUCKETS // 2
    nf = np.maximum(dist, max_exact).astype(np.float32)
    large = max_exact + (np.log(nf / max_exact) / math.log(MAX_DISTANCE / max_exact)
                         * (NUM_BUCKETS - max_exact)).astype(np.int32)
    large = np.minimum(large, NUM_BUCKETS - 1)
    bucket = np.where(dist < max_exact, dist, large)
    table = rel_bias[:, gi * HEADS_PER_GROUP:(gi + 1) * HEADS_PER_GROUP].astype(F32)
    bias = jnp.transpose(table[bucket], (2, 0, 1))
    return jnp.where(in_band[None], bias, NEG_BIG)


def _silu(v):
    return v * (1.0 / (1.0 + jnp.exp(-v)))


def _conv_silu(ext_ref, u, w_ref, b_ref, first):
    rows = u.shape[0]

    @pl.when(first)
    def _():
        ext_ref[0:8, :] = jnp.zeros((8, u.shape[1]), F32)

    ext_ref[8:8 + rows, :] = u
    acc = u * w_ref[CONV_WIDTH - 1:CONV_WIDTH, :] + b_ref[...]
    for k in range(1, CONV_WIDTH):
        acc = acc + ext_ref[8 - k:8 - k + rows, :] * w_ref[CONV_WIDTH - 1 - k:CONV_WIDTH - k, :]
    ext_ref[0:8, :] = u[rows - 8:, :]
    return _silu(acc)


def _ssd_kernel(x_ref, bc_ref, z_ref, dt_ref, cwx_ref, cbx_ref, cwbc_ref, cbbc_ref, dtb_ref, a_ref,
                dskip_ref, nw_ref, expand_ref, o_ref, extx_ref, extbc_ref, state_ref):
    first = pl.program_id(1) == 0
    gw = SSM_HEADS_PER_GROUP * SSM_HEAD_DIM

    @pl.when(first)
    def _():
        state_ref[...] = jnp.zeros(state_ref.shape, F32)

    xs_all = _conv_silu(extx_ref, x_ref[...].astype(F32), cwx_ref, cbx_ref, first)
    bc_all = _conv_silu(extbc_ref, bc_ref[...].astype(F32), cwbc_ref, cbbc_ref, first)

    row = lax.broadcasted_iota(jnp.int32, (CHUNK, CHUNK), 0)
    colm = lax.broadcasted_iota(jnp.int32, (CHUNK, CHUNK), 1)
    tril = row >= colm
    tril_f = tril.astype(F32)
    eye = (row == colm).astype(F32)
    hi = lax.Precision.HIGHEST
    nt = (((1,), (1,)), ((), ()))
    tn = (((0,), (0,)), ((), ()))

    for g in range(N_SSM_GROUPS):
        lanes = slice(g * LANES, (g + 1) * LANES)
        ch = slice(g * gw, (g + 1) * gw)
        v = dt_ref[:, lanes] + dtb_ref[:, lanes]
        dt = jnp.maximum(v, 0.0) + jnp.log1p(jnp.exp(-jnp.abs(v)))
        a_d = dt * a_ref[:, lanes]
        acs = jnp.dot(tril_f, a_d, precision=hi, preferred_element_type=F32)
        acs_t = lax.dot_general(eye, acs, nt, precision=hi, preferred_element_type=F32)
        expand = expand_ref[...]
        dt_e = jnp.dot(dt, expand, precision=hi, preferred_element_type=F32)
        acs_e = jnp.dot(acs, expand, precision=hi, preferred_element_type=F32)
        last_e = acs_e[CHUNK - 1:CHUNK, :]

        xs = xs_all[:, ch]
        bm = bc_all[:, lanes].astype(BF16)
        cm = bc_all[:, N_SSM_GROUPS * D_STATE + g * D_STATE:N_SSM_GROUPS * D_STATE + (g + 1) * D_STATE].astype(BF16)
        x_d = xs * dt_e
        cb = lax.dot_general(cm, bm, nt, preferred_element_type=F32)

        prev = state_ref[g]
        y = jnp.dot(cm, prev.astype(BF16), preferred_element_type=F32) * jnp.exp(acs_e)
        x_d16 = x_d.astype(BF16)
        parts = []
        for j in range(SSM_HEADS_PER_GROUP):
            seg = acs[:, j:j + 1] - acs_t[j:j + 1, :]
            m_h = (cb * jnp.exp(jnp.where(tril, seg, NEG_BIG))).astype(BF16)
            parts.append(jnp.dot(m_h, x_d16[:, j * SSM_HEAD_DIM:(j + 1) * SSM_HEAD_DIM],
                                 preferred_element_type=F32))
        y = y + jnp.concatenate(parts, axis=1)

        xw = (x_d * jnp.exp(last_e - acs_e)).astype(BF16)
        state_ref[g] = prev * jnp.exp(last_e) + lax.dot_general(bm, xw, tn, preferred_element_type=F32)

        y = y + dskip_ref[:, ch] * xs
        y = y * _silu(z_ref[:, ch].astype(F32))
        y = y * lax.rsqrt(jnp.mean(y * y, axis=-1, keepdims=True) + EPS)
        o_ref[:, ch] = (y * nw_ref[:, ch]).astype(o_ref.dtype)


def ssd_mixer(rest, dt_raw, conv_w, conv_b, dt_bias_p, a_p, d_skip_e, norm_w, batch, seq):
    t = rest.shape[0]
    d_inner = N_SSM_GROUPS * SSM_HEADS_PER_GROUP * SSM_HEAD_DIM
    bc_w = 2 * N_SSM_GROUPS * D_STATE
    nc = seq // CHUNK
    gw = SSM_HEADS_PER_GROUP * SSM_HEAD_DIM
    expand = (np.arange(LANES)[:, None] == (np.arange(gw)[None, :] // SSM_HEAD_DIM)).astype(np.float32)
    rowmap = lambda b, c: (b * nc + c, 0)
    const = lambda b, c: (0, 0)
    return pl.pallas_call(
        _ssd_kernel,
        grid=(batch, nc),
        in_specs=[
            pl.BlockSpec((CHUNK, d_inner), lambda b, c: (b * nc + c, 2)),
            pl.BlockSpec((CHUNK, bc_w), lambda b, c: (b * nc + c, 6)),
            pl.BlockSpec((CHUNK, d_inner), rowmap),
            pl.BlockSpec((CHUNK, N_SSM_GROUPS * LANES), rowmap),
            pl.BlockSpec((CONV_WIDTH, d_inner), const),
            pl.BlockSpec((1, d_inner), const),
            pl.BlockSpec((CONV_WIDTH, bc_w), const),
            pl.BlockSpec((1, bc_w), const),
            pl.BlockSpec((1, N_SSM_GROUPS * LANES), const),
            pl.BlockSpec((1, N_SSM_GROUPS * LANES), const),
            pl.BlockSpec((1, d_inner), const),
            pl.BlockSpec((1, d_inner), const),
            pl.BlockSpec((LANES, gw), const),
        ],
        out_specs=pl.BlockSpec((CHUNK, d_inner), rowmap),
        out_shape=jax.ShapeDtypeStruct((t, d_inner), BF16),
        scratch_shapes=[
            pltpu.VMEM((CHUNK + 8, d_inner), F32),
            pltpu.VMEM((CHUNK + 8, bc_w), F32),
            pltpu.VMEM((N_SSM_GROUPS, D_STATE, gw), F32),
        ],
        compiler_params=_params(("parallel", "arbitrary")),
        name="ssd_mixer",
    )(rest, rest, rest, dt_raw, conv_w[:, :d_inner], conv_b[:, :d_inner], conv_w[:, d_inner:],
      conv_b[:, d_inner:], dt_bias_p, a_p, d_skip_e, norm_w, jnp.asarray(expand))


def _merge_kernel(o0_ref, o1_ref, o2_ref, l0_ref, l1_ref, l2_ref, ys_ref, gl_ref, x_ref, wa_ref, ws_ref,
                  wo_ref, gb_ref, nf_ref, rw_ref, rb_ref, expand_ref, x1_ref, h2_ref, ti_ref, tw_ref):
    hi = lax.Precision.HIGHEST
    d = x_ref.shape[1]
    l0, l1, l2 = l0_ref[...], l1_ref[...], l2_ref[...]
    lm = jnp.maximum(jnp.maximum(l0, l1), l2)
    e0, e1, e2 = jnp.exp(l0 - lm), jnp.exp(l1 - lm), jnp.exp(l2 - lm)
    inv = 1.0 / (e0 + e1 + e2)
    expand = expand_ref[...]
    attn = jnp.zeros(o0_ref.shape, F32)
    for e, o_ref in ((e0, o0_ref), (e1, o1_ref), (e2, o2_ref)):
        w_e = jnp.dot(e * inv, expand, precision=hi, preferred_element_type=F32)
        attn = attn + w_e * o_ref[...].astype(F32)
    y_attn = jnp.dot(attn.astype(BF16), wa_ref[...], preferred_element_type=F32)
    y_ssm = jnp.dot(ys_ref[...], ws_ref[...], preferred_element_type=F32)
    gv = gl_ref[...].astype(F32) + gb_ref[...]
    gates = 1.0 / (1.0 + jnp.exp(-gv))
    merged = gates[:, :d] * y_attn + gates[:, d:] * y_ssm
    x1 = x_ref[...] + jnp.dot(merged.astype(BF16), wo_ref[...], preferred_element_type=F32)
    x1_ref[...] = x1
    h2 = x1 * lax.rsqrt(jnp.mean(x1 * x1, axis=-1, keepdims=True) + EPS) * nf_ref[...]
    h2_ref[...] = h2
    logits = jnp.dot(h2, rw_ref[...], precision=hi, preferred_element_type=F32) + rb_ref[...]
    lane = lax.broadcasted_iota(jnp.int32, logits.shape, 1)
    top_i = jnp.zeros(logits.shape, jnp.int32)
    top_v = jnp.full(logits.shape, NEG_BIG, F32)
    work = logits
    for k in range(TOP_K):
        m = jnp.max(work, axis=-1, keepdims=True)
        idx = jnp.min(jnp.where(work == m, lane, LANES), axis=-1, keepdims=True)
        top_i = jnp.where(lane == k, idx, top_i)
        top_v = jnp.where(lane == k, m, top_v)
        work = jnp.where(lane == idx, NEG_BIG * 2.0, work)
    ev = jnp.exp(top_v - jnp.max(top_v, axis=-1, keepdims=True))
    ti_ref[...] = top_i
    tw_ref[...] = ev / jnp.sum(ev, axis=-1, keepdims=True)


def merge_project(outs, lses, y_ssm, rest, x, wa, ws, wo, gate_bias, norm_ffn, router_w_p, router_b_p):
    t, d = x.shape
    tm = 256
    d_inner = y_ssm.shape[1]
    expand = (np.arange(LANES)[:, None] == (np.arange(GROUP_WIDTH)[None, :] // HEAD_DIM)).astype(np.float32)
    rowmap = lambda i: (i, 0)
    const = lambda i: (0, 0)
    full = lambda a: pl.BlockSpec(a.shape, const)
    args = [*outs, *lses, y_ssm, rest, x, wa, ws, wo, gate_bias, norm_ffn, router_w_p, router_b_p,
            jnp.asarray(expand)]
    in_specs = (
        [pl.BlockSpec((tm, GROUP_WIDTH), rowmap)] * 3 + [pl.BlockSpec((tm, LANES), rowmap)] * 3
        + [pl.BlockSpec((tm, d_inner), rowmap),
           pl.BlockSpec((tm, 2 * d), lambda i: (i, 1)),
           pl.BlockSpec((tm, d), rowmap)]
        + [full(a) for a in args[9:]]
    )
    return pl.pallas_call(
        _merge_kernel,
        grid=(t // tm,),
        in_specs=in_specs,
        out_specs=[pl.BlockSpec((tm, d), rowmap), pl.BlockSpec((tm, d), rowmap),
                   pl.BlockSpec((tm, LANES), rowmap), pl.BlockSpec((tm, LANES), rowmap)],
        out_shape=[jax.ShapeDtypeStruct((t, d), F32), jax.ShapeDtypeStruct((t, d), F32),
                   jax.ShapeDtypeStruct((t, LANES), jnp.int32), jax.ShapeDtypeStruct((t, LANES), F32)],
        compiler_params=_params(("parallel",)),
        name="merge_project",
    )(*args)


def _expert_kernel(be_ref, nu_ref, x_ref, w1g_ref, w1l_ref, w2_ref, b1g_ref, b1l_ref, b2_ref, o_ref):
    used = pl.program_id(0) < nu_ref[0]

    @pl.when(used)
    def _():
        xb = x_ref[...]
        nt = (((1,), (1,)), ((), ()))
        glu = lax.dot_general(xb, w1g_ref[0], nt, preferred_element_type=F32) + b1g_ref[0]
        lin = lax.dot_general(xb, w1l_ref[0], nt, preferred_element_type=F32) + b1l_ref[0]
        glu = jnp.minimum(glu, SWIGLU_LIMIT)
        lin = jnp.clip(lin, -SWIGLU_LIMIT, SWIGLU_LIMIT)
        act = glu * (1.0 / (1.0 + jnp.exp(-SWIGLU_ALPHA * glu))) * (lin + 1.0)
        o_ref[...] = (jnp.dot(act.astype(BF16), w2_ref[0], preferred_element_type=F32)
                      + b2_ref[0]).astype(o_ref.dtype)

    @pl.when(jnp.logical_not(used))
    def _():
        o_ref[...] = jnp.zeros(o_ref.shape, o_ref.dtype)


def expert_ffn(xs, block_e, n_used, w1g, w1l, w2, b1g, b1l, b2):
    cap, d = xs.shape
    de = w1g.shape[1]
    bm = EXPERT_ROWS
    wmap = lambda i, be, nu: (be[i], 0, 0)
    grid_spec = pltpu.PrefetchScalarGridSpec(
        num_scalar_prefetch=2,
        grid=(cap // bm,),
        in_specs=[
            pl.BlockSpec((bm, d), lambda i, be, nu: (i, 0)),
            pl.BlockSpec((1, de, d), wmap),
            pl.BlockSpec((1, de, d), wmap),
            pl.BlockSpec((1, de, d), wmap),
            pl.BlockSpec((1, 1, de), wmap),
            pl.BlockSpec((1, 1, de), wmap),
            pl.BlockSpec((1, 1, d), wmap),
        ],
        out_specs=pl.BlockSpec((bm, d), lambda i, be, nu: (i, 0)),
    )
    return pl.pallas_call(
        _expert_kernel,
        grid_spec=grid_spec,
        out_shape=jax.ShapeDtypeStruct((cap, d), BF16),
        compiler_params=_params(("arbitrary",)),
        name="expert_ffn",
    )(block_e, n_used, xs, w1g, w1l, w2, b1g, b1l, b2)


def _combine_kernel(x1_ref, y0_ref, y1_ref, y2_ref, y3_ref, tw_ref, nw_ref, o_ref, *, normalize):
    acc = x1_ref[...]
    for k, y_ref in enumerate((y0_ref, y1_ref, y2_ref, y3_ref)):
        acc = acc + tw_ref[:, k:k + 1] * y_ref[...].astype(F32)
    if normalize:
        acc = acc * lax.rsqrt(jnp.mean(acc * acc, axis=-1, keepdims=True) + EPS) * nw_ref[...]
    o_ref[...] = acc


def combine_norm(x1, y4, top_w, norm_w, normalize):
    t, d = x1.shape
    tm = 256
    return pl.pallas_call(
        functools.partial(_combine_kernel, normalize=normalize),
        grid=(t // tm,),
        in_specs=[pl.BlockSpec((tm, d), lambda i: (i, 0))] * (1 + TOP_K)
        + [pl.BlockSpec((tm, LANES), lambda i: (i, 0)), pl.BlockSpec((1, d), lambda i: (0, 0))],
        out_specs=pl.BlockSpec((tm, d), lambda i: (i, 0)),
        out_shape=jax.ShapeDtypeStruct((t, d), F32),
        compiler_params=_params(("parallel",)),
        name="combine_norm",
    )(x1, *y4, top_w, norm_w)


def routing_layout(top_i, n_tokens):
    n_assign = n_tokens * TOP_K
    bm = EXPERT_ROWS
    cap = n_assign + N_EXPERTS * bm
    n_blocks = cap // bm
    flat_e = top_i[:, :TOP_K].reshape(n_assign)
    order = jnp.argsort(flat_e, stable=True).astype(jnp.int32)
    counts = jnp.zeros((N_EXPERTS,), jnp.int32).at[flat_e].add(1)
    padded = (counts + bm - 1) // bm * bm
    start = jnp.cumsum(counts) - counts
    pend = jnp.cumsum(padded)
    pstart = pend - padded
    n_used = (pend[-1] // bm).astype(jnp.int32)
    block_e = jnp.minimum(jnp.searchsorted(pend, jnp.arange(n_blocks, dtype=jnp.int32) * bm, side='right'),
                          N_EXPERTS - 1).astype(jnp.int32)
    rows = jnp.arange(cap, dtype=jnp.int32)
    row_e = block_e[rows // bm]
    within = rows - pstart[row_e]
    src = start[row_e] + jnp.minimum(within, jnp.maximum(counts[row_e] - 1, 0))
    row_tok = order[jnp.clip(src, 0, n_assign - 1)] // TOP_K
    rank = jnp.zeros((n_assign,), jnp.int32).at[order].set(jnp.arange(n_assign, dtype=jnp.int32) - start[flat_e[order]])
    dest = pstart[flat_e] + rank
    return block_e, n_used.reshape(1), row_tok, dest


def kernel(x, w_in, rel_bias, w_branch_attn, conv_w, conv_b, dt_bias, a_log, d_skip, ssm_norm_w,
           w_branch_ssm, gate_bias, w_out, norm_mix, norm_ffn, router_w, router_b, w1, b1, w2, b2,
           norm_final):
    batch, seq, d = x.shape
    t = batch * seq
    depth = w_in.shape[0]
    n_groups = len(ATTN_GROUPS)
    attn_w = n_groups * GROUP_WIDTH
    d_inner = N_SSM_GROUPS * SSM_HEADS_PER_GROUP * SSM_HEAD_DIM
    n_heads = N_SSM_GROUPS * SSM_HEADS_PER_GROUP
    bc_w = 2 * N_SSM_GROUPS * D_STATE
    xf = x.reshape(t, d)
    for l in range(depth):
        wl = w_in[l]
        o_z = 3 * attn_w
        o_xbc = o_z + d_inner
        o_dt = o_xbc + d_inner + bc_w
        o_gate = o_dt + n_heads
        w_qkv = wl[:, :o_z].astype(BF16)
        w_rest = jnp.concatenate([wl[:, o_z:o_xbc], wl[:, o_gate:], wl[:, o_xbc:o_dt]], axis=1).astype(BF16)
        lane_of_head = (np.arange(n_heads) // SSM_HEADS_PER_GROUP) * LANES + np.arange(n_heads) % SSM_HEADS_PER_GROUP
        w_dt = jnp.zeros((d, N_SSM_GROUPS * LANES), F32).at[:, lane_of_head].set(wl[:, o_dt:o_gate]).astype(BF16)
        dt_bias_p = jnp.zeros((1, N_SSM_GROUPS * LANES), F32).at[0, lane_of_head].set(dt_bias[l].astype(F32))
        a_p = jnp.zeros((1, N_SSM_GROUPS * LANES), F32).at[0, lane_of_head].set(-jnp.exp(a_log[l].astype(F32)))
        d_skip_e = jnp.repeat(d_skip[l].astype(F32), SSM_HEAD_DIM)[None, :]
        zero_side = jnp.zeros((d, LANES), BF16)

        g_mix = norm_mix[l].astype(F32)[None, :]
        qkv, _ = rms_matmul(xf, g_mix, w_qkv, zero_side, tn=attn_w)
        rest, dt_raw = rms_matmul(xf, g_mix, w_rest, w_dt, tn=1792)
        outs, lses = [], []
        for gi in range(n_groups):
            o, lse = dilated_attention(qkv, attention_bias(rel_bias, gi), gi, batch, seq)
            outs.append(o)
            lses.append(lse)
        y_ssm = ssd_mixer(rest, dt_raw, conv_w[l].astype(F32), conv_b[l].astype(F32)[None, :], dt_bias_p, a_p,
                          d_skip_e, ssm_norm_w[l].astype(F32)[None, :], batch, seq)
        router_w_p = jnp.zeros((d, LANES), F32).at[:, :N_EXPERTS].set(router_w[l].astype(F32))
        router_b_p = jnp.full((1, LANES), NEG_BIG, F32).at[0, :N_EXPERTS].set(router_b[l].astype(F32))
        x1, h2, top_i, top_w = merge_project(
            outs, lses, y_ssm, rest, xf, w_branch_attn[l].astype(BF16), w_branch_ssm[l].astype(BF16),
            w_out[l].astype(BF16), gate_bias[l].astype(F32)[None, :], norm_ffn[l].astype(F32)[None, :],
            router_w_p, router_b_p)

        block_e, n_used, row_tok, dest = routing_layout(top_i, t)
        xs = h2.astype(BF16)[row_tok]
        w1t = jnp.transpose(w1[l], (0, 2, 1))
        ybuf = expert_ffn(xs, block_e, n_used,
                          w1t[:, 0::2, :].astype(BF16), w1t[:, 1::2, :].astype(BF16), w2[l].astype(BF16),
                          b1[l][:, None, 0::2].astype(F32), b1[l][:, None, 1::2].astype(F32),
                          b2[l][:, None, :].astype(F32))
        dest4 = dest.reshape(t, TOP_K)
        y4 = [ybuf[dest4[:, k]] for k in range(TOP_K)]
        xf = combine_norm(x1, y4, top_w, norm_final.astype(F32)[None, :], normalize=(l == depth - 1))
    return xf.reshape(batch, seq, d)
```

```python
import functools
import math

import jax
import jax.numpy as jnp
import numpy as np
from jax import lax
from jax.experimental import pallas as pl
from jax.experimental.pallas import tpu as pltpu

F32 = jnp.float32
BF16 = jnp.bfloat16

EPS = 1e-5
NEG_BIG = -1e30

HEAD_DIM = 64
ATTN_GROUPS = ((128, 1), (512, 4), (2048, 16))
HEADS_PER_GROUP = 8
GROUP_WIDTH = HEADS_PER_GROUP * HEAD_DIM
ATTN_BLOCK = 128
NUM_BUCKETS = 32
MAX_DISTANCE = 2048
SSM_HEAD_DIM = 64
N_SSM_GROUPS = 4
SSM_HEADS_PER_GROUP = 8
D_STATE = 128
CONV_WIDTH = 4
CHUNK = 128
N_EXPERTS = 32
TOP_K = 4
SWIGLU_LIMIT = 7.0
SWIGLU_ALPHA = 1.702

LANES = 128
V7X_VMEM_BYTES = 64 * 1024 * 1024
VMEM_LIMIT = 48 * 1024 * 1024

ROW_TILE = 1024
EXPERT_ROWS = 256


def _params(semantics):
    return pltpu.CompilerParams(dimension_semantics=semantics, vmem_limit_bytes=VMEM_LIMIT)


def _rms_matmul_kernel(x_ref, g_ref, w_ref, ws_ref, o_ref, os_ref, hb_ref, h_ref):
    @pl.when(pl.program_id(1) == 0)
    def _():
        x = x_ref[...]
        ms = jnp.mean(x * x, axis=-1, keepdims=True)
        h_ref[...] = (x * lax.rsqrt(ms + EPS) * g_ref[...]).astype(BF16)
        hb_ref[...] = h_ref[...]
        os_ref[...] = jnp.dot(h_ref[...], ws_ref[...], preferred_element_type=F32)

    o_ref[...] = jnp.dot(h_ref[...], w_ref[...], preferred_element_type=F32).astype(o_ref.dtype)


def rms_matmul(x, g, w, w_side, tn):
    t, d = x.shape
    n = w.shape[1]
    ns = w_side.shape[1]
    tm = ROW_TILE
    return pl.pallas_call(
        _rms_matmul_kernel,
        grid=(t // tm, n // tn),
        in_specs=[
            pl.BlockSpec((tm, d), lambda i, j: (i, 0)),
            pl.BlockSpec((1, d), lambda i, j: (0, 0)),
            pl.BlockSpec((d, tn), lambda i, j: (0, j)),
            pl.BlockSpec((d, ns), lambda i, j: (0, 0)),
        ],
        out_specs=[
            pl.BlockSpec((tm, tn), lambda i, j: (i, j)),
            pl.BlockSpec((tm, ns), lambda i, j: (i, 0)),
            pl.BlockSpec((tm, d), lambda i, j: (i, 0)),
        ],
        out_shape=[jax.ShapeDtypeStruct((t, n), BF16), jax.ShapeDtypeStruct((t, ns), F32),
                   jax.ShapeDtypeStruct((t, d), BF16)],
        scratch_shapes=[pltpu.VMEM((tm, d), BF16)],
        compiler_params=_params(("parallel", "arbitrary")),
        name="rms_matmul",
    )(x, g, w, w_side)


def _qkv_kernel(h_ref, w_ref, o0_ref, o1_ref, o2_ref, acc_ref):
    seq = h_ref.shape[0]
    slabs = GROUP_WIDTH // LANES
    for gi, o_ref in enumerate((o0_ref, o1_ref, o2_ref)):
        dil = ATTN_GROUPS[gi][1]
        acc = jnp.dot(h_ref[...], w_ref[:, gi * GROUP_WIDTH:(gi + 1) * GROUP_WIDTH], preferred_element_type=F32)
        if dil == 1:
            o_ref[0, 0, 0] = acc.astype(o_ref.dtype)
            continue
        for s in range(slabs):
            acc_ref[s] = acc[:, s * LANES:(s + 1) * LANES]
        for r in range(dil):
            for s in range(slabs):
                o_ref[0, 0, r, :, s * LANES:(s + 1) * LANES] = (
                    acc_ref[s, pl.ds(r, seq // dil, stride=dil), :].astype(o_ref.dtype))


def qkv_project(h, w_qkv, batch, seq):
    t, d = h.shape
    n_groups = len(ATTN_GROUPS)
    tn = n_groups * GROUP_WIDTH
    out_shapes, out_specs = [], []
    for _, dil in ATTN_GROUPS:
        shape = (3, batch, dil, seq // dil, GROUP_WIDTH)
        out_shapes.append(jax.ShapeDtypeStruct(shape, BF16))
        out_specs.append(pl.BlockSpec((1, 1) + shape[2:], lambda w, b: (w, b, 0, 0, 0)))
    return pl.pallas_call(
        _qkv_kernel,
        grid=(3, batch),
        in_specs=[pl.BlockSpec((seq, d), lambda w, b: (b, 0)), pl.BlockSpec((d, tn), lambda w, b: (0, w))],
        out_specs=out_specs,
        out_shape=out_shapes,
        scratch_shapes=[pltpu.VMEM((GROUP_WIDTH // LANES, seq, LANES), F32)],
        compiler_params=_params(("parallel", "parallel")),
        name="qkv_project",
    )(h, w_qkv)


def _attn_kernel(q0_ref, q1_ref, q2_ref, bias_ref, o_ref, out_ref, lse_ref):
    heads = LANES // HEAD_DIM
    nt = (((1,), (1,)), ((), ()))
    scale = jnp.asarray(HEAD_DIM ** -0.5, BF16)

    def block(g, ref, r, q_row, k_row, k_len, out_row, stride):
        outs, lses = [], []
        for h in range(heads):
            cols = slice(h * HEAD_DIM, (h + 1) * HEAD_DIM)
            q = ref[0, 0, r, pl.ds(q_row, ATTN_BLOCK), cols] * scale
            k = ref[1, 0, r, pl.ds(k_row, k_len), cols]
            v = ref[2, 0, r, pl.ds(k_row, k_len), cols]
            s = lax.dot_general(q, k, nt, preferred_element_type=F32) + bias_ref[g, h, :, 2 * ATTN_BLOCK - k_len:]
            m = jnp.max(s, axis=-1, keepdims=True)
            p = jnp.exp(s - m)
            denom = jnp.sum(p, axis=-1, keepdims=True)
            acc = jnp.dot(p.astype(BF16), v, preferred_element_type=F32)
            outs.append(acc / denom)
            lses.append(jnp.broadcast_to(m + jnp.log(denom), (ATTN_BLOCK, HEAD_DIM)))
        o2 = jnp.concatenate(outs, axis=1)
        l2 = jnp.concatenate(lses, axis=1)
        rows = pl.ds(out_row, ATTN_BLOCK) if stride == 1 else pl.ds(out_row, ATTN_BLOCK, stride=stride)
        out_ref[g, rows, :] = o2
        lse_ref[g, rows, :] = l2

    for g, ref in enumerate((q0_ref, q1_ref, q2_ref)):
        dil = ATTN_GROUPS[g][1]
        sub_len = ref.shape[3]
        nb = sub_len // ATTN_BLOCK

        def residue(r, carry, g=g, ref=ref, dil=dil, sub_len=sub_len, nb=nb):
            block(g, ref, r, 0, 0, ATTN_BLOCK, r, dil)

            def later(n, c):
                q_row = pl.multiple_of(n * ATTN_BLOCK, ATTN_BLOCK)
                k_row = pl.multiple_of((n - 1) * ATTN_BLOCK, ATTN_BLOCK)
                block(g, ref, r, q_row, k_row, 2 * ATTN_BLOCK, r + n * ATTN_BLOCK * dil, dil)
                return c

            if nb > 1:
                lax.fori_loop(1, nb, later, 0)
            return carry

        if dil == 1:
            residue(0, 0)
        else:
            lax.fori_loop(0, dil, residue, 0)

    l0, l1, l2 = lse_ref[0], lse_ref[1], lse_ref[2]
    lm = jnp.maximum(jnp.maximum(l0, l1), l2)
    e0, e1, e2 = jnp.exp(l0 - lm), jnp.exp(l1 - lm), jnp.exp(l2 - lm)
    mixed = (e0 * out_ref[0] + e1 * out_ref[1] + e2 * out_ref[2]) / (e0 + e1 + e2)
    o_ref[...] = mixed.astype(o_ref.dtype)


def dilated_attention(qkv_groups, bias, batch, seq):
    heads = LANES // HEAD_DIM
    in_specs = [pl.BlockSpec((3, 1) + a.shape[2:4] + (LANES,), lambda b, hp: (0, b, 0, 0, hp)) for a in qkv_groups]
    in_specs.append(pl.BlockSpec((len(ATTN_GROUPS), heads, ATTN_BLOCK, 2 * ATTN_BLOCK), lambda b, hp: (0, hp, 0, 0)))
    return pl.pallas_call(
        _attn_kernel,
        grid=(batch, GROUP_WIDTH // LANES),
        in_specs=in_specs,
        out_specs=pl.BlockSpec((seq, LANES), lambda b, hp: (b, hp)),
        out_shape=jax.ShapeDtypeStruct((batch * seq, GROUP_WIDTH), BF16),
        scratch_shapes=[pltpu.VMEM((len(ATTN_GROUPS), seq, LANES), F32),
                        pltpu.VMEM((len(ATTN_GROUPS), seq, LANES), F32)],
        compiler_params=_params(("parallel", "parallel")),
        name="dilated_attn",
    )(*qkv_groups, bias)


def attention_bias(rel_bias, gi):
    window, dil = ATTN_GROUPS[gi]
    w_sub = window // dil
    q_idx = np.arange(ATTN_BLOCK)[:, None]
    k_idx = np.arange(2 * ATTN_BLOCK)[None, :]
    delta = q_idx + ATTN_BLOCK - k_idx
    in_band = (delta >= 0) & (delta <= w_sub)
    dist = np.clip(delta, 0, w_sub) * dil
    max_exact = NUM_BUCKETS // 2
    nf = np.maximum(dist, max_exact).astype(np.float32)
    large = max_exact + (np.log(nf / max_exact) / math.log(MAX_DISTANCE / max_exact)
                         * (NUM_BUCKETS - max_exact)).astype(np.int32)
    large = np.minimum(large, NUM_BUCKETS - 1)
    bucket = np.where(dist < max_exact, dist, large)
    table = rel_bias[:, gi * HEADS_PER_GROUP:(gi + 1) * HEADS_PER_GROUP].astype(F32)
    bias = jnp.transpose(table[bucket], (2, 0, 1))
    return jnp.where(in_band[None], bias, NEG_BIG)


def _silu(v):
    return v * (1.0 / (1.0 + jnp.exp(-v)))


def _conv_silu(ext_ref, u, w_ref, b_ref, first):
    rows = u.shape[0]

    @pl.when(first)
    def _():
        ext_ref[0:8, :] = jnp.zeros((8, u.shape[1]), F32)

    ext_ref[8:8 + rows, :] = u
    acc = u * w_ref[CONV_WIDTH - 1:CONV_WIDTH, :] + b_ref[...]
    for k in range(1, CONV_WIDTH):
        acc = acc + ext_ref[8 - k:8 - k + rows, :] * w_ref[CONV_WIDTH - 1 - k:CONV_WIDTH - k, :]
    ext_ref[0:8, :] = u[rows - 8:, :]
    return _silu(acc)


def _ssd_kernel(x_ref, bc_ref, z_ref, dt_ref, cwx_ref, cbx_ref, cwbc_ref, cbbc_ref, dtb_ref, a_ref,
                dskip_ref, nw_ref, expand_ref, o_ref, extx_ref, extbc_ref, state_ref):
    first = pl.program_id(1) == 0
    gw = SSM_HEADS_PER_GROUP * SSM_HEAD_DIM

    @pl.when(first)
    def _():
        state_ref[...] = jnp.zeros(state_ref.shape, F32)

    xs_all = _conv_silu(extx_ref, x_ref[...].astype(F32), cwx_ref, cbx_ref, first)
    bc_all = _conv_silu(extbc_ref, bc_ref[...].astype(F32), cwbc_ref, cbbc_ref, first)

    row = lax.broadcasted_iota(jnp.int32, (CHUNK, CHUNK), 0)
    colm = lax.broadcasted_iota(jnp.int32, (CHUNK, CHUNK), 1)
    tril = row >= colm
    tril_f = tril.astype(F32)
    eye = (row == colm).astype(F32)
    hi = lax.Precision.HIGHEST
    nt = (((1,), (1,)), ((), ()))
    tn = (((0,), (0,)), ((), ()))

    for g in range(N_SSM_GROUPS):
        lanes = slice(g * LANES, (g + 1) * LANES)
        ch = slice(g * gw, (g + 1) * gw)
        v = dt_ref[:, lanes] + dtb_ref[:, lanes]
        dt = jnp.maximum(v, 0.0) + jnp.log1p(jnp.exp(-jnp.abs(v)))
        a_d = dt * a_ref[:, lanes]
        acs = jnp.dot(tril_f, a_d, precision=hi, preferred_element_type=F32)
        acs_t = lax.dot_general(eye, acs, nt, precision=hi, preferred_element_type=F32)
        expand = expand_ref[...]
        dt_e = jnp.dot(dt, expand, precision=hi, preferred_element_type=F32)
        acs_e = jnp.dot(acs, expand, precision=hi, preferred_element_type=F32)
        last_e = acs_e[CHUNK - 1:CHUNK, :]

        xs = xs_all[:, ch]
        bm = bc_all[:, lanes].astype(BF16)
        cm = bc_all[:, N_SSM_GROUPS * D_STATE + g * D_STATE:N_SSM_GROUPS * D_STATE + (g + 1) * D_STATE].astype(BF16)
        x_d = xs * dt_e
        cb = lax.dot_general(cm, bm, nt, preferred_element_type=F32)

        prev = state_ref[g]
        y = jnp.dot(cm, prev.astype(BF16), preferred_element_type=F32) * jnp.exp(acs_e)
        x_d16 = x_d.astype(BF16)
        parts = []
        for j in range(SSM_HEADS_PER_GROUP):
            seg = acs[:, j:j + 1] - acs_t[j:j + 1, :]
            m_h = (cb * jnp.exp(jnp.where(tril, seg, NEG_BIG))).astype(BF16)
            parts.append(jnp.dot(m_h, x_d16[:, j * SSM_HEAD_DIM:(j + 1) * SSM_HEAD_DIM],
                                 preferred_element_type=F32))
        y = y + jnp.concatenate(parts, axis=1)

        xw = (x_d * jnp.exp(last_e - acs_e)).astype(BF16)
        state_ref[g] = prev * jnp.exp(last_e) + lax.dot_general(bm, xw, tn, preferred_element_type=F32)

        y = y + dskip_ref[:, ch] * xs
        y = y * _silu(z_ref[:, ch].astype(F32))
        y = y * lax.rsqrt(jnp.mean(y * y, axis=-1, keepdims=True) + EPS)
        o_ref[:, ch] = (y * nw_ref[:, ch]).astype(o_ref.dtype)


def ssd_mixer(rest, dt_raw, conv_w, conv_b, dt_bias_p, a_p, d_skip_e, norm_w, batch, seq):
    t = rest.shape[0]
    d_inner = N_SSM_GROUPS * SSM_HEADS_PER_GROUP * SSM_HEAD_DIM
    bc_w = 2 * N_SSM_GROUPS * D_STATE
    nc = seq // CHUNK
    gw = SSM_HEADS_PER_GROUP * SSM_HEAD_DIM
    expand = (np.arange(LANES)[:, None] == (np.arange(gw)[None, :] // SSM_HEAD_DIM)).astype(np.float32)
    rowmap = lambda b, c: (b * nc + c, 0)
    const = lambda b, c: (0, 0)
    return pl.pallas_call(
        _ssd_kernel,
        grid=(batch, nc),
        in_specs=[
            pl.BlockSpec((CHUNK, d_inner), lambda b, c: (b * nc + c, 2)),
            pl.BlockSpec((CHUNK, bc_w), lambda b, c: (b * nc + c, 6)),
            pl.BlockSpec((CHUNK, d_inner), rowmap),
            pl.BlockSpec((CHUNK, N_SSM_GROUPS * LANES), rowmap),
            pl.BlockSpec((CONV_WIDTH, d_inner), const),
            pl.BlockSpec((1, d_inner), const),
            pl.BlockSpec((CONV_WIDTH, bc_w), const),
            pl.BlockSpec((1, bc_w), const),
            pl.BlockSpec((1, N_SSM_GROUPS * LANES), const),
            pl.BlockSpec((1, N_SSM_GROUPS * LANES), const),
            pl.BlockSpec((1, d_inner), const),
            pl.BlockSpec((1, d_inner), const),
            pl.BlockSpec((LANES, gw), const),
        ],
        out_specs=pl.BlockSpec((CHUNK, d_inner), rowmap),
        out_shape=jax.ShapeDtypeStruct((t, d_inner), BF16),
        scratch_shapes=[
            pltpu.VMEM((CHUNK + 8, d_inner), F32),
            pltpu.VMEM((CHUNK + 8, bc_w), F32),
            pltpu.VMEM((N_SSM_GROUPS, D_STATE, gw), F32),
        ],
        compiler_params=_params(("parallel", "arbitrary")),
        name="ssd_mixer",
    )(rest, rest, rest, dt_raw, conv_w[:, :d_inner], conv_b[:, :d_inner], conv_w[:, d_inner:],
      conv_b[:, d_inner:], dt_bias_p, a_p, d_skip_e, norm_w, jnp.asarray(expand))


def _merge_kernel(at_ref, ys_ref, gl_ref, x_ref, wa_ref, ws_ref,
                  wo_ref, gb_ref, nf_ref, rw_ref, rb_ref, x1_ref, h2_ref, ti_ref, tw_ref):
    hi = lax.Precision.HIGHEST
    d = x_ref.shape[1]
    y_attn = jnp.dot(at_ref[...], wa_ref[...], preferred_element_type=F32)
    y_ssm = jnp.dot(ys_ref[...], ws_ref[...], preferred_element_type=F32)
    gv = gl_ref[...].astype(F32) + gb_ref[...]
    gates = 1.0 / (1.0 + jnp.exp(-gv))
    merged = gates[:, :d] * y_attn + gates[:, d:] * y_ssm
    x1 = x_ref[...] + jnp.dot(merged.astype(BF16), wo_ref[...], preferred_element_type=F32)
    x1_ref[...] = x1
    h2 = x1 * lax.rsqrt(jnp.mean(x1 * x1, axis=-1, keepdims=True) + EPS) * nf_ref[...]
    h2_ref[...] = h2.astype(h2_ref.dtype)
    logits = jnp.dot(h2, rw_ref[...], precision=hi, preferred_element_type=F32) + rb_ref[...]
    lane = lax.broadcasted_iota(jnp.int32, logits.shape, 1)
    top_i = jnp.zeros(logits.shape, jnp.int32)
    top_v = jnp.full(logits.shape, NEG_BIG, F32)
    work = logits
    for k in range(TOP_K):
        m = jnp.max(work, axis=-1, keepdims=True)
        idx = jnp.min(jnp.where(work == m, lane, LANES), axis=-1, keepdims=True)
        top_i = jnp.where(lane == k, idx, top_i)
        top_v = jnp.where(lane == k, m, top_v)
        work = jnp.where(lane == idx, NEG_BIG * 2.0, work)
    ev = jnp.exp(top_v - jnp.max(top_v, axis=-1, keepdims=True))
    ti_ref[...] = top_i
    tw_ref[...] = ev / jnp.sum(ev, axis=-1, keepdims=True)


def merge_project(attn, y_ssm, rest, x, wa, ws, wo, gate_bias, norm_ffn, router_w_p, router_b_p):
    t, d = x.shape
    tm = 256
    d_inner = y_ssm.shape[1]
    rowmap = lambda i: (i, 0)
    const = lambda i: (0, 0)
    full = lambda a: pl.BlockSpec(a.shape, const)
    args = [attn, y_ssm, rest, x, wa, ws, wo, gate_bias, norm_ffn, router_w_p, router_b_p]
    in_specs = (
        [pl.BlockSpec((tm, GROUP_WIDTH), rowmap),
         pl.BlockSpec((tm, d_inner), rowmap),
         pl.BlockSpec((tm, 2 * d), lambda i: (i, 1)),
         pl.BlockSpec((tm, d), rowmap)]
        + [full(a) for a in args[4:]]
    )
    return pl.pallas_call(
        _merge_kernel,
        grid=(t // tm,),
        in_specs=in_specs,
        out_specs=[pl.BlockSpec((tm, d), rowmap), pl.BlockSpec((tm, d), rowmap),
                   pl.BlockSpec((tm, LANES), rowmap), pl.BlockSpec((tm, LANES), rowmap)],
        out_shape=[jax.ShapeDtypeStruct((t, d), F32), jax.ShapeDtypeStruct((t, d), BF16),
                   jax.ShapeDtypeStruct((t, LANES), jnp.int32), jax.ShapeDtypeStruct((t, LANES), F32)],
        compiler_params=_params(("parallel",)),
        name="merge_project",
    )(*args)


def _split_w1_kernel(w_ref, g_ref, l_ref, t_ref):
    de = g_ref.shape[1]
    for s in range(t_ref.shape[0]):
        cols = slice(s * LANES, (s + 1) * LANES)
        t_ref[s] = w_ref[0, cols, :].T
        g_ref[0, :, cols] = t_ref[s, pl.ds(0, de, stride=2), :].astype(g_ref.dtype)
        l_ref[0, :, cols] = t_ref[s, pl.ds(1, de, stride=2), :].astype(l_ref.dtype)


def split_w1(w1):
    e, d, de2 = w1.shape
    de = de2 // 2
    tk = 512
    out = jax.ShapeDtypeStruct((e, de, d), BF16)
    return pl.pallas_call(
        _split_w1_kernel,
        grid=(e, d // tk),
        in_specs=[pl.BlockSpec((1, tk, de2), lambda i, k: (i, k, 0))],
        out_specs=[pl.BlockSpec((1, de, tk), lambda i, k: (i, 0, k))] * 2,
        out_shape=[out, out],
        scratch_shapes=[pltpu.VMEM((tk // LANES, de2, LANES), F32)],
        compiler_params=_params(("parallel", "parallel")),
        name="split_w1",
    )(w1)


def _expert_kernel(be_ref, nu_ref, x_ref, w1g_ref, w1l_ref, w2_ref, b1g_ref, b1l_ref, b2_ref, o_ref):
    used = pl.program_id(0) < nu_ref[0]

    @pl.when(used)
    def _():
        xb = x_ref[...]
        nt = (((1,), (1,)), ((), ()))
        glu = lax.dot_general(xb, w1g_ref[0], nt, preferred_element_type=F32) + b1g_ref[0]
        lin = lax.dot_general(xb, w1l_ref[0], nt, preferred_element_type=F32) + b1l_ref[0]
        glu = jnp.minimum(glu, SWIGLU_LIMIT)
        lin = jnp.clip(lin, -SWIGLU_LIMIT, SWIGLU_LIMIT)
        act = glu * (1.0 / (1.0 + jnp.exp(-SWIGLU_ALPHA * glu))) * (lin + 1.0)
        o_ref[...] = (jnp.dot(act.astype(BF16), w2_ref[0], preferred_element_type=F32)
                      + b2_ref[0]).astype(o_ref.dtype)

    @pl.when(jnp.logical_not(used))
    def _():
        o_ref[...] = jnp.zeros(o_ref.shape, o_ref.dtype)


def expert_ffn(xs, block_e, n_used, w1g, w1l, w2, b1g, b1l, b2):
    cap, d = xs.shape
    de = w1g.shape[1]
    bm = EXPERT_ROWS
    wmap = lambda i, be, nu: (be[i], 0, 0)
    grid_spec = pltpu.PrefetchScalarGridSpec(
        num_scalar_prefetch=2,
        grid=(cap // bm,),
        in_specs=[
            pl.BlockSpec((bm, d), lambda i, be, nu: (i, 0)),
            pl.BlockSpec((1, de, d), wmap),
            pl.BlockSpec((1, de, d), wmap),
            pl.BlockSpec((1, de, d), wmap),
            pl.BlockSpec((1, 1, de), wmap),
            pl.BlockSpec((1, 1, de), wmap),
            pl.BlockSpec((1, 1, d), wmap),
        ],
        out_specs=pl.BlockSpec((bm, d), lambda i, be, nu: (i, 0)),
    )
    return pl.pallas_call(
        _expert_kernel,
        grid_spec=grid_spec,
        out_shape=jax.ShapeDtypeStruct((cap, d), BF16),
        compiler_params=_params(("arbitrary",)),
        name="expert_ffn",
    )(block_e, n_used, xs, w1g, w1l, w2, b1g, b1l, b2)


def _combine_kernel(x1_ref, y0_ref, y1_ref, y2_ref, y3_ref, tw_ref, nw_ref, o_ref, *, normalize):
    acc = x1_ref[...]
    for k, y_ref in enumerate((y0_ref, y1_ref, y2_ref, y3_ref)):
        acc = acc + tw_ref[:, k:k + 1] * y_ref[...].astype(F32)
    if normalize:
        acc = acc * lax.rsqrt(jnp.mean(acc * acc, axis=-1, keepdims=True) + EPS) * nw_ref[...]
    o_ref[...] = acc


def combine_norm(x1, y4, top_w, norm_w, normalize):
    t, d = x1.shape
    tm = 256
    return pl.pallas_call(
        functools.partial(_combine_kernel, normalize=normalize),
        grid=(t // tm,),
        in_specs=[pl.BlockSpec((tm, d), lambda i: (i, 0))] * (1 + TOP_K)
        + [pl.BlockSpec((tm, LANES), lambda i: (i, 0)), pl.BlockSpec((1, d), lambda i: (0, 0))],
        out_specs=pl.BlockSpec((tm, d), lambda i: (i, 0)),
        out_shape=jax.ShapeDtypeStruct((t, d), F32),
        compiler_params=_params(("parallel",)),
        name="combine_norm",
    )(x1, *y4, top_w, norm_w)


def routing_layout(top_i, n_tokens):
    n_assign = n_tokens * TOP_K
    bm = EXPERT_ROWS
    cap = n_assign + N_EXPERTS * bm
    n_blocks = cap // bm
    flat_e = top_i[:, :TOP_K].reshape(n_assign)
    order = jnp.argsort(flat_e, stable=True).astype(jnp.int32)
    counts = jnp.zeros((N_EXPERTS,), jnp.int32).at[flat_e].add(1)
    padded = (counts + bm - 1) // bm * bm
    start = jnp.cumsum(counts) - counts
    pend = jnp.cumsum(padded)
    pstart = pend - padded
    n_used = (pend[-1] // bm).astype(jnp.int32)
    block_e = jnp.minimum(jnp.searchsorted(pend, jnp.arange(n_blocks, dtype=jnp.int32) * bm, side='right'),
                          N_EXPERTS - 1).astype(jnp.int32)
    rows = jnp.arange(cap, dtype=jnp.int32)
    row_e = block_e[rows // bm]
    within = rows - pstart[row_e]
    src = start[row_e] + jnp.minimum(within, jnp.maximum(counts[row_e] - 1, 0))
    row_tok = order[jnp.clip(src, 0, n_assign - 1)] // TOP_K
    rank = jnp.zeros((n_assign,), jnp.int32).at[order].set(jnp.arange(n_assign, dtype=jnp.int32) - start[flat_e[order]])
    dest = pstart[flat_e] + rank
    return block_e, n_used.reshape(1), row_tok, dest


def kernel(x, w_in, rel_bias, w_branch_attn, conv_w, conv_b, dt_bias, a_log, d_skip, ssm_norm_w,
           w_branch_ssm, gate_bias, w_out, norm_mix, norm_ffn, router_w, router_b, w1, b1, w2, b2,
           norm_final):
    batch, seq, d = x.shape
    t = batch * seq
    depth = w_in.shape[0]
    n_groups = len(ATTN_GROUPS)
    attn_w = n_groups * GROUP_WIDTH
    d_inner = N_SSM_GROUPS * SSM_HEADS_PER_GROUP * SSM_HEAD_DIM
    n_heads = N_SSM_GROUPS * SSM_HEADS_PER_GROUP
    bc_w = 2 * N_SSM_GROUPS * D_STATE
    xf = x.reshape(t, d)
    for l in range(depth):
        wl = w_in[l]
        o_z = 3 * attn_w
        o_xbc = o_z + d_inner
        o_dt = o_xbc + d_inner + bc_w
        o_gate = o_dt + n_heads
        w_qkv = wl[:, :o_z].astype(BF16)
        w_rest = jnp.concatenate([wl[:, o_z:o_xbc], wl[:, o_gate:], wl[:, o_xbc:o_dt]], axis=1).astype(BF16)
        lane_of_head = (np.arange(n_heads) // SSM_HEADS_PER_GROUP) * LANES + np.arange(n_heads) % SSM_HEADS_PER_GROUP
        w_dt = jnp.zeros((d, N_SSM_GROUPS * LANES), F32).at[:, lane_of_head].set(wl[:, o_dt:o_gate]).astype(BF16)
        dt_bias_p = jnp.zeros((1, N_SSM_GROUPS * LANES), F32).at[0, lane_of_head].set(dt_bias[l].astype(F32))
        a_p = jnp.zeros((1, N_SSM_GROUPS * LANES), F32).at[0, lane_of_head].set(-jnp.exp(a_log[l].astype(F32)))
        d_skip_e = jnp.repeat(d_skip[l].astype(F32), SSM_HEAD_DIM)[None, :]

        g_mix = norm_mix[l].astype(F32)[None, :]
        rest, dt_raw, h = rms_matmul(xf, g_mix, w_rest, w_dt, tn=1792)
        qkv_groups = qkv_project(h, w_qkv, batch, seq)
        bias = jnp.stack([attention_bias(rel_bias, gi) for gi in range(n_groups)])
        attn = dilated_attention(qkv_groups, bias, batch, seq)
        y_ssm = ssd_mixer(rest, dt_raw, conv_w[l].astype(F32), conv_b[l].astype(F32)[None, :], dt_bias_p, a_p,
                          d_skip_e, ssm_norm_w[l].astype(F32)[None, :], batch, seq)
        router_w_p = jnp.zeros((d, LANES), F32).at[:, :N_EXPERTS].set(router_w[l].astype(F32))
        router_b_p = jnp.full((1, LANES), NEG_BIG, F32).at[0, :N_EXPERTS].set(router_b[l].astype(F32))
        x1, h2, top_i, top_w = merge_project(
            attn, y_ssm, rest, xf, w_branch_attn[l].astype(BF16), w_branch_ssm[l].astype(BF16),
            w_out[l].astype(BF16), gate_bias[l].astype(F32)[None, :], norm_ffn[l].astype(F32)[None, :],
            router_w_p, router_b_p)

        block_e, n_used, row_tok, dest = routing_layout(top_i, t)
        xs = h2[row_tok]
        w1g_t, w1l_t = split_w1(w1[l].astype(F32))
        ybuf = expert_ffn(xs, block_e, n_used, w1g_t, w1l_t, w2[l].astype(BF16),
                          b1[l][:, None, 0::2].astype(F32), b1[l][:, None, 1::2].astype(F32),
                          b2[l][:, None, :].astype(F32))
        dest4 = dest.reshape(t, TOP_K)
        y4 = [ybuf[dest4[:, k]] for k in range(TOP_K)]
        xf = combine_norm(x1, y4, top_w, norm_final.astype(F32)[None, :], normalize=(l == depth - 1))
    return xf.reshape(batch, seq, d)
```

```python
import functools
import math

import jax
import jax.numpy as jnp
import numpy as np
from jax import lax
from jax.experimental import pallas as pl
from jax.experimental.pallas import tpu as pltpu

F32 = jnp.float32
BF16 = jnp.bfloat16

EPS = 1e-5
NEG_BIG = -1e30

HEAD_DIM = 64
ATTN_GROUPS = ((128, 1), (512, 4), (2048, 16))
HEADS_PER_GROUP = 8
GROUP_WIDTH = HEADS_PER_GROUP * HEAD_DIM
ATTN_BLOCK = 128
NUM_BUCKETS = 32
MAX_DISTANCE = 2048
SSM_HEAD_DIM = 64
N_SSM_GROUPS = 4
SSM_HEADS_PER_GROUP = 8
D_STATE = 128
CONV_WIDTH = 4
CHUNK = 128
N_EXPERTS = 32
TOP_K = 4
SWIGLU_LIMIT = 7.0
SWIGLU_ALPHA = 1.702

LANES = 128
V7X_VMEM_BYTES = 64 * 1024 * 1024
VMEM_LIMIT = 48 * 1024 * 1024

ROW_TILE = 1024
EXPERT_ROWS = 256


def _params(semantics):
    return pltpu.CompilerParams(dimension_semantics=semantics, vmem_limit_bytes=VMEM_LIMIT)


def _rms_matmul_kernel(x_ref, g_ref, w_ref, ws_ref, o_ref, os_ref, hb_ref, h_ref):
    @pl.when(pl.program_id(1) == 0)
    def _():
        x = x_ref[...]
        ms = jnp.mean(x * x, axis=-1, keepdims=True)
        h_ref[...] = (x * lax.rsqrt(ms + EPS) * g_ref[...]).astype(BF16)
        hb_ref[...] = h_ref[...]
        os_ref[...] = jnp.dot(h_ref[...], ws_ref[...], preferred_element_type=F32)

    o_ref[...] = jnp.dot(h_ref[...], w_ref[...], preferred_element_type=F32).astype(o_ref.dtype)


def rms_matmul(x, g, w, w_side, tn):
    t, d = x.shape
    n = w.shape[1]
    ns = w_side.shape[1]
    tm = ROW_TILE
    return pl.pallas_call(
        _rms_matmul_kernel,
        grid=(t // tm, n // tn),
        in_specs=[
            pl.BlockSpec((tm, d), lambda i, j: (i, 0)),
            pl.BlockSpec((1, d), lambda i, j: (0, 0)),
            pl.BlockSpec((d, tn), lambda i, j: (0, j)),
            pl.BlockSpec((d, ns), lambda i, j: (0, 0)),
        ],
        out_specs=[
            pl.BlockSpec((tm, tn), lambda i, j: (i, j)),
            pl.BlockSpec((tm, ns), lambda i, j: (i, 0)),
            pl.BlockSpec((tm, d), lambda i, j: (i, 0)),
        ],
        out_shape=[jax.ShapeDtypeStruct((t, n), BF16), jax.ShapeDtypeStruct((t, ns), F32),
                   jax.ShapeDtypeStruct((t, d), BF16)],
        scratch_shapes=[pltpu.VMEM((tm, d), BF16)],
        compiler_params=_params(("parallel", "arbitrary")),
        name="rms_matmul",
    )(x, g, w, w_side)


def _qkv_kernel(h_ref, w_ref, o0_ref, o1_ref, o2_ref, acc_ref):
    seq = h_ref.shape[0]
    slabs = GROUP_WIDTH // LANES
    n_blocks = seq // ATTN_BLOCK
    for gi, o_ref in enumerate((o0_ref, o1_ref, o2_ref)):
        dil = ATTN_GROUPS[gi][1]
        nb = n_blocks // dil
        acc = jnp.dot(h_ref[...], w_ref[:, gi * GROUP_WIDTH:(gi + 1) * GROUP_WIDTH], preferred_element_type=F32)
        if dil == 1:
            for n in range(n_blocks):
                o_ref[0, 0, n] = acc[n * ATTN_BLOCK:(n + 1) * ATTN_BLOCK, :].astype(o_ref.dtype)
            continue
        for s in range(slabs):
            acc_ref[s] = acc[:, s * LANES:(s + 1) * LANES]
        for r in range(dil):
            for n in range(nb):
                for s in range(slabs):
                    rows = pl.ds(r + n * ATTN_BLOCK * dil, ATTN_BLOCK, stride=dil)
                    o_ref[0, 0, r * nb + n, :, s * LANES:(s + 1) * LANES] = acc_ref[s, rows, :].astype(o_ref.dtype)


def qkv_project(h, w_qkv, batch, seq):
    t, d = h.shape
    n_groups = len(ATTN_GROUPS)
    tn = n_groups * GROUP_WIDTH
    out_shapes, out_specs = [], []
    for _ in ATTN_GROUPS:
        shape = (3, batch, seq // ATTN_BLOCK, ATTN_BLOCK, GROUP_WIDTH)
        out_shapes.append(jax.ShapeDtypeStruct(shape, BF16))
        out_specs.append(pl.BlockSpec((1, 1) + shape[2:], lambda w, b: (w, b, 0, 0, 0)))
    return pl.pallas_call(
        _qkv_kernel,
        grid=(3, batch),
        in_specs=[pl.BlockSpec((seq, d), lambda w, b: (b, 0)), pl.BlockSpec((d, tn), lambda w, b: (0, w))],
        out_specs=out_specs,
        out_shape=out_shapes,
        scratch_shapes=[pltpu.VMEM((GROUP_WIDTH // LANES, seq, LANES), F32)],
        compiler_params=_params(("parallel", "parallel")),
        name="qkv_project",
    )(h, w_qkv)


def _attn_kernel(q0_ref, q1_ref, q2_ref, bias_ref, o_ref, out_ref, lse_ref):
    heads = LANES // HEAD_DIM
    scale = jnp.asarray(HEAD_DIM ** -0.5, BF16)
    n_blocks = q0_ref.shape[2]
    qk = (((2,), (2,)), ((0,), (0,)))
    pv = (((2,), (1,)), ((0,), (0,)))
    blk = lax.broadcasted_iota(jnp.int32, (n_blocks, 1, 1), 0)
    lane = lax.broadcasted_iota(jnp.int32, (1, 1, LANES), 2)

    def shifted(x):
        return jnp.concatenate([x[n_blocks - 1:], x[:n_blocks - 1]], axis=0)

    for g, ref in enumerate((q0_ref, q1_ref, q2_ref)):
        dil = ATTN_GROUPS[g][1]
        nb = n_blocks // dil
        q2h = ref[0, 0] * scale
        keys, vals = ref[1, 0], ref[2, 0]
        if nb > 1:
            keys = jnp.concatenate([shifted(keys), keys], axis=1)
            vals = jnp.concatenate([shifted(vals), vals], axis=1)
        n_keys = keys.shape[1]
        ones = jnp.ones((n_keys, LANES), BF16)
        pvs, dens, maxs = [], [], []
        for h in range(heads):
            in_head = (lane >= h * HEAD_DIM) & (lane < (h + 1) * HEAD_DIM)
            q = jnp.where(in_head, q2h, jnp.zeros_like(q2h))
            s = lax.dot_general(q, keys, qk, preferred_element_type=F32)
            s = s + bias_ref[g, h, :, 2 * ATTN_BLOCK - n_keys:][None]
            if nb > 1:
                key_is_prev = lax.broadcasted_iota(jnp.int32, (1, 1, n_keys), 2) < ATTN_BLOCK
                s = jnp.where((blk % nb == 0) & key_is_prev, NEG_BIG, s)
            m = jnp.max(s, axis=-1, keepdims=True)
            p = jnp.exp(s - m).astype(BF16)
            pvs.append(lax.dot_general(p, vals, pv, preferred_element_type=F32))
            dens.append(jnp.dot(p.reshape(n_blocks * ATTN_BLOCK, n_keys), ones,
                                preferred_element_type=F32).reshape(n_blocks, ATTN_BLOCK, LANES))
            maxs.append(m)
        first = lane < HEAD_DIM
        den = jnp.where(first, dens[0], dens[1])
        o2 = jnp.where(first, pvs[0], pvs[1]) / den
        l2 = jnp.where(first, maxs[0], maxs[1]) + jnp.log(den)
        for r in range(dil):
            for n in range(nb):
                start = r + n * ATTN_BLOCK * dil
                rows = pl.ds(start, ATTN_BLOCK) if dil == 1 else pl.ds(start, ATTN_BLOCK, stride=dil)
                out_ref[g, rows, :] = o2[r * nb + n]
                lse_ref[g, rows, :] = l2[r * nb + n]

    l0, l1, l2 = lse_ref[0], lse_ref[1], lse_ref[2]
    lm = jnp.maximum(jnp.maximum(l0, l1), l2)
    e0, e1, e2 = jnp.exp(l0 - lm), jnp.exp(l1 - lm), jnp.exp(l2 - lm)
    mixed = (e0 * out_ref[0] + e1 * out_ref[1] + e2 * out_ref[2]) / (e0 + e1 + e2)
    o_ref[...] = mixed.astype(o_ref.dtype)


def dilated_attention(qkv_groups, bias, batch, seq):
    heads = LANES // HEAD_DIM
    in_specs = [pl.BlockSpec((3, 1) + a.shape[2:4] + (LANES,), lambda b, hp: (0, b, 0, 0, hp)) for a in qkv_groups]
    in_specs.append(pl.BlockSpec((len(ATTN_GROUPS), heads, ATTN_BLOCK, 2 * ATTN_BLOCK), lambda b, hp: (0, hp, 0, 0)))
    return pl.pallas_call(
        _attn_kernel,
        grid=(batch, GROUP_WIDTH // LANES),
        in_specs=in_specs,
        out_specs=pl.BlockSpec((seq, LANES), lambda b, hp: (b, hp)),
        out_shape=jax.ShapeDtypeStruct((batch * seq, GROUP_WIDTH), BF16),
        scratch_shapes=[pltpu.VMEM((len(ATTN_GROUPS), seq, LANES), F32),
                        pltpu.VMEM((len(ATTN_GROUPS), seq, LANES), F32)],
        compiler_params=_params(("parallel", "parallel")),
        name="dilated_attn",
    )(*qkv_groups, bias)


def attention_bias(rel_bias, gi):
    window, dil = ATTN_GROUPS[gi]
    w_sub = window // dil
    q_idx = np.arange(ATTN_BLOCK)[:, None]
    k_idx = np.arange(2 * ATTN_BLOCK)[None, :]
    delta = q_idx + ATTN_BLOCK - k_idx
    in_band = (delta >= 0) & (delta <= w_sub)
    dist = np.clip(delta, 0, w_sub) * dil
    max_exact = NUM_BUCKETS // 2
    nf = np.maximum(dist, max_exact).astype(np.float32)
    large = max_exact + (np.log(nf / max_exact) / math.log(MAX_DISTANCE / max_exact)
                         * (NUM_BUCKETS - max_exact)).astype(np.int32)
    large = np.minimum(large, NUM_BUCKETS - 1)
    bucket = np.where(dist < max_exact, dist, large)
    table = rel_bias[:, gi * HEADS_PER_GROUP:(gi + 1) * HEADS_PER_GROUP].astype(F32)
    bias = jnp.transpose(table[bucket], (2, 0, 1))
    return jnp.where(in_band[None], bias, NEG_BIG)


def _silu(v):
    return v * (1.0 / (1.0 + jnp.exp(-v)))


def _conv_silu(ext_ref, u, w_ref, b_ref, first):
    rows = u.shape[0]

    @pl.when(first)
    def _():
        ext_ref[0:8, :] = jnp.zeros((8, u.shape[1]), F32)

    ext_ref[8:8 + rows, :] = u
    acc = u * w_ref[CONV_WIDTH - 1:CONV_WIDTH, :] + b_ref[...]
    for k in range(1, CONV_WIDTH):
        acc = acc + ext_ref[8 - k:8 - k + rows, :] * w_ref[CONV_WIDTH - 1 - k:CONV_WIDTH - k, :]
    ext_ref[0:8, :] = u[rows - 8:, :]
    return _silu(acc)


def _ssd_kernel(x_ref, bc_ref, z_ref, dt_ref, cwx_ref, cbx_ref, cwbc_ref, cbbc_ref, dtb_ref, a_ref,
                dskip_ref, nw_ref, expand_ref, o_ref, extx_ref, extbc_ref, state_ref):
    first = pl.program_id(1) == 0
    gw = SSM_HEADS_PER_GROUP * SSM_HEAD_DIM

    @pl.when(first)
    def _():
        state_ref[...] = jnp.zeros(state_ref.shape, F32)

    xs_all = _conv_silu(extx_ref, x_ref[...].astype(F32), cwx_ref, cbx_ref, first)
    bc_all = _conv_silu(extbc_ref, bc_ref[...].astype(F32), cwbc_ref, cbbc_ref, first)

    row = lax.broadcasted_iota(jnp.int32, (CHUNK, CHUNK), 0)
    colm = lax.broadcasted_iota(jnp.int32, (CHUNK, CHUNK), 1)
    tril = row >= colm
    tril_f = tril.astype(F32)
    eye = (row == colm).astype(F32)
    hi = lax.Precision.HIGHEST
    nt = (((1,), (1,)), ((), ()))
    tn = (((0,), (0,)), ((), ()))

    for g in range(N_SSM_GROUPS):
        lanes = slice(g * LANES, (g + 1) * LANES)
        ch = slice(g * gw, (g + 1) * gw)
        v = dt_ref[:, lanes] + dtb_ref[:, lanes]
        dt = jnp.maximum(v, 0.0) + jnp.log1p(jnp.exp(-jnp.abs(v)))
        a_d = dt * a_ref[:, lanes]
        acs = jnp.dot(tril_f, a_d, precision=hi, preferred_element_type=F32)
        acs_t = lax.dot_general(eye, acs, nt, precision=hi, preferred_element_type=F32)
        expand = expand_ref[...]
        dt_e = jnp.dot(dt, expand, precision=hi, preferred_element_type=F32)
        acs_e = jnp.dot(acs, expand, precision=hi, preferred_element_type=F32)
        last_e = acs_e[CHUNK - 1:CHUNK, :]

        xs = xs_all[:, ch]
        bm = bc_all[:, lanes].astype(BF16)
        cm = bc_all[:, N_SSM_GROUPS * D_STATE + g * D_STATE:N_SSM_GROUPS * D_STATE + (g + 1) * D_STATE].astype(BF16)
        x_d = xs * dt_e
        cb = lax.dot_general(cm, bm, nt, preferred_element_type=F32)

        prev = state_ref[g]
        y = jnp.dot(cm, prev.astype(BF16), preferred_element_type=F32) * jnp.exp(acs_e)
        x_d16 = x_d.astype(BF16)
        parts = []
        for j in range(SSM_HEADS_PER_GROUP):
            seg = acs[:, j:j + 1] - acs_t[j:j + 1, :]
            m_h = (cb * jnp.exp(jnp.where(tril, seg, NEG_BIG))).astype(BF16)
            parts.append(jnp.dot(m_h, x_d16[:, j * SSM_HEAD_DIM:(j + 1) * SSM_HEAD_DIM],
                                 preferred_element_type=F32))
        y = y + jnp.concatenate(parts, axis=1)

        xw = (x_d * jnp.exp(last_e - acs_e)).astype(BF16)
        state_ref[g] = prev * jnp.exp(last_e) + lax.dot_general(bm, xw, tn, preferred_element_type=F32)

        y = y + dskip_ref[:, ch] * xs
        y = y * _silu(z_ref[:, ch].astype(F32))
        y = y * lax.rsqrt(jnp.mean(y * y, axis=-1, keepdims=True) + EPS)
        o_ref[:, ch] = (y * nw_ref[:, ch]).astype(o_ref.dtype)


def ssd_mixer(rest, dt_raw, conv_w, conv_b, dt_bias_p, a_p, d_skip_e, norm_w, batch, seq):
    t = rest.shape[0]
    d_inner = N_SSM_GROUPS * SSM_HEADS_PER_GROUP * SSM_HEAD_DIM
    bc_w = 2 * N_SSM_GROUPS * D_STATE
    nc = seq // CHUNK
    gw = SSM_HEADS_PER_GROUP * SSM_HEAD_DIM
    expand = (np.arange(LANES)[:, None] == (np.arange(gw)[None, :] // SSM_HEAD_DIM)).astype(np.float32)
    rowmap = lambda b, c: (b * nc + c, 0)
    const = lambda b, c: (0, 0)
    return pl.pallas_call(
        _ssd_kernel,
        grid=(batch, nc),
        in_specs=[
            pl.BlockSpec((CHUNK, d_inner), lambda b, c: (b * nc + c, 2)),
            pl.BlockSpec((CHUNK, bc_w), lambda b, c: (b * nc + c, 6)),
            pl.BlockSpec((CHUNK, d_inner), rowmap),
            pl.BlockSpec((CHUNK, N_SSM_GROUPS * LANES), rowmap),
            pl.BlockSpec((CONV_WIDTH, d_inner), const),
            pl.BlockSpec((1, d_inner), const),
            pl.BlockSpec((CONV_WIDTH, bc_w), const),
            pl.BlockSpec((1, bc_w), const),
            pl.BlockSpec((1, N_SSM_GROUPS * LANES), const),
            pl.BlockSpec((1, N_SSM_GROUPS * LANES), const),
            pl.BlockSpec((1, d_inner), const),
            pl.BlockSpec((1, d_inner), const),
            pl.BlockSpec((LANES, gw), const),
        ],
        out_specs=pl.BlockSpec((CHUNK, d_inner), rowmap),
        out_shape=jax.ShapeDtypeStruct((t, d_inner), BF16),
        scratch_shapes=[
            pltpu.VMEM((CHUNK + 8, d_inner), F32),
            pltpu.VMEM((CHUNK + 8, bc_w), F32),
            pltpu.VMEM((N_SSM_GROUPS, D_STATE, gw), F32),
        ],
        compiler_params=_params(("parallel", "arbitrary")),
        name="ssd_mixer",
    )(rest, rest, rest, dt_raw, conv_w[:, :d_inner], conv_b[:, :d_inner], conv_w[:, d_inner:],
      conv_b[:, d_inner:], dt_bias_p, a_p, d_skip_e, norm_w, jnp.asarray(expand))


def _merge_kernel(at_ref, ys_ref, gl_ref, x_ref, wa_ref, ws_ref,
                  wo_ref, gb_ref, nf_ref, rw_ref, rb_ref, x1_ref, h2_ref, ti_ref, tw_ref):
    hi = lax.Precision.HIGHEST
    d = x_ref.shape[1]
    y_attn = jnp.dot(at_ref[...], wa_ref[...], preferred_element_type=F32)
    y_ssm = jnp.dot(ys_ref[...], ws_ref[...], preferred_element_type=F32)
    gv = gl_ref[...].astype(F32) + gb_ref[...]
    gates = 1.0 / (1.0 + jnp.exp(-gv))
    merged = gates[:, :d] * y_attn + gates[:, d:] * y_ssm
    x1 = x_ref[...] + jnp.dot(merged.astype(BF16), wo_ref[...], preferred_element_type=F32)
    x1_ref[...] = x1
    h2 = x1 * lax.rsqrt(jnp.mean(x1 * x1, axis=-1, keepdims=True) + EPS) * nf_ref[...]
    h2_ref[...] = h2.astype(h2_ref.dtype)
    logits = jnp.dot(h2, rw_ref[...], precision=hi, preferred_element_type=F32) + rb_ref[...]
    lane = lax.broadcasted_iota(jnp.int32, logits.shape, 1)
    top_i = jnp.zeros(logits.shape, jnp.int32)
    top_v = jnp.full(logits.shape, NEG_BIG, F32)
    work = logits
    for k in range(TOP_K):
        m = jnp.max(work, axis=-1, keepdims=True)
        idx = jnp.min(jnp.where(work == m, lane, LANES), axis=-1, keepdims=True)
        top_i = jnp.where(lane == k, idx, top_i)
        top_v = jnp.where(lane == k, m, top_v)
        work = jnp.where(lane == idx, NEG_BIG * 2.0, work)
    ev = jnp.exp(top_v - jnp.max(top_v, axis=-1, keepdims=True))
    ti_ref[...] = top_i
    tw_ref[...] = ev / jnp.sum(ev, axis=-1, keepdims=True)


def merge_project(attn, y_ssm, rest, x, wa, ws, wo, gate_bias, norm_ffn, router_w_p, router_b_p):
    t, d = x.shape
    tm = 256
    d_inner = y_ssm.shape[1]
    rowmap = lambda i: (i, 0)
    const = lambda i: (0, 0)
    full = lambda a: pl.BlockSpec(a.shape, const)
    args = [attn, y_ssm, rest, x, wa, ws, wo, gate_bias, norm_ffn, router_w_p, router_b_p]
    in_specs = (
        [pl.BlockSpec((tm, GROUP_WIDTH), rowmap),
         pl.BlockSpec((tm, d_inner), rowmap),
         pl.BlockSpec((tm, 2 * d), lambda i: (i, 1)),
         pl.BlockSpec((tm, d), rowmap)]
        + [full(a) for a in args[4:]]
    )
    return pl.pallas_call(
        _merge_kernel,
        grid=(t // tm,),
        in_specs=in_specs,
        out_specs=[pl.BlockSpec((tm, d), rowmap), pl.BlockSpec((tm, d), rowmap),
                   pl.BlockSpec((tm, LANES), rowmap), pl.BlockSpec((tm, LANES), rowmap)],
        out_shape=[jax.ShapeDtypeStruct((t, d), F32), jax.ShapeDtypeStruct((t, d), BF16),
                   jax.ShapeDtypeStruct((t, LANES), jnp.int32), jax.ShapeDtypeStruct((t, LANES), F32)],
        compiler_params=_params(("parallel",)),
        name="merge_project",
    )(*args)


def _split_w1_kernel(w_ref, g_ref, l_ref, t_ref):
    de = g_ref.shape[1]
    for s in range(t_ref.shape[0]):
        cols = slice(s * LANES, (s + 1) * LANES)
        t_ref[s] = w_ref[0, cols, :].T
        g_ref[0, :, cols] = t_ref[s, pl.ds(0, de, stride=2), :].astype(g_ref.dtype)
        l_ref[0, :, cols] = t_ref[s, pl.ds(1, de, stride=2), :].astype(l_ref.dtype)


def split_w1(w1):
    e, d, de2 = w1.shape
    de = de2 // 2
    tk = 512
    out = jax.ShapeDtypeStruct((e, de, d), BF16)
    return pl.pallas_call(
        _split_w1_kernel,
        grid=(e, d // tk),
        in_specs=[pl.BlockSpec((1, tk, de2), lambda i, k: (i, k, 0))],
        out_specs=[pl.BlockSpec((1, de, tk), lambda i, k: (i, 0, k))] * 2,
        out_shape=[out, out],
        scratch_shapes=[pltpu.VMEM((tk // LANES, de2, LANES), F32)],
        compiler_params=_params(("parallel", "parallel")),
        name="split_w1",
    )(w1)


def _expert_kernel(be_ref, nu_ref, x_ref, w1g_ref, w1l_ref, w2_ref, b1g_ref, b1l_ref, b2_ref, o_ref):
    used = pl.program_id(0) < nu_ref[0]

    @pl.when(used)
    def _():
        xb = x_ref[...]
        nt = (((1,), (1,)), ((), ()))
        glu = lax.dot_general(xb, w1g_ref[0], nt, preferred_element_type=F32) + b1g_ref[0]
        lin = lax.dot_general(xb, w1l_ref[0], nt, preferred_element_type=F32) + b1l_ref[0]
        glu = jnp.minimum(glu, SWIGLU_LIMIT)
        lin = jnp.clip(lin, -SWIGLU_LIMIT, SWIGLU_LIMIT)
        act = glu * (1.0 / (1.0 + jnp.exp(-SWIGLU_ALPHA * glu))) * (lin + 1.0)
        o_ref[...] = (jnp.dot(act.astype(BF16), w2_ref[0], preferred_element_type=F32)
                      + b2_ref[0]).astype(o_ref.dtype)

    @pl.when(jnp.logical_not(used))
    def _():
        o_ref[...] = jnp.zeros(o_ref.shape, o_ref.dtype)


def expert_ffn(xs, block_e, n_used, w1g, w1l, w2, b1g, b1l, b2):
    cap, d = xs.shape
    de = w1g.shape[1]
    bm = EXPERT_ROWS
    wmap = lambda i, be, nu: (be[i], 0, 0)
    grid_spec = pltpu.PrefetchScalarGridSpec(
        num_scalar_prefetch=2,
        grid=(cap // bm,),
        in_specs=[
            pl.BlockSpec((bm, d), lambda i, be, nu: (i, 0)),
            pl.BlockSpec((1, de, d), wmap),
            pl.BlockSpec((1, de, d), wmap),
            pl.BlockSpec((1, de, d), wmap),
            pl.BlockSpec((1, 1, de), wmap),
            pl.BlockSpec((1, 1, de), wmap),
            pl.BlockSpec((1, 1, d), wmap),
        ],
        out_specs=pl.BlockSpec((bm, d), lambda i, be, nu: (i, 0)),
    )
    return pl.pallas_call(
        _expert_kernel,
        grid_spec=grid_spec,
        out_shape=jax.ShapeDtypeStruct((cap, d), BF16),
        compiler_params=_params(("arbitrary",)),
        name="expert_ffn",
    )(block_e, n_used, xs, w1g, w1l, w2, b1g, b1l, b2)


def _combine_kernel(x1_ref, y0_ref, y1_ref, y2_ref, y3_ref, tw_ref, nw_ref, o_ref, *, normalize):
    acc = x1_ref[...]
    for k, y_ref in enumerate((y0_ref, y1_ref, y2_ref, y3_ref)):
        acc = acc + tw_ref[:, k:k + 1] * y_ref[...].astype(F32)
    if normalize:
        acc = acc * lax.rsqrt(jnp.mean(acc * acc, axis=-1, keepdims=True) + EPS) * nw_ref[...]
    o_ref[...] = acc


def combine_norm(x1, y4, top_w, norm_w, normalize):
    t, d = x1.shape
    tm = 256
    return pl.pallas_call(
        functools.partial(_combine_kernel, normalize=normalize),
        grid=(t // tm,),
        in_specs=[pl.BlockSpec((tm, d), lambda i: (i, 0))] * (1 + TOP_K)
        + [pl.BlockSpec((tm, LANES), lambda i: (i, 0)), pl.BlockSpec((1, d), lambda i: (0, 0))],
        out_specs=pl.BlockSpec((tm, d), lambda i: (i, 0)),
        out_shape=jax.ShapeDtypeStruct((t, d), F32),
        compiler_params=_params(("parallel",)),
        name="combine_norm",
    )(x1, *y4, top_w, norm_w)


def routing_layout(top_i, n_tokens):
    n_assign = n_tokens * TOP_K
    bm = EXPERT_ROWS
    cap = n_assign + N_EXPERTS * bm
    n_blocks = cap // bm
    flat_e = top_i[:, :TOP_K].reshape(n_assign)
    order = jnp.argsort(flat_e, stable=True).astype(jnp.int32)
    counts = jnp.zeros((N_EXPERTS,), jnp.int32).at[flat_e].add(1)
    padded = (counts + bm - 1) // bm * bm
    start = jnp.cumsum(counts) - counts
    pend = jnp.cumsum(padded)
    pstart = pend - padded
    n_used = (pend[-1] // bm).astype(jnp.int32)
    block_e = jnp.minimum(jnp.searchsorted(pend, jnp.arange(n_blocks, dtype=jnp.int32) * bm, side='right'),
                          N_EXPERTS - 1).astype(jnp.int32)
    rows = jnp.arange(cap, dtype=jnp.int32)
    row_e = block_e[rows // bm]
    within = rows - pstart[row_e]
    src = start[row_e] + jnp.minimum(within, jnp.maximum(counts[row_e] - 1, 0))
    row_tok = order[jnp.clip(src, 0, n_assign - 1)] // TOP_K
    rank = jnp.zeros((n_assign,), jnp.int32).at[order].set(jnp.arange(n_assign, dtype=jnp.int32) - start[flat_e[order]])
    dest = pstart[flat_e] + rank
    return block_e, n_used.reshape(1), row_tok, dest


def kernel(x, w_in, rel_bias, w_branch_attn, conv_w, conv_b, dt_bias, a_log, d_skip, ssm_norm_w,
           w_branch_ssm, gate_bias, w_out, norm_mix, norm_ffn, router_w, router_b, w1, b1, w2, b2,
           norm_final):
    batch, seq, d = x.shape
    t = batch * seq
    depth = w_in.shape[0]
    n_groups = len(ATTN_GROUPS)
    attn_w = n_groups * GROUP_WIDTH
    d_inner = N_SSM_GROUPS * SSM_HEADS_PER_GROUP * SSM_HEAD_DIM
    n_heads = N_SSM_GROUPS * SSM_HEADS_PER_GROUP
    bc_w = 2 * N_SSM_GROUPS * D_STATE
    xf = x.reshape(t, d)
    for l in range(depth):
        wl = w_in[l]
        o_z = 3 * attn_w
        o_xbc = o_z + d_inner
        o_dt = o_xbc + d_inner + bc_w
        o_gate = o_dt + n_heads
        w_qkv = wl[:, :o_z].astype(BF16)
        w_rest = jnp.concatenate([wl[:, o_z:o_xbc], wl[:, o_gate:], wl[:, o_xbc:o_dt]], axis=1).astype(BF16)
        lane_of_head = (np.arange(n_heads) // SSM_HEADS_PER_GROUP) * LANES + np.arange(n_heads) % SSM_HEADS_PER_GROUP
        w_dt = jnp.zeros((d, N_SSM_GROUPS * LANES), F32).at[:, lane_of_head].set(wl[:, o_dt:o_gate]).astype(BF16)
        dt_bias_p = jnp.zeros((1, N_SSM_GROUPS * LANES), F32).at[0, lane_of_head].set(dt_bias[l].astype(F32))
        a_p = jnp.zeros((1, N_SSM_GROUPS * LANES), F32).at[0, lane_of_head].set(-jnp.exp(a_log[l].astype(F32)))
        d_skip_e = jnp.repeat(d_skip[l].astype(F32), SSM_HEAD_DIM)[None, :]

        g_mix = norm_mix[l].astype(F32)[None, :]
        rest, dt_raw, h = rms_matmul(xf, g_mix, w_rest, w_dt, tn=1792)
        qkv_groups = qkv_project(h, w_qkv, batch, seq)
        bias = jnp.stack([attention_bias(rel_bias, gi) for gi in range(n_groups)])
        attn = dilated_attention(qkv_groups, bias, batch, seq)
        y_ssm = ssd_mixer(rest, dt_raw, conv_w[l].astype(F32), conv_b[l].astype(F32)[None, :], dt_bias_p, a_p,
                          d_skip_e, ssm_norm_w[l].astype(F32)[None, :], batch, seq)
        router_w_p = jnp.zeros((d, LANES), F32).at[:, :N_EXPERTS].set(router_w[l].astype(F32))
        router_b_p = jnp.full((1, LANES), NEG_BIG, F32).at[0, :N_EXPERTS].set(router_b[l].astype(F32))
        x1, h2, top_i, top_w = merge_project(
            attn, y_ssm, rest, xf, w_branch_attn[l].astype(BF16), w_branch_ssm[l].astype(BF16),
            w_out[l].astype(BF16), gate_bias[l].astype(F32)[None, :], norm_ffn[l].astype(F32)[None, :],
            router_w_p, router_b_p)

        block_e, n_used, row_tok, dest = routing_layout(top_i, t)
        xs = h2[row_tok]
        w1g_t, w1l_t = split_w1(w1[l].astype(F32))
        ybuf = expert_ffn(xs, block_e, n_used, w1g_t, w1l_t, w2[l].astype(BF16),
                          b1[l][:, None, 0::2].astype(F32), b1[l][:, None, 1::2].astype(F32),
                          b2[l][:, None, :].astype(F32))
        dest4 = dest.reshape(t, TOP_K)
        y4 = [ybuf[dest4[:, k]] for k in range(TOP_K)]
        xf = combine_norm(x1, y4, top_w, norm_final.astype(F32)[None, :], normalize=(l == depth - 1))
    return xf.reshape(batch, seq, d)
```

```python
import functools
import math

import jax
import jax.numpy as jnp
import numpy as np
from jax import lax
from jax.experimental import pallas as pl
from jax.experimental.pallas import tpu as pltpu

F32 = jnp.float32
BF16 = jnp.bfloat16

EPS = 1e-5
NEG_BIG = -1e30

HEAD_DIM = 64
ATTN_GROUPS = ((128, 1), (512, 4), (2048, 16))
HEADS_PER_GROUP = 8
GROUP_WIDTH = HEADS_PER_GROUP * HEAD_DIM
ATTN_BLOCK = 128
NUM_BUCKETS = 32
MAX_DISTANCE = 2048
SSM_HEAD_DIM = 64
N_SSM_GROUPS = 4
SSM_HEADS_PER_GROUP = 8
D_STATE = 128
CONV_WIDTH = 4
CHUNK = 128
N_EXPERTS = 32
TOP_K = 4
SWIGLU_LIMIT = 7.0
SWIGLU_ALPHA = 1.702

LANES = 128
V7X_VMEM_BYTES = 64 * 1024 * 1024
VMEM_LIMIT = 48 * 1024 * 1024

ROW_TILE = 1024
MERGE_SUBTILES = 2
EXPERT_ROWS = 256


def padded_tokens(n_tokens):
    return n_tokens + N_EXPERTS * EXPERT_ROWS // TOP_K + 2 * EXPERT_ROWS


def _params(semantics):
    return pltpu.CompilerParams(dimension_semantics=semantics, vmem_limit_bytes=VMEM_LIMIT)


def _rms_matmul_kernel(x_ref, g_ref, w_ref, ws_ref, o_ref, os_ref, hb_ref, h_ref):
    @pl.when(pl.program_id(1) == 0)
    def _():
        x = x_ref[...]
        ms = jnp.mean(x * x, axis=-1, keepdims=True)
        h_ref[...] = (x * lax.rsqrt(ms + EPS) * g_ref[...]).astype(BF16)
        hb_ref[...] = h_ref[...]
        os_ref[...] = jnp.dot(h_ref[...], ws_ref[...], preferred_element_type=F32)

    o_ref[...] = jnp.dot(h_ref[...], w_ref[...], preferred_element_type=F32).astype(o_ref.dtype)


def rms_matmul(x, g, w, w_side, tn):
    t, d = x.shape
    n = w.shape[1]
    ns = w_side.shape[1]
    tm = ROW_TILE
    return pl.pallas_call(
        _rms_matmul_kernel,
        grid=(t // tm, n // tn),
        in_specs=[
            pl.BlockSpec((tm, d), lambda i, j: (i, 0)),
            pl.BlockSpec((1, d), lambda i, j: (0, 0)),
            pl.BlockSpec((d, tn), lambda i, j: (0, j)),
            pl.BlockSpec((d, ns), lambda i, j: (0, 0)),
        ],
        out_specs=[
            pl.BlockSpec((tm, tn), lambda i, j: (i, j)),
            pl.BlockSpec((tm, ns), lambda i, j: (i, 0)),
            pl.BlockSpec((tm, d), lambda i, j: (i, 0)),
        ],
        out_shape=[jax.ShapeDtypeStruct((t, n), BF16), jax.ShapeDtypeStruct((t, ns), F32),
                   jax.ShapeDtypeStruct((t, d), BF16)],
        scratch_shapes=[pltpu.VMEM((tm, d), BF16)],
        compiler_params=_params(("parallel", "arbitrary")),
        name="rms_matmul",
    )(x, g, w, w_side)


def _qkv_kernel(h_ref, w_ref, o0_ref, o1_ref, o2_ref, acc_ref):
    seq = h_ref.shape[0]
    slabs = GROUP_WIDTH // LANES
    n_blocks = seq // ATTN_BLOCK
    for gi, o_ref in enumerate((o0_ref, o1_ref, o2_ref)):
        dil = ATTN_GROUPS[gi][1]
        nb = n_blocks // dil
        acc = jnp.dot(h_ref[...], w_ref[:, gi * GROUP_WIDTH:(gi + 1) * GROUP_WIDTH], preferred_element_type=F32)
        if dil == 1:
            for n in range(n_blocks):
                o_ref[0, 0, n] = acc[n * ATTN_BLOCK:(n + 1) * ATTN_BLOCK, :].astype(o_ref.dtype)
            continue
        for s in range(slabs):
            acc_ref[s] = acc[:, s * LANES:(s + 1) * LANES]
        for r in range(dil):
            for n in range(nb):
                for s in range(slabs):
                    rows = pl.ds(r + n * ATTN_BLOCK * dil, ATTN_BLOCK, stride=dil)
                    o_ref[0, 0, r * nb + n, :, s * LANES:(s + 1) * LANES] = acc_ref[s, rows, :].astype(o_ref.dtype)


def qkv_project(h, w_qkv, batch, seq):
    t, d = h.shape
    n_groups = len(ATTN_GROUPS)
    tn = n_groups * GROUP_WIDTH
    out_shapes, out_specs = [], []
    for _ in ATTN_GROUPS:
        shape = (3, batch, seq // ATTN_BLOCK, ATTN_BLOCK, GROUP_WIDTH)
        out_shapes.append(jax.ShapeDtypeStruct(shape, BF16))
        out_specs.append(pl.BlockSpec((1, 1) + shape[2:], lambda w, b: (w, b, 0, 0, 0)))
    return pl.pallas_call(
        _qkv_kernel,
        grid=(3, batch),
        in_specs=[pl.BlockSpec((seq, d), lambda w, b: (b, 0)), pl.BlockSpec((d, tn), lambda w, b: (0, w))],
        out_specs=out_specs,
        out_shape=out_shapes,
        scratch_shapes=[pltpu.VMEM((GROUP_WIDTH // LANES, seq, LANES), F32)],
        compiler_params=_params(("parallel", "parallel")),
        name="qkv_project",
    )(h, w_qkv)


def _attn_kernel(q0_ref, q1_ref, q2_ref, bias_ref, o_ref, out_ref, lse_ref):
    heads = LANES // HEAD_DIM
    scale = jnp.asarray(HEAD_DIM ** -0.5, BF16)
    n_blocks = q0_ref.shape[2]
    qk = (((2,), (2,)), ((0,), (0,)))
    pv = (((2,), (1,)), ((0,), (0,)))
    blk = lax.broadcasted_iota(jnp.int32, (n_blocks, 1, 1), 0)
    lane = lax.broadcasted_iota(jnp.int32, (1, 1, LANES), 2)

    def shifted(x):
        return jnp.concatenate([x[n_blocks - 1:], x[:n_blocks - 1]], axis=0)

    for g, ref in enumerate((q0_ref, q1_ref, q2_ref)):
        dil = ATTN_GROUPS[g][1]
        nb = n_blocks // dil
        q2h = ref[0, 0] * scale
        keys, vals = ref[1, 0], ref[2, 0]
        if nb > 1:
            keys = jnp.concatenate([shifted(keys), keys], axis=1)
            vals = jnp.concatenate([shifted(vals), vals], axis=1)
        n_keys = keys.shape[1]
        ones = jnp.ones((n_keys, LANES), BF16)
        pvs, dens, maxs = [], [], []
        for h in range(heads):
            in_head = (lane >= h * HEAD_DIM) & (lane < (h + 1) * HEAD_DIM)
            q = jnp.where(in_head, q2h, jnp.zeros_like(q2h))
            s = lax.dot_general(q, keys, qk, preferred_element_type=F32)
            s = s + bias_ref[g, h, :, 2 * ATTN_BLOCK - n_keys:][None]
            if nb > 1:
                key_is_prev = lax.broadcasted_iota(jnp.int32, (1, 1, n_keys), 2) < ATTN_BLOCK
                s = jnp.where((blk % nb == 0) & key_is_prev, NEG_BIG, s)
            m = jnp.max(s, axis=-1, keepdims=True)
            p = jnp.exp(s - m).astype(BF16)
            pvs.append(lax.dot_general(p, vals, pv, preferred_element_type=F32))
            dens.append(jnp.dot(p.reshape(n_blocks * ATTN_BLOCK, n_keys), ones,
                                preferred_element_type=F32).reshape(n_blocks, ATTN_BLOCK, LANES))
            maxs.append(m)
        first = lane < HEAD_DIM
        den = jnp.where(first, dens[0], dens[1])
        o2 = jnp.where(first, pvs[0], pvs[1]) / den
        l2 = jnp.where(first, maxs[0], maxs[1]) + jnp.log(den)
        for r in range(dil):
            for n in range(nb):
                start = r + n * ATTN_BLOCK * dil
                rows = pl.ds(start, ATTN_BLOCK) if dil == 1 else pl.ds(start, ATTN_BLOCK, stride=dil)
                out_ref[g, rows, :] = o2[r * nb + n]
                lse_ref[g, rows, :] = l2[r * nb + n]

    l0, l1, l2 = lse_ref[0], lse_ref[1], lse_ref[2]
    lm = jnp.maximum(jnp.maximum(l0, l1), l2)
    e0, e1, e2 = jnp.exp(l0 - lm), jnp.exp(l1 - lm), jnp.exp(l2 - lm)
    mixed = (e0 * out_ref[0] + e1 * out_ref[1] + e2 * out_ref[2]) / (e0 + e1 + e2)
    o_ref[...] = mixed.astype(o_ref.dtype)


def dilated_attention(qkv_groups, bias, batch, seq):
    heads = LANES // HEAD_DIM
    in_specs = [pl.BlockSpec((3, 1) + a.shape[2:4] + (LANES,), lambda b, hp: (0, b, 0, 0, hp)) for a in qkv_groups]
    in_specs.append(pl.BlockSpec((len(ATTN_GROUPS), heads, ATTN_BLOCK, 2 * ATTN_BLOCK), lambda b, hp: (0, hp, 0, 0)))
    return pl.pallas_call(
        _attn_kernel,
        grid=(batch, GROUP_WIDTH // LANES),
        in_specs=in_specs,
        out_specs=pl.BlockSpec((seq, LANES), lambda b, hp: (b, hp)),
        out_shape=jax.ShapeDtypeStruct((batch * seq, GROUP_WIDTH), BF16),
        scratch_shapes=[pltpu.VMEM((len(ATTN_GROUPS), seq, LANES), F32),
                        pltpu.VMEM((len(ATTN_GROUPS), seq, LANES), F32)],
        compiler_params=_params(("parallel", "parallel")),
        name="dilated_attn",
    )(*qkv_groups, bias)


def attention_bias(rel_bias, gi):
    window, dil = ATTN_GROUPS[gi]
    w_sub = window // dil
    q_idx = np.arange(ATTN_BLOCK)[:, None]
    k_idx = np.arange(2 * ATTN_BLOCK)[None, :]
    delta = q_idx + ATTN_BLOCK - k_idx
    in_band = (delta >= 0) & (delta <= w_sub)
    dist = np.clip(delta, 0, w_sub) * dil
    max_exact = NUM_BUCKETS // 2
    nf = np.maximum(dist, max_exact).astype(np.float32)
    large = max_exact + (np.log(nf / max_exact) / math.log(MAX_DISTANCE / max_exact)
                         * (NUM_BUCKETS - max_exact)).astype(np.int32)
    large = np.minimum(large, NUM_BUCKETS - 1)
    bucket = np.where(dist < max_exact, dist, large)
    table = rel_bias[:, gi * HEADS_PER_GROUP:(gi + 1) * HEADS_PER_GROUP].astype(F32)
    bias = jnp.transpose(table[bucket], (2, 0, 1))
    return jnp.where(in_band[None], bias, NEG_BIG)


def _silu(v):
    return v * (1.0 / (1.0 + jnp.exp(-v)))


def _conv_silu(ext_ref, u, w_ref, b_ref, first):
    rows = u.shape[0]

    @pl.when(first)
    def _():
        ext_ref[0:8, :] = jnp.zeros((8, u.shape[1]), F32)

    ext_ref[8:8 + rows, :] = u
    acc = u * w_ref[CONV_WIDTH - 1:CONV_WIDTH, :] + b_ref[...]
    for k in range(1, CONV_WIDTH):
        acc = acc + ext_ref[8 - k:8 - k + rows, :] * w_ref[CONV_WIDTH - 1 - k:CONV_WIDTH - k, :]
    ext_ref[0:8, :] = u[rows - 8:, :]
    return _silu(acc)


def _ssd_kernel(x_ref, bc_ref, z_ref, dt_ref, cwx_ref, cbx_ref, cwbc_ref, cbbc_ref, dtb_ref, a_ref,
                dskip_ref, nw_ref, expand_ref, o_ref, extx_ref, extbc_ref, state_ref):
    first = pl.program_id(1) == 0
    gw = SSM_HEADS_PER_GROUP * SSM_HEAD_DIM

    @pl.when(first)
    def _():
        state_ref[...] = jnp.zeros(state_ref.shape, F32)

    xs_all = _conv_silu(extx_ref, x_ref[...].astype(F32), cwx_ref, cbx_ref, first)
    bc_all = _conv_silu(extbc_ref, bc_ref[...].astype(F32), cwbc_ref, cbbc_ref, first)

    row = lax.broadcasted_iota(jnp.int32, (CHUNK, CHUNK), 0)
    colm = lax.broadcasted_iota(jnp.int32, (CHUNK, CHUNK), 1)
    tril = row >= colm
    tril_f = tril.astype(F32)
    eye = (row == colm).astype(F32)
    hi = lax.Precision.HIGHEST
    nt = (((1,), (1,)), ((), ()))
    tn = (((0,), (0,)), ((), ()))

    for g in range(N_SSM_GROUPS):
        lanes = slice(g * LANES, (g + 1) * LANES)
        ch = slice(g * gw, (g + 1) * gw)
        v = dt_ref[:, lanes] + dtb_ref[:, lanes]
        dt = jnp.maximum(v, 0.0) + jnp.log1p(jnp.exp(-jnp.abs(v)))
        a_d = dt * a_ref[:, lanes]
        acs = jnp.dot(tril_f, a_d, precision=hi, preferred_element_type=F32)
        acs_t = lax.dot_general(eye, acs, nt, precision=hi, preferred_element_type=F32)
        expand = expand_ref[...]
        dt_e = jnp.dot(dt, expand, precision=hi, preferred_element_type=F32)
        acs_e = jnp.dot(acs, expand, precision=hi, preferred_element_type=F32)
        last_e = acs_e[CHUNK - 1:CHUNK, :]

        xs = xs_all[:, ch]
        bm = bc_all[:, lanes].astype(BF16)
        cm = bc_all[:, N_SSM_GROUPS * D_STATE + g * D_STATE:N_SSM_GROUPS * D_STATE + (g + 1) * D_STATE].astype(BF16)
        x_d = xs * dt_e
        cb = lax.dot_general(cm, bm, nt, preferred_element_type=F32)

        prev = state_ref[g]
        y = jnp.dot(cm, prev.astype(BF16), preferred_element_type=F32) * jnp.exp(acs_e)
        x_d16 = x_d.astype(BF16)
        parts = []
        for j in range(SSM_HEADS_PER_GROUP):
            seg = acs[:, j:j + 1] - acs_t[j:j + 1, :]
            m_h = (cb * jnp.exp(jnp.where(tril, seg, NEG_BIG))).astype(BF16)
            parts.append(jnp.dot(m_h, x_d16[:, j * SSM_HEAD_DIM:(j + 1) * SSM_HEAD_DIM],
                                 preferred_element_type=F32))
        y = y + jnp.concatenate(parts, axis=1)

        xw = (x_d * jnp.exp(last_e - acs_e)).astype(BF16)
        state_ref[g] = prev * jnp.exp(last_e) + lax.dot_general(bm, xw, tn, preferred_element_type=F32)

        y = y + dskip_ref[:, ch] * xs
        y = y * _silu(z_ref[:, ch].astype(F32))
        y = y * lax.rsqrt(jnp.mean(y * y, axis=-1, keepdims=True) + EPS)
        o_ref[:, ch] = (y * nw_ref[:, ch]).astype(o_ref.dtype)


def ssd_mixer(rest, dt_raw, conv_w, conv_b, dt_bias_p, a_p, d_skip_e, norm_w, batch, seq):
    t = rest.shape[0]
    d_inner = N_SSM_GROUPS * SSM_HEADS_PER_GROUP * SSM_HEAD_DIM
    bc_w = 2 * N_SSM_GROUPS * D_STATE
    nc = seq // CHUNK
    gw = SSM_HEADS_PER_GROUP * SSM_HEAD_DIM
    expand = (np.arange(LANES)[:, None] == (np.arange(gw)[None, :] // SSM_HEAD_DIM)).astype(np.float32)
    rowmap = lambda b, c: (b * nc + c, 0)
    const = lambda b, c: (0, 0)
    return pl.pallas_call(
        _ssd_kernel,
        grid=(batch, nc),
        in_specs=[
            pl.BlockSpec((CHUNK, d_inner), lambda b, c: (b * nc + c, 2)),
            pl.BlockSpec((CHUNK, bc_w), lambda b, c: (b * nc + c, 6)),
            pl.BlockSpec((CHUNK, d_inner), rowmap),
            pl.BlockSpec((CHUNK, N_SSM_GROUPS * LANES), rowmap),
            pl.BlockSpec((CONV_WIDTH, d_inner), const),
            pl.BlockSpec((1, d_inner), const),
            pl.BlockSpec((CONV_WIDTH, bc_w), const),
            pl.BlockSpec((1, bc_w), const),
            pl.BlockSpec((1, N_SSM_GROUPS * LANES), const),
            pl.BlockSpec((1, N_SSM_GROUPS * LANES), const),
            pl.BlockSpec((1, d_inner), const),
            pl.BlockSpec((1, d_inner), const),
            pl.BlockSpec((LANES, gw), const),
        ],
        out_specs=pl.BlockSpec((CHUNK, d_inner), rowmap),
        out_shape=jax.ShapeDtypeStruct((t, d_inner), BF16),
        scratch_shapes=[
            pltpu.VMEM((CHUNK + 8, d_inner), F32),
            pltpu.VMEM((CHUNK + 8, bc_w), F32),
            pltpu.VMEM((N_SSM_GROUPS, D_STATE, gw), F32),
        ],
        compiler_params=_params(("parallel", "arbitrary")),
        name="ssd_mixer",
    )(rest, rest, rest, dt_raw, conv_w[:, :d_inner], conv_b[:, :d_inner], conv_w[:, d_inner:],
      conv_b[:, d_inner:], dt_bias_p, a_p, d_skip_e, norm_w, jnp.asarray(expand))


def _merge_kernel(at_ref, ys_ref, gl_ref, x_ref, wa_ref, ws_ref,
                  wo_ref, gb_ref, nf_ref, rw_ref, rb_ref, x1_ref, h2_ref, ti_ref, tw_ref):
    hi = lax.Precision.HIGHEST
    d = x_ref.shape[1]
    sub = x_ref.shape[0] // MERGE_SUBTILES
    for part in range(MERGE_SUBTILES):
        rows = slice(part * sub, (part + 1) * sub)
        y_attn = jnp.dot(at_ref[rows, :], wa_ref[...], preferred_element_type=F32)
        y_ssm = jnp.dot(ys_ref[rows, :], ws_ref[...], preferred_element_type=F32)
        gv = gl_ref[rows, :].astype(F32) + gb_ref[...]
        gates = 1.0 / (1.0 + jnp.exp(-gv))
        merged = gates[:, :d] * y_attn + gates[:, d:] * y_ssm
        x1 = x_ref[rows, :] + jnp.dot(merged.astype(BF16), wo_ref[...], preferred_element_type=F32)
        x1_ref[rows, :] = x1
        h2 = x1 * lax.rsqrt(jnp.mean(x1 * x1, axis=-1, keepdims=True) + EPS) * nf_ref[...]
        _store_slabs(h2_ref, part * sub, sub, _pack_bf16_pairs(h2))
        logits = jnp.dot(h2, rw_ref[...], precision=hi, preferred_element_type=F32) + rb_ref[...]
        lane = lax.broadcasted_iota(jnp.int32, logits.shape, 1)
        top_i = jnp.zeros(logits.shape, jnp.int32)
        top_v = jnp.full(logits.shape, NEG_BIG, F32)
        work = logits
        for k in range(TOP_K):
            m = jnp.max(work, axis=-1, keepdims=True)
            idx = jnp.min(jnp.where(work == m, lane, LANES), axis=-1, keepdims=True)
            top_i = jnp.where(lane == k, idx, top_i)
            top_v = jnp.where(lane == k, m, top_v)
            work = jnp.where(lane == idx, NEG_BIG * 2.0, work)
        ev = jnp.exp(top_v - jnp.max(top_v, axis=-1, keepdims=True))
        ti_ref[rows, :] = top_i
        tw_ref[rows, :] = ev / jnp.sum(ev, axis=-1, keepdims=True)


def merge_project(attn, y_ssm, rest, x, wa, ws, wo, gate_bias, norm_ffn, router_w_p, router_b_p):
    t, d = x.shape
    slab = d // 2 // LANES
    tm = 256 * MERGE_SUBTILES
    d_inner = y_ssm.shape[1]
    rowmap = lambda i: (i, 0)
    const = lambda i: (0, 0)
    full = lambda a: pl.BlockSpec(a.shape, const)
    args = [attn, y_ssm, rest, x, wa, ws, wo, gate_bias, norm_ffn, router_w_p, router_b_p]
    in_specs = (
        [pl.BlockSpec((tm, GROUP_WIDTH), rowmap),
         pl.BlockSpec((tm, d_inner), rowmap),
         pl.BlockSpec((tm, 2 * d), lambda i: (i, 1)),
         pl.BlockSpec((tm, d), rowmap)]
        + [full(a) for a in args[4:]]
    )
    return pl.pallas_call(
        _merge_kernel,
        grid=(t // tm,),
        in_specs=in_specs,
        out_specs=[pl.BlockSpec((tm, d), rowmap), pl.BlockSpec((tm * slab, LANES), rowmap),
                   pl.BlockSpec((tm, LANES), rowmap), pl.BlockSpec((tm, LANES), rowmap)],
        out_shape=[jax.ShapeDtypeStruct((t, d), F32), jax.ShapeDtypeStruct((t * slab, LANES), jnp.uint32),
                   jax.ShapeDtypeStruct((t, LANES), jnp.int32), jax.ShapeDtypeStruct((t, LANES), F32)],
        compiler_params=_params(("parallel",)),
        name="merge_project",
    )(*args)


def _split_w1_kernel(w_ref, g_ref, l_ref, t_ref):
    de = g_ref.shape[1]
    for s in range(t_ref.shape[0]):
        cols = slice(s * LANES, (s + 1) * LANES)
        t_ref[s] = w_ref[0, cols, :].T
        g_ref[0, :, cols] = t_ref[s, pl.ds(0, de, stride=2), :].astype(g_ref.dtype)
        l_ref[0, :, cols] = t_ref[s, pl.ds(1, de, stride=2), :].astype(l_ref.dtype)


def split_w1(w1):
    e, d, de2 = w1.shape
    de = de2 // 2
    tk = 512
    out = jax.ShapeDtypeStruct((e, de, d), BF16)
    return pl.pallas_call(
        _split_w1_kernel,
        grid=(e, d // tk),
        in_specs=[pl.BlockSpec((1, tk, de2), lambda i, k: (i, k, 0))],
        out_specs=[pl.BlockSpec((1, de, tk), lambda i, k: (i, 0, k))] * 2,
        out_shape=[out, out],
        scratch_shapes=[pltpu.VMEM((tk // LANES, de2, LANES), F32)],
        compiler_params=_params(("parallel", "parallel")),
        name="split_w1",
    )(w1)


def _pack_bf16_pairs(v):
    w = v.shape[1] // 2
    lo = lax.bitcast_convert_type(v[:, :w].astype(BF16).astype(F32), jnp.uint32) >> 16
    hi = lax.bitcast_convert_type(v[:, w:].astype(BF16).astype(F32), jnp.uint32) & jnp.uint32(0xFFFF0000)
    return lo | hi


def _unpack_bf16_pairs(p):
    lo = lax.bitcast_convert_type(p << 16, F32)
    hi = lax.bitcast_convert_type(p & jnp.uint32(0xFFFF0000), F32)
    return lo, hi


def _store_slabs(ref, row0, rows, packed):
    slab = packed.shape[1] // LANES
    for s in range(slab):
        ref[pl.ds(row0 * slab + s, rows, stride=slab), :] = packed[:, s * LANES:(s + 1) * LANES]


def _load_slabs(ref, rows, slab):
    return jnp.concatenate([ref[pl.ds(s, rows, stride=slab), :] for s in range(slab)], axis=1)


def _expert_kernel(be_ref, nu_ref, gc_ref, gn_ref, sp_ref, sc_ref, h2p_ref, w1g_ref, w1l_ref, w2_ref, b1g_ref, b1l_ref,
                   b2_ref, y4p_ref, xbuf, obuf, gsem, ssem, *, n_tokens, t_pad):
    i = pl.program_id(0)
    n_used = nu_ref[0]
    slot = lax.rem(i, 2)
    other = 1 - slot
    bm = EXPERT_ROWS
    slab = xbuf.shape[1] // bm

    def gather_copy(src_ref, j, buf):
        src = pl.multiple_of(src_ref[0, 0, j], slab)
        return pltpu.make_async_copy(h2p_ref.at[pl.ds(src, slab)],
                                     xbuf.at[buf, pl.ds(j * slab, slab)], gsem.at[buf])

    def scatter_copy(dst_ref, j, buf):
        dst = pl.multiple_of(dst_ref[0, 0, j], slab)
        return pltpu.make_async_copy(obuf.at[buf, pl.ds(j * slab, slab)],
                                     y4p_ref.at[pl.ds(dst, slab)], ssem.at[buf])

    def wait_block(kind, buf):
        if kind == "gather":
            pltpu.make_async_copy(h2p_ref.at[pl.ds(0, bm * slab)], xbuf.at[buf], gsem.at[buf]).wait()
        else:
            pltpu.make_async_copy(obuf.at[buf], y4p_ref.at[pl.ds(0, bm * slab)], ssem.at[buf]).wait()

    @pl.when(i == 0)
    def _():
        obuf[...] = jnp.zeros(obuf.shape, obuf.dtype)
        fills = [pltpu.make_async_copy(obuf.at[1], y4p_ref.at[pl.ds((k * t_pad + n_tokens) * slab + c * bm * slab,
                                                                     bm * slab)], ssem.at[1])
                 for k in range(TOP_K) for c in range((t_pad - n_tokens) // bm)]
        for fill in fills:
            fill.start()
        for fill in fills:
            fill.wait()
        for j in range(bm):
            gather_copy(gc_ref, j, 0).start()
        spare_row = t_pad + t_pad - 2 * bm
        pltpu.make_async_copy(obuf.at[0], y4p_ref.at[pl.ds(spare_row * slab, bm * slab)], ssem.at[0]).start()

    @pl.when(i < n_used)
    def _():
        nt = (((1,), (1,)), ((), ()))
        wait_block("gather", slot)
        lo, hi = _unpack_bf16_pairs(_load_slabs(xbuf.at[slot], bm, slab))
        xb = jnp.concatenate([lo, hi], axis=1).astype(BF16)
        for j in range(bm):
            gather_copy(gn_ref, j, other).start()
        for j in range(bm):
            scatter_copy(sp_ref, j, other).start()
        glu = lax.dot_general(xb, w1g_ref[0], nt, preferred_element_type=F32) + b1g_ref[0]
        lin = lax.dot_general(xb, w1l_ref[0], nt, preferred_element_type=F32) + b1l_ref[0]
        glu = jnp.minimum(glu, SWIGLU_LIMIT)
        lin = jnp.clip(lin, -SWIGLU_LIMIT, SWIGLU_LIMIT)
        act = glu * (1.0 / (1.0 + jnp.exp(-SWIGLU_ALPHA * glu))) * (lin + 1.0)
        y = jnp.dot(act.astype(BF16), w2_ref[0], preferred_element_type=F32) + b2_ref[0]
        wait_block("scatter", slot)
        _store_slabs(obuf.at[slot], 0, bm, _pack_bf16_pairs(y))

    @pl.when(i == n_used - 1)
    def _():
        for j in range(bm):
            scatter_copy(sc_ref, j, slot).start()
        wait_block("scatter", other)
        wait_block("scatter", slot)
        wait_block("gather", other)


def expert_ffn(h2p, ids, block_e, n_used, w1g, w1l, w2, b1g, b1l, b2, n_tokens):
    n_blocks, _, bm = ids.shape
    slab = h2p.shape[0] // n_tokens
    de, d = w1g.shape[1], w1g.shape[2]
    t_pad = padded_tokens(n_tokens)
    src = jnp.minimum(ids >> 2, n_tokens - 1) * slab
    dst = ((ids & 3) * t_pad + (ids >> 2)) * slab
    first = ((t_pad - 2 * bm + jnp.arange(bm, dtype=jnp.int32)) * slab).reshape(1, 1, bm)
    dst = jnp.concatenate([first, dst], axis=0)
    wmap = lambda i, be, nu: (be[i], 0, 0)
    smem_ids = lambda imap: pl.BlockSpec((1, 1, bm), imap, memory_space=pltpu.SMEM)
    grid_spec = pltpu.PrefetchScalarGridSpec(
        num_scalar_prefetch=2,
        grid=(n_blocks,),
        in_specs=[
            smem_ids(lambda i, be, nu: (i, 0, 0)),
            smem_ids(lambda i, be, nu: (jnp.minimum(i + 1, n_blocks - 1), 0, 0)),
            smem_ids(lambda i, be, nu: (i, 0, 0)),
            smem_ids(lambda i, be, nu: (i + 1, 0, 0)),
            pl.BlockSpec(memory_space=pl.ANY),
            pl.BlockSpec((1, de, d), wmap),
            pl.BlockSpec((1, de, d), wmap),
            pl.BlockSpec((1, de, d), wmap),
            pl.BlockSpec((1, 1, de), wmap),
            pl.BlockSpec((1, 1, de), wmap),
            pl.BlockSpec((1, 1, d), wmap),
        ],
        out_specs=pl.BlockSpec(memory_space=pl.ANY),
        scratch_shapes=[
            pltpu.VMEM((2, bm * slab, LANES), jnp.uint32),
            pltpu.VMEM((2, bm * slab, LANES), jnp.uint32),
            pltpu.SemaphoreType.DMA((2,)),
            pltpu.SemaphoreType.DMA((2,)),
        ],
    )
    return pl.pallas_call(
        functools.partial(_expert_kernel, n_tokens=n_tokens, t_pad=t_pad),
        grid_spec=grid_spec,
        out_shape=jax.ShapeDtypeStruct((TOP_K * t_pad * slab, LANES), jnp.uint32),
        compiler_params=_params(("arbitrary",)),
        name="expert_ffn",
    )(block_e, n_used, src, src, dst, dst, h2p, w1g, w1l, w2, b1g, b1l, b2)


def _combine_kernel(x1_ref, y0_ref, y1_ref, y2_ref, y3_ref, tw_ref, nw_ref, o_ref, *, normalize):
    tm, d = x1_ref.shape
    slab = y0_ref.shape[0] // tm
    lo_sum = jnp.zeros((tm, d // 2), F32)
    hi_sum = jnp.zeros((tm, d // 2), F32)
    for k, y_ref in enumerate((y0_ref, y1_ref, y2_ref, y3_ref)):
        lo, hi = _unpack_bf16_pairs(_load_slabs(y_ref, tm, slab))
        w = tw_ref[:, k:k + 1]
        lo_sum = lo_sum + w * lo
        hi_sum = hi_sum + w * hi
    acc = x1_ref[...] + jnp.concatenate([lo_sum, hi_sum], axis=1)
    if normalize:
        acc = acc * lax.rsqrt(jnp.mean(acc * acc, axis=-1, keepdims=True) + EPS) * nw_ref[...]
    o_ref[...] = acc


def combine_norm(x1, y4p, top_w, norm_w, normalize):
    t, d = x1.shape
    tm = 256
    t_pad = padded_tokens(t)
    slab = y4p.shape[0] // (TOP_K * t_pad)
    y_specs = [pl.BlockSpec((tm * slab, LANES), functools.partial(lambda i, k: (k * (t_pad // tm) + i, 0), k=k))
               for k in range(TOP_K)]
    return pl.pallas_call(
        functools.partial(_combine_kernel, normalize=normalize),
        grid=(t // tm,),
        in_specs=[pl.BlockSpec((tm, d), lambda i: (i, 0))] + y_specs
        + [pl.BlockSpec((tm, LANES), lambda i: (i, 0)), pl.BlockSpec((1, d), lambda i: (0, 0))],
        out_specs=pl.BlockSpec((tm, d), lambda i: (i, 0)),
        out_shape=jax.ShapeDtypeStruct((t, d), F32),
        compiler_params=_params(("parallel",)),
        name="combine_norm",
    )(x1, y4p, y4p, y4p, y4p, top_w, norm_w)


def routing_layout(top_i, n_tokens):
    n_assign = n_tokens * TOP_K
    bm = EXPERT_ROWS
    n_pad = N_EXPERTS * bm
    n_blocks = (n_assign + n_pad) // bm
    flat_e = top_i[:, :TOP_K].reshape(n_assign)
    counts = jnp.sum(flat_e[:, None] == jnp.arange(N_EXPERTS, dtype=jnp.int32)[None, :], axis=0, dtype=jnp.int32)
    padded = (counts + bm - 1) // bm * bm
    pend = jnp.cumsum(padded)
    n_used = (pend[-1] // bm).astype(jnp.int32)
    block_e = jnp.minimum(jnp.searchsorted(pend, jnp.arange(n_blocks, dtype=jnp.int32) * bm, side='right'),
                          N_EXPERTS - 1).astype(jnp.int32)
    spare = jnp.arange(n_pad, dtype=jnp.int32)
    spare_key = jnp.where(spare % bm < (padded - counts)[spare // bm], spare // bm, N_EXPERTS)
    order = jnp.argsort(jnp.concatenate([flat_e, spare_key]), stable=True).astype(jnp.int32)
    return order.reshape(n_blocks, 1, bm), block_e, n_used.reshape(1)


def kernel(x, w_in, rel_bias, w_branch_attn, conv_w, conv_b, dt_bias, a_log, d_skip, ssm_norm_w,
           w_branch_ssm, gate_bias, w_out, norm_mix, norm_ffn, router_w, router_b, w1, b1, w2, b2,
           norm_final):
    batch, seq, d = x.shape
    t = batch * seq
    depth = w_in.shape[0]
    n_groups = len(ATTN_GROUPS)
    attn_w = n_groups * GROUP_WIDTH
    d_inner = N_SSM_GROUPS * SSM_HEADS_PER_GROUP * SSM_HEAD_DIM
    n_heads = N_SSM_GROUPS * SSM_HEADS_PER_GROUP
    bc_w = 2 * N_SSM_GROUPS * D_STATE
    xf = x.reshape(t, d)
    for l in range(depth):
        wl = w_in[l]
        o_z = 3 * attn_w
        o_xbc = o_z + d_inner
        o_dt = o_xbc + d_inner + bc_w
        o_gate = o_dt + n_heads
        w_qkv = wl[:, :o_z].astype(BF16)
        w_rest = jnp.concatenate([wl[:, o_z:o_xbc], wl[:, o_gate:], wl[:, o_xbc:o_dt]], axis=1).astype(BF16)
        lane_of_head = (np.arange(n_heads) // SSM_HEADS_PER_GROUP) * LANES + np.arange(n_heads) % SSM_HEADS_PER_GROUP
        w_dt = jnp.zeros((d, N_SSM_GROUPS * LANES), F32).at[:, lane_of_head].set(wl[:, o_dt:o_gate]).astype(BF16)
        dt_bias_p = jnp.zeros((1, N_SSM_GROUPS * LANES), F32).at[0, lane_of_head].set(dt_bias[l].astype(F32))
        a_p = jnp.zeros((1, N_SSM_GROUPS * LANES), F32).at[0, lane_of_head].set(-jnp.exp(a_log[l].astype(F32)))
        d_skip_e = jnp.repeat(d_skip[l].astype(F32), SSM_HEAD_DIM)[None, :]

        g_mix = norm_mix[l].astype(F32)[None, :]
        rest, dt_raw, h = rms_matmul(xf, g_mix, w_rest, w_dt, tn=1792)
        qkv_groups = qkv_project(h, w_qkv, batch, seq)
        bias = jnp.stack([attention_bias(rel_bias, gi) for gi in range(n_groups)])
        attn = dilated_attention(qkv_groups, bias, batch, seq)
        y_ssm = ssd_mixer(rest, dt_raw, conv_w[l].astype(F32), conv_b[l].astype(F32)[None, :], dt_bias_p, a_p,
                          d_skip_e, ssm_norm_w[l].astype(F32)[None, :], batch, seq)
        router_w_p = jnp.zeros((d, LANES), F32).at[:, :N_EXPERTS].set(router_w[l].astype(F32))
        router_b_p = jnp.full((1, LANES), NEG_BIG, F32).at[0, :N_EXPERTS].set(router_b[l].astype(F32))
        x1, h2, top_i, top_w = merge_project(
            attn, y_ssm, rest, xf, w_branch_attn[l].astype(BF16), w_branch_ssm[l].astype(BF16),
            w_out[l].astype(BF16), gate_bias[l].astype(F32)[None, :], norm_ffn[l].astype(F32)[None, :],
            router_w_p, router_b_p)

        ids, block_e, n_used = routing_layout(top_i, t)
        w1g_t, w1l_t = split_w1(w1[l].astype(F32))
        y4p = expert_ffn(h2, ids, block_e, n_used, w1g_t, w1l_t, w2[l].astype(BF16),
                         b1[l][:, None, 0::2].astype(F32), b1[l][:, None, 1::2].astype(F32),
                         b2[l][:, None, :].astype(F32), t)
        xf = combine_norm(x1, y4p, top_w, norm_final.astype(F32)[None, :], normalize=(l == depth - 1))
    return xf.reshape(batch, seq, d)
```

```python
import functools
import math

import jax
import jax.numpy as jnp
import numpy as np
from jax import lax
from jax.experimental import pallas as pl
from jax.experimental.pallas import tpu as pltpu

F32 = jnp.float32
BF16 = jnp.bfloat16

EPS = 1e-5
NEG_BIG = -1e30

HEAD_DIM = 64
ATTN_GROUPS = ((128, 1), (512, 4), (2048, 16))
HEADS_PER_GROUP = 8
GROUP_WIDTH = HEADS_PER_GROUP * HEAD_DIM
ATTN_BLOCK = 128
NUM_BUCKETS = 32
MAX_DISTANCE = 2048
SSM_HEAD_DIM = 64
N_SSM_GROUPS = 4
SSM_HEADS_PER_GROUP = 8
D_STATE = 128
CONV_WIDTH = 4
CHUNK = 128
N_EXPERTS = 32
TOP_K = 4
SWIGLU_LIMIT = 7.0
SWIGLU_ALPHA = 1.702

LANES = 128
V7X_VMEM_BYTES = 64 * 1024 * 1024
VMEM_LIMIT = 48 * 1024 * 1024

ROW_TILE = 1024
MERGE_SUBTILES = 2
EXPERT_ROWS = 256


def padded_tokens(n_tokens):
    return n_tokens + N_EXPERTS * EXPERT_ROWS // TOP_K + 2 * EXPERT_ROWS


def _params(semantics):
    return pltpu.CompilerParams(dimension_semantics=semantics, vmem_limit_bytes=VMEM_LIMIT)


def _rms_matmul_kernel(x_ref, g_ref, w_ref, ws_ref, o_ref, os_ref, hb_ref, h_ref):
    @pl.when(pl.program_id(1) == 0)
    def _():
        x = x_ref[...]
        ms = jnp.mean(x * x, axis=-1, keepdims=True)
        h_ref[...] = (x * lax.rsqrt(ms + EPS) * g_ref[...]).astype(BF16)
        hb_ref[...] = h_ref[...]
        os_ref[...] = jnp.dot(h_ref[...], ws_ref[...], preferred_element_type=F32)

    o_ref[...] = jnp.dot(h_ref[...], w_ref[...], preferred_element_type=F32).astype(o_ref.dtype)


def rms_matmul(x, g, w, w_side, tn):
    t, d = x.shape
    n = w.shape[1]
    ns = w_side.shape[1]
    tm = ROW_TILE
    return pl.pallas_call(
        _rms_matmul_kernel,
        grid=(t // tm, n // tn),
        in_specs=[
            pl.BlockSpec((tm, d), lambda i, j: (i, 0)),
            pl.BlockSpec((1, d), lambda i, j: (0, 0)),
            pl.BlockSpec((d, tn), lambda i, j: (0, j)),
            pl.BlockSpec((d, ns), lambda i, j: (0, 0)),
        ],
        out_specs=[
            pl.BlockSpec((tm, tn), lambda i, j: (i, j)),
            pl.BlockSpec((tm, ns), lambda i, j: (i, 0)),
            pl.BlockSpec((tm, d), lambda i, j: (i, 0)),
        ],
        out_shape=[jax.ShapeDtypeStruct((t, n), BF16), jax.ShapeDtypeStruct((t, ns), F32),
                   jax.ShapeDtypeStruct((t, d), BF16)],
        scratch_shapes=[pltpu.VMEM((tm, d), BF16)],
        compiler_params=_params(("parallel", "arbitrary")),
        name="rms_matmul",
    )(x, g, w, w_side)


def _qkv_kernel(h_ref, w_ref, o0_ref, o1_ref, o2_ref, acc_ref):
    seq = h_ref.shape[0]
    slabs = GROUP_WIDTH // LANES
    n_blocks = seq // ATTN_BLOCK
    for gi, o_ref in enumerate((o0_ref, o1_ref, o2_ref)):
        dil = ATTN_GROUPS[gi][1]
        nb = n_blocks // dil
        acc = jnp.dot(h_ref[...], w_ref[:, gi * GROUP_WIDTH:(gi + 1) * GROUP_WIDTH], preferred_element_type=F32)
        if dil == 1:
            for n in range(n_blocks):
                o_ref[0, 0, n] = acc[n * ATTN_BLOCK:(n + 1) * ATTN_BLOCK, :].astype(o_ref.dtype)
            continue
        for s in range(slabs):
            acc_ref[s] = acc[:, s * LANES:(s + 1) * LANES]
        for r in range(dil):
            for n in range(nb):
                for s in range(slabs):
                    rows = pl.ds(r + n * ATTN_BLOCK * dil, ATTN_BLOCK, stride=dil)
                    o_ref[0, 0, r * nb + n, :, s * LANES:(s + 1) * LANES] = acc_ref[s, rows, :].astype(o_ref.dtype)


def qkv_project(h, w_qkv, batch, seq):
    t, d = h.shape
    n_groups = len(ATTN_GROUPS)
    tn = n_groups * GROUP_WIDTH
    out_shapes, out_specs = [], []
    for _ in ATTN_GROUPS:
        shape = (3, batch, seq // ATTN_BLOCK, ATTN_BLOCK, GROUP_WIDTH)
        out_shapes.append(jax.ShapeDtypeStruct(shape, BF16))
        out_specs.append(pl.BlockSpec((1, 1) + shape[2:], lambda w, b: (w, b, 0, 0, 0)))
    return pl.pallas_call(
        _qkv_kernel,
        grid=(3, batch),
        in_specs=[pl.BlockSpec((seq, d), lambda w, b: (b, 0)), pl.BlockSpec((d, tn), lambda w, b: (0, w))],
        out_specs=out_specs,
        out_shape=out_shapes,
        scratch_shapes=[pltpu.VMEM((GROUP_WIDTH // LANES, seq, LANES), F32)],
        compiler_params=_params(("parallel", "parallel")),
        name="qkv_project",
    )(h, w_qkv)


def _attn_kernel(q0_ref, q1_ref, q2_ref, bias_ref, o_ref, out_ref, lse_ref):
    heads = LANES // HEAD_DIM
    scale = jnp.asarray(HEAD_DIM ** -0.5, BF16)
    n_blocks = q0_ref.shape[2]
    qk = (((2,), (2,)), ((0,), (0,)))
    pv = (((2,), (1,)), ((0,), (0,)))
    blk = lax.broadcasted_iota(jnp.int32, (n_blocks, 1, 1), 0)
    lane = lax.broadcasted_iota(jnp.int32, (1, 1, LANES), 2)

    def shifted(x):
        return jnp.concatenate([x[n_blocks - 1:], x[:n_blocks - 1]], axis=0)

    for g, ref in enumerate((q0_ref, q1_ref, q2_ref)):
        dil = ATTN_GROUPS[g][1]
        nb = n_blocks // dil
        q2h = ref[0, 0] * scale
        keys, vals = ref[1, 0], ref[2, 0]
        if nb > 1:
            keys = jnp.concatenate([shifted(keys), keys], axis=1)
            vals = jnp.concatenate([shifted(vals), vals], axis=1)
        n_keys = keys.shape[1]
        ones = jnp.ones((n_keys, LANES), BF16)
        pvs, dens, maxs = [], [], []
        for h in range(heads):
            in_head = (lane >= h * HEAD_DIM) & (lane < (h + 1) * HEAD_DIM)
            q = jnp.where(in_head, q2h, jnp.zeros_like(q2h))
            s = lax.dot_general(q, keys, qk, preferred_element_type=F32)
            s = s + bias_ref[g, h, :, 2 * ATTN_BLOCK - n_keys:][None]
            if nb > 1:
                key_is_prev = lax.broadcasted_iota(jnp.int32, (1, 1, n_keys), 2) < ATTN_BLOCK
                s = jnp.where((blk % nb == 0) & key_is_prev, NEG_BIG, s)
            m = jnp.max(s, axis=-1, keepdims=True)
            p = jnp.exp(s - m).astype(BF16)
            pvs.append(lax.dot_general(p, vals, pv, preferred_element_type=F32))
            dens.append(jnp.dot(p.reshape(n_blocks * ATTN_BLOCK, n_keys), ones,
                                preferred_element_type=F32).reshape(n_blocks, ATTN_BLOCK, LANES))
            maxs.append(m)
        first = lane < HEAD_DIM
        den = jnp.where(first, dens[0], dens[1])
        o2 = jnp.where(first, pvs[0], pvs[1]) / den
        l2 = jnp.where(first, maxs[0], maxs[1]) + jnp.log(den)
        for r in range(dil):
            for n in range(nb):
                start = r + n * ATTN_BLOCK * dil
                rows = pl.ds(start, ATTN_BLOCK) if dil == 1 else pl.ds(start, ATTN_BLOCK, stride=dil)
                out_ref[g, rows, :] = o2[r * nb + n]
                lse_ref[g, rows, :] = l2[r * nb + n]

    l0, l1, l2 = lse_ref[0], lse_ref[1], lse_ref[2]
    lm = jnp.maximum(jnp.maximum(l0, l1), l2)
    e0, e1, e2 = jnp.exp(l0 - lm), jnp.exp(l1 - lm), jnp.exp(l2 - lm)
    mixed = (e0 * out_ref[0] + e1 * out_ref[1] + e2 * out_ref[2]) / (e0 + e1 + e2)
    o_ref[...] = mixed.astype(o_ref.dtype)


def dilated_attention(qkv_groups, bias, batch, seq):
    heads = LANES // HEAD_DIM
    in_specs = [pl.BlockSpec((3, 1) + a.shape[2:4] + (LANES,), lambda b, hp: (0, b, 0, 0, hp)) for a in qkv_groups]
    in_specs.append(pl.BlockSpec((len(ATTN_GROUPS), heads, ATTN_BLOCK, 2 * ATTN_BLOCK), lambda b, hp: (0, hp, 0, 0)))
    return pl.pallas_call(
        _attn_kernel,
        grid=(batch, GROUP_WIDTH // LANES),
        in_specs=in_specs,
        out_specs=pl.BlockSpec((seq, LANES), lambda b, hp: (b, hp)),
        out_shape=jax.ShapeDtypeStruct((batch * seq, GROUP_WIDTH), BF16),
        scratch_shapes=[pltpu.VMEM((len(ATTN_GROUPS), seq, LANES), F32),
                        pltpu.VMEM((len(ATTN_GROUPS), seq, LANES), F32)],
        compiler_params=_params(("parallel", "parallel")),
        name="dilated_attn",
    )(*qkv_groups, bias)


def attention_bias(rel_bias, gi):
    window, dil = ATTN_GROUPS[gi]
    w_sub = window // dil
    q_idx = np.arange(ATTN_BLOCK)[:, None]
    k_idx = np.arange(2 * ATTN_BLOCK)[None, :]
    delta = q_idx + ATTN_BLOCK - k_idx
    in_band = (delta >= 0) & (delta <= w_sub)
    dist = np.clip(delta, 0, w_sub) * dil
    max_exact = NUM_BUCKETS // 2
    nf = np.maximum(dist, max_exact).astype(np.float32)
    large = max_exact + (np.log(nf / max_exact) / math.log(MAX_DISTANCE / max_exact)
                         * (NUM_BUCKETS - max_exact)).astype(np.int32)
    large = np.minimum(large, NUM_BUCKETS - 1)
    bucket = np.where(dist < max_exact, dist, large)
    table = rel_bias[:, gi * HEADS_PER_GROUP:(gi + 1) * HEADS_PER_GROUP].astype(F32)
    onehot = (bucket.reshape(-1, 1) == np.arange(NUM_BUCKETS)[None, :]).astype(np.float32)
    bias = jnp.einsum('bh,nb->hn', table, jnp.asarray(onehot), precision=lax.Precision.HIGHEST)
    bias = bias.reshape(HEADS_PER_GROUP, ATTN_BLOCK, 2 * ATTN_BLOCK)
    return jnp.where(in_band[None], bias, NEG_BIG)


def _silu(v):
    return v * (1.0 / (1.0 + jnp.exp(-v)))


def _conv_silu(ext_ref, u, w_ref, b_ref, first):
    rows = u.shape[0]

    @pl.when(first)
    def _():
        ext_ref[0:8, :] = jnp.zeros((8, u.shape[1]), F32)

    ext_ref[8:8 + rows, :] = u
    acc = u * w_ref[CONV_WIDTH - 1:CONV_WIDTH, :] + b_ref[...]
    for k in range(1, CONV_WIDTH):
        acc = acc + ext_ref[8 - k:8 - k + rows, :] * w_ref[CONV_WIDTH - 1 - k:CONV_WIDTH - k, :]
    ext_ref[0:8, :] = u[rows - 8:, :]
    return _silu(acc)


def _ssd_kernel(x_ref, bc_ref, z_ref, dt_ref, cwx_ref, cbx_ref, cwbc_ref, cbbc_ref, dtb_ref, a_ref,
                dskip_ref, nw_ref, expand_ref, o_ref, extx_ref, extbc_ref, state_ref):
    first = pl.program_id(1) == 0
    gw = SSM_HEADS_PER_GROUP * SSM_HEAD_DIM

    @pl.when(first)
    def _():
        state_ref[...] = jnp.zeros(state_ref.shape, F32)

    xs_all = _conv_silu(extx_ref, x_ref[...].astype(F32), cwx_ref, cbx_ref, first)
    bc_all = _conv_silu(extbc_ref, bc_ref[...].astype(F32), cwbc_ref, cbbc_ref, first)

    row = lax.broadcasted_iota(jnp.int32, (CHUNK, CHUNK), 0)
    colm = lax.broadcasted_iota(jnp.int32, (CHUNK, CHUNK), 1)
    tril = row >= colm
    tril_f = tril.astype(F32)
    eye = (row == colm).astype(F32)
    hi = lax.Precision.HIGHEST
    nt = (((1,), (1,)), ((), ()))
    tn = (((0,), (0,)), ((), ()))

    for g in range(N_SSM_GROUPS):
        lanes = slice(g * LANES, (g + 1) * LANES)
        ch = slice(g * gw, (g + 1) * gw)
        v = dt_ref[:, lanes] + dtb_ref[:, lanes]
        dt = jnp.maximum(v, 0.0) + jnp.log1p(jnp.exp(-jnp.abs(v)))
        a_d = dt * a_ref[:, lanes]
        acs = jnp.dot(tril_f, a_d, precision=hi, preferred_element_type=F32)
        acs_t = lax.dot_general(eye, acs, nt, precision=hi, preferred_element_type=F32)
        expand = expand_ref[...]
        dt_e = jnp.dot(dt, expand, precision=hi, preferred_element_type=F32)
        acs_e = jnp.dot(acs, expand, precision=hi, preferred_element_type=F32)
        last_e = acs_e[CHUNK - 1:CHUNK, :]

        xs = xs_all[:, ch]
        bm = bc_all[:, lanes].astype(BF16)
        cm = bc_all[:, N_SSM_GROUPS * D_STATE + g * D_STATE:N_SSM_GROUPS * D_STATE + (g + 1) * D_STATE].astype(BF16)
        x_d = xs * dt_e
        cb = lax.dot_general(cm, bm, nt, preferred_element_type=F32)

        prev = state_ref[g]
        y = jnp.dot(cm, prev.astype(BF16), preferred_element_type=F32) * jnp.exp(acs_e)
        x_d16 = x_d.astype(BF16)
        parts = []
        for j in range(SSM_HEADS_PER_GROUP):
            seg = acs[:, j:j + 1] - acs_t[j:j + 1, :]
            m_h = (cb * jnp.exp(jnp.where(tril, seg, NEG_BIG))).astype(BF16)
            parts.append(jnp.dot(m_h, x_d16[:, j * SSM_HEAD_DIM:(j + 1) * SSM_HEAD_DIM],
                                 preferred_element_type=F32))
        y = y + jnp.concatenate(parts, axis=1)

        xw = (x_d * jnp.exp(last_e - acs_e)).astype(BF16)
        state_ref[g] = prev * jnp.exp(last_e) + lax.dot_general(bm, xw, tn, preferred_element_type=F32)

        y = y + dskip_ref[:, ch] * xs
        y = y * _silu(z_ref[:, ch].astype(F32))
        y = y * lax.rsqrt(jnp.mean(y * y, axis=-1, keepdims=True) + EPS)
        o_ref[:, ch] = (y * nw_ref[:, ch]).astype(o_ref.dtype)


def ssd_mixer(rest, dt_raw, conv_w, conv_b, dt_bias_p, a_p, d_skip_e, norm_w, batch, seq):
    t = rest.shape[0]
    d_inner = N_SSM_GROUPS * SSM_HEADS_PER_GROUP * SSM_HEAD_DIM
    bc_w = 2 * N_SSM_GROUPS * D_STATE
    nc = seq // CHUNK
    gw = SSM_HEADS_PER_GROUP * SSM_HEAD_DIM
    expand = (np.arange(LANES)[:, None] == (np.arange(gw)[None, :] // SSM_HEAD_DIM)).astype(np.float32)
    rowmap = lambda b, c: (b * nc + c, 0)
    const = lambda b, c: (0, 0)
    return pl.pallas_call(
        _ssd_kernel,
        grid=(batch, nc),
        in_specs=[
            pl.BlockSpec((CHUNK, d_inner), lambda b, c: (b * nc + c, 2)),
            pl.BlockSpec((CHUNK, bc_w), lambda b, c: (b * nc + c, 6)),
            pl.BlockSpec((CHUNK, d_inner), rowmap),
            pl.BlockSpec((CHUNK, N_SSM_GROUPS * LANES), rowmap),
            pl.BlockSpec((CONV_WIDTH, d_inner), const),
            pl.BlockSpec((1, d_inner), const),
            pl.BlockSpec((CONV_WIDTH, bc_w), const),
            pl.BlockSpec((1, bc_w), const),
            pl.BlockSpec((1, N_SSM_GROUPS * LANES), const),
            pl.BlockSpec((1, N_SSM_GROUPS * LANES), const),
            pl.BlockSpec((1, d_inner), const),
            pl.BlockSpec((1, d_inner), const),
            pl.BlockSpec((LANES, gw), const),
        ],
        out_specs=pl.BlockSpec((CHUNK, d_inner), rowmap),
        out_shape=jax.ShapeDtypeStruct((t, d_inner), BF16),
        scratch_shapes=[
            pltpu.VMEM((CHUNK + 8, d_inner), F32),
            pltpu.VMEM((CHUNK + 8, bc_w), F32),
            pltpu.VMEM((N_SSM_GROUPS, D_STATE, gw), F32),
        ],
        compiler_params=_params(("parallel", "arbitrary")),
        name="ssd_mixer",
    )(rest, rest, rest, dt_raw, conv_w[:, :d_inner], conv_b[:, :d_inner], conv_w[:, d_inner:],
      conv_b[:, d_inner:], dt_bias_p, a_p, d_skip_e, norm_w, jnp.asarray(expand))


def _merge_kernel(at_ref, ys_ref, gl_ref, x_ref, wa_ref, ws_ref,
                  wo_ref, gb_ref, nf_ref, rw_ref, rb_ref, x1_ref, h2_ref, ti_ref, tw_ref):
    hi = lax.Precision.HIGHEST
    d = x_ref.shape[1]
    sub = x_ref.shape[0] // MERGE_SUBTILES
    for part in range(MERGE_SUBTILES):
        rows = slice(part * sub, (part + 1) * sub)
        y_attn = jnp.dot(at_ref[rows, :], wa_ref[...], preferred_element_type=F32)
        y_ssm = jnp.dot(ys_ref[rows, :], ws_ref[...], preferred_element_type=F32)
        gv = gl_ref[rows, :].astype(F32) + gb_ref[...]
        gates = 1.0 / (1.0 + jnp.exp(-gv))
        merged = gates[:, :d] * y_attn + gates[:, d:] * y_ssm
        x1 = x_ref[rows, :] + jnp.dot(merged.astype(BF16), wo_ref[...], preferred_element_type=F32)
        x1_ref[rows, :] = x1
        h2 = x1 * lax.rsqrt(jnp.mean(x1 * x1, axis=-1, keepdims=True) + EPS) * nf_ref[...]
        _store_slabs(h2_ref, part * sub, sub, _pack_bf16_pairs(h2))
        logits = jnp.dot(h2, rw_ref[...], precision=hi, preferred_element_type=F32) + rb_ref[...]
        lane = lax.broadcasted_iota(jnp.int32, logits.shape, 1)
        top_i = jnp.zeros(logits.shape, jnp.int32)
        top_v = jnp.full(logits.shape, NEG_BIG, F32)
        work = logits
        for k in range(TOP_K):
            m = jnp.max(work, axis=-1, keepdims=True)
            idx = jnp.min(jnp.where(work == m, lane, LANES), axis=-1, keepdims=True)
            top_i = jnp.where(lane == k, idx, top_i)
            top_v = jnp.where(lane == k, m, top_v)
            work = jnp.where(lane == idx, NEG_BIG * 2.0, work)
        ev = jnp.exp(top_v - jnp.max(top_v, axis=-1, keepdims=True))
        ti_ref[rows, :] = top_i
        tw_ref[rows, :] = ev / jnp.sum(ev, axis=-1, keepdims=True)


def merge_project(attn, y_ssm, rest, x, wa, ws, wo, gate_bias, norm_ffn, router_w_p, router_b_p):
    t, d = x.shape
    slab = d // 2 // LANES
    tm = 256 * MERGE_SUBTILES
    d_inner = y_ssm.shape[1]
    rowmap = lambda i: (i, 0)
    const = lambda i: (0, 0)
    full = lambda a: pl.BlockSpec(a.shape, const)
    args = [attn, y_ssm, rest, x, wa, ws, wo, gate_bias, norm_ffn, router_w_p, router_b_p]
    in_specs = (
        [pl.BlockSpec((tm, GROUP_WIDTH), rowmap),
         pl.BlockSpec((tm, d_inner), rowmap),
         pl.BlockSpec((tm, 2 * d), lambda i: (i, 1)),
         pl.BlockSpec((tm, d), rowmap)]
        + [full(a) for a in args[4:]]
    )
    return pl.pallas_call(
        _merge_kernel,
        grid=(t // tm,),
        in_specs=in_specs,
        out_specs=[pl.BlockSpec((tm, d), rowmap), pl.BlockSpec((tm * slab, LANES), rowmap),
                   pl.BlockSpec((tm, LANES), rowmap), pl.BlockSpec((tm, LANES), rowmap)],
        out_shape=[jax.ShapeDtypeStruct((t, d), F32), jax.ShapeDtypeStruct((t * slab, LANES), jnp.uint32),
                   jax.ShapeDtypeStruct((t, LANES), jnp.int32), jax.ShapeDtypeStruct((t, LANES), F32)],
        compiler_params=_params(("parallel",)),
        name="merge_project",
    )(*args)


def _split_w1_kernel(w_ref, g_ref, l_ref, t_ref):
    de = g_ref.shape[1]
    for s in range(t_ref.shape[0]):
        cols = slice(s * LANES, (s + 1) * LANES)
        t_ref[s] = w_ref[0, cols, :].T
        g_ref[0, :, cols] = t_ref[s, pl.ds(0, de, stride=2), :].astype(g_ref.dtype)
        l_ref[0, :, cols] = t_ref[s, pl.ds(1, de, stride=2), :].astype(l_ref.dtype)


def split_w1(w1):
    e, d, de2 = w1.shape
    de = de2 // 2
    tk = 512
    out = jax.ShapeDtypeStruct((e, de, d), BF16)
    return pl.pallas_call(
        _split_w1_kernel,
        grid=(e, d // tk),
        in_specs=[pl.BlockSpec((1, tk, de2), lambda i, k: (i, k, 0))],
        out_specs=[pl.BlockSpec((1, de, tk), lambda i, k: (i, 0, k))] * 2,
        out_shape=[out, out],
        scratch_shapes=[pltpu.VMEM((tk // LANES, de2, LANES), F32)],
        compiler_params=_params(("parallel", "parallel")),
        name="split_w1",
    )(w1)


def _pack_bf16_pairs(v):
    w = v.shape[1] // 2
    lo = lax.bitcast_convert_type(v[:, :w].astype(BF16).astype(F32), jnp.uint32) >> 16
    hi = lax.bitcast_convert_type(v[:, w:].astype(BF16).astype(F32), jnp.uint32) & jnp.uint32(0xFFFF0000)
    return lo | hi


def _unpack_bf16_pairs(p):
    lo = lax.bitcast_convert_type(p << 16, F32)
    hi = lax.bitcast_convert_type(p & jnp.uint32(0xFFFF0000), F32)
    return lo, hi


def _store_slabs(ref, row0, rows, packed):
    slab = packed.shape[1] // LANES
    for s in range(slab):
        ref[pl.ds(row0 * slab + s, rows, stride=slab), :] = packed[:, s * LANES:(s + 1) * LANES]


def _load_slabs(ref, rows, slab):
    return jnp.concatenate([ref[pl.ds(s, rows, stride=slab), :] for s in range(slab)], axis=1)


def _expert_kernel(be_ref, nu_ref, g0_ref, g1_ref, g2_ref, sp_ref, sc_ref, h2p_ref, w1g_ref, w1l_ref, w2_ref, b1g_ref,
                   b1l_ref, b2_ref, y4p_ref, xbuf, obuf, gsem, ssem, *, n_tokens, t_pad):
    i = pl.program_id(0)
    n_used = nu_ref[0]
    slot = lax.rem(i, 2)
    other = 1 - slot
    gslot = lax.rem(i, 3)
    gslot1 = lax.rem(i + 1, 3)
    gslot2 = lax.rem(i + 2, 3)
    bm = EXPERT_ROWS
    slab = xbuf.shape[1] // bm

    def gather_copy(src_ref, j, buf):
        src = pl.multiple_of(src_ref[0, 0, j], slab)
        return pltpu.make_async_copy(h2p_ref.at[pl.ds(src, slab)],
                                     xbuf.at[buf, pl.ds(j * slab, slab)], gsem.at[buf])

    def scatter_copy(dst_ref, j, buf):
        dst = pl.multiple_of(dst_ref[0, 0, j], slab)
        return pltpu.make_async_copy(obuf.at[buf, pl.ds(j * slab, slab)],
                                     y4p_ref.at[pl.ds(dst, slab)], ssem.at[buf])

    def wait_block(kind, buf):
        if kind == "gather":
            pltpu.make_async_copy(h2p_ref.at[pl.ds(0, bm * slab)], xbuf.at[buf], gsem.at[buf]).wait()
        else:
            pltpu.make_async_copy(obuf.at[buf], y4p_ref.at[pl.ds(0, bm * slab)], ssem.at[buf]).wait()

    @pl.when(i == 0)
    def _():
        obuf[...] = jnp.zeros(obuf.shape, obuf.dtype)
        fills = [pltpu.make_async_copy(obuf.at[1], y4p_ref.at[pl.ds((k * t_pad + n_tokens) * slab + c * bm * slab,
                                                                     bm * slab)], ssem.at[1])
                 for k in range(TOP_K) for c in range((t_pad - n_tokens) // bm)]
        for fill in fills:
            fill.start()
        for fill in fills:
            fill.wait()
        for j in range(bm):
            gather_copy(g0_ref, j, 0).start()
        for j in range(bm):
            gather_copy(g1_ref, j, 1).start()
        spare_row = t_pad + t_pad - 2 * bm
        pltpu.make_async_copy(obuf.at[0], y4p_ref.at[pl.ds(spare_row * slab, bm * slab)], ssem.at[0]).start()

    @pl.when(i < n_used)
    def _():
        nt = (((1,), (1,)), ((), ()))
        wait_block("gather", gslot)
        lo, hi = _unpack_bf16_pairs(_load_slabs(xbuf.at[gslot], bm, slab))
        xb = jnp.concatenate([lo, hi], axis=1).astype(BF16)
        for j in range(bm):
            gather_copy(g2_ref, j, gslot2).start()
        for j in range(bm):
            scatter_copy(sp_ref, j, other).start()
        glu = lax.dot_general(xb, w1g_ref[0], nt, preferred_element_type=F32) + b1g_ref[0]
        lin = lax.dot_general(xb, w1l_ref[0], nt, preferred_element_type=F32) + b1l_ref[0]
        glu = jnp.minimum(glu, SWIGLU_LIMIT)
        lin = jnp.clip(lin, -SWIGLU_LIMIT, SWIGLU_LIMIT)
        act = glu * (1.0 / (1.0 + jnp.exp(-SWIGLU_ALPHA * glu))) * (lin + 1.0)
        y = jnp.dot(act.astype(BF16), w2_ref[0], preferred_element_type=F32) + b2_ref[0]
        wait_block("scatter", slot)
        _store_slabs(obuf.at[slot], 0, bm, _pack_bf16_pairs(y))

    @pl.when(i == n_used - 1)
    def _():
        for j in range(bm):
            scatter_copy(sc_ref, j, slot).start()
        wait_block("scatter", other)
        wait_block("scatter", slot)
        wait_block("gather", gslot1)
        wait_block("gather", gslot2)


def expert_ffn(h2p, ids, block_e, n_used, w1g, w1l, w2, b1g, b1l, b2, n_tokens):
    n_blocks, _, bm = ids.shape
    slab = h2p.shape[0] // n_tokens
    de, d = w1g.shape[1], w1g.shape[2]
    t_pad = padded_tokens(n_tokens)
    src = jnp.minimum(ids >> 2, n_tokens - 1) * slab
    dst = ((ids & 3) * t_pad + (ids >> 2)) * slab
    first = ((t_pad - 2 * bm + jnp.arange(bm, dtype=jnp.int32)) * slab).reshape(1, 1, bm)
    dst = jnp.concatenate([first, dst], axis=0)
    wmap = lambda i, be, nu: (be[i], 0, 0)
    smem_ids = lambda imap: pl.BlockSpec((1, 1, bm), imap, memory_space=pltpu.SMEM)
    grid_spec = pltpu.PrefetchScalarGridSpec(
        num_scalar_prefetch=2,
        grid=(n_blocks,),
        in_specs=[
            smem_ids(lambda i, be, nu: (i, 0, 0)),
            smem_ids(lambda i, be, nu: (jnp.minimum(i + 1, n_blocks - 1), 0, 0)),
            smem_ids(lambda i, be, nu: (jnp.minimum(i + 2, n_blocks - 1), 0, 0)),
            smem_ids(lambda i, be, nu: (i, 0, 0)),
            smem_ids(lambda i, be, nu: (i + 1, 0, 0)),
            pl.BlockSpec(memory_space=pl.ANY),
            pl.BlockSpec((1, de, d), wmap),
            pl.BlockSpec((1, de, d), wmap),
            pl.BlockSpec((1, de, d), wmap),
            pl.BlockSpec((1, 1, de), wmap),
            pl.BlockSpec((1, 1, de), wmap),
            pl.BlockSpec((1, 1, d), wmap),
        ],
        out_specs=pl.BlockSpec(memory_space=pl.ANY),
        scratch_shapes=[
            pltpu.VMEM((3, bm * slab, LANES), jnp.uint32),
            pltpu.VMEM((2, bm * slab, LANES), jnp.uint32),
            pltpu.SemaphoreType.DMA((3,)),
            pltpu.SemaphoreType.DMA((2,)),
        ],
    )
    return pl.pallas_call(
        functools.partial(_expert_kernel, n_tokens=n_tokens, t_pad=t_pad),
        grid_spec=grid_spec,
        out_shape=jax.ShapeDtypeStruct((TOP_K * t_pad * slab, LANES), jnp.uint32),
        compiler_params=_params(("arbitrary",)),
        name="expert_ffn",
    )(block_e, n_used, src, src, src, dst, dst, h2p, w1g, w1l, w2, b1g, b1l, b2)


def _combine_kernel(x1_ref, y0_ref, y1_ref, y2_ref, y3_ref, tw_ref, nw_ref, o_ref, *, normalize):
    tm, d = x1_ref.shape
    slab = y0_ref.shape[0] // tm
    lo_sum = jnp.zeros((tm, d // 2), F32)
    hi_sum = jnp.zeros((tm, d // 2), F32)
    for k, y_ref in enumerate((y0_ref, y1_ref, y2_ref, y3_ref)):
        lo, hi = _unpack_bf16_pairs(_load_slabs(y_ref, tm, slab))
        w = tw_ref[:, k:k + 1]
        lo_sum = lo_sum + w * lo
        hi_sum = hi_sum + w * hi
    acc = x1_ref[...] + jnp.concatenate([lo_sum, hi_sum], axis=1)
    if normalize:
        acc = acc * lax.rsqrt(jnp.mean(acc * acc, axis=-1, keepdims=True) + EPS) * nw_ref[...]
    o_ref[...] = acc


def combine_norm(x1, y4p, top_w, norm_w, normalize):
    t, d = x1.shape
    tm = 256
    t_pad = padded_tokens(t)
    slab = y4p.shape[0] // (TOP_K * t_pad)
    y_specs = [pl.BlockSpec((tm * slab, LANES), functools.partial(lambda i, k: (k * (t_pad // tm) + i, 0), k=k))
               for k in range(TOP_K)]
    return pl.pallas_call(
        functools.partial(_combine_kernel, normalize=normalize),
        grid=(t // tm,),
        in_specs=[pl.BlockSpec((tm, d), lambda i: (i, 0))] + y_specs
        + [pl.BlockSpec((tm, LANES), lambda i: (i, 0)), pl.BlockSpec((1, d), lambda i: (0, 0))],
        out_specs=pl.BlockSpec((tm, d), lambda i: (i, 0)),
        out_shape=jax.ShapeDtypeStruct((t, d), F32),
        compiler_params=_params(("parallel",)),
        name="combine_norm",
    )(x1, y4p, y4p, y4p, y4p, top_w, norm_w)


def routing_layout(top_i, n_tokens):
    n_assign = n_tokens * TOP_K
    bm = EXPERT_ROWS
    n_pad = N_EXPERTS * bm
    n_blocks = (n_assign + n_pad) // bm
    flat_e = top_i[:, :TOP_K].reshape(n_assign)
    counts = jnp.sum(flat_e[:, None] == jnp.arange(N_EXPERTS, dtype=jnp.int32)[None, :], axis=0, dtype=jnp.int32)
    padded = (counts + bm - 1) // bm * bm
    pend = jnp.cumsum(padded)
    n_used = (pend[-1] // bm).astype(jnp.int32)
    block_row = jnp.arange(n_blocks, dtype=jnp.int32) * bm
    block_e = jnp.minimum(jnp.sum(pend[None, :] <= block_row[:, None], axis=1, dtype=jnp.int32), N_EXPERTS - 1)
    spare_pos = jnp.arange(bm, dtype=jnp.int32)[None, :]
    spare_key = jnp.where(spare_pos < (padded - counts)[:, None], jnp.arange(N_EXPERTS, dtype=jnp.int32)[:, None],
                          N_EXPERTS).reshape(n_pad)
    order = jnp.argsort(jnp.concatenate([flat_e, spare_key]), stable=True).astype(jnp.int32)
    return order.reshape(n_blocks, 1, bm), block_e, n_used.reshape(1)


def kernel(x, w_in, rel_bias, w_branch_attn, conv_w, conv_b, dt_bias, a_log, d_skip, ssm_norm_w,
           w_branch_ssm, gate_bias, w_out, norm_mix, norm_ffn, router_w, router_b, w1, b1, w2, b2,
           norm_final):
    batch, seq, d = x.shape
    t = batch * seq
    depth = w_in.shape[0]
    n_groups = len(ATTN_GROUPS)
    attn_w = n_groups * GROUP_WIDTH
    d_inner = N_SSM_GROUPS * SSM_HEADS_PER_GROUP * SSM_HEAD_DIM
    n_heads = N_SSM_GROUPS * SSM_HEADS_PER_GROUP
    bc_w = 2 * N_SSM_GROUPS * D_STATE
    xf = x.reshape(t, d)
    for l in range(depth):
        wl = w_in[l]
        o_z = 3 * attn_w
        o_xbc = o_z + d_inner
        o_dt = o_xbc + d_inner + bc_w
        o_gate = o_dt + n_heads
        w_qkv = wl[:, :o_z].astype(BF16)
        w_rest = jnp.concatenate([wl[:, o_z:o_xbc], wl[:, o_gate:], wl[:, o_xbc:o_dt]], axis=1).astype(BF16)
        lane_of_head = (np.arange(n_heads) // SSM_HEADS_PER_GROUP) * LANES + np.arange(n_heads) % SSM_HEADS_PER_GROUP
        w_dt = jnp.zeros((d, N_SSM_GROUPS * LANES), F32).at[:, lane_of_head].set(wl[:, o_dt:o_gate]).astype(BF16)
        dt_bias_p = jnp.zeros((1, N_SSM_GROUPS * LANES), F32).at[0, lane_of_head].set(dt_bias[l].astype(F32))
        a_p = jnp.zeros((1, N_SSM_GROUPS * LANES), F32).at[0, lane_of_head].set(-jnp.exp(a_log[l].astype(F32)))
        d_skip_e = jnp.repeat(d_skip[l].astype(F32), SSM_HEAD_DIM)[None, :]

        g_mix = norm_mix[l].astype(F32)[None, :]
        rest, dt_raw, h = rms_matmul(xf, g_mix, w_rest, w_dt, tn=1792)
        qkv_groups = qkv_project(h, w_qkv, batch, seq)
        bias = jnp.stack([attention_bias(rel_bias, gi) for gi in range(n_groups)])
        attn = dilated_attention(qkv_groups, bias, batch, seq)
        y_ssm = ssd_mixer(rest, dt_raw, conv_w[l].astype(F32), conv_b[l].astype(F32)[None, :], dt_bias_p, a_p,
                          d_skip_e, ssm_norm_w[l].astype(F32)[None, :], batch, seq)
        router_w_p = jnp.zeros((d, LANES), F32).at[:, :N_EXPERTS].set(router_w[l].astype(F32))
        router_b_p = jnp.full((1, LANES), NEG_BIG, F32).at[0, :N_EXPERTS].set(router_b[l].astype(F32))
        x1, h2, top_i, top_w = merge_project(
            attn, y_ssm, rest, xf, w_branch_attn[l].astype(BF16), w_branch_ssm[l].astype(BF16),
            w_out[l].astype(BF16), gate_bias[l].astype(F32)[None, :], norm_ffn[l].astype(F32)[None, :],
            router_w_p, router_b_p)

        ids, block_e, n_used = routing_layout(top_i, t)
        w1g_t, w1l_t = split_w1(w1[l].astype(F32))
        y4p = expert_ffn(h2, ids, block_e, n_used, w1g_t, w1l_t, w2[l].astype(BF16),
                         b1[l][:, None, 0::2].astype(F32), b1[l][:, None, 1::2].astype(F32),
                         b2[l][:, None, :].astype(F32), t)
        xf = combine_norm(x1, y4p, top_w, norm_final.astype(F32)[None, :], normalize=(l == depth - 1))
    return xf.reshape(batch, seq, d)
```

```python
import functools
import math

import jax
import jax.numpy as jnp
import numpy as np
from jax import lax
from jax.experimental import pallas as pl
from jax.experimental.pallas import tpu as pltpu

F32 = jnp.float32
BF16 = jnp.bfloat16

EPS = 1e-5
NEG_BIG = -1e30

HEAD_DIM = 64
ATTN_GROUPS = ((128, 1), (512, 4), (2048, 16))
HEADS_PER_GROUP = 8
GROUP_WIDTH = HEADS_PER_GROUP * HEAD_DIM
ATTN_BLOCK = 128
NUM_BUCKETS = 32
MAX_DISTANCE = 2048
SSM_HEAD_DIM = 64
N_SSM_GROUPS = 4
SSM_HEADS_PER_GROUP = 8
D_STATE = 128
CONV_WIDTH = 4
CHUNK = 128
N_EXPERTS = 32
TOP_K = 4
SWIGLU_LIMIT = 7.0
SWIGLU_ALPHA = 1.702

LANES = 128
V7X_VMEM_BYTES = 64 * 1024 * 1024
VMEM_LIMIT = 48 * 1024 * 1024

ROW_TILE = 1024
MERGE_SUBTILES = 2
EXPERT_ROWS = 256


def padded_tokens(n_tokens):
    return n_tokens + N_EXPERTS * EXPERT_ROWS // TOP_K + 2 * EXPERT_ROWS


def _params(semantics):
    return pltpu.CompilerParams(dimension_semantics=semantics, vmem_limit_bytes=VMEM_LIMIT)


def _rms_matmul_kernel(x_ref, g_ref, w_ref, ws_ref, o_ref, os_ref, hb_ref, h_ref):
    @pl.when(pl.program_id(1) == 0)
    def _():
        x = x_ref[...]
        ms = jnp.mean(x * x, axis=-1, keepdims=True)
        h_ref[...] = (x * lax.rsqrt(ms + EPS) * g_ref[...]).astype(BF16)
        hb_ref[...] = h_ref[...]
        os_ref[...] = jnp.dot(h_ref[...], ws_ref[...], preferred_element_type=F32)

    o_ref[...] = jnp.dot(h_ref[...], w_ref[...], preferred_element_type=F32).astype(o_ref.dtype)


def rms_matmul(x, g, w, w_side, tn):
    t, d = x.shape
    n = w.shape[1]
    ns = w_side.shape[1]
    tm = ROW_TILE
    return pl.pallas_call(
        _rms_matmul_kernel,
        grid=(t // tm, n // tn),
        in_specs=[
            pl.BlockSpec((tm, d), lambda i, j: (i, 0)),
            pl.BlockSpec((1, d), lambda i, j: (0, 0)),
            pl.BlockSpec((d, tn), lambda i, j: (0, j)),
            pl.BlockSpec((d, ns), lambda i, j: (0, 0)),
        ],
        out_specs=[
            pl.BlockSpec((tm, tn), lambda i, j: (i, j)),
            pl.BlockSpec((tm, ns), lambda i, j: (i, 0)),
            pl.BlockSpec((tm, d), lambda i, j: (i, 0)),
        ],
        out_shape=[jax.ShapeDtypeStruct((t, n), BF16), jax.ShapeDtypeStruct((t, ns), F32),
                   jax.ShapeDtypeStruct((t, d), BF16)],
        scratch_shapes=[pltpu.VMEM((tm, d), BF16)],
        compiler_params=_params(("parallel", "arbitrary")),
        name="rms_matmul",
    )(x, g, w, w_side)


def _qkv_kernel(h_ref, w_ref, o0_ref, o1_ref, o2_ref, acc_ref):
    seq = h_ref.shape[0]
    slabs = GROUP_WIDTH // LANES
    n_blocks = seq // ATTN_BLOCK
    for gi, o_ref in enumerate((o0_ref, o1_ref, o2_ref)):
        dil = ATTN_GROUPS[gi][1]
        nb = n_blocks // dil
        acc = jnp.dot(h_ref[...], w_ref[:, gi * GROUP_WIDTH:(gi + 1) * GROUP_WIDTH], preferred_element_type=F32)
        if dil == 1:
            for n in range(n_blocks):
                o_ref[0, 0, n] = acc[n * ATTN_BLOCK:(n + 1) * ATTN_BLOCK, :].astype(o_ref.dtype)
            continue
        for s in range(slabs):
            acc_ref[s] = acc[:, s * LANES:(s + 1) * LANES]
        for r in range(dil):
            for n in range(nb):
                for s in range(slabs):
                    rows = pl.ds(r + n * ATTN_BLOCK * dil, ATTN_BLOCK, stride=dil)
                    o_ref[0, 0, r * nb + n, :, s * LANES:(s + 1) * LANES] = acc_ref[s, rows, :].astype(o_ref.dtype)


def qkv_project(h, w_qkv, batch, seq):
    t, d = h.shape
    n_groups = len(ATTN_GROUPS)
    tn = n_groups * GROUP_WIDTH
    out_shapes, out_specs = [], []
    for _ in ATTN_GROUPS:
        shape = (3, batch, seq // ATTN_BLOCK, ATTN_BLOCK, GROUP_WIDTH)
        out_shapes.append(jax.ShapeDtypeStruct(shape, BF16))
        out_specs.append(pl.BlockSpec((1, 1) + shape[2:], lambda w, b: (w, b, 0, 0, 0)))
    return pl.pallas_call(
        _qkv_kernel,
        grid=(3, batch),
        in_specs=[pl.BlockSpec((seq, d), lambda w, b: (b, 0)), pl.BlockSpec((d, tn), lambda w, b: (0, w))],
        out_specs=out_specs,
        out_shape=out_shapes,
        scratch_shapes=[pltpu.VMEM((GROUP_WIDTH // LANES, seq, LANES), F32)],
        compiler_params=_params(("parallel", "parallel")),
        name="qkv_project",
    )(h, w_qkv)


def _attn_kernel(q0_ref, q1_ref, q2_ref, bias_ref, o_ref, out_ref, lse_ref):
    heads = LANES // HEAD_DIM
    scale = jnp.asarray(HEAD_DIM ** -0.5, BF16)
    n_blocks = q0_ref.shape[2]
    qk = (((2,), (2,)), ((0,), (0,)))
    pv = (((2,), (1,)), ((0,), (0,)))
    blk = lax.broadcasted_iota(jnp.int32, (n_blocks, 1, 1), 0)
    lane = lax.broadcasted_iota(jnp.int32, (1, 1, LANES), 2)

    def shifted(x):
        return jnp.concatenate([x[n_blocks - 1:], x[:n_blocks - 1]], axis=0)

    for g, ref in enumerate((q0_ref, q1_ref, q2_ref)):
        dil = ATTN_GROUPS[g][1]
        nb = n_blocks // dil
        q2h = ref[0, 0] * scale
        keys, vals = ref[1, 0], ref[2, 0]
        if nb > 1:
            keys = jnp.concatenate([shifted(keys), keys], axis=1)
            vals = jnp.concatenate([shifted(vals), vals], axis=1)
        n_keys = keys.shape[1]
        ones = jnp.ones((n_keys, LANES), BF16)
        pvs, dens, maxs = [], [], []
        for h in range(heads):
            in_head = (lane >= h * HEAD_DIM) & (lane < (h + 1) * HEAD_DIM)
            q = jnp.where(in_head, q2h, jnp.zeros_like(q2h))
            s = lax.dot_general(q, keys, qk, preferred_element_type=F32)
            s = s + bias_ref[g, h, :, 2 * ATTN_BLOCK - n_keys:][None]
            if nb > 1:
                key_is_prev = lax.broadcasted_iota(jnp.int32, (1, 1, n_keys), 2) < ATTN_BLOCK
                s = jnp.where((blk % nb == 0) & key_is_prev, NEG_BIG, s)
            m = jnp.max(s, axis=-1, keepdims=True)
            p = jnp.exp(s - m).astype(BF16)
            pvs.append(lax.dot_general(p, vals, pv, preferred_element_type=F32))
            dens.append(jnp.dot(p.reshape(n_blocks * ATTN_BLOCK, n_keys), ones,
                                preferred_element_type=F32).reshape(n_blocks, ATTN_BLOCK, LANES))
            maxs.append(m)
        first = lane < HEAD_DIM
        den = jnp.where(first, dens[0], dens[1])
        o2 = jnp.where(first, pvs[0], pvs[1]) / den
        l2 = jnp.where(first, maxs[0], maxs[1]) + jnp.log(den)
        for r in range(dil):
            for n in range(nb):
                start = r + n * ATTN_BLOCK * dil
                rows = pl.ds(start, ATTN_BLOCK) if dil == 1 else pl.ds(start, ATTN_BLOCK, stride=dil)
                out_ref[g, rows, :] = o2[r * nb + n]
                lse_ref[g, rows, :] = l2[r * nb + n]

    l0, l1, l2 = lse_ref[0], lse_ref[1], lse_ref[2]
    lm = jnp.maximum(jnp.maximum(l0, l1), l2)
    e0, e1, e2 = jnp.exp(l0 - lm), jnp.exp(l1 - lm), jnp.exp(l2 - lm)
    mixed = (e0 * out_ref[0] + e1 * out_ref[1] + e2 * out_ref[2]) / (e0 + e1 + e2)
    o_ref[...] = mixed.astype(o_ref.dtype)


def dilated_attention(qkv_groups, bias, batch, seq):
    heads = LANES // HEAD_DIM
    in_specs = [pl.BlockSpec((3, 1) + a.shape[2:4] + (LANES,), lambda b, hp: (0, b, 0, 0, hp)) for a in qkv_groups]
    in_specs.append(pl.BlockSpec((len(ATTN_GROUPS), heads, ATTN_BLOCK, 2 * ATTN_BLOCK), lambda b, hp: (0, hp, 0, 0)))
    return pl.pallas_call(
        _attn_kernel,
        grid=(batch, GROUP_WIDTH // LANES),
        in_specs=in_specs,
        out_specs=pl.BlockSpec((seq, LANES), lambda b, hp: (b, hp)),
        out_shape=jax.ShapeDtypeStruct((batch * seq, GROUP_WIDTH), BF16),
        scratch_shapes=[pltpu.VMEM((len(ATTN_GROUPS), seq, LANES), F32),
                        pltpu.VMEM((len(ATTN_GROUPS), seq, LANES), F32)],
        compiler_params=_params(("parallel", "parallel")),
        name="dilated_attn",
    )(*qkv_groups, bias)


def attention_bias(rel_bias, gi):
    window, dil = ATTN_GROUPS[gi]
    w_sub = window // dil
    q_idx = np.arange(ATTN_BLOCK)[:, None]
    k_idx = np.arange(2 * ATTN_BLOCK)[None, :]
    delta = q_idx + ATTN_BLOCK - k_idx
    in_band = (delta >= 0) & (delta <= w_sub)
    dist = np.clip(delta, 0, w_sub) * dil
    max_exact = NUM_BUCKETS // 2
    nf = np.maximum(dist, max_exact).astype(np.float32)
    large = max_exact + (np.log(nf / max_exact) / math.log(MAX_DISTANCE / max_exact)
                         * (NUM_BUCKETS - max_exact)).astype(np.int32)
    large = np.minimum(large, NUM_BUCKETS - 1)
    bucket = np.where(dist < max_exact, dist, large)
    table = rel_bias[:, gi * HEADS_PER_GROUP:(gi + 1) * HEADS_PER_GROUP].astype(F32)
    onehot = (bucket.reshape(-1, 1) == np.arange(NUM_BUCKETS)[None, :]).astype(np.float32)
    bias = jnp.einsum('bh,nb->hn', table, jnp.asarray(onehot), precision=lax.Precision.HIGHEST)
    bias = bias.reshape(HEADS_PER_GROUP, ATTN_BLOCK, 2 * ATTN_BLOCK)
    return jnp.where(in_band[None], bias, NEG_BIG)


def _silu(v):
    return v * (1.0 / (1.0 + jnp.exp(-v)))


CONV_TAIL = 16


def _conv_silu(ext_ref, u_ref, w_ref, b_ref, first):
    rows = u_ref.shape[0]
    ext_rows = ext_ref.shape[0]

    @pl.when(first)
    def _():
        ext_ref[0:CONV_TAIL, :] = jnp.zeros((CONV_TAIL, ext_ref.shape[1]), ext_ref.dtype)

    u = u_ref[...]
    ext_ref[CONV_TAIL:, :] = u
    taps = CONV_WIDTH - 1
    out_row = lax.broadcasted_iota(jnp.int32, (rows, ext_rows), 0)
    src_row = lax.broadcasted_iota(jnp.int32, (rows, ext_rows), 1)
    shift_mat = jnp.concatenate([(src_row == out_row + (CONV_TAIL - k)).astype(BF16) for k in range(1, CONV_WIDTH)],
                                axis=0)
    shifted = jnp.dot(shift_mat, ext_ref[...], preferred_element_type=F32)
    acc = u.astype(F32) * w_ref[taps:CONV_WIDTH, :] + b_ref[...]
    for k in range(1, CONV_WIDTH):
        acc = acc + shifted[(k - 1) * rows:k * rows, :] * w_ref[taps - k:CONV_WIDTH - k, :]
    ext_ref[0:CONV_TAIL, :] = ext_ref[rows:, :]
    return _silu(acc)


def _ssd_kernel(x_ref, bc_ref, z_ref, dt_ref, cwx_ref, cbx_ref, cwbc_ref, cbbc_ref, dtb_ref, a_ref,
                dskip_ref, nw_ref, expand_ref, o_ref, extx_ref, extbc_ref, state_ref):
    first = pl.program_id(1) == 0
    gw = SSM_HEADS_PER_GROUP * SSM_HEAD_DIM

    @pl.when(first)
    def _():
        state_ref[...] = jnp.zeros(state_ref.shape, F32)

    xs_all = _conv_silu(extx_ref, x_ref, cwx_ref, cbx_ref, first)
    bc_all = _conv_silu(extbc_ref, bc_ref, cwbc_ref, cbbc_ref, first)

    row = lax.broadcasted_iota(jnp.int32, (CHUNK, CHUNK), 0)
    colm = lax.broadcasted_iota(jnp.int32, (CHUNK, CHUNK), 1)
    tril = row >= colm
    tril_b = tril.astype(BF16)
    head_rows = 8
    eye_b = (lax.broadcasted_iota(jnp.int32, (head_rows, CHUNK), 0)
             == lax.broadcasted_iota(jnp.int32, (head_rows, CHUNK), 1)).astype(BF16)
    nt = (((1,), (1,)), ((), ()))
    tn = (((0,), (0,)), ((), ()))

    def split3(v):
        p0 = v.astype(BF16)
        r1 = v - p0.astype(F32)
        p1 = r1.astype(BF16)
        p2 = (r1 - p1.astype(F32)).astype(BF16)
        return p0, p1, p2

    def select_rows(mat01, v):
        return sum(jnp.dot(mat01, p, preferred_element_type=F32) for p in split3(v))

    def select_cols(v, mat01):
        return sum(jnp.dot(p, mat01, preferred_element_type=F32) for p in split3(v))

    expand = expand_ref[...]
    for g in range(N_SSM_GROUPS):
        lanes = slice(g * LANES, (g + 1) * LANES)
        ch = slice(g * gw, (g + 1) * gw)
        v = dt_ref[:, lanes] + dtb_ref[:, lanes]
        dt = jnp.maximum(v, 0.0) + jnp.log1p(jnp.exp(-jnp.abs(v)))
        a_d = dt * a_ref[:, lanes]
        acs = select_rows(tril_b, a_d)
        acs_t = sum(lax.dot_general(eye_b, p, nt, preferred_element_type=F32) for p in split3(acs))
        last = acs[CHUNK - 1:CHUNK, :]
        dt_e = select_cols(dt, expand)
        eacs_e = select_cols(jnp.exp(acs), expand)
        edec_e = select_cols(jnp.exp(last - acs), expand)
        elast_e = eacs_e[CHUNK - 1:CHUNK, :]

        xs = xs_all[:, ch]
        bm = bc_all[:, lanes].astype(BF16)
        cm = bc_all[:, N_SSM_GROUPS * D_STATE + g * D_STATE:N_SSM_GROUPS * D_STATE + (g + 1) * D_STATE].astype(BF16)
        x_d = xs * dt_e
        cb = lax.dot_general(cm, bm, nt, preferred_element_type=F32)

        prev = state_ref[g]
        y = jnp.dot(cm, prev.astype(BF16), preferred_element_type=F32) * eacs_e
        x_d16 = x_d.astype(BF16)
        parts = []
        for j in range(SSM_HEADS_PER_GROUP):
            seg = acs[:, j:j + 1] - acs_t[j:j + 1, :]
            m_h = (cb * jnp.exp(jnp.where(tril, seg, NEG_BIG))).astype(BF16)
            parts.append(jnp.dot(m_h, x_d16[:, j * SSM_HEAD_DIM:(j + 1) * SSM_HEAD_DIM],
                                 preferred_element_type=F32))
        y = y + jnp.concatenate(parts, axis=1)

        xw = (x_d * edec_e).astype(BF16)
        state_ref[g] = prev * elast_e + lax.dot_general(bm, xw, tn, preferred_element_type=F32)

        y = y + dskip_ref[:, ch] * xs
        y = y * _silu(z_ref[:, ch].astype(F32))
        y = y * lax.rsqrt(jnp.mean(y * y, axis=-1, keepdims=True) + EPS)
        o_ref[:, ch] = (y * nw_ref[:, ch]).astype(o_ref.dtype)


def ssd_mixer(rest, dt_raw, conv_w, conv_b, dt_bias_p, a_p, d_skip_e, norm_w, batch, seq):
    t = rest.shape[0]
    d_inner = N_SSM_GROUPS * SSM_HEADS_PER_GROUP * SSM_HEAD_DIM
    bc_w = 2 * N_SSM_GROUPS * D_STATE
    nc = seq // CHUNK
    gw = SSM_HEADS_PER_GROUP * SSM_HEAD_DIM
    expand = (np.arange(LANES)[:, None] == (np.arange(gw)[None, :] // SSM_HEAD_DIM)).astype(np.float32)
    rowmap = lambda b, c: (b * nc + c, 0)
    const = lambda b, c: (0, 0)
    return pl.pallas_call(
        _ssd_kernel,
        grid=(batch, nc),
        in_specs=[
            pl.BlockSpec((CHUNK, d_inner), lambda b, c: (b * nc + c, 2)),
            pl.BlockSpec((CHUNK, bc_w), lambda b, c: (b * nc + c, 6)),
            pl.BlockSpec((CHUNK, d_inner), rowmap),
            pl.BlockSpec((CHUNK, N_SSM_GROUPS * LANES), rowmap),
            pl.BlockSpec((CONV_WIDTH, d_inner), const),
            pl.BlockSpec((1, d_inner), const),
            pl.BlockSpec((CONV_WIDTH, bc_w), const),
            pl.BlockSpec((1, bc_w), const),
            pl.BlockSpec((1, N_SSM_GROUPS * LANES), const),
            pl.BlockSpec((1, N_SSM_GROUPS * LANES), const),
            pl.BlockSpec((1, d_inner), const),
            pl.BlockSpec((1, d_inner), const),
            pl.BlockSpec((LANES, gw), const),
        ],
        out_specs=pl.BlockSpec((CHUNK, d_inner), rowmap),
        out_shape=jax.ShapeDtypeStruct((t, d_inner), BF16),
        scratch_shapes=[
            pltpu.VMEM((CONV_TAIL + CHUNK, d_inner), BF16),
            pltpu.VMEM((CONV_TAIL + CHUNK, bc_w), BF16),
            pltpu.VMEM((N_SSM_GROUPS, D_STATE, gw), F32),
        ],
        compiler_params=_params(("parallel", "arbitrary")),
        name="ssd_mixer",
    )(rest, rest, rest, dt_raw, conv_w[:, :d_inner], conv_b[:, :d_inner], conv_w[:, d_inner:],
      conv_b[:, d_inner:], dt_bias_p, a_p, d_skip_e, norm_w, jnp.asarray(expand, dtype=BF16))


def _merge_kernel(at_ref, ys_ref, gl_ref, x_ref, wa_ref, ws_ref,
                  wo_ref, gb_ref, nf_ref, rw_ref, rb_ref, x1_ref, h2_ref, ti_ref, tw_ref):
    hi = lax.Precision.HIGHEST
    d = x_ref.shape[1]
    sub = x_ref.shape[0] // MERGE_SUBTILES
    for part in range(MERGE_SUBTILES):
        rows = slice(part * sub, (part + 1) * sub)
        y_attn = jnp.dot(at_ref[rows, :], wa_ref[...], preferred_element_type=F32)
        y_ssm = jnp.dot(ys_ref[rows, :], ws_ref[...], preferred_element_type=F32)
        gv = gl_ref[rows, :].astype(F32) + gb_ref[...]
        gates = 1.0 / (1.0 + jnp.exp(-gv))
        merged = gates[:, :d] * y_attn + gates[:, d:] * y_ssm
        x1 = x_ref[rows, :] + jnp.dot(merged.astype(BF16), wo_ref[...], preferred_element_type=F32)
        x1_ref[rows, :] = x1
        h2 = x1 * lax.rsqrt(jnp.mean(x1 * x1, axis=-1, keepdims=True) + EPS) * nf_ref[...]
        _store_slabs(h2_ref, part * sub, sub, _pack_bf16_pairs(h2))
        logits = jnp.dot(h2, rw_ref[...], precision=hi, preferred_element_type=F32) + rb_ref[...]
        lane = lax.broadcasted_iota(jnp.int32, logits.shape, 1)
        top_i = jnp.zeros(logits.shape, jnp.int32)
        top_v = jnp.full(logits.shape, NEG_BIG, F32)
        work = logits
        for k in range(TOP_K):
            m = jnp.max(work, axis=-1, keepdims=True)
            idx = jnp.min(jnp.where(work == m, lane, LANES), axis=-1, keepdims=True)
            top_i = jnp.where(lane == k, idx, top_i)
            top_v = jnp.where(lane == k, m, top_v)
            work = jnp.where(lane == idx, NEG_BIG * 2.0, work)
        ev = jnp.exp(top_v - jnp.max(top_v, axis=-1, keepdims=True))
        ti_ref[rows, :] = top_i
        tw_ref[rows, :] = ev / jnp.sum(ev, axis=-1, keepdims=True)


def merge_project(attn, y_ssm, rest, x, wa, ws, wo, gate_bias, norm_ffn, router_w_p, router_b_p):
    t, d = x.shape
    slab = d // 2 // LANES
    tm = 256 * MERGE_SUBTILES
    d_inner = y_ssm.shape[1]
    rowmap = lambda i: (i, 0)
    const = lambda i: (0, 0)
    full = lambda a: pl.BlockSpec(a.shape, const)
    args = [attn, y_ssm, rest, x, wa, ws, wo, gate_bias, norm_ffn, router_w_p, router_b_p]
    in_specs = (
        [pl.BlockSpec((tm, GROUP_WIDTH), rowmap),
         pl.BlockSpec((tm, d_inner), rowmap),
         pl.BlockSpec((tm, 2 * d), lambda i: (i, 1)),
         pl.BlockSpec((tm, d), rowmap)]
        + [full(a) for a in args[4:]]
    )
    return pl.pallas_call(
        _merge_kernel,
        grid=(t // tm,),
        in_specs=in_specs,
        out_specs=[pl.BlockSpec((tm, d), rowmap), pl.BlockSpec((tm * slab, LANES), rowmap),
                   pl.BlockSpec((tm, LANES), rowmap), pl.BlockSpec((tm, LANES), rowmap)],
        out_shape=[jax.ShapeDtypeStruct((t, d), F32), jax.ShapeDtypeStruct((t * slab, LANES), jnp.uint32),
                   jax.ShapeDtypeStruct((t, LANES), jnp.int32), jax.ShapeDtypeStruct((t, LANES), F32)],
        compiler_params=_params(("parallel",)),
        name="merge_project",
    )(*args)


def _split_w1_kernel(w_ref, g_ref, l_ref, t_ref):
    de = g_ref.shape[1]
    for s in range(t_ref.shape[0]):
        cols = slice(s * LANES, (s + 1) * LANES)
        t_ref[s] = w_ref[0, cols, :].T
        g_ref[0, :, cols] = t_ref[s, pl.ds(0, de, stride=2), :].astype(g_ref.dtype)
        l_ref[0, :, cols] = t_ref[s, pl.ds(1, de, stride=2), :].astype(l_ref.dtype)


def split_w1(w1):
    e, d, de2 = w1.shape
    de = de2 // 2
    tk = 512
    out = jax.ShapeDtypeStruct((e, de, d), BF16)
    return pl.pallas_call(
        _split_w1_kernel,
        grid=(e, d // tk),
        in_specs=[pl.BlockSpec((1, tk, de2), lambda i, k: (i, k, 0))],
        out_specs=[pl.BlockSpec((1, de, tk), lambda i, k: (i, 0, k))] * 2,
        out_shape=[out, out],
        scratch_shapes=[pltpu.VMEM((tk // LANES, de2, LANES), F32)],
        compiler_params=_params(("parallel", "parallel")),
        name="split_w1",
    )(w1)


def _pack_bf16_pairs(v):
    w = v.shape[1] // 2
    lo = lax.bitcast_convert_type(v[:, :w].astype(BF16).astype(F32), jnp.uint32) >> 16
    hi = lax.bitcast_convert_type(v[:, w:].astype(BF16).astype(F32), jnp.uint32) & jnp.uint32(0xFFFF0000)
    return lo | hi


def _unpack_bf16_pairs(p):
    lo = lax.bitcast_convert_type(p << 16, F32)
    hi = lax.bitcast_convert_type(p & jnp.uint32(0xFFFF0000), F32)
    return lo, hi


def _store_slabs(ref, row0, rows, packed):
    slab = packed.shape[1] // LANES
    for s in range(slab):
        ref[pl.ds(row0 * slab + s, rows, stride=slab), :] = packed[:, s * LANES:(s + 1) * LANES]


def _load_slabs(ref, rows, slab):
    return jnp.concatenate([ref[pl.ds(s, rows, stride=slab), :] for s in range(slab)], axis=1)


def _expert_kernel(be_ref, nu_ref, g0_ref, g1_ref, g2_ref, sp_ref, sc_ref, h2p_ref, w1g_ref, w1l_ref, w2_ref, b1g_ref,
                   b1l_ref, b2_ref, y4p_ref, xbuf, obuf, gsem, ssem, *, n_tokens, t_pad):
    i = pl.program_id(0)
    n_used = nu_ref[0]
    slot = lax.rem(i, 2)
    other = 1 - slot
    gslot = lax.rem(i, 3)
    gslot1 = lax.rem(i + 1, 3)
    gslot2 = lax.rem(i + 2, 3)
    bm = EXPERT_ROWS
    slab = xbuf.shape[1] // bm

    def gather_copy(src_ref, j, buf):
        src = pl.multiple_of(src_ref[0, 0, j], slab)
        return pltpu.make_async_copy(h2p_ref.at[pl.ds(src, slab)],
                                     xbuf.at[buf, pl.ds(j * slab, slab)], gsem.at[buf])

    def scatter_copy(dst_ref, j, buf):
        dst = pl.multiple_of(dst_ref[0, 0, j], slab)
        return pltpu.make_async_copy(obuf.at[buf, pl.ds(j * slab, slab)],
                                     y4p_ref.at[pl.ds(dst, slab)], ssem.at[buf])

    def wait_block(kind, buf):
        if kind == "gather":
            pltpu.make_async_copy(h2p_ref.at[pl.ds(0, bm * slab)], xbuf.at[buf], gsem.at[buf]).wait()
        else:
            pltpu.make_async_copy(obuf.at[buf], y4p_ref.at[pl.ds(0, bm * slab)], ssem.at[buf]).wait()

    @pl.when(i == 0)
    def _():
        obuf[...] = jnp.zeros(obuf.shape, obuf.dtype)
        fills = [pltpu.make_async_copy(obuf.at[1], y4p_ref.at[pl.ds((k * t_pad + n_tokens) * slab + c * bm * slab,
                                                                     bm * slab)], ssem.at[1])
                 for k in range(TOP_K) for c in range((t_pad - n_tokens) // bm)]
        for fill in fills:
            fill.start()
        for fill in fills:
            fill.wait()
        for j in range(bm):
            gather_copy(g0_ref, j, 0).start()
        for j in range(bm):
            gather_copy(g1_ref, j, 1).start()
        spare_row = t_pad + t_pad - 2 * bm
        pltpu.make_async_copy(obuf.at[0], y4p_ref.at[pl.ds(spare_row * slab, bm * slab)], ssem.at[0]).start()

    @pl.when(i < n_used)
    def _():
        nt = (((1,), (1,)), ((), ()))
        wait_block("gather", gslot)
        lo, hi = _unpack_bf16_pairs(_load_slabs(xbuf.at[gslot], bm, slab))
        xb = jnp.concatenate([lo, hi], axis=1).astype(BF16)
        for j in range(bm):
            gather_copy(g2_ref, j, gslot2).start()
        for j in range(bm):
            scatter_copy(sp_ref, j, other).start()
        glu = lax.dot_general(xb, w1g_ref[0], nt, preferred_element_type=F32) + b1g_ref[0]
        lin = lax.dot_general(xb, w1l_ref[0], nt, preferred_element_type=F32) + b1l_ref[0]
        glu = jnp.minimum(glu, SWIGLU_LIMIT)
        lin = jnp.clip(lin, -SWIGLU_LIMIT, SWIGLU_LIMIT)
        act = glu * (1.0 / (1.0 + jnp.exp(-SWIGLU_ALPHA * glu))) * (lin + 1.0)
        y = jnp.dot(act.astype(BF16), w2_ref[0], preferred_element_type=F32) + b2_ref[0]
        wait_block("scatter", slot)
        _store_slabs(obuf.at[slot], 0, bm, _pack_bf16_pairs(y))

    @pl.when(i == n_used - 1)
    def _():
        for j in range(bm):
            scatter_copy(sc_ref, j, slot).start()
        wait_block("scatter", other)
        wait_block("scatter", slot)
        wait_block("gather", gslot1)
        wait_block("gather", gslot2)


def expert_ffn(h2p, ids, block_e, n_used, w1g, w1l, w2, b1g, b1l, b2, n_tokens):
    n_blocks, _, bm = ids.shape
    slab = h2p.shape[0] // n_tokens
    de, d = w1g.shape[1], w1g.shape[2]
    t_pad = padded_tokens(n_tokens)
    src = jnp.minimum(ids >> 2, n_tokens - 1) * slab
    dst = ((ids & 3) * t_pad + (ids >> 2)) * slab
    first = ((t_pad - 2 * bm + jnp.arange(bm, dtype=jnp.int32)) * slab).reshape(1, 1, bm)
    dst = jnp.concatenate([first, dst], axis=0)
    wmap = lambda i, be, nu: (be[i], 0, 0)
    smem_ids = lambda imap: pl.BlockSpec((1, 1, bm), imap, memory_space=pltpu.SMEM)
    grid_spec = pltpu.PrefetchScalarGridSpec(
        num_scalar_prefetch=2,
        grid=(n_blocks,),
        in_specs=[
            smem_ids(lambda i, be, nu: (i, 0, 0)),
            smem_ids(lambda i, be, nu: (jnp.minimum(i + 1, n_blocks - 1), 0, 0)),
            smem_ids(lambda i, be, nu: (jnp.minimum(i + 2, n_blocks - 1), 0, 0)),
            smem_ids(lambda i, be, nu: (i, 0, 0)),
            smem_ids(lambda i, be, nu: (i + 1, 0, 0)),
            pl.BlockSpec(memory_space=pl.ANY),
            pl.BlockSpec((1, de, d), wmap),
            pl.BlockSpec((1, de, d), wmap),
            pl.BlockSpec((1, de, d), wmap),
            pl.BlockSpec((1, 1, de), wmap),
            pl.BlockSpec((1, 1, de), wmap),
            pl.BlockSpec((1, 1, d), wmap),
        ],
        out_specs=pl.BlockSpec(memory_space=pl.ANY),
        scratch_shapes=[
            pltpu.VMEM((3, bm * slab, LANES), jnp.uint32),
            pltpu.VMEM((2, bm * slab, LANES), jnp.uint32),
            pltpu.SemaphoreType.DMA((3,)),
            pltpu.SemaphoreType.DMA((2,)),
        ],
    )
    return pl.pallas_call(
        functools.partial(_expert_kernel, n_tokens=n_tokens, t_pad=t_pad),
        grid_spec=grid_spec,
        out_shape=jax.ShapeDtypeStruct((TOP_K * t_pad * slab, LANES), jnp.uint32),
        compiler_params=_params(("arbitrary",)),
        name="expert_ffn",
    )(block_e, n_used, src, src, src, dst, dst, h2p, w1g, w1l, w2, b1g, b1l, b2)


def _combine_kernel(x1_ref, y0_ref, y1_ref, y2_ref, y3_ref, tw_ref, nw_ref, o_ref, *, normalize):
    tm, d = x1_ref.shape
    slab = y0_ref.shape[0] // tm
    lo_sum = jnp.zeros((tm, d // 2), F32)
    hi_sum = jnp.zeros((tm, d // 2), F32)
    for k, y_ref in enumerate((y0_ref, y1_ref, y2_ref, y3_ref)):
        lo, hi = _unpack_bf16_pairs(_load_slabs(y_ref, tm, slab))
        w = tw_ref[:, k:k + 1]
        lo_sum = lo_sum + w * lo
        hi_sum = hi_sum + w * hi
    acc = x1_ref[...] + jnp.concatenate([lo_sum, hi_sum], axis=1)
    if normalize:
        acc = acc * lax.rsqrt(jnp.mean(acc * acc, axis=-1, keepdims=True) + EPS) * nw_ref[...]
    o_ref[...] = acc


def combine_norm(x1, y4p, top_w, norm_w, normalize):
    t, d = x1.shape
    tm = 256
    t_pad = padded_tokens(t)
    slab = y4p.shape[0] // (TOP_K * t_pad)
    y_specs = [pl.BlockSpec((tm * slab, LANES), functools.partial(lambda i, k: (k * (t_pad // tm) + i, 0), k=k))
               for k in range(TOP_K)]
    return pl.pallas_call(
        functools.partial(_combine_kernel, normalize=normalize),
        grid=(t // tm,),
        in_specs=[pl.BlockSpec((tm, d), lambda i: (i, 0))] + y_specs
        + [pl.BlockSpec((tm, LANES), lambda i: (i, 0)), pl.BlockSpec((1, d), lambda i: (0, 0))],
        out_specs=pl.BlockSpec((tm, d), lambda i: (i, 0)),
        out_shape=jax.ShapeDtypeStruct((t, d), F32),
        compiler_params=_params(("parallel",)),
        name="combine_norm",
    )(x1, y4p, y4p, y4p, y4p, top_w, norm_w)


def routing_layout(top_i, n_tokens):
    n_assign = n_tokens * TOP_K
    bm = EXPERT_ROWS
    n_pad = N_EXPERTS * bm
    n_blocks = (n_assign + n_pad) // bm
    flat_e = top_i[:, :TOP_K].reshape(n_assign)
    counts = jnp.sum(flat_e[:, None] == jnp.arange(N_EXPERTS, dtype=jnp.int32)[None, :], axis=0, dtype=jnp.int32)
    padded = (counts + bm - 1) // bm * bm
    pend = jnp.cumsum(padded)
    n_used = (pend[-1] // bm).astype(jnp.int32)
    block_row = jnp.arange(n_blocks, dtype=jnp.int32) * bm
    block_e = jnp.minimum(jnp.sum(pend[None, :] <= block_row[:, None], axis=1, dtype=jnp.int32), N_EXPERTS - 1)
    spare_pos = jnp.arange(bm, dtype=jnp.int32)[None, :]
    spare_key = jnp.where(spare_pos < (padded - counts)[:, None], jnp.arange(N_EXPERTS, dtype=jnp.int32)[:, None],
                          N_EXPERTS).reshape(n_pad)
    keys = jnp.concatenate([flat_e, spare_key])
    pos_bits = (n_assign + n_pad - 1).bit_length()
    assert (N_EXPERTS + 1) << pos_bits < 2 ** 31
    packed = jnp.sort((keys << pos_bits) | jnp.arange(n_assign + n_pad, dtype=jnp.int32))
    order = packed & ((1 << pos_bits) - 1)
    return order.reshape(n_blocks, 1, bm), block_e, n_used.reshape(1)


def kernel(x, w_in, rel_bias, w_branch_attn, conv_w, conv_b, dt_bias, a_log, d_skip, ssm_norm_w,
           w_branch_ssm, gate_bias, w_out, norm_mix, norm_ffn, router_w, router_b, w1, b1, w2, b2,
           norm_final):
    batch, seq, d = x.shape
    t = batch * seq
    depth = w_in.shape[0]
    n_groups = len(ATTN_GROUPS)
    attn_w = n_groups * GROUP_WIDTH
    d_inner = N_SSM_GROUPS * SSM_HEADS_PER_GROUP * SSM_HEAD_DIM
    n_heads = N_SSM_GROUPS * SSM_HEADS_PER_GROUP
    bc_w = 2 * N_SSM_GROUPS * D_STATE
    xf = x.reshape(t, d)
    for l in range(depth):
        wl = w_in[l]
        o_z = 3 * attn_w
        o_xbc = o_z + d_inner
        o_dt = o_xbc + d_inner + bc_w
        o_gate = o_dt + n_heads
        w_qkv = wl[:, :o_z].astype(BF16)
        w_rest = jnp.concatenate([wl[:, o_z:o_xbc], wl[:, o_gate:], wl[:, o_xbc:o_dt]], axis=1).astype(BF16)
        lane_of_head = (np.arange(n_heads) // SSM_HEADS_PER_GROUP) * LANES + np.arange(n_heads) % SSM_HEADS_PER_GROUP
        w_dt = jnp.zeros((d, N_SSM_GROUPS * LANES), F32).at[:, lane_of_head].set(wl[:, o_dt:o_gate]).astype(BF16)
        dt_bias_p = jnp.zeros((1, N_SSM_GROUPS * LANES), F32).at[0, lane_of_head].set(dt_bias[l].astype(F32))
        a_p = jnp.zeros((1, N_SSM_GROUPS * LANES), F32).at[0, lane_of_head].set(-jnp.exp(a_log[l].astype(F32)))
        d_skip_e = jnp.repeat(d_skip[l].astype(F32), SSM_HEAD_DIM)[None, :]

        g_mix = norm_mix[l].astype(F32)[None, :]
        rest, dt_raw, h = rms_matmul(xf, g_mix, w_rest, w_dt, tn=1792)
        qkv_groups = qkv_project(h, w_qkv, batch, seq)
        bias = jnp.stack([attention_bias(rel_bias, gi) for gi in range(n_groups)])
        attn = dilated_attention(qkv_groups, bias, batch, seq)
        y_ssm = ssd_mixer(rest, dt_raw, conv_w[l].astype(F32), conv_b[l].astype(F32)[None, :], dt_bias_p, a_p,
                          d_skip_e, ssm_norm_w[l].astype(F32)[None, :], batch, seq)
        router_w_p = jnp.zeros((d, LANES), F32).at[:, :N_EXPERTS].set(router_w[l].astype(F32))
        router_b_p = jnp.full((1, LANES), NEG_BIG, F32).at[0, :N_EXPERTS].set(router_b[l].astype(F32))
        x1, h2, top_i, top_w = merge_project(
            attn, y_ssm, rest, xf, w_branch_attn[l].astype(BF16), w_branch_ssm[l].astype(BF16),
            w_out[l].astype(BF16), gate_bias[l].astype(F32)[None, :], norm_ffn[l].astype(F32)[None, :],
            router_w_p, router_b_p)

        ids, block_e, n_used = routing_layout(top_i, t)
        w1g_t, w1l_t = split_w1(w1[l].astype(F32))
        y4p = expert_ffn(h2, ids, block_e, n_used, w1g_t, w1l_t, w2[l].astype(BF16),
                         b1[l][:, None, 0::2].astype(F32), b1[l][:, None, 1::2].astype(F32),
                         b2[l][:, None, :].astype(F32), t)
        xf = combine_norm(x1, y4p, top_w, norm_final.astype(F32)[None, :], normalize=(l == depth - 1))
    return xf.reshape(batch, seq, d)
```

```python
import functools
import math

import jax
import jax.numpy as jnp
import numpy as np
from jax import lax
from jax.experimental import pallas as pl
from jax.experimental.pallas import tpu as pltpu

F32 = jnp.float32
BF16 = jnp.bfloat16

EPS = 1e-5
NEG_BIG = -1e30

HEAD_DIM = 64
ATTN_GROUPS = ((128, 1), (512, 4), (2048, 16))
HEADS_PER_GROUP = 8
GROUP_WIDTH = HEADS_PER_GROUP * HEAD_DIM
ATTN_BLOCK = 128
NUM_BUCKETS = 32
MAX_DISTANCE = 2048
SSM_HEAD_DIM = 64
N_SSM_GROUPS = 4
SSM_HEADS_PER_GROUP = 8
D_STATE = 128
CONV_WIDTH = 4
CHUNK = 128
N_EXPERTS = 32
TOP_K = 4
SWIGLU_LIMIT = 7.0
SWIGLU_ALPHA = 1.702

LANES = 128
V7X_VMEM_BYTES = 64 * 1024 * 1024
VMEM_LIMIT = 48 * 1024 * 1024

ROW_TILE = 1024
MERGE_SUBTILES = 2
EXPERT_ROWS = 256


def padded_tokens(n_tokens):
    return n_tokens + N_EXPERTS * EXPERT_ROWS // TOP_K + 2 * EXPERT_ROWS


def _params(semantics):
    return pltpu.CompilerParams(dimension_semantics=semantics, vmem_limit_bytes=VMEM_LIMIT)


def _rms_matmul_kernel(x_ref, g_ref, w_ref, ws_ref, o_ref, os_ref, hb_ref, h_ref):
    @pl.when(pl.program_id(1) == 0)
    def _():
        x = x_ref[...]
        ms = jnp.mean(x * x, axis=-1, keepdims=True)
        h_ref[...] = (x * lax.rsqrt(ms + EPS) * g_ref[...]).astype(BF16)
        hb_ref[...] = h_ref[...]
        os_ref[...] = jnp.dot(h_ref[...], ws_ref[...], preferred_element_type=F32)

    o_ref[...] = jnp.dot(h_ref[...], w_ref[...], preferred_element_type=F32).astype(o_ref.dtype)


def rms_matmul(x, g, w, w_side, tn):
    t, d = x.shape
    n = w.shape[1]
    ns = w_side.shape[1]
    tm = ROW_TILE
    return pl.pallas_call(
        _rms_matmul_kernel,
        grid=(t // tm, n // tn),
        in_specs=[
            pl.BlockSpec((tm, d), lambda i, j: (i, 0)),
            pl.BlockSpec((1, d), lambda i, j: (0, 0)),
            pl.BlockSpec((d, tn), lambda i, j: (0, j)),
            pl.BlockSpec((d, ns), lambda i, j: (0, 0)),
        ],
        out_specs=[
            pl.BlockSpec((tm, tn), lambda i, j: (i, j)),
            pl.BlockSpec((tm, ns), lambda i, j: (i, 0)),
            pl.BlockSpec((tm, d), lambda i, j: (i, 0)),
        ],
        out_shape=[jax.ShapeDtypeStruct((t, n), BF16), jax.ShapeDtypeStruct((t, ns), F32),
                   jax.ShapeDtypeStruct((t, d), BF16)],
        scratch_shapes=[pltpu.VMEM((tm, d), BF16)],
        compiler_params=_params(("parallel", "arbitrary")),
        name="rms_matmul",
    )(x, g, w, w_side)


def _qkv_kernel(h_ref, w_ref, o0_ref, o1_ref, o2_ref, acc_ref):
    seq = h_ref.shape[0]
    slabs = GROUP_WIDTH // LANES
    n_blocks = seq // ATTN_BLOCK
    for gi, o_ref in enumerate((o0_ref, o1_ref, o2_ref)):
        dil = ATTN_GROUPS[gi][1]
        nb = n_blocks // dil
        acc = jnp.dot(h_ref[...], w_ref[:, gi * GROUP_WIDTH:(gi + 1) * GROUP_WIDTH], preferred_element_type=F32)
        if dil == 1:
            for n in range(n_blocks):
                o_ref[0, 0, n] = acc[n * ATTN_BLOCK:(n + 1) * ATTN_BLOCK, :].astype(o_ref.dtype)
            continue
        for s in range(slabs):
            acc_ref[s] = acc[:, s * LANES:(s + 1) * LANES]
        for r in range(dil):
            for n in range(nb):
                for s in range(slabs):
                    rows = pl.ds(r + n * ATTN_BLOCK * dil, ATTN_BLOCK, stride=dil)
                    o_ref[0, 0, r * nb + n, :, s * LANES:(s + 1) * LANES] = acc_ref[s, rows, :].astype(o_ref.dtype)


def qkv_project(h, w_qkv, batch, seq):
    t, d = h.shape
    n_groups = len(ATTN_GROUPS)
    tn = n_groups * GROUP_WIDTH
    out_shapes, out_specs = [], []
    for _ in ATTN_GROUPS:
        shape = (3, batch, seq // ATTN_BLOCK, ATTN_BLOCK, GROUP_WIDTH)
        out_shapes.append(jax.ShapeDtypeStruct(shape, BF16))
        out_specs.append(pl.BlockSpec((1, 1) + shape[2:], lambda w, b: (w, b, 0, 0, 0)))
    return pl.pallas_call(
        _qkv_kernel,
        grid=(3, batch),
        in_specs=[pl.BlockSpec((seq, d), lambda w, b: (b, 0)), pl.BlockSpec((d, tn), lambda w, b: (0, w))],
        out_specs=out_specs,
        out_shape=out_shapes,
        scratch_shapes=[pltpu.VMEM((GROUP_WIDTH // LANES, seq, LANES), F32)],
        compiler_params=_params(("parallel", "parallel")),
        name="qkv_project",
    )(h, w_qkv)


def _attn_kernel(q0_ref, q1_ref, q2_ref, bias_ref, o_ref, out_ref, lse_ref):
    heads = LANES // HEAD_DIM
    scale = jnp.asarray(HEAD_DIM ** -0.5, BF16)
    n_blocks = q0_ref.shape[2]
    qk = (((2,), (2,)), ((0,), (0,)))
    pv = (((2,), (1,)), ((0,), (0,)))
    blk = lax.broadcasted_iota(jnp.int32, (n_blocks, 1, 1), 0)
    lane = lax.broadcasted_iota(jnp.int32, (1, 1, LANES), 2)

    def shifted(x):
        return jnp.concatenate([x[n_blocks - 1:], x[:n_blocks - 1]], axis=0)

    for g, ref in enumerate((q0_ref, q1_ref, q2_ref)):
        dil = ATTN_GROUPS[g][1]
        nb = n_blocks // dil
        q2h = ref[0, 0] * scale
        keys, vals = ref[1, 0], ref[2, 0]
        if nb > 1:
            keys = jnp.concatenate([shifted(keys), keys], axis=1)
            vals = jnp.concatenate([shifted(vals), vals], axis=1)
        n_keys = keys.shape[1]
        ones = jnp.ones((n_keys, LANES), BF16)
        pvs, dens, maxs = [], [], []
        for h in range(heads):
            in_head = (lane >= h * HEAD_DIM) & (lane < (h + 1) * HEAD_DIM)
            q = jnp.where(in_head, q2h, jnp.zeros_like(q2h))
            s = lax.dot_general(q, keys, qk, preferred_element_type=F32)
            s = s + bias_ref[g, h, :, 2 * ATTN_BLOCK - n_keys:][None]
            if nb > 1:
                key_is_prev = lax.broadcasted_iota(jnp.int32, (1, 1, n_keys), 2) < ATTN_BLOCK
                s = jnp.where((blk % nb == 0) & key_is_prev, NEG_BIG, s)
            m = jnp.max(s, axis=-1, keepdims=True)
            p = jnp.exp(s - m).astype(BF16)
            pvs.append(lax.dot_general(p, vals, pv, preferred_element_type=F32))
            dens.append(jnp.dot(p.reshape(n_blocks * ATTN_BLOCK, n_keys), ones,
                                preferred_element_type=F32).reshape(n_blocks, ATTN_BLOCK, LANES))
            maxs.append(m)
        first = lane < HEAD_DIM
        den = jnp.where(first, dens[0], dens[1])
        o2 = jnp.where(first, pvs[0], pvs[1]) / den
        l2 = jnp.where(first, maxs[0], maxs[1]) + jnp.log(den)
        for r in range(dil):
            for n in range(nb):
                start = r + n * ATTN_BLOCK * dil
                rows = pl.ds(start, ATTN_BLOCK) if dil == 1 else pl.ds(start, ATTN_BLOCK, stride=dil)
                out_ref[g, rows, :] = o2[r * nb + n]
                lse_ref[g, rows, :] = l2[r * nb + n]

    l0, l1, l2 = lse_ref[0], lse_ref[1], lse_ref[2]
    lm = jnp.maximum(jnp.maximum(l0, l1), l2)
    e0, e1, e2 = jnp.exp(l0 - lm), jnp.exp(l1 - lm), jnp.exp(l2 - lm)
    mixed = (e0 * out_ref[0] + e1 * out_ref[1] + e2 * out_ref[2]) / (e0 + e1 + e2)
    o_ref[...] = mixed.astype(o_ref.dtype)


def dilated_attention(qkv_groups, bias, batch, seq):
    heads = LANES // HEAD_DIM
    in_specs = [pl.BlockSpec((3, 1) + a.shape[2:4] + (LANES,), lambda b, hp: (0, b, 0, 0, hp)) for a in qkv_groups]
    in_specs.append(pl.BlockSpec((len(ATTN_GROUPS), heads, ATTN_BLOCK, 2 * ATTN_BLOCK), lambda b, hp: (0, hp, 0, 0)))
    return pl.pallas_call(
        _attn_kernel,
        grid=(batch, GROUP_WIDTH // LANES),
        in_specs=in_specs,
        out_specs=pl.BlockSpec((seq, LANES), lambda b, hp: (b, hp)),
        out_shape=jax.ShapeDtypeStruct((batch * seq, GROUP_WIDTH), BF16),
        scratch_shapes=[pltpu.VMEM((len(ATTN_GROUPS), seq, LANES), F32),
                        pltpu.VMEM((len(ATTN_GROUPS), seq, LANES), F32)],
        compiler_params=_params(("parallel", "parallel")),
        name="dilated_attn",
    )(*qkv_groups, bias)


def attention_bias(rel_bias, gi):
    window, dil = ATTN_GROUPS[gi]
    w_sub = window // dil
    q_idx = np.arange(ATTN_BLOCK)[:, None]
    k_idx = np.arange(2 * ATTN_BLOCK)[None, :]
    delta = q_idx + ATTN_BLOCK - k_idx
    in_band = (delta >= 0) & (delta <= w_sub)
    dist = np.clip(delta, 0, w_sub) * dil
    max_exact = NUM_BUCKETS // 2
    nf = np.maximum(dist, max_exact).astype(np.float32)
    large = max_exact + (np.log(nf / max_exact) / math.log(MAX_DISTANCE / max_exact)
                         * (NUM_BUCKETS - max_exact)).astype(np.int32)
    large = np.minimum(large, NUM_BUCKETS - 1)
    bucket = np.where(dist < max_exact, dist, large)
    table = rel_bias[:, gi * HEADS_PER_GROUP:(gi + 1) * HEADS_PER_GROUP].astype(F32)
    onehot = (bucket.reshape(-1, 1) == np.arange(NUM_BUCKETS)[None, :]).astype(np.float32)
    bias = jnp.einsum('bh,nb->hn', table, jnp.asarray(onehot), precision=lax.Precision.HIGHEST)
    bias = bias.reshape(HEADS_PER_GROUP, ATTN_BLOCK, 2 * ATTN_BLOCK)
    return jnp.where(in_band[None], bias, NEG_BIG)


def _silu(v):
    half = 0.5 * v
    return half + half * jnp.tanh(half)


SSD_CHUNKS_PER_STEP = 4
CONV_TAIL = 16


def _conv_silu(ext_ref, u_ref, w_ref, b_ref, first):
    rows = u_ref.shape[0]
    ext_rows = ext_ref.shape[0]

    if first is not None:
        @pl.when(first)
        def _():
            ext_ref[0:CONV_TAIL, :] = jnp.zeros((CONV_TAIL, ext_ref.shape[1]), ext_ref.dtype)

    u = u_ref[...]
    ext_ref[CONV_TAIL:, :] = u
    taps = CONV_WIDTH - 1
    out_row = lax.broadcasted_iota(jnp.int32, (rows, ext_rows), 0)
    src_row = lax.broadcasted_iota(jnp.int32, (rows, ext_rows), 1)
    shift_mat = jnp.concatenate([(src_row == out_row + (CONV_TAIL - k)).astype(BF16) for k in range(1, CONV_WIDTH)],
                                axis=0)
    shifted = jnp.dot(shift_mat, ext_ref[...], preferred_element_type=F32)
    acc = u.astype(F32) * w_ref[taps:CONV_WIDTH, :] + b_ref[...]
    for k in range(1, CONV_WIDTH):
        acc = acc + shifted[(k - 1) * rows:k * rows, :] * w_ref[taps - k:CONV_WIDTH - k, :]
    ext_ref[0:CONV_TAIL, :] = ext_ref[rows:, :]
    return _silu(acc)


def _ssd_kernel(x_ref, bc_ref, z_ref, dt_ref, cwx_ref, cbx_ref, cwbc_ref, cbbc_ref, dtb_ref, a_ref,
                dskip_ref, nw_ref, expand_ref, o_ref, extx_ref, extbc_ref, state_ref):
    first = pl.program_id(1) == 0
    gw = SSM_HEADS_PER_GROUP * SSM_HEAD_DIM

    @pl.when(first)
    def _():
        state_ref[...] = jnp.zeros(state_ref.shape, F32)

    row = lax.broadcasted_iota(jnp.int32, (CHUNK, CHUNK), 0)
    colm = lax.broadcasted_iota(jnp.int32, (CHUNK, CHUNK), 1)
    tril = row >= colm
    tril_b = tril.astype(BF16)
    head_rows = 8
    eye_b = (lax.broadcasted_iota(jnp.int32, (head_rows, CHUNK), 0)
             == lax.broadcasted_iota(jnp.int32, (head_rows, CHUNK), 1)).astype(BF16)
    nt = (((1,), (1,)), ((), ()))
    tn = (((0,), (0,)), ((), ()))

    def split3(v):
        p0 = v.astype(BF16)
        r1 = v - p0.astype(F32)
        p1 = r1.astype(BF16)
        p2 = (r1 - p1.astype(F32)).astype(BF16)
        return p0, p1, p2

    def select_rows(mat01, v):
        return sum(jnp.dot(mat01, p, preferred_element_type=F32) for p in split3(v))

    def select_cols(v, mat01):
        return sum(jnp.dot(p, mat01, preferred_element_type=F32) for p in split3(v))

    expand = expand_ref[...]
    for sub, g in [(sub, g) for sub in range(x_ref.shape[0] // CHUNK) for g in range(N_SSM_GROUPS)]:
        rws = pl.ds(sub * CHUNK, CHUNK)
        if g == 0:
            sub_first = first if sub == 0 else None
            xs_all = _conv_silu(extx_ref, x_ref.at[rws], cwx_ref, cbx_ref, sub_first)
            bc_all = _conv_silu(extbc_ref, bc_ref.at[rws], cwbc_ref, cbbc_ref, sub_first)
        lanes = slice(g * LANES, (g + 1) * LANES)
        ch = slice(g * gw, (g + 1) * gw)
        v = dt_ref[rws, lanes] + dtb_ref[:, lanes]
        dt = jnp.maximum(v, 0.0) + jnp.log1p(jnp.exp(-jnp.abs(v)))
        a_d = dt * a_ref[:, lanes]
        acs = select_rows(tril_b, a_d)
        acs_t = sum(lax.dot_general(eye_b, p, nt, preferred_element_type=F32) for p in split3(acs))
        last = acs[CHUNK - 1:CHUNK, :]
        dt_e = select_cols(dt, expand)
        eacs_e = select_cols(jnp.exp(acs), expand)
        edec_e = select_cols(jnp.exp(last - acs), expand)
        elast_e = eacs_e[CHUNK - 1:CHUNK, :]

        xs = xs_all[:, ch]
        bm = bc_all[:, lanes].astype(BF16)
        cm = bc_all[:, N_SSM_GROUPS * D_STATE + g * D_STATE:N_SSM_GROUPS * D_STATE + (g + 1) * D_STATE].astype(BF16)
        x_d = xs * dt_e
        cb = lax.dot_general(cm, bm, nt, preferred_element_type=F32)

        prev = state_ref[g]
        y = jnp.dot(cm, prev.astype(BF16), preferred_element_type=F32) * eacs_e
        x_d16 = x_d.astype(BF16)
        parts = []
        for j in range(SSM_HEADS_PER_GROUP):
            seg = acs[:, j:j + 1] - acs_t[j:j + 1, :]
            m_h = (cb * jnp.exp(jnp.where(tril, seg, NEG_BIG))).astype(BF16)
            parts.append(jnp.dot(m_h, x_d16[:, j * SSM_HEAD_DIM:(j + 1) * SSM_HEAD_DIM],
                                 preferred_element_type=F32))
        y = y + jnp.concatenate(parts, axis=1)

        xw = (x_d * edec_e).astype(BF16)
        state_ref[g] = prev * elast_e + lax.dot_general(bm, xw, tn, preferred_element_type=F32)

        y = y + dskip_ref[:, ch] * xs
        y = y * _silu(z_ref[rws, ch].astype(F32))
        y = y * lax.rsqrt(jnp.mean(y * y, axis=-1, keepdims=True) + EPS)
        o_ref[rws, ch] = (y * nw_ref[:, ch]).astype(o_ref.dtype)


def ssd_mixer(rest, dt_raw, conv_w, conv_b, dt_bias_p, a_p, d_skip_e, norm_w, batch, seq):
    t = rest.shape[0]
    d_inner = N_SSM_GROUPS * SSM_HEADS_PER_GROUP * SSM_HEAD_DIM
    bc_w = 2 * N_SSM_GROUPS * D_STATE
    rows = CHUNK * SSD_CHUNKS_PER_STEP
    nc = seq // rows
    gw = SSM_HEADS_PER_GROUP * SSM_HEAD_DIM
    expand = (np.arange(LANES)[:, None] == (np.arange(gw)[None, :] // SSM_HEAD_DIM)).astype(np.float32)
    rowmap = lambda b, c: (b * nc + c, 0)
    const = lambda b, c: (0, 0)
    return pl.pallas_call(
        _ssd_kernel,
        grid=(batch, nc),
        in_specs=[
            pl.BlockSpec((rows, d_inner), lambda b, c: (b * nc + c, 2)),
            pl.BlockSpec((rows, bc_w), lambda b, c: (b * nc + c, 6)),
            pl.BlockSpec((rows, d_inner), rowmap),
            pl.BlockSpec((rows, N_SSM_GROUPS * LANES), rowmap),
            pl.BlockSpec((CONV_WIDTH, d_inner), const),
            pl.BlockSpec((1, d_inner), const),
            pl.BlockSpec((CONV_WIDTH, bc_w), const),
            pl.BlockSpec((1, bc_w), const),
            pl.BlockSpec((1, N_SSM_GROUPS * LANES), const),
            pl.BlockSpec((1, N_SSM_GROUPS * LANES), const),
            pl.BlockSpec((1, d_inner), const),
            pl.BlockSpec((1, d_inner), const),
            pl.BlockSpec((LANES, gw), const),
        ],
        out_specs=pl.BlockSpec((rows, d_inner), rowmap),
        out_shape=jax.ShapeDtypeStruct((t, d_inner), BF16),
        scratch_shapes=[
            pltpu.VMEM((CONV_TAIL + CHUNK, d_inner), BF16),
            pltpu.VMEM((CONV_TAIL + CHUNK, bc_w), BF16),
            pltpu.VMEM((N_SSM_GROUPS, D_STATE, gw), F32),
        ],
        compiler_params=_params(("parallel", "arbitrary")),
        name="ssd_mixer",
    )(rest, rest, rest, dt_raw, conv_w[:, :d_inner], conv_b[:, :d_inner], conv_w[:, d_inner:],
      conv_b[:, d_inner:], dt_bias_p, a_p, d_skip_e, norm_w, jnp.asarray(expand, dtype=BF16))


def _merge_kernel(at_ref, ys_ref, gl_ref, x_ref, wa_ref, ws_ref,
                  wo_ref, gb_ref, nf_ref, rw_ref, rb_ref, x1_ref, h2_ref, ti_ref, tw_ref):
    hi = lax.Precision.HIGHEST
    d = x_ref.shape[1]
    sub = x_ref.shape[0] // MERGE_SUBTILES
    for part in range(MERGE_SUBTILES):
        rows = slice(part * sub, (part + 1) * sub)
        y_attn = jnp.dot(at_ref[rows, :], wa_ref[...], preferred_element_type=F32)
        y_ssm = jnp.dot(ys_ref[rows, :], ws_ref[...], preferred_element_type=F32)
        gv = gl_ref[rows, :].astype(F32) + gb_ref[...]
        gates = 1.0 / (1.0 + jnp.exp(-gv))
        merged = gates[:, :d] * y_attn + gates[:, d:] * y_ssm
        x1 = x_ref[rows, :] + jnp.dot(merged.astype(BF16), wo_ref[...], preferred_element_type=F32)
        x1_ref[rows, :] = x1
        h2 = x1 * lax.rsqrt(jnp.mean(x1 * x1, axis=-1, keepdims=True) + EPS) * nf_ref[...]
        _store_slabs(h2_ref, part * sub, sub, _pack_bf16_pairs(h2))
        logits = jnp.dot(h2, rw_ref[...], precision=hi, preferred_element_type=F32) + rb_ref[...]
        lane = lax.broadcasted_iota(jnp.int32, logits.shape, 1)
        top_i = jnp.zeros(logits.shape, jnp.int32)
        top_v = jnp.full(logits.shape, NEG_BIG, F32)
        work = logits
        for k in range(TOP_K):
            m = jnp.max(work, axis=-1, keepdims=True)
            idx = jnp.min(jnp.where(work == m, lane, LANES), axis=-1, keepdims=True)
            top_i = jnp.where(lane == k, idx, top_i)
            top_v = jnp.where(lane == k, m, top_v)
            work = jnp.where(lane == idx, NEG_BIG * 2.0, work)
        ev = jnp.exp(top_v - jnp.max(top_v, axis=-1, keepdims=True))
        ti_ref[rows, :] = top_i
        tw_ref[rows, :] = ev / jnp.sum(ev, axis=-1, keepdims=True)


def merge_project(attn, y_ssm, rest, x, wa, ws, wo, gate_bias, norm_ffn, router_w_p, router_b_p):
    t, d = x.shape
    slab = d // 2 // LANES
    tm = 256 * MERGE_SUBTILES
    d_inner = y_ssm.shape[1]
    rowmap = lambda i: (i, 0)
    const = lambda i: (0, 0)
    full = lambda a: pl.BlockSpec(a.shape, const)
    args = [attn, y_ssm, rest, x, wa, ws, wo, gate_bias, norm_ffn, router_w_p, router_b_p]
    in_specs = (
        [pl.BlockSpec((tm, GROUP_WIDTH), rowmap),
         pl.BlockSpec((tm, d_inner), rowmap),
         pl.BlockSpec((tm, 2 * d), lambda i: (i, 1)),
         pl.BlockSpec((tm, d), rowmap)]
        + [full(a) for a in args[4:]]
    )
    return pl.pallas_call(
        _merge_kernel,
        grid=(t // tm,),
        in_specs=in_specs,
        out_specs=[pl.BlockSpec((tm, d), rowmap), pl.BlockSpec((tm * slab, LANES), rowmap),
                   pl.BlockSpec((tm, LANES), rowmap), pl.BlockSpec((tm, LANES), rowmap)],
        out_shape=[jax.ShapeDtypeStruct((t, d), F32), jax.ShapeDtypeStruct((t * slab, LANES), jnp.uint32),
                   jax.ShapeDtypeStruct((t, LANES), jnp.int32), jax.ShapeDtypeStruct((t, LANES), F32)],
        compiler_params=_params(("parallel",)),
        name="merge_project",
    )(*args)


def _split_w1_kernel(w_ref, g_ref, l_ref, t_ref):
    de = g_ref.shape[1]
    for s in range(t_ref.shape[0]):
        cols = slice(s * LANES, (s + 1) * LANES)
        t_ref[s] = w_ref[0, cols, :].T
        g_ref[0, :, cols] = t_ref[s, pl.ds(0, de, stride=2), :].astype(g_ref.dtype)
        l_ref[0, :, cols] = t_ref[s, pl.ds(1, de, stride=2), :].astype(l_ref.dtype)


def split_w1(w1):
    e, d, de2 = w1.shape
    de = de2 // 2
    tk = 512
    out = jax.ShapeDtypeStruct((e, de, d), BF16)
    return pl.pallas_call(
        _split_w1_kernel,
        grid=(e, d // tk),
        in_specs=[pl.BlockSpec((1, tk, de2), lambda i, k: (i, k, 0))],
        out_specs=[pl.BlockSpec((1, de, tk), lambda i, k: (i, 0, k))] * 2,
        out_shape=[out, out],
        scratch_shapes=[pltpu.VMEM((tk // LANES, de2, LANES), F32)],
        compiler_params=_params(("parallel", "parallel")),
        name="split_w1",
    )(w1)


def _pack_bf16_pairs(v):
    w = v.shape[1] // 2
    lo = lax.bitcast_convert_type(v[:, :w].astype(BF16).astype(F32), jnp.uint32) >> 16
    hi = lax.bitcast_convert_type(v[:, w:].astype(BF16).astype(F32), jnp.uint32) & jnp.uint32(0xFFFF0000)
    return lo | hi


def _unpack_bf16_pairs(p):
    lo = lax.bitcast_convert_type(p << 16, F32)
    hi = lax.bitcast_convert_type(p & jnp.uint32(0xFFFF0000), F32)
    return lo, hi


def _store_slabs(ref, row0, rows, packed):
    slab = packed.shape[1] // LANES
    for s in range(slab):
        ref[pl.ds(row0 * slab + s, rows, stride=slab), :] = packed[:, s * LANES:(s + 1) * LANES]


def _load_slabs(ref, rows, slab):
    return jnp.concatenate([ref[pl.ds(s, rows, stride=slab), :] for s in range(slab)], axis=1)


def _expert_kernel(be_ref, nu_ref, g0_ref, g1_ref, g2_ref, sp_ref, sc_ref, h2p_ref, w1g_ref, w1l_ref, w2_ref, b1g_ref,
                   b1l_ref, b2_ref, y4p_ref, xbuf, obuf, gsem, ssem, *, n_tokens, t_pad):
    i = pl.program_id(0)
    n_used = nu_ref[0]
    slot = lax.rem(i, 2)
    other = 1 - slot
    gslot = lax.rem(i, 3)
    gslot1 = lax.rem(i + 1, 3)
    gslot2 = lax.rem(i + 2, 3)
    bm = EXPERT_ROWS
    slab = xbuf.shape[1] // bm

    def gather_copy(src_ref, j, buf):
        src = pl.multiple_of(src_ref[0, 0, j], slab)
        return pltpu.make_async_copy(h2p_ref.at[pl.ds(src, slab)],
                                     xbuf.at[buf, pl.ds(j * slab, slab)], gsem.at[buf])

    def scatter_copy(dst_ref, j, buf):
        dst = pl.multiple_of(dst_ref[0, 0, j], slab)
        return pltpu.make_async_copy(obuf.at[buf, pl.ds(j * slab, slab)],
                                     y4p_ref.at[pl.ds(dst, slab)], ssem.at[buf])

    def wait_block(kind, buf):
        if kind == "gather":
            pltpu.make_async_copy(h2p_ref.at[pl.ds(0, bm * slab)], xbuf.at[buf], gsem.at[buf]).wait()
        else:
            pltpu.make_async_copy(obuf.at[buf], y4p_ref.at[pl.ds(0, bm * slab)], ssem.at[buf]).wait()

    @pl.when(i == 0)
    def _():
        obuf[...] = jnp.zeros(obuf.shape, obuf.dtype)
        fills = [pltpu.make_async_copy(obuf.at[1], y4p_ref.at[pl.ds((k * t_pad + n_tokens) * slab + c * bm * slab,
                                                                     bm * slab)], ssem.at[1])
                 for k in range(TOP_K) for c in range((t_pad - n_tokens) // bm)]
        for fill in fills:
            fill.start()
        for fill in fills:
            fill.wait()
        for j in range(bm):
            gather_copy(g0_ref, j, 0).start()
        for j in range(bm):
            gather_copy(g1_ref, j, 1).start()
        spare_row = t_pad + t_pad - 2 * bm
        pltpu.make_async_copy(obuf.at[0], y4p_ref.at[pl.ds(spare_row * slab, bm * slab)], ssem.at[0]).start()

    @pl.when(i < n_used)
    def _():
        nt = (((1,), (1,)), ((), ()))
        wait_block("gather", gslot)
        lo, hi = _unpack_bf16_pairs(_load_slabs(xbuf.at[gslot], bm, slab))
        xb = jnp.concatenate([lo, hi], axis=1).astype(BF16)
        for j in range(bm):
            gather_copy(g2_ref, j, gslot2).start()
        for j in range(bm):
            scatter_copy(sp_ref, j, other).start()
        glu = lax.dot_general(xb, w1g_ref[0], nt, preferred_element_type=F32) + b1g_ref[0]
        lin = lax.dot_general(xb, w1l_ref[0], nt, preferred_element_type=F32) + b1l_ref[0]
        glu = jnp.minimum(glu, SWIGLU_LIMIT)
        lin = jnp.clip(lin, -SWIGLU_LIMIT, SWIGLU_LIMIT)
        act = glu * (1.0 / (1.0 + jnp.exp(-SWIGLU_ALPHA * glu))) * (lin + 1.0)
        y = jnp.dot(act.astype(BF16), w2_ref[0], preferred_element_type=F32) + b2_ref[0]
        wait_block("scatter", slot)
        _store_slabs(obuf.at[slot], 0, bm, _pack_bf16_pairs(y))

    @pl.when(i == n_used - 1)
    def _():
        for j in range(bm):
            scatter_copy(sc_ref, j, slot).start()
        wait_block("scatter", other)
        wait_block("scatter", slot)
        wait_block("gather", gslot1)
        wait_block("gather", gslot2)


def expert_ffn(h2p, ids, block_e, n_used, w1g, w1l, w2, b1g, b1l, b2, n_tokens):
    n_blocks, _, bm = ids.shape
    slab = h2p.shape[0] // n_tokens
    de, d = w1g.shape[1], w1g.shape[2]
    t_pad = padded_tokens(n_tokens)
    src = jnp.minimum(ids >> 2, n_tokens - 1) * slab
    dst = ((ids & 3) * t_pad + (ids >> 2)) * slab
    first = ((t_pad - 2 * bm + jnp.arange(bm, dtype=jnp.int32)) * slab).reshape(1, 1, bm)
    dst = jnp.concatenate([first, dst], axis=0)
    wmap = lambda i, be, nu: (be[i], 0, 0)
    smem_ids = lambda imap: pl.BlockSpec((1, 1, bm), imap, memory_space=pltpu.SMEM)
    grid_spec = pltpu.PrefetchScalarGridSpec(
        num_scalar_prefetch=2,
        grid=(n_blocks,),
        in_specs=[
            smem_ids(lambda i, be, nu: (i, 0, 0)),
            smem_ids(lambda i, be, nu: (jnp.minimum(i + 1, n_blocks - 1), 0, 0)),
            smem_ids(lambda i, be, nu: (jnp.minimum(i + 2, n_blocks - 1), 0, 0)),
            smem_ids(lambda i, be, nu: (i, 0, 0)),
            smem_ids(lambda i, be, nu: (i + 1, 0, 0)),
            pl.BlockSpec(memory_space=pl.ANY),
            pl.BlockSpec((1, de, d), wmap),
            pl.BlockSpec((1, de, d), wmap),
            pl.BlockSpec((1, de, d), wmap),
            pl.BlockSpec((1, 1, de), wmap),
            pl.BlockSpec((1, 1, de), wmap),
            pl.BlockSpec((1, 1, d), wmap),
        ],
        out_specs=pl.BlockSpec(memory_space=pl.ANY),
        scratch_shapes=[
            pltpu.VMEM((3, bm * slab, LANES), jnp.uint32),
            pltpu.VMEM((2, bm * slab, LANES), jnp.uint32),
            pltpu.SemaphoreType.DMA((3,)),
            pltpu.SemaphoreType.DMA((2,)),
        ],
    )
    return pl.pallas_call(
        functools.partial(_expert_kernel, n_tokens=n_tokens, t_pad=t_pad),
        grid_spec=grid_spec,
        out_shape=jax.ShapeDtypeStruct((TOP_K * t_pad * slab, LANES), jnp.uint32),
        compiler_params=_params(("arbitrary",)),
        name="expert_ffn",
    )(block_e, n_used, src, src, src, dst, dst, h2p, w1g, w1l, w2, b1g, b1l, b2)


def _combine_kernel(x1_ref, y0_ref, y1_ref, y2_ref, y3_ref, tw_ref, nw_ref, o_ref, *, normalize):
    tm, d = x1_ref.shape
    slab = y0_ref.shape[0] // tm
    lo_sum = jnp.zeros((tm, d // 2), F32)
    hi_sum = jnp.zeros((tm, d // 2), F32)
    for k, y_ref in enumerate((y0_ref, y1_ref, y2_ref, y3_ref)):
        lo, hi = _unpack_bf16_pairs(_load_slabs(y_ref, tm, slab))
        w = tw_ref[:, k:k + 1]
        lo_sum = lo_sum + w * lo
        hi_sum = hi_sum + w * hi
    acc = x1_ref[...] + jnp.concatenate([lo_sum, hi_sum], axis=1)
    if normalize:
        acc = acc * lax.rsqrt(jnp.mean(acc * acc, axis=-1, keepdims=True) + EPS) * nw_ref[...]
    o_ref[...] = acc


def combine_norm(x1, y4p, top_w, norm_w, normalize):
    t, d = x1.shape
    tm = 256
    t_pad = padded_tokens(t)
    slab = y4p.shape[0] // (TOP_K * t_pad)
    y_specs = [pl.BlockSpec((tm * slab, LANES), functools.partial(lambda i, k: (k * (t_pad // tm) + i, 0), k=k))
               for k in range(TOP_K)]
    return pl.pallas_call(
        functools.partial(_combine_kernel, normalize=normalize),
        grid=(t // tm,),
        in_specs=[pl.BlockSpec((tm, d), lambda i: (i, 0))] + y_specs
        + [pl.BlockSpec((tm, LANES), lambda i: (i, 0)), pl.BlockSpec((1, d), lambda i: (0, 0))],
        out_specs=pl.BlockSpec((tm, d), lambda i: (i, 0)),
        out_shape=jax.ShapeDtypeStruct((t, d), F32),
        compiler_params=_params(("parallel",)),
        name="combine_norm",
    )(x1, y4p, y4p, y4p, y4p, top_w, norm_w)


def routing_layout(top_i, n_tokens):
    n_assign = n_tokens * TOP_K
    bm = EXPERT_ROWS
    n_pad = N_EXPERTS * bm
    n_blocks = (n_assign + n_pad) // bm
    flat_e = top_i[:, :TOP_K].reshape(n_assign)
    counts = jnp.sum(flat_e[:, None] == jnp.arange(N_EXPERTS, dtype=jnp.int32)[None, :], axis=0, dtype=jnp.int32)
    padded = (counts + bm - 1) // bm * bm
    pend = jnp.cumsum(padded)
    n_used = (pend[-1] // bm).astype(jnp.int32)
    block_row = jnp.arange(n_blocks, dtype=jnp.int32) * bm
    block_e = jnp.minimum(jnp.sum(pend[None, :] <= block_row[:, None], axis=1, dtype=jnp.int32), N_EXPERTS - 1)
    spare_pos = jnp.arange(bm, dtype=jnp.int32)[None, :]
    spare_key = jnp.where(spare_pos < (padded - counts)[:, None], jnp.arange(N_EXPERTS, dtype=jnp.int32)[:, None],
                          N_EXPERTS).reshape(n_pad)
    keys = jnp.concatenate([flat_e, spare_key])
    pos_bits = (n_assign + n_pad - 1).bit_length()
    assert (N_EXPERTS + 1) << pos_bits < 2 ** 31
    packed = jnp.sort((keys << pos_bits) | jnp.arange(n_assign + n_pad, dtype=jnp.int32))
    order = packed & ((1 << pos_bits) - 1)
    return order.reshape(n_blocks, 1, bm), block_e, n_used.reshape(1)


def kernel(x, w_in, rel_bias, w_branch_attn, conv_w, conv_b, dt_bias, a_log, d_skip, ssm_norm_w,
           w_branch_ssm, gate_bias, w_out, norm_mix, norm_ffn, router_w, router_b, w1, b1, w2, b2,
           norm_final):
    batch, seq, d = x.shape
    t = batch * seq
    depth = w_in.shape[0]
    n_groups = len(ATTN_GROUPS)
    attn_w = n_groups * GROUP_WIDTH
    d_inner = N_SSM_GROUPS * SSM_HEADS_PER_GROUP * SSM_HEAD_DIM
    n_heads = N_SSM_GROUPS * SSM_HEADS_PER_GROUP
    bc_w = 2 * N_SSM_GROUPS * D_STATE
    xf = x.reshape(t, d)
    for l in range(depth):
        wl = w_in[l]
        o_z = 3 * attn_w
        o_xbc = o_z + d_inner
        o_dt = o_xbc + d_inner + bc_w
        o_gate = o_dt + n_heads
        w_qkv = wl[:, :o_z].astype(BF16)
        w_rest = jnp.concatenate([wl[:, o_z:o_xbc], wl[:, o_gate:], wl[:, o_xbc:o_dt]], axis=1).astype(BF16)
        lane_of_head = (np.arange(n_heads) // SSM_HEADS_PER_GROUP) * LANES + np.arange(n_heads) % SSM_HEADS_PER_GROUP
        w_dt = jnp.zeros((d, N_SSM_GROUPS * LANES), F32).at[:, lane_of_head].set(wl[:, o_dt:o_gate]).astype(BF16)
        dt_bias_p = jnp.zeros((1, N_SSM_GROUPS * LANES), F32).at[0, lane_of_head].set(dt_bias[l].astype(F32))
        a_p = jnp.zeros((1, N_SSM_GROUPS * LANES), F32).at[0, lane_of_head].set(-jnp.exp(a_log[l].astype(F32)))
        d_skip_e = jnp.repeat(d_skip[l].astype(F32), SSM_HEAD_DIM)[None, :]

        g_mix = norm_mix[l].astype(F32)[None, :]
        rest, dt_raw, h = rms_matmul(xf, g_mix, w_rest, w_dt, tn=1792)
        qkv_groups = qkv_project(h, w_qkv, batch, seq)
        bias = jnp.stack([attention_bias(rel_bias, gi) for gi in range(n_groups)])
        attn = dilated_attention(qkv_groups, bias, batch, seq)
        y_ssm = ssd_mixer(rest, dt_raw, conv_w[l].astype(F32), conv_b[l].astype(F32)[None, :], dt_bias_p, a_p,
                          d_skip_e, ssm_norm_w[l].astype(F32)[None, :], batch, seq)
        router_w_p = jnp.zeros((d, LANES), F32).at[:, :N_EXPERTS].set(router_w[l].astype(F32))
        router_b_p = jnp.full((1, LANES), NEG_BIG, F32).at[0, :N_EXPERTS].set(router_b[l].astype(F32))
        x1, h2, top_i, top_w = merge_project(
            attn, y_ssm, rest, xf, w_branch_attn[l].astype(BF16), w_branch_ssm[l].astype(BF16),
            w_out[l].astype(BF16), gate_bias[l].astype(F32)[None, :], norm_ffn[l].astype(F32)[None, :],
            router_w_p, router_b_p)

        ids, block_e, n_used = routing_layout(top_i, t)
        w1g_t, w1l_t = split_w1(w1[l].astype(F32))
        y4p = expert_ffn(h2, ids, block_e, n_used, w1g_t, w1l_t, w2[l].astype(BF16),
                         b1[l][:, None, 0::2].astype(F32), b1[l][:, None, 1::2].astype(F32),
                         b2[l][:, None, :].astype(F32), t)
        xf = combine_norm(x1, y4p, top_w, norm_final.astype(F32)[None, :], normalize=(l == depth - 1))
    return xf.reshape(batch, seq, d)
```

```python
import functools
import math

import jax
import jax.numpy as jnp
import numpy as np
from jax import lax
from jax.experimental import pallas as pl
from jax.experimental.pallas import tpu as pltpu

F32 = jnp.float32
BF16 = jnp.bfloat16

EPS = 1e-5
NEG_BIG = -1e30

HEAD_DIM = 64
ATTN_GROUPS = ((128, 1), (512, 4), (2048, 16))
HEADS_PER_GROUP = 8
GROUP_WIDTH = HEADS_PER_GROUP * HEAD_DIM
ATTN_BLOCK = 128
NUM_BUCKETS = 32
MAX_DISTANCE = 2048
SSM_HEAD_DIM = 64
N_SSM_GROUPS = 4
SSM_HEADS_PER_GROUP = 8
D_STATE = 128
CONV_WIDTH = 4
CHUNK = 128
N_EXPERTS = 32
TOP_K = 4
SWIGLU_LIMIT = 7.0
SWIGLU_ALPHA = 1.702

LANES = 128
V7X_VMEM_BYTES = 64 * 1024 * 1024
VMEM_LIMIT = 48 * 1024 * 1024

ROW_TILE = 1024
MERGE_SUBTILES = 2
EXPERT_ROWS = 256


def padded_tokens(n_tokens):
    return n_tokens + N_EXPERTS * EXPERT_ROWS // TOP_K + 2 * EXPERT_ROWS


def _params(semantics):
    return pltpu.CompilerParams(dimension_semantics=semantics, vmem_limit_bytes=VMEM_LIMIT)


def _rms_matmul_kernel(x_ref, g_ref, w_ref, ws_ref, o_ref, os_ref, hb_ref, h_ref):
    @pl.when(pl.program_id(1) == 0)
    def _():
        x = x_ref[...]
        ms = jnp.mean(x * x, axis=-1, keepdims=True)
        h_ref[...] = (x * lax.rsqrt(ms + EPS) * g_ref[...]).astype(BF16)
        hb_ref[...] = h_ref[...]
        os_ref[...] = jnp.dot(h_ref[...], ws_ref[...], preferred_element_type=F32)

    o_ref[...] = jnp.dot(h_ref[...], w_ref[...], preferred_element_type=F32).astype(o_ref.dtype)


def rms_matmul(x, g, w, w_side, tn):
    t, d = x.shape
    n = w.shape[1]
    ns = w_side.shape[1]
    tm = ROW_TILE
    return pl.pallas_call(
        _rms_matmul_kernel,
        grid=(t // tm, n // tn),
        in_specs=[
            pl.BlockSpec((tm, d), lambda i, j: (i, 0)),
            pl.BlockSpec((1, d), lambda i, j: (0, 0)),
            pl.BlockSpec((d, tn), lambda i, j: (0, j)),
            pl.BlockSpec((d, ns), lambda i, j: (0, 0)),
        ],
        out_specs=[
            pl.BlockSpec((tm, tn), lambda i, j: (i, j)),
            pl.BlockSpec((tm, ns), lambda i, j: (i, 0)),
            pl.BlockSpec((tm, d), lambda i, j: (i, 0)),
        ],
        out_shape=[jax.ShapeDtypeStruct((t, n), BF16), jax.ShapeDtypeStruct((t, ns), F32),
                   jax.ShapeDtypeStruct((t, d), BF16)],
        scratch_shapes=[pltpu.VMEM((tm, d), BF16)],
        compiler_params=_params(("parallel", "arbitrary")),
        name="rms_matmul",
    )(x, g, w, w_side)


def _qkv_kernel(h_ref, w_ref, o0_ref, o1_ref, o2_ref, acc_ref):
    seq = h_ref.shape[0]
    slabs = GROUP_WIDTH // LANES
    n_blocks = seq // ATTN_BLOCK
    for gi, o_ref in enumerate((o0_ref, o1_ref, o2_ref)):
        dil = ATTN_GROUPS[gi][1]
        nb = n_blocks // dil
        acc = jnp.dot(h_ref[...], w_ref[:, gi * GROUP_WIDTH:(gi + 1) * GROUP_WIDTH], preferred_element_type=F32)
        if dil == 1:
            for n in range(n_blocks):
                o_ref[0, 0, n] = acc[n * ATTN_BLOCK:(n + 1) * ATTN_BLOCK, :].astype(o_ref.dtype)
            continue
        for s in range(slabs):
            acc_ref[s] = acc[:, s * LANES:(s + 1) * LANES]
        for r in range(dil):
            for n in range(nb):
                for s in range(slabs):
                    rows = pl.ds(r + n * ATTN_BLOCK * dil, ATTN_BLOCK, stride=dil)
                    o_ref[0, 0, r * nb + n, :, s * LANES:(s + 1) * LANES] = acc_ref[s, rows, :].astype(o_ref.dtype)


def qkv_project(h, w_qkv, batch, seq):
    t, d = h.shape
    n_groups = len(ATTN_GROUPS)
    tn = n_groups * GROUP_WIDTH
    out_shapes, out_specs = [], []
    for _ in ATTN_GROUPS:
        shape = (3, batch, seq // ATTN_BLOCK, ATTN_BLOCK, GROUP_WIDTH)
        out_shapes.append(jax.ShapeDtypeStruct(shape, BF16))
        out_specs.append(pl.BlockSpec((1, 1) + shape[2:], lambda w, b: (w, b, 0, 0, 0)))
    return pl.pallas_call(
        _qkv_kernel,
        grid=(3, batch),
        in_specs=[pl.BlockSpec((seq, d), lambda w, b: (b, 0)), pl.BlockSpec((d, tn), lambda w, b: (0, w))],
        out_specs=out_specs,
        out_shape=out_shapes,
        scratch_shapes=[pltpu.VMEM((GROUP_WIDTH // LANES, seq, LANES), F32)],
        compiler_params=_params(("parallel", "parallel")),
        name="qkv_project",
    )(h, w_qkv)


def _attn_kernel(q0_ref, q1_ref, q2_ref, bias_ref, o_ref, out_ref, lse_ref):
    heads = LANES // HEAD_DIM
    scale = jnp.asarray(HEAD_DIM ** -0.5, BF16)
    n_blocks = q0_ref.shape[2]
    qk = (((2,), (2,)), ((0,), (0,)))
    pv = (((2,), (1,)), ((0,), (0,)))
    blk = lax.broadcasted_iota(jnp.int32, (n_blocks, 1, 1), 0)
    lane = lax.broadcasted_iota(jnp.int32, (1, 1, LANES), 2)

    def shifted(x):
        return jnp.concatenate([x[n_blocks - 1:], x[:n_blocks - 1]], axis=0)

    for g, ref in enumerate((q0_ref, q1_ref, q2_ref)):
        dil = ATTN_GROUPS[g][1]
        nb = n_blocks // dil
        q2h = ref[0, 0] * scale
        keys, vals = ref[1, 0], ref[2, 0]
        if nb > 1:
            keys = jnp.concatenate([shifted(keys), keys], axis=1)
            vals = jnp.concatenate([shifted(vals), vals], axis=1)
        n_keys = keys.shape[1]
        ones = jnp.ones((n_keys, LANES), BF16)
        pvs, dens, maxs = [], [], []
        for h in range(heads):
            in_head = (lane >= h * HEAD_DIM) & (lane < (h + 1) * HEAD_DIM)
            q = jnp.where(in_head, q2h, jnp.zeros_like(q2h))
            s = lax.dot_general(q, keys, qk, preferred_element_type=F32)
            s = s + bias_ref[g, h, :, 2 * ATTN_BLOCK - n_keys:][None]
            if nb > 1:
                key_is_prev = lax.broadcasted_iota(jnp.int32, (1, 1, n_keys), 2) < ATTN_BLOCK
                s = jnp.where((blk % nb == 0) & key_is_prev, NEG_BIG, s)
            m = jnp.max(s, axis=-1, keepdims=True)
            p = jnp.exp(s - m).astype(BF16)
            pvs.append(lax.dot_general(p, vals, pv, preferred_element_type=F32))
            dens.append(jnp.dot(p.reshape(n_blocks * ATTN_BLOCK, n_keys), ones,
                                preferred_element_type=F32).reshape(n_blocks, ATTN_BLOCK, LANES))
            maxs.append(m)
        first = lane < HEAD_DIM
        den = jnp.where(first, dens[0], dens[1])
        o2 = jnp.where(first, pvs[0], pvs[1]) / den
        l2 = jnp.where(first, maxs[0], maxs[1]) + jnp.log(den)
        for r in range(dil):
            for n in range(nb):
                start = r + n * ATTN_BLOCK * dil
                rows = pl.ds(start, ATTN_BLOCK) if dil == 1 else pl.ds(start, ATTN_BLOCK, stride=dil)
                out_ref[g, rows, :] = o2[r * nb + n]
                lse_ref[g, rows, :] = l2[r * nb + n]

    l0, l1, l2 = lse_ref[0], lse_ref[1], lse_ref[2]
    lm = jnp.maximum(jnp.maximum(l0, l1), l2)
    e0, e1, e2 = jnp.exp(l0 - lm), jnp.exp(l1 - lm), jnp.exp(l2 - lm)
    mixed = (e0 * out_ref[0] + e1 * out_ref[1] + e2 * out_ref[2]) / (e0 + e1 + e2)
    o_ref[...] = mixed.astype(o_ref.dtype)


def dilated_attention(qkv_groups, bias, batch, seq):
    heads = LANES // HEAD_DIM
    in_specs = [pl.BlockSpec((3, 1) + a.shape[2:4] + (LANES,), lambda b, hp: (0, b, 0, 0, hp)) for a in qkv_groups]
    in_specs.append(pl.BlockSpec((len(ATTN_GROUPS), heads, ATTN_BLOCK, 2 * ATTN_BLOCK), lambda b, hp: (0, hp, 0, 0)))
    return pl.pallas_call(
        _attn_kernel,
        grid=(batch, GROUP_WIDTH // LANES),
        in_specs=in_specs,
        out_specs=pl.BlockSpec((seq, LANES), lambda b, hp: (b, hp)),
        out_shape=jax.ShapeDtypeStruct((batch * seq, GROUP_WIDTH), BF16),
        scratch_shapes=[pltpu.VMEM((len(ATTN_GROUPS), seq, LANES), F32),
                        pltpu.VMEM((len(ATTN_GROUPS), seq, LANES), F32)],
        compiler_params=_params(("parallel", "parallel")),
        name="dilated_attn",
    )(*qkv_groups, bias)


def attention_bias(rel_bias, gi):
    window, dil = ATTN_GROUPS[gi]
    w_sub = window // dil
    q_idx = np.arange(ATTN_BLOCK)[:, None]
    k_idx = np.arange(2 * ATTN_BLOCK)[None, :]
    delta = q_idx + ATTN_BLOCK - k_idx
    in_band = (delta >= 0) & (delta <= w_sub)
    dist = np.clip(delta, 0, w_sub) * dil
    max_exact = NUM_BUCKETS // 2
    nf = np.maximum(dist, max_exact).astype(np.float32)
    large = max_exact + (np.log(nf / max_exact) / math.log(MAX_DISTANCE / max_exact)
                         * (NUM_BUCKETS - max_exact)).astype(np.int32)
    large = np.minimum(large, NUM_BUCKETS - 1)
    bucket = np.where(dist < max_exact, dist, large)
    table = rel_bias[:, gi * HEADS_PER_GROUP:(gi + 1) * HEADS_PER_GROUP].astype(F32)
    onehot = (bucket.reshape(-1, 1) == np.arange(NUM_BUCKETS)[None, :]).astype(np.float32)
    bias = jnp.einsum('bh,nb->hn', table, jnp.asarray(onehot), precision=lax.Precision.HIGHEST)
    bias = bias.reshape(HEADS_PER_GROUP, ATTN_BLOCK, 2 * ATTN_BLOCK)
    return jnp.where(in_band[None], bias, NEG_BIG)


def _silu(v):
    half = 0.5 * v
    return half + half * jnp.tanh(half)


SSD_CHUNKS_PER_STEP = 4
CONV_TAIL = 16


def _conv_silu(ext_ref, u_ref, w_ref, b_ref, first):
    rows = u_ref.shape[0]
    ext_rows = ext_ref.shape[0]

    if first is not None:
        @pl.when(first)
        def _():
            ext_ref[0:CONV_TAIL, :] = jnp.zeros((CONV_TAIL, ext_ref.shape[1]), ext_ref.dtype)

    u = u_ref[...]
    ext_ref[CONV_TAIL:, :] = u
    taps = CONV_WIDTH - 1
    out_row = lax.broadcasted_iota(jnp.int32, (rows, ext_rows), 0)
    src_row = lax.broadcasted_iota(jnp.int32, (rows, ext_rows), 1)
    shift_mat = jnp.concatenate([(src_row == out_row + (CONV_TAIL - k)).astype(BF16) for k in range(1, CONV_WIDTH)],
                                axis=0)
    shifted = jnp.dot(shift_mat, ext_ref[...], preferred_element_type=F32)
    acc = u.astype(F32) * w_ref[taps:CONV_WIDTH, :] + b_ref[...]
    for k in range(1, CONV_WIDTH):
        acc = acc + shifted[(k - 1) * rows:k * rows, :] * w_ref[taps - k:CONV_WIDTH - k, :]
    ext_ref[0:CONV_TAIL, :] = ext_ref[rows:, :]
    return _silu(acc)


def _ssd_kernel(x_ref, bc_ref, z_ref, dt_ref, cwx_ref, cbx_ref, cwbc_ref, cbbc_ref, dtb_ref, a_ref,
                dskip_ref, nw_ref, expand_ref, o_ref, extx_ref, extbc_ref, state_ref):
    first = pl.program_id(1) == 0
    gw = SSM_HEADS_PER_GROUP * SSM_HEAD_DIM

    @pl.when(first)
    def _():
        state_ref[...] = jnp.zeros(state_ref.shape, F32)

    row = lax.broadcasted_iota(jnp.int32, (CHUNK, CHUNK), 0)
    colm = lax.broadcasted_iota(jnp.int32, (CHUNK, CHUNK), 1)
    tril = row >= colm
    tril_b = tril.astype(BF16)
    head_rows = 8
    eye_b = (lax.broadcasted_iota(jnp.int32, (head_rows, CHUNK), 0)
             == lax.broadcasted_iota(jnp.int32, (head_rows, CHUNK), 1)).astype(BF16)
    nt = (((1,), (1,)), ((), ()))
    tn = (((0,), (0,)), ((), ()))

    def split3(v):
        p0 = v.astype(BF16)
        r1 = v - p0.astype(F32)
        p1 = r1.astype(BF16)
        p2 = (r1 - p1.astype(F32)).astype(BF16)
        return p0, p1, p2

    def select_rows(mat01, v):
        return sum(jnp.dot(mat01, p, preferred_element_type=F32) for p in split3(v))

    def select_cols(v, mat01):
        return sum(jnp.dot(p, mat01, preferred_element_type=F32) for p in split3(v)[:2])

    expand = expand_ref[...]
    for sub, g in [(sub, g) for sub in range(x_ref.shape[0] // CHUNK) for g in range(N_SSM_GROUPS)]:
        rws = pl.ds(sub * CHUNK, CHUNK)
        if g == 0:
            sub_first = first if sub == 0 else None
            xs_all = _conv_silu(extx_ref, x_ref.at[rws], cwx_ref, cbx_ref, sub_first)
            bc_all = _conv_silu(extbc_ref, bc_ref.at[rws], cwbc_ref, cbbc_ref, sub_first)
        lanes = slice(g * LANES, (g + 1) * LANES)
        ch = slice(g * gw, (g + 1) * gw)
        v = dt_ref[rws, lanes] + dtb_ref[:, lanes]
        dt = jnp.maximum(v, 0.0) + jnp.log1p(jnp.exp(-jnp.abs(v)))
        a_d = dt * a_ref[:, lanes]
        acs = select_rows(tril_b, a_d)
        acs_t = sum(lax.dot_general(eye_b, p, nt, preferred_element_type=F32) for p in split3(acs))
        last = acs[CHUNK - 1:CHUNK, :]
        dt_e = select_cols(dt, expand)
        eacs_e = select_cols(jnp.exp(acs), expand)
        edec_e = select_cols(jnp.exp(last - acs), expand)
        elast_e = eacs_e[CHUNK - 1:CHUNK, :]

        xs = xs_all[:, ch]
        bm = bc_all[:, lanes].astype(BF16)
        cm = bc_all[:, N_SSM_GROUPS * D_STATE + g * D_STATE:N_SSM_GROUPS * D_STATE + (g + 1) * D_STATE].astype(BF16)
        x_d = xs * dt_e
        cb = lax.dot_general(cm, bm, nt, preferred_element_type=F32)

        prev = state_ref[g]
        y = jnp.dot(cm, prev.astype(BF16), preferred_element_type=F32) * eacs_e
        x_d16 = x_d.astype(BF16)
        parts = []
        for j in range(SSM_HEADS_PER_GROUP):
            seg = acs[:, j:j + 1] - acs_t[j:j + 1, :]
            m_h = (cb * jnp.exp(jnp.where(tril, seg, NEG_BIG))).astype(BF16)
            parts.append(jnp.dot(m_h, x_d16[:, j * SSM_HEAD_DIM:(j + 1) * SSM_HEAD_DIM],
                                 preferred_element_type=F32))
        y = y + jnp.concatenate(parts, axis=1)

        xw = (x_d * edec_e).astype(BF16)
        state_ref[g] = prev * elast_e + lax.dot_general(bm, xw, tn, preferred_element_type=F32)

        y = y + dskip_ref[:, ch] * xs
        y = y * _silu(z_ref[rws, ch].astype(F32))
        y = y * lax.rsqrt(jnp.mean(y * y, axis=-1, keepdims=True) + EPS)
        o_ref[rws, ch] = (y * nw_ref[:, ch]).astype(o_ref.dtype)


def ssd_mixer(rest, dt_raw, conv_w, conv_b, dt_bias_p, a_p, d_skip_e, norm_w, batch, seq):
    t = rest.shape[0]
    d_inner = N_SSM_GROUPS * SSM_HEADS_PER_GROUP * SSM_HEAD_DIM
    bc_w = 2 * N_SSM_GROUPS * D_STATE
    rows = CHUNK * SSD_CHUNKS_PER_STEP
    nc = seq // rows
    gw = SSM_HEADS_PER_GROUP * SSM_HEAD_DIM
    expand = (np.arange(LANES)[:, None] == (np.arange(gw)[None, :] // SSM_HEAD_DIM)).astype(np.float32)
    rowmap = lambda b, c: (b * nc + c, 0)
    const = lambda b, c: (0, 0)
    return pl.pallas_call(
        _ssd_kernel,
        grid=(batch, nc),
        in_specs=[
            pl.BlockSpec((rows, d_inner), lambda b, c: (b * nc + c, 2)),
            pl.BlockSpec((rows, bc_w), lambda b, c: (b * nc + c, 6)),
            pl.BlockSpec((rows, d_inner), rowmap),
            pl.BlockSpec((rows, N_SSM_GROUPS * LANES), rowmap),
            pl.BlockSpec((CONV_WIDTH, d_inner), const),
            pl.BlockSpec((1, d_inner), const),
            pl.BlockSpec((CONV_WIDTH, bc_w), const),
            pl.BlockSpec((1, bc_w), const),
            pl.BlockSpec((1, N_SSM_GROUPS * LANES), const),
            pl.BlockSpec((1, N_SSM_GROUPS * LANES), const),
            pl.BlockSpec((1, d_inner), const),
            pl.BlockSpec((1, d_inner), const),
            pl.BlockSpec((LANES, gw), const),
        ],
        out_specs=pl.BlockSpec((rows, d_inner), rowmap),
        out_shape=jax.ShapeDtypeStruct((t, d_inner), BF16),
        scratch_shapes=[
            pltpu.VMEM((CONV_TAIL + CHUNK, d_inner), BF16),
            pltpu.VMEM((CONV_TAIL + CHUNK, bc_w), BF16),
            pltpu.VMEM((N_SSM_GROUPS, D_STATE, gw), F32),
        ],
        compiler_params=_params(("parallel", "arbitrary")),
        name="ssd_mixer",
    )(rest, rest, rest, dt_raw, conv_w[:, :d_inner], conv_b[:, :d_inner], conv_w[:, d_inner:],
      conv_b[:, d_inner:], dt_bias_p, a_p, d_skip_e, norm_w, jnp.asarray(expand, dtype=BF16))


def _merge_kernel(at_ref, ys_ref, gl_ref, x_ref, wa_ref, ws_ref,
                  wo_ref, gb_ref, nf_ref, rw_ref, rb_ref, x1_ref, h2_ref, ti_ref, tw_ref):
    d = x_ref.shape[1]
    sub = x_ref.shape[0] // MERGE_SUBTILES
    for part in range(MERGE_SUBTILES):
        rows = slice(part * sub, (part + 1) * sub)
        y_attn = jnp.dot(at_ref[rows, :], wa_ref[...], preferred_element_type=F32)
        y_ssm = jnp.dot(ys_ref[rows, :], ws_ref[...], preferred_element_type=F32)
        gv = gl_ref[rows, :].astype(F32) + gb_ref[...]
        gates = 1.0 / (1.0 + jnp.exp(-gv))
        merged = gates[:, :d] * y_attn + gates[:, d:] * y_ssm
        x1 = x_ref[rows, :] + jnp.dot(merged.astype(BF16), wo_ref[...], preferred_element_type=F32)
        x1_ref[rows, :] = x1
        h2 = x1 * lax.rsqrt(jnp.mean(x1 * x1, axis=-1, keepdims=True) + EPS) * nf_ref[...]
        _store_slabs(h2_ref, part * sub, sub, _pack_bf16_pairs(h2))
        h_hi = h2.astype(BF16)
        h_mid = (h2 - h_hi.astype(F32)).astype(BF16)
        both = jnp.dot(h_hi, rw_ref[...], preferred_element_type=F32)
        logits = (both[:, :LANES] + both[:, LANES:]
                  + jnp.dot(h_mid, rw_ref[:, :LANES], preferred_element_type=F32) + rb_ref[...])
        lane = lax.broadcasted_iota(jnp.int32, logits.shape, 1)
        top_i = jnp.zeros(logits.shape, jnp.int32)
        top_v = jnp.full(logits.shape, NEG_BIG, F32)
        work = logits
        for k in range(TOP_K):
            m = jnp.max(work, axis=-1, keepdims=True)
            idx = jnp.min(jnp.where(work == m, lane, LANES), axis=-1, keepdims=True)
            top_i = jnp.where(lane == k, idx, top_i)
            top_v = jnp.where(lane == k, m, top_v)
            work = jnp.where(lane == idx, NEG_BIG * 2.0, work)
        ev = jnp.exp(top_v - jnp.max(top_v, axis=-1, keepdims=True))
        ti_ref[rows, :] = top_i
        tw_ref[rows, :] = ev / jnp.sum(ev, axis=-1, keepdims=True)


def merge_project(attn, y_ssm, rest, x, wa, ws, wo, gate_bias, norm_ffn, router_w_p, router_b_p):
    t, d = x.shape
    slab = d // 2 // LANES
    tm = 256 * MERGE_SUBTILES
    d_inner = y_ssm.shape[1]
    rowmap = lambda i: (i, 0)
    const = lambda i: (0, 0)
    full = lambda a: pl.BlockSpec(a.shape, const)
    args = [attn, y_ssm, rest, x, wa, ws, wo, gate_bias, norm_ffn, router_w_p, router_b_p]
    in_specs = (
        [pl.BlockSpec((tm, GROUP_WIDTH), rowmap),
         pl.BlockSpec((tm, d_inner), rowmap),
         pl.BlockSpec((tm, 2 * d), lambda i: (i, 1)),
         pl.BlockSpec((tm, d), rowmap)]
        + [full(a) for a in args[4:]]
    )
    return pl.pallas_call(
        _merge_kernel,
        grid=(t // tm,),
        in_specs=in_specs,
        out_specs=[pl.BlockSpec((tm, d), rowmap), pl.BlockSpec((tm * slab, LANES), rowmap),
                   pl.BlockSpec((tm, LANES), rowmap), pl.BlockSpec((tm, LANES), rowmap)],
        out_shape=[jax.ShapeDtypeStruct((t, d), F32), jax.ShapeDtypeStruct((t * slab, LANES), jnp.uint32),
                   jax.ShapeDtypeStruct((t, LANES), jnp.int32), jax.ShapeDtypeStruct((t, LANES), F32)],
        compiler_params=_params(("parallel",)),
        name="merge_project",
    )(*args)


def _split_w1_kernel(w_ref, g_ref, l_ref, t_ref):
    de = g_ref.shape[1]
    for s in range(t_ref.shape[0]):
        cols = slice(s * LANES, (s + 1) * LANES)
        t_ref[s] = w_ref[0, cols, :].T
        g_ref[0, :, cols] = t_ref[s, pl.ds(0, de, stride=2), :].astype(g_ref.dtype)
        l_ref[0, :, cols] = t_ref[s, pl.ds(1, de, stride=2), :].astype(l_ref.dtype)


def split_w1(w1):
    e, d, de2 = w1.shape
    de = de2 // 2
    tk = 512
    out = jax.ShapeDtypeStruct((e, de, d), BF16)
    return pl.pallas_call(
        _split_w1_kernel,
        grid=(e, d // tk),
        in_specs=[pl.BlockSpec((1, tk, de2), lambda i, k: (i, k, 0))],
        out_specs=[pl.BlockSpec((1, de, tk), lambda i, k: (i, 0, k))] * 2,
        out_shape=[out, out],
        scratch_shapes=[pltpu.VMEM((tk // LANES, de2, LANES), F32)],
        compiler_params=_params(("parallel", "parallel")),
        name="split_w1",
    )(w1)


def _pack_bf16_pairs(v):
    w = v.shape[1] // 2
    lo = lax.bitcast_convert_type(v[:, :w].astype(BF16).astype(F32), jnp.uint32) >> 16
    hi = lax.bitcast_convert_type(v[:, w:].astype(BF16).astype(F32), jnp.uint32) & jnp.uint32(0xFFFF0000)
    return lo | hi


def _unpack_bf16_pairs(p):
    lo = lax.bitcast_convert_type(p << 16, F32)
    hi = lax.bitcast_convert_type(p & jnp.uint32(0xFFFF0000), F32)
    return lo, hi


def _store_slabs(ref, row0, rows, packed):
    slab = packed.shape[1] // LANES
    for s in range(slab):
        ref[pl.ds(row0 * slab + s, rows, stride=slab), :] = packed[:, s * LANES:(s + 1) * LANES]


def _load_slabs(ref, rows, slab):
    return jnp.concatenate([ref[pl.ds(s, rows, stride=slab), :] for s in range(slab)], axis=1)


def _expert_kernel(be_ref, nu_ref, g0_ref, g1_ref, g2_ref, sp_ref, sc_ref, h2p_ref, w1g_ref, w1l_ref, w2_ref, b1g_ref,
                   b1l_ref, b2_ref, y4p_ref, xbuf, obuf, gsem, ssem, *, n_tokens, t_pad):
    i = pl.program_id(0)
    n_used = nu_ref[0]
    slot = lax.rem(i, 2)
    other = 1 - slot
    gslot = lax.rem(i, 3)
    gslot1 = lax.rem(i + 1, 3)
    gslot2 = lax.rem(i + 2, 3)
    bm = EXPERT_ROWS
    slab = xbuf.shape[1] // bm

    def gather_copy(src_ref, j, buf):
        src = pl.multiple_of(src_ref[0, 0, j], slab)
        return pltpu.make_async_copy(h2p_ref.at[pl.ds(src, slab)],
                                     xbuf.at[buf, pl.ds(j * slab, slab)], gsem.at[buf])

    def scatter_copy(dst_ref, j, buf):
        dst = pl.multiple_of(dst_ref[0, 0, j], slab)
        return pltpu.make_async_copy(obuf.at[buf, pl.ds(j * slab, slab)],
                                     y4p_ref.at[pl.ds(dst, slab)], ssem.at[buf])

    def wait_block(kind, buf):
        if kind == "gather":
            pltpu.make_async_copy(h2p_ref.at[pl.ds(0, bm * slab)], xbuf.at[buf], gsem.at[buf]).wait()
        else:
            pltpu.make_async_copy(obuf.at[buf], y4p_ref.at[pl.ds(0, bm * slab)], ssem.at[buf]).wait()

    @pl.when(i == 0)
    def _():
        obuf[...] = jnp.zeros(obuf.shape, obuf.dtype)
        fills = [pltpu.make_async_copy(obuf.at[1], y4p_ref.at[pl.ds((k * t_pad + n_tokens) * slab + c * bm * slab,
                                                                     bm * slab)], ssem.at[1])
                 for k in range(TOP_K) for c in range((t_pad - n_tokens) // bm)]
        for fill in fills:
            fill.start()
        for fill in fills:
            fill.wait()
        for j in range(bm):
            gather_copy(g0_ref, j, 0).start()
        for j in range(bm):
            gather_copy(g1_ref, j, 1).start()
        spare_row = t_pad + t_pad - 2 * bm
        pltpu.make_async_copy(obuf.at[0], y4p_ref.at[pl.ds(spare_row * slab, bm * slab)], ssem.at[0]).start()

    @pl.when(i < n_used)
    def _():
        nt = (((1,), (1,)), ((), ()))
        wait_block("gather", gslot)
        lo, hi = _unpack_bf16_pairs(_load_slabs(xbuf.at[gslot], bm, slab))
        xb = jnp.concatenate([lo, hi], axis=1).astype(BF16)
        for j in range(bm):
            gather_copy(g2_ref, j, gslot2).start(priority=j % 2)
        for j in range(bm):
            scatter_copy(sp_ref, j, other).start(priority=j % 2)
        glu = lax.dot_general(xb, w1g_ref[0], nt, preferred_element_type=F32) + b1g_ref[0]
        lin = lax.dot_general(xb, w1l_ref[0], nt, preferred_element_type=F32) + b1l_ref[0]
        glu = jnp.minimum(glu, SWIGLU_LIMIT)
        lin = jnp.clip(lin, -SWIGLU_LIMIT, SWIGLU_LIMIT)
        act = glu * (1.0 / (1.0 + jnp.exp(-SWIGLU_ALPHA * glu))) * (lin + 1.0)
        y = jnp.dot(act.astype(BF16), w2_ref[0], preferred_element_type=F32) + b2_ref[0]
        wait_block("scatter", slot)
        _store_slabs(obuf.at[slot], 0, bm, _pack_bf16_pairs(y))

    @pl.when(i == n_used - 1)
    def _():
        for j in range(bm):
            scatter_copy(sc_ref, j, slot).start()
        wait_block("scatter", other)
        wait_block("scatter", slot)
        wait_block("gather", gslot1)
        wait_block("gather", gslot2)


def expert_ffn(h2p, ids, block_e, n_used, w1g, w1l, w2, b1g, b1l, b2, n_tokens):
    n_blocks, _, bm = ids.shape
    slab = h2p.shape[0] // n_tokens
    de, d = w1g.shape[1], w1g.shape[2]
    t_pad = padded_tokens(n_tokens)
    src = jnp.minimum(ids >> 2, n_tokens - 1) * slab
    dst = ((ids & 3) * t_pad + (ids >> 2)) * slab
    first = ((t_pad - 2 * bm + jnp.arange(bm, dtype=jnp.int32)) * slab).reshape(1, 1, bm)
    dst = jnp.concatenate([first, dst], axis=0)
    wmap = lambda i, be, nu: (be[i], 0, 0)
    smem_ids = lambda imap: pl.BlockSpec((1, 1, bm), imap, memory_space=pltpu.SMEM)
    grid_spec = pltpu.PrefetchScalarGridSpec(
        num_scalar_prefetch=2,
        grid=(n_blocks,),
        in_specs=[
            smem_ids(lambda i, be, nu: (i, 0, 0)),
            smem_ids(lambda i, be, nu: (jnp.minimum(i + 1, n_blocks - 1), 0, 0)),
            smem_ids(lambda i, be, nu: (jnp.minimum(i + 2, n_blocks - 1), 0, 0)),
            smem_ids(lambda i, be, nu: (i, 0, 0)),
            smem_ids(lambda i, be, nu: (i + 1, 0, 0)),
            pl.BlockSpec(memory_space=pl.ANY),
            pl.BlockSpec((1, de, d), wmap),
            pl.BlockSpec((1, de, d), wmap),
            pl.BlockSpec((1, de, d), wmap),
            pl.BlockSpec((1, 1, de), wmap),
            pl.BlockSpec((1, 1, de), wmap),
            pl.BlockSpec((1, 1, d), wmap),
        ],
        out_specs=pl.BlockSpec(memory_space=pl.ANY),
        scratch_shapes=[
            pltpu.VMEM((3, bm * slab, LANES), jnp.uint32),
            pltpu.VMEM((2, bm * slab, LANES), jnp.uint32),
            pltpu.SemaphoreType.DMA((3,)),
            pltpu.SemaphoreType.DMA((2,)),
        ],
    )
    return pl.pallas_call(
        functools.partial(_expert_kernel, n_tokens=n_tokens, t_pad=t_pad),
        grid_spec=grid_spec,
        out_shape=jax.ShapeDtypeStruct((TOP_K * t_pad * slab, LANES), jnp.uint32),
        compiler_params=_params(("arbitrary",)),
        name="expert_ffn",
    )(block_e, n_used, src, src, src, dst, dst, h2p, w1g, w1l, w2, b1g, b1l, b2)


def _combine_kernel(x1_ref, y0_ref, y1_ref, y2_ref, y3_ref, tw_ref, nw_ref, o_ref, *, normalize):
    tm, d = x1_ref.shape
    slab = y0_ref.shape[0] // tm
    lo_sum = jnp.zeros((tm, d // 2), F32)
    hi_sum = jnp.zeros((tm, d // 2), F32)
    for k, y_ref in enumerate((y0_ref, y1_ref, y2_ref, y3_ref)):
        lo, hi = _unpack_bf16_pairs(_load_slabs(y_ref, tm, slab))
        w = tw_ref[:, k:k + 1]
        lo_sum = lo_sum + w * lo
        hi_sum = hi_sum + w * hi
    acc = x1_ref[...] + jnp.concatenate([lo_sum, hi_sum], axis=1)
    if normalize:
        acc = acc * lax.rsqrt(jnp.mean(acc * acc, axis=-1, keepdims=True) + EPS) * nw_ref[...]
    o_ref[...] = acc


def combine_norm(x1, y4p, top_w, norm_w, normalize):
    t, d = x1.shape
    tm = 256
    t_pad = padded_tokens(t)
    slab = y4p.shape[0] // (TOP_K * t_pad)
    y_specs = [pl.BlockSpec((tm * slab, LANES), functools.partial(lambda i, k: (k * (t_pad // tm) + i, 0), k=k))
               for k in range(TOP_K)]
    return pl.pallas_call(
        functools.partial(_combine_kernel, normalize=normalize),
        grid=(t // tm,),
        in_specs=[pl.BlockSpec((tm, d), lambda i: (i, 0))] + y_specs
        + [pl.BlockSpec((tm, LANES), lambda i: (i, 0)), pl.BlockSpec((1, d), lambda i: (0, 0))],
        out_specs=pl.BlockSpec((tm, d), lambda i: (i, 0)),
        out_shape=jax.ShapeDtypeStruct((t, d), F32),
        compiler_params=_params(("parallel",)),
        name="combine_norm",
    )(x1, y4p, y4p, y4p, y4p, top_w, norm_w)


def routing_layout(top_i, n_tokens):
    n_assign = n_tokens * TOP_K
    bm = EXPERT_ROWS
    n_pad = N_EXPERTS * bm
    n_blocks = (n_assign + n_pad) // bm
    flat_e = top_i[:, :TOP_K].reshape(n_assign)
    counts = jnp.sum(flat_e[:, None] == jnp.arange(N_EXPERTS, dtype=jnp.int32)[None, :], axis=0, dtype=jnp.int32)
    padded = (counts + bm - 1) // bm * bm
    pend = jnp.cumsum(padded)
    n_used = (pend[-1] // bm).astype(jnp.int32)
    block_row = jnp.arange(n_blocks, dtype=jnp.int32) * bm
    block_e = jnp.minimum(jnp.sum(pend[None, :] <= block_row[:, None], axis=1, dtype=jnp.int32), N_EXPERTS - 1)
    spare_pos = jnp.arange(bm, dtype=jnp.int32)[None, :]
    spare_key = jnp.where(spare_pos < (padded - counts)[:, None], jnp.arange(N_EXPERTS, dtype=jnp.int32)[:, None],
                          N_EXPERTS).reshape(n_pad)
    keys = jnp.concatenate([flat_e, spare_key])
    pos_bits = (n_assign + n_pad - 1).bit_length()
    assert (N_EXPERTS + 1) << pos_bits < 2 ** 31
    packed = jnp.sort((keys << pos_bits) | jnp.arange(n_assign + n_pad, dtype=jnp.int32))
    order = packed & ((1 << pos_bits) - 1)
    return order.reshape(n_blocks, 1, bm), block_e, n_used.reshape(1)


def kernel(x, w_in, rel_bias, w_branch_attn, conv_w, conv_b, dt_bias, a_log, d_skip, ssm_norm_w,
           w_branch_ssm, gate_bias, w_out, norm_mix, norm_ffn, router_w, router_b, w1, b1, w2, b2,
           norm_final):
    batch, seq, d = x.shape
    t = batch * seq
    depth = w_in.shape[0]
    n_groups = len(ATTN_GROUPS)
    attn_w = n_groups * GROUP_WIDTH
    d_inner = N_SSM_GROUPS * SSM_HEADS_PER_GROUP * SSM_HEAD_DIM
    n_heads = N_SSM_GROUPS * SSM_HEADS_PER_GROUP
    bc_w = 2 * N_SSM_GROUPS * D_STATE
    xf = x.reshape(t, d)
    for l in range(depth):
        wl = w_in[l]
        o_z = 3 * attn_w
        o_xbc = o_z + d_inner
        o_dt = o_xbc + d_inner + bc_w
        o_gate = o_dt + n_heads
        w_qkv = wl[:, :o_z].astype(BF16)
        w_rest = jnp.concatenate([wl[:, o_z:o_xbc], wl[:, o_gate:], wl[:, o_xbc:o_dt]], axis=1).astype(BF16)
        lane_of_head = (np.arange(n_heads) // SSM_HEADS_PER_GROUP) * LANES + np.arange(n_heads) % SSM_HEADS_PER_GROUP
        w_dt = jnp.zeros((d, N_SSM_GROUPS * LANES), F32).at[:, lane_of_head].set(wl[:, o_dt:o_gate]).astype(BF16)
        dt_bias_p = jnp.zeros((1, N_SSM_GROUPS * LANES), F32).at[0, lane_of_head].set(dt_bias[l].astype(F32))
        a_p = jnp.zeros((1, N_SSM_GROUPS * LANES), F32).at[0, lane_of_head].set(-jnp.exp(a_log[l].astype(F32)))
        d_skip_e = jnp.repeat(d_skip[l].astype(F32), SSM_HEAD_DIM)[None, :]

        g_mix = norm_mix[l].astype(F32)[None, :]
        rest, dt_raw, h = rms_matmul(xf, g_mix, w_rest, w_dt, tn=1792)
        qkv_groups = qkv_project(h, w_qkv, batch, seq)
        bias = jnp.stack([attention_bias(rel_bias, gi) for gi in range(n_groups)])
        attn = dilated_attention(qkv_groups, bias, batch, seq)
        y_ssm = ssd_mixer(rest, dt_raw, conv_w[l].astype(F32), conv_b[l].astype(F32)[None, :], dt_bias_p, a_p,
                          d_skip_e, ssm_norm_w[l].astype(F32)[None, :], batch, seq)
        rw = jnp.zeros((d, LANES), F32).at[:, :N_EXPERTS].set(router_w[l].astype(F32))
        rw_hi = rw.astype(BF16)
        router_w_p = jnp.concatenate([rw_hi, (rw - rw_hi.astype(F32)).astype(BF16)], axis=1)
        router_b_p = jnp.full((1, LANES), NEG_BIG, F32).at[0, :N_EXPERTS].set(router_b[l].astype(F32))
        x1, h2, top_i, top_w = merge_project(
            attn, y_ssm, rest, xf, w_branch_attn[l].astype(BF16), w_branch_ssm[l].astype(BF16),
            w_out[l].astype(BF16), gate_bias[l].astype(F32)[None, :], norm_ffn[l].astype(F32)[None, :],
            router_w_p, router_b_p)

        ids, block_e, n_used = routing_layout(top_i, t)
        w1g_t, w1l_t = split_w1(w1[l].astype(F32))
        y4p = expert_ffn(h2, ids, block_e, n_used, w1g_t, w1l_t, w2[l].astype(BF16),
                         b1[l][:, None, 0::2].astype(F32), b1[l][:, None, 1::2].astype(F32),
                         b2[l][:, None, :].astype(F32), t)
        xf = combine_norm(x1, y4p, top_w, norm_final.astype(F32)[None, :], normalize=(l == depth - 1))
    return xf.reshape(batch, seq, d)
```

```python
import functools
import math

import jax
import jax.numpy as jnp
import numpy as np
from jax import lax
from jax.experimental import pallas as pl
from jax.experimental.pallas import tpu as pltpu

F32 = jnp.float32
BF16 = jnp.bfloat16

EPS = 1e-5
NEG_BIG = -1e30

HEAD_DIM = 64
ATTN_GROUPS = ((128, 1), (512, 4), (2048, 16))
HEADS_PER_GROUP = 8
GROUP_WIDTH = HEADS_PER_GROUP * HEAD_DIM
ATTN_BLOCK = 128
NUM_BUCKETS = 32
MAX_DISTANCE = 2048
SSM_HEAD_DIM = 64
N_SSM_GROUPS = 4
SSM_HEADS_PER_GROUP = 8
D_STATE = 128
CONV_WIDTH = 4
CHUNK = 128
N_EXPERTS = 32
TOP_K = 4
SWIGLU_LIMIT = 7.0
SWIGLU_ALPHA = 1.702

LANES = 128
V7X_VMEM_BYTES = 64 * 1024 * 1024
VMEM_LIMIT = 48 * 1024 * 1024

ROW_TILE = 1024
MERGE_SUBTILES = 2
EXPERT_ROWS = 512


def padded_tokens(n_tokens):
    return n_tokens + N_EXPERTS * EXPERT_ROWS // TOP_K + 2 * EXPERT_ROWS


def _params(semantics):
    return pltpu.CompilerParams(dimension_semantics=semantics, vmem_limit_bytes=VMEM_LIMIT)


def _rms_matmul_kernel(x_ref, g_ref, w_ref, ws_ref, o_ref, os_ref, hb_ref, h_ref):
    @pl.when(pl.program_id(1) == 0)
    def _():
        x = x_ref[...]
        ms = jnp.mean(x * x, axis=-1, keepdims=True)
        h_ref[...] = (x * lax.rsqrt(ms + EPS) * g_ref[...]).astype(BF16)
        hb_ref[...] = h_ref[...]
        os_ref[...] = jnp.dot(h_ref[...], ws_ref[...], preferred_element_type=F32)

    o_ref[...] = jnp.dot(h_ref[...], w_ref[...], preferred_element_type=F32).astype(o_ref.dtype)


def rms_matmul(x, g, w, w_side, tn):
    t, d = x.shape
    n = w.shape[1]
    ns = w_side.shape[1]
    tm = ROW_TILE
    return pl.pallas_call(
        _rms_matmul_kernel,
        grid=(t // tm, n // tn),
        in_specs=[
            pl.BlockSpec((tm, d), lambda i, j: (i, 0)),
            pl.BlockSpec((1, d), lambda i, j: (0, 0)),
            pl.BlockSpec((d, tn), lambda i, j: (0, j)),
            pl.BlockSpec((d, ns), lambda i, j: (0, 0)),
        ],
        out_specs=[
            pl.BlockSpec((tm, tn), lambda i, j: (i, j)),
            pl.BlockSpec((tm, ns), lambda i, j: (i, 0)),
            pl.BlockSpec((tm, d), lambda i, j: (i, 0)),
        ],
        out_shape=[jax.ShapeDtypeStruct((t, n), BF16), jax.ShapeDtypeStruct((t, ns), F32),
                   jax.ShapeDtypeStruct((t, d), BF16)],
        scratch_shapes=[pltpu.VMEM((tm, d), BF16)],
        compiler_params=_params(("parallel", "arbitrary")),
        name="rms_matmul",
    )(x, g, w, w_side)


def _qkv_kernel(h_ref, w_ref, o0_ref, o1_ref, o2_ref, acc_ref):
    seq = h_ref.shape[0]
    slabs = GROUP_WIDTH // LANES
    n_blocks = seq // ATTN_BLOCK
    for gi, o_ref in enumerate((o0_ref, o1_ref, o2_ref)):
        dil = ATTN_GROUPS[gi][1]
        nb = n_blocks // dil
        acc = jnp.dot(h_ref[...], w_ref[:, gi * GROUP_WIDTH:(gi + 1) * GROUP_WIDTH], preferred_element_type=F32)
        if dil == 1:
            for n in range(n_blocks):
                o_ref[0, 0, n] = acc[n * ATTN_BLOCK:(n + 1) * ATTN_BLOCK, :].astype(o_ref.dtype)
            continue
        for s in range(slabs):
            acc_ref[s] = acc[:, s * LANES:(s + 1) * LANES]
        for r in range(dil):
            for n in range(nb):
                for s in range(slabs):
                    rows = pl.ds(r + n * ATTN_BLOCK * dil, ATTN_BLOCK, stride=dil)
                    o_ref[0, 0, r * nb + n, :, s * LANES:(s + 1) * LANES] = acc_ref[s, rows, :].astype(o_ref.dtype)


def qkv_project(h, w_qkv, batch, seq):
    t, d = h.shape
    n_groups = len(ATTN_GROUPS)
    tn = n_groups * GROUP_WIDTH
    out_shapes, out_specs = [], []
    for _ in ATTN_GROUPS:
        shape = (3, batch, seq // ATTN_BLOCK, ATTN_BLOCK, GROUP_WIDTH)
        out_shapes.append(jax.ShapeDtypeStruct(shape, BF16))
        out_specs.append(pl.BlockSpec((1, 1) + shape[2:], lambda w, b: (w, b, 0, 0, 0)))
    return pl.pallas_call(
        _qkv_kernel,
        grid=(3, batch),
        in_specs=[pl.BlockSpec((seq, d), lambda w, b: (b, 0)), pl.BlockSpec((d, tn), lambda w, b: (0, w))],
        out_specs=out_specs,
        out_shape=out_shapes,
        scratch_shapes=[pltpu.VMEM((GROUP_WIDTH // LANES, seq, LANES), F32)],
        compiler_params=_params(("parallel", "parallel")),
        name="qkv_project",
    )(h, w_qkv)


def _attn_kernel(q0_ref, q1_ref, q2_ref, bias_ref, o_ref, out_ref, lse_ref):
    heads = LANES // HEAD_DIM
    scale = jnp.asarray(HEAD_DIM ** -0.5, BF16)
    n_blocks = q0_ref.shape[2]
    qk = (((2,), (2,)), ((0,), (0,)))
    pv = (((2,), (1,)), ((0,), (0,)))
    blk = lax.broadcasted_iota(jnp.int32, (n_blocks, 1, 1), 0)
    lane = lax.broadcasted_iota(jnp.int32, (1, 1, LANES), 2)

    def shifted(x):
        return jnp.concatenate([x[n_blocks - 1:], x[:n_blocks - 1]], axis=0)

    for g, ref in enumerate((q0_ref, q1_ref, q2_ref)):
        dil = ATTN_GROUPS[g][1]
        nb = n_blocks // dil
        q2h = ref[0, 0] * scale
        keys, vals = ref[1, 0], ref[2, 0]
        if nb > 1:
            keys = jnp.concatenate([shifted(keys), keys], axis=1)
            vals = jnp.concatenate([shifted(vals), vals], axis=1)
        n_keys = keys.shape[1]
        vals = jnp.concatenate([vals, jnp.ones(vals.shape, BF16)], axis=2)
        pvs, dens, maxs = [], [], []
        for h in range(heads):
            in_head = (lane >= h * HEAD_DIM) & (lane < (h + 1) * HEAD_DIM)
            q = jnp.where(in_head, q2h, jnp.zeros_like(q2h))
            s = lax.dot_general(q, keys, qk, preferred_element_type=F32)
            s = s + bias_ref[g, h, :, 2 * ATTN_BLOCK - n_keys:][None]
            if nb > 1:
                key_is_prev = lax.broadcasted_iota(jnp.int32, (1, 1, n_keys), 2) < ATTN_BLOCK
                s = jnp.where((blk % nb == 0) & key_is_prev, NEG_BIG, s)
            m = jnp.max(s, axis=-1, keepdims=True)
            p = jnp.exp(s - m).astype(BF16)
            acc = lax.dot_general(p, vals, pv, preferred_element_type=F32)
            pvs.append(acc[:, :, :LANES])
            dens.append(acc[:, :, LANES:])
            maxs.append(m)
        first = lane < HEAD_DIM
        den = jnp.where(first, dens[0], dens[1])
        o2 = jnp.where(first, pvs[0], pvs[1]) / den
        l2 = jnp.where(first, maxs[0], maxs[1]) + jnp.log(den)
        for r in range(dil):
            for n in range(nb):
                start = r + n * ATTN_BLOCK * dil
                rows = pl.ds(start, ATTN_BLOCK) if dil == 1 else pl.ds(start, ATTN_BLOCK, stride=dil)
                out_ref[g, rows, :] = o2[r * nb + n]
                lse_ref[g, rows, :] = l2[r * nb + n]

    l0, l1, l2 = lse_ref[0], lse_ref[1], lse_ref[2]
    lm = jnp.maximum(jnp.maximum(l0, l1), l2)
    e0, e1, e2 = jnp.exp(l0 - lm), jnp.exp(l1 - lm), jnp.exp(l2 - lm)
    mixed = (e0 * out_ref[0] + e1 * out_ref[1] + e2 * out_ref[2]) / (e0 + e1 + e2)
    o_ref[...] = mixed.astype(o_ref.dtype)


def dilated_attention(qkv_groups, bias, batch, seq):
    heads = LANES // HEAD_DIM
    in_specs = [pl.BlockSpec((3, 1) + a.shape[2:4] + (LANES,), lambda b, hp: (0, b, 0, 0, hp)) for a in qkv_groups]
    in_specs.append(pl.BlockSpec((len(ATTN_GROUPS), heads, ATTN_BLOCK, 2 * ATTN_BLOCK), lambda b, hp: (0, hp, 0, 0)))
    return pl.pallas_call(
        _attn_kernel,
        grid=(batch, GROUP_WIDTH // LANES),
        in_specs=in_specs,
        out_specs=pl.BlockSpec((seq, LANES), lambda b, hp: (b, hp)),
        out_shape=jax.ShapeDtypeStruct((batch * seq, GROUP_WIDTH), BF16),
        scratch_shapes=[pltpu.VMEM((len(ATTN_GROUPS), seq, LANES), F32),
                        pltpu.VMEM((len(ATTN_GROUPS), seq, LANES), F32)],
        compiler_params=_params(("parallel", "parallel")),
        name="dilated_attn",
    )(*qkv_groups, bias)


def attention_bias(rel_bias, gi):
    window, dil = ATTN_GROUPS[gi]
    w_sub = window // dil
    q_idx = np.arange(ATTN_BLOCK)[:, None]
    k_idx = np.arange(2 * ATTN_BLOCK)[None, :]
    delta = q_idx + ATTN_BLOCK - k_idx
    in_band = (delta >= 0) & (delta <= w_sub)
    dist = np.clip(delta, 0, w_sub) * dil
    max_exact = NUM_BUCKETS // 2
    nf = np.maximum(dist, max_exact).astype(np.float32)
    large = max_exact + (np.log(nf / max_exact) / math.log(MAX_DISTANCE / max_exact)
                         * (NUM_BUCKETS - max_exact)).astype(np.int32)
    large = np.minimum(large, NUM_BUCKETS - 1)
    bucket = np.where(dist < max_exact, dist, large)
    table = rel_bias[:, gi * HEADS_PER_GROUP:(gi + 1) * HEADS_PER_GROUP].astype(F32)
    onehot = (bucket.reshape(-1, 1) == np.arange(NUM_BUCKETS)[None, :]).astype(np.float32)
    bias = jnp.einsum('bh,nb->hn', table, jnp.asarray(onehot), precision=lax.Precision.HIGHEST)
    bias = bias.reshape(HEADS_PER_GROUP, ATTN_BLOCK, 2 * ATTN_BLOCK)
    return jnp.where(in_band[None], bias, NEG_BIG)


def _silu(v):
    half = 0.5 * v
    return half + half * jnp.tanh(half)


SSD_CHUNKS_PER_STEP = 4
CONV_TAIL = 16


def _conv_silu(ext_ref, u_ref, w_ref, b_ref, first):
    rows = u_ref.shape[0]
    ext_rows = ext_ref.shape[0]

    if first is not None:
        @pl.when(first)
        def _():
            ext_ref[0:CONV_TAIL, :] = jnp.zeros((CONV_TAIL, ext_ref.shape[1]), ext_ref.dtype)

    u = u_ref[...]
    ext_ref[CONV_TAIL:, :] = u
    taps = CONV_WIDTH - 1
    out_row = lax.broadcasted_iota(jnp.int32, (rows, ext_rows), 0)
    src_row = lax.broadcasted_iota(jnp.int32, (rows, ext_rows), 1)
    shift_mat = jnp.concatenate([(src_row == out_row + (CONV_TAIL - k)).astype(BF16) for k in range(1, CONV_WIDTH)],
                                axis=0)
    shifted = jnp.dot(shift_mat, ext_ref[...], preferred_element_type=F32)
    acc = u.astype(F32) * w_ref[taps:CONV_WIDTH, :] + b_ref[...]
    for k in range(1, CONV_WIDTH):
        acc = acc + shifted[(k - 1) * rows:k * rows, :] * w_ref[taps - k:CONV_WIDTH - k, :]
    ext_ref[0:CONV_TAIL, :] = ext_ref[rows:, :]
    return _silu(acc)


def _ssd_kernel(x_ref, bc_ref, z_ref, dt_ref, cwx_ref, cbx_ref, cwbc_ref, cbbc_ref, dtb_ref, a_ref,
                dskip_ref, nw_ref, expand_ref, o_ref, extx_ref, extbc_ref, state_ref):
    first = pl.program_id(1) == 0
    gw = SSM_HEADS_PER_GROUP * SSM_HEAD_DIM

    @pl.when(first)
    def _():
        state_ref[...] = jnp.zeros(state_ref.shape, F32)

    row = lax.broadcasted_iota(jnp.int32, (CHUNK, CHUNK), 0)
    colm = lax.broadcasted_iota(jnp.int32, (CHUNK, CHUNK), 1)
    tril = row >= colm
    tril_b = tril.astype(BF16)
    head_rows = 8
    eye_b = (lax.broadcasted_iota(jnp.int32, (head_rows, CHUNK), 0)
             == lax.broadcasted_iota(jnp.int32, (head_rows, CHUNK), 1)).astype(BF16)
    nt = (((1,), (1,)), ((), ()))
    tn = (((0,), (0,)), ((), ()))

    def split3(v):
        p0 = v.astype(BF16)
        r1 = v - p0.astype(F32)
        p1 = r1.astype(BF16)
        p2 = (r1 - p1.astype(F32)).astype(BF16)
        return p0, p1, p2

    def select_rows(mat01, v):
        return sum(jnp.dot(mat01, p, preferred_element_type=F32) for p in split3(v))

    def select_cols(v, mat01):
        return sum(jnp.dot(p, mat01, preferred_element_type=F32) for p in split3(v)[:2])

    expand = expand_ref[...]
    for sub, g in [(sub, g) for sub in range(x_ref.shape[0] // CHUNK) for g in range(N_SSM_GROUPS)]:
        rws = pl.ds(sub * CHUNK, CHUNK)
        if g == 0:
            sub_first = first if sub == 0 else None
            xs_all = _conv_silu(extx_ref, x_ref.at[rws], cwx_ref, cbx_ref, sub_first)
            bc_all = _conv_silu(extbc_ref, bc_ref.at[rws], cwbc_ref, cbbc_ref, sub_first)
        lanes = slice(g * LANES, (g + 1) * LANES)
        ch = slice(g * gw, (g + 1) * gw)
        v = dt_ref[rws, lanes] + dtb_ref[:, lanes]
        dt = jnp.maximum(v, 0.0) + jnp.log1p(jnp.exp(-jnp.abs(v)))
        a_d = dt * a_ref[:, lanes]
        acs = select_rows(tril_b, a_d)
        acs_t = sum(lax.dot_general(eye_b, p, nt, preferred_element_type=F32) for p in split3(acs))
        last = acs[CHUNK - 1:CHUNK, :]
        dt_e = select_cols(dt, expand)
        eacs_e = select_cols(jnp.exp(acs), expand)
        edec_e = select_cols(jnp.exp(last - acs), expand)
        elast_e = eacs_e[CHUNK - 1:CHUNK, :]

        xs = xs_all[:, ch]
        bm = bc_all[:, lanes].astype(BF16)
        cm = bc_all[:, N_SSM_GROUPS * D_STATE + g * D_STATE:N_SSM_GROUPS * D_STATE + (g + 1) * D_STATE].astype(BF16)
        x_d = xs * dt_e
        cb = lax.dot_general(cm, bm, nt, preferred_element_type=F32)

        prev = state_ref[g]
        y = jnp.dot(cm, prev.astype(BF16), preferred_element_type=F32) * eacs_e
        x_d16 = x_d.astype(BF16)
        parts = []
        for j in range(SSM_HEADS_PER_GROUP):
            seg = acs[:, j:j + 1] - acs_t[j:j + 1, :]
            m_h = (cb * jnp.exp(jnp.where(tril, seg, NEG_BIG))).astype(BF16)
            parts.append(jnp.dot(m_h, x_d16[:, j * SSM_HEAD_DIM:(j + 1) * SSM_HEAD_DIM],
                                 preferred_element_type=F32))
        y = y + jnp.concatenate(parts, axis=1)

        xw = (x_d * edec_e).astype(BF16)
        state_ref[g] = prev * elast_e + lax.dot_general(bm, xw, tn, preferred_element_type=F32)

        y = y + dskip_ref[:, ch] * xs
        y = y * _silu(z_ref[rws, ch].astype(F32))
        y = y * lax.rsqrt(jnp.mean(y * y, axis=-1, keepdims=True) + EPS)
        o_ref[rws, ch] = (y * nw_ref[:, ch]).astype(o_ref.dtype)


def ssd_mixer(rest, dt_raw, conv_w, conv_b, dt_bias_p, a_p, d_skip_e, norm_w, batch, seq):
    t = rest.shape[0]
    d_inner = N_SSM_GROUPS * SSM_HEADS_PER_GROUP * SSM_HEAD_DIM
    bc_w = 2 * N_SSM_GROUPS * D_STATE
    rows = CHUNK * SSD_CHUNKS_PER_STEP
    nc = seq // rows
    gw = SSM_HEADS_PER_GROUP * SSM_HEAD_DIM
    expand = (np.arange(LANES)[:, None] == (np.arange(gw)[None, :] // SSM_HEAD_DIM)).astype(np.float32)
    rowmap = lambda b, c: (b * nc + c, 0)
    const = lambda b, c: (0, 0)
    return pl.pallas_call(
        _ssd_kernel,
        grid=(batch, nc),
        in_specs=[
            pl.BlockSpec((rows, d_inner), lambda b, c: (b * nc + c, 2)),
            pl.BlockSpec((rows, bc_w), lambda b, c: (b * nc + c, 6)),
            pl.BlockSpec((rows, d_inner), rowmap),
            pl.BlockSpec((rows, N_SSM_GROUPS * LANES), rowmap),
            pl.BlockSpec((CONV_WIDTH, d_inner), const),
            pl.BlockSpec((1, d_inner), const),
            pl.BlockSpec((CONV_WIDTH, bc_w), const),
            pl.BlockSpec((1, bc_w), const),
            pl.BlockSpec((1, N_SSM_GROUPS * LANES), const),
            pl.BlockSpec((1, N_SSM_GROUPS * LANES), const),
            pl.BlockSpec((1, d_inner), const),
            pl.BlockSpec((1, d_inner), const),
            pl.BlockSpec((LANES, gw), const),
        ],
        out_specs=pl.BlockSpec((rows, d_inner), rowmap),
        out_shape=jax.ShapeDtypeStruct((t, d_inner), BF16),
        scratch_shapes=[
            pltpu.VMEM((CONV_TAIL + CHUNK, d_inner), BF16),
            pltpu.VMEM((CONV_TAIL + CHUNK, bc_w), BF16),
            pltpu.VMEM((N_SSM_GROUPS, D_STATE, gw), F32),
        ],
        compiler_params=_params(("parallel", "arbitrary")),
        name="ssd_mixer",
    )(rest, rest, rest, dt_raw, conv_w[:, :d_inner], conv_b[:, :d_inner], conv_w[:, d_inner:],
      conv_b[:, d_inner:], dt_bias_p, a_p, d_skip_e, norm_w, jnp.asarray(expand, dtype=BF16))


def _merge_kernel(at_ref, ys_ref, gl_ref, x_ref, wa_ref, ws_ref,
                  wo_ref, gb_ref, nf_ref, rw_ref, rb_ref, x1_ref, h2_ref, ti_ref, tw_ref):
    d = x_ref.shape[1]
    sub = x_ref.shape[0] // MERGE_SUBTILES
    for part in range(MERGE_SUBTILES):
        rows = slice(part * sub, (part + 1) * sub)
        y_attn = jnp.dot(at_ref[rows, :], wa_ref[...], preferred_element_type=F32)
        y_ssm = jnp.dot(ys_ref[rows, :], ws_ref[...], preferred_element_type=F32)
        gv = gl_ref[rows, :].astype(F32) + gb_ref[...]
        gates = 1.0 / (1.0 + jnp.exp(-gv))
        merged = gates[:, :d] * y_attn + gates[:, d:] * y_ssm
        x1 = x_ref[rows, :] + jnp.dot(merged.astype(BF16), wo_ref[...], preferred_element_type=F32)
        x1_ref[rows, :] = x1
        h2 = x1 * lax.rsqrt(jnp.mean(x1 * x1, axis=-1, keepdims=True) + EPS) * nf_ref[...]
        _store_slabs(h2_ref, part * sub, sub, _pack_bf16_pairs(h2))
        h_hi = h2.astype(BF16)
        h_mid = (h2 - h_hi.astype(F32)).astype(BF16)
        both = jnp.dot(h_hi, rw_ref[...], preferred_element_type=F32)
        logits = (both[:, :LANES] + both[:, LANES:]
                  + jnp.dot(h_mid, rw_ref[:, :LANES], preferred_element_type=F32) + rb_ref[...])
        lane = lax.broadcasted_iota(jnp.int32, logits.shape, 1)
        top_i = jnp.zeros(logits.shape, jnp.int32)
        top_v = jnp.full(logits.shape, NEG_BIG, F32)
        work = logits
        for k in range(TOP_K):
            m = jnp.max(work, axis=-1, keepdims=True)
            idx = jnp.min(jnp.where(work == m, lane, LANES), axis=-1, keepdims=True)
            top_i = jnp.where(lane == k, idx, top_i)
            top_v = jnp.where(lane == k, m, top_v)
            work = jnp.where(lane == idx, NEG_BIG * 2.0, work)
        ev = jnp.exp(top_v - jnp.max(top_v, axis=-1, keepdims=True))
        ti_ref[rows, :] = top_i
        tw_ref[rows, :] = ev / jnp.sum(ev, axis=-1, keepdims=True)


def merge_project(attn, y_ssm, rest, x, wa, ws, wo, gate_bias, norm_ffn, router_w_p, router_b_p):
    t, d = x.shape
    slab = d // 2 // LANES
    tm = 256 * MERGE_SUBTILES
    d_inner = y_ssm.shape[1]
    rowmap = lambda i: (i, 0)
    const = lambda i: (0, 0)
    full = lambda a: pl.BlockSpec(a.shape, const)
    args = [attn, y_ssm, rest, x, wa, ws, wo, gate_bias, norm_ffn, router_w_p, router_b_p]
    in_specs = (
        [pl.BlockSpec((tm, GROUP_WIDTH), rowmap),
         pl.BlockSpec((tm, d_inner), rowmap),
         pl.BlockSpec((tm, 2 * d), lambda i: (i, 1)),
         pl.BlockSpec((tm, d), rowmap)]
        + [full(a) for a in args[4:]]
    )
    return pl.pallas_call(
        _merge_kernel,
        grid=(t // tm,),
        in_specs=in_specs,
        out_specs=[pl.BlockSpec((tm, d), rowmap), pl.BlockSpec((tm * slab, LANES), rowmap),
                   pl.BlockSpec((tm, LANES), rowmap), pl.BlockSpec((tm, LANES), rowmap)],
        out_shape=[jax.ShapeDtypeStruct((t, d), F32), jax.ShapeDtypeStruct((t * slab, LANES), jnp.uint32),
                   jax.ShapeDtypeStruct((t, LANES), jnp.int32), jax.ShapeDtypeStruct((t, LANES), F32)],
        compiler_params=_params(("parallel",)),
        name="merge_project",
    )(*args)


def _split_w1_kernel(w_ref, g_ref, l_ref, t_ref):
    de = g_ref.shape[1]
    for s in range(t_ref.shape[0]):
        cols = slice(s * LANES, (s + 1) * LANES)
        t_ref[s] = w_ref[0, cols, :].T
        g_ref[0, :, cols] = t_ref[s, pl.ds(0, de, stride=2), :].astype(g_ref.dtype)
        l_ref[0, :, cols] = t_ref[s, pl.ds(1, de, stride=2), :].astype(l_ref.dtype)


def split_w1(w1):
    e, d, de2 = w1.shape
    de = de2 // 2
    tk = 512
    out = jax.ShapeDtypeStruct((e, de, d), BF16)
    return pl.pallas_call(
        _split_w1_kernel,
        grid=(e, d // tk),
        in_specs=[pl.BlockSpec((1, tk, de2), lambda i, k: (i, k, 0))],
        out_specs=[pl.BlockSpec((1, de, tk), lambda i, k: (i, 0, k))] * 2,
        out_shape=[out, out],
        scratch_shapes=[pltpu.VMEM((tk // LANES, de2, LANES), F32)],
        compiler_params=_params(("parallel", "parallel")),
        name="split_w1",
    )(w1)


def _pack_bf16_pairs(v):
    w = v.shape[1] // 2
    lo = lax.bitcast_convert_type(v[:, :w].astype(BF16).astype(F32), jnp.uint32) >> 16
    hi = lax.bitcast_convert_type(v[:, w:].astype(BF16).astype(F32), jnp.uint32) & jnp.uint32(0xFFFF0000)
    return lo | hi


def _unpack_bf16_pairs(p):
    lo = lax.bitcast_convert_type(p << 16, F32)
    hi = lax.bitcast_convert_type(p & jnp.uint32(0xFFFF0000), F32)
    return lo, hi


def _store_slabs(ref, row0, rows, packed):
    slab = packed.shape[1] // LANES
    for s in range(slab):
        ref[pl.ds(row0 * slab + s, rows, stride=slab), :] = packed[:, s * LANES:(s + 1) * LANES]


def _load_slabs(ref, rows, slab):
    return jnp.concatenate([ref[pl.ds(s, rows, stride=slab), :] for s in range(slab)], axis=1)


def _expert_kernel(be_ref, nu_ref, g0_ref, g1_ref, g2_ref, sp_ref, sc_ref, h2p_ref, w1g_ref, w1l_ref, w2_ref, b1g_ref,
                   b1l_ref, b2_ref, y4p_ref, xbuf, obuf, gsem, ssem, *, n_tokens, t_pad):
    i = pl.program_id(0)
    n_used = nu_ref[0]
    slot = lax.rem(i, 2)
    other = 1 - slot
    gslot = lax.rem(i, 3)
    gslot1 = lax.rem(i + 1, 3)
    gslot2 = lax.rem(i + 2, 3)
    bm = EXPERT_ROWS
    slab = xbuf.shape[1] // bm

    def gather_copy(src_ref, j, buf):
        src = pl.multiple_of(src_ref[0, 0, j], slab)
        return pltpu.make_async_copy(h2p_ref.at[pl.ds(src, slab)],
                                     xbuf.at[buf, pl.ds(j * slab, slab)], gsem.at[buf])

    def scatter_copy(dst_ref, j, buf):
        dst = pl.multiple_of(dst_ref[0, 0, j], slab)
        return pltpu.make_async_copy(obuf.at[buf, pl.ds(j * slab, slab)],
                                     y4p_ref.at[pl.ds(dst, slab)], ssem.at[buf])

    def wait_block(kind, buf):
        if kind == "gather":
            pltpu.make_async_copy(h2p_ref.at[pl.ds(0, bm * slab)], xbuf.at[buf], gsem.at[buf]).wait()
        else:
            pltpu.make_async_copy(obuf.at[buf], y4p_ref.at[pl.ds(0, bm * slab)], ssem.at[buf]).wait()

    @pl.when(i == 0)
    def _():
        obuf[...] = jnp.zeros(obuf.shape, obuf.dtype)
        fills = [pltpu.make_async_copy(obuf.at[1], y4p_ref.at[pl.ds((k * t_pad + n_tokens) * slab + c * bm * slab,
                                                                     bm * slab)], ssem.at[1])
                 for k in range(TOP_K) for c in range((t_pad - n_tokens) // bm)]
        for fill in fills:
            fill.start()
        for fill in fills:
            fill.wait()
        for j in range(bm):
            gather_copy(g0_ref, j, 0).start()
        for j in range(bm):
            gather_copy(g1_ref, j, 1).start()
        spare_row = t_pad + t_pad - 2 * bm
        pltpu.make_async_copy(obuf.at[0], y4p_ref.at[pl.ds(spare_row * slab, bm * slab)], ssem.at[0]).start()

    @pl.when(i < n_used)
    def _():
        nt = (((1,), (1,)), ((), ()))
        wait_block("gather", gslot)
        lo, hi = _unpack_bf16_pairs(_load_slabs(xbuf.at[gslot], bm, slab))
        xb = jnp.concatenate([lo, hi], axis=1).astype(BF16)
        for j in range(bm):
            gather_copy(g2_ref, j, gslot2).start(priority=j % 2)
        for j in range(bm):
            scatter_copy(sp_ref, j, other).start(priority=j % 2)
        glu = lax.dot_general(xb, w1g_ref[0], nt, preferred_element_type=F32) + b1g_ref[0]
        lin = lax.dot_general(xb, w1l_ref[0], nt, preferred_element_type=F32) + b1l_ref[0]
        glu = jnp.minimum(glu, SWIGLU_LIMIT)
        lin = jnp.clip(lin, -SWIGLU_LIMIT, SWIGLU_LIMIT)
        act = glu * (1.0 / (1.0 + jnp.exp(-SWIGLU_ALPHA * glu))) * (lin + 1.0)
        y = jnp.dot(act.astype(BF16), w2_ref[0], preferred_element_type=F32) + b2_ref[0]
        wait_block("scatter", slot)
        _store_slabs(obuf.at[slot], 0, bm, _pack_bf16_pairs(y))

    @pl.when(i == n_used - 1)
    def _():
        for j in range(bm):
            scatter_copy(sc_ref, j, slot).start()
        wait_block("scatter", other)
        wait_block("scatter", slot)
        wait_block("gather", gslot1)
        wait_block("gather", gslot2)


def expert_ffn(h2p, ids, block_e, n_used, w1g, w1l, w2, b1g, b1l, b2, n_tokens):
    n_blocks, _, bm = ids.shape
    slab = h2p.shape[0] // n_tokens
    de, d = w1g.shape[1], w1g.shape[2]
    t_pad = padded_tokens(n_tokens)
    src = jnp.minimum(ids >> 2, n_tokens - 1) * slab
    dst = ((ids & 3) * t_pad + (ids >> 2)) * slab
    first = ((t_pad - 2 * bm + jnp.arange(bm, dtype=jnp.int32)) * slab).reshape(1, 1, bm)
    dst = jnp.concatenate([first, dst], axis=0)
    wmap = lambda i, be, nu: (be[i], 0, 0)
    smem_ids = lambda imap: pl.BlockSpec((1, 1, bm), imap, memory_space=pltpu.SMEM)
    grid_spec = pltpu.PrefetchScalarGridSpec(
        num_scalar_prefetch=2,
        grid=(n_blocks,),
        in_specs=[
            smem_ids(lambda i, be, nu: (i, 0, 0)),
            smem_ids(lambda i, be, nu: (jnp.minimum(i + 1, n_blocks - 1), 0, 0)),
            smem_ids(lambda i, be, nu: (jnp.minimum(i + 2, n_blocks - 1), 0, 0)),
            smem_ids(lambda i, be, nu: (i, 0, 0)),
            smem_ids(lambda i, be, nu: (i + 1, 0, 0)),
            pl.BlockSpec(memory_space=pl.ANY),
            pl.BlockSpec((1, de, d), wmap),
            pl.BlockSpec((1, de, d), wmap),
            pl.BlockSpec((1, de, d), wmap),
            pl.BlockSpec((1, 1, de), wmap),
            pl.BlockSpec((1, 1, de), wmap),
            pl.BlockSpec((1, 1, d), wmap),
        ],
        out_specs=pl.BlockSpec(memory_space=pl.ANY),
        scratch_shapes=[
            pltpu.VMEM((3, bm * slab, LANES), jnp.uint32),
            pltpu.VMEM((2, bm * slab, LANES), jnp.uint32),
            pltpu.SemaphoreType.DMA((3,)),
            pltpu.SemaphoreType.DMA((2,)),
        ],
    )
    return pl.pallas_call(
        functools.partial(_expert_kernel, n_tokens=n_tokens, t_pad=t_pad),
        grid_spec=grid_spec,
        out_shape=jax.ShapeDtypeStruct((TOP_K * t_pad * slab, LANES), jnp.uint32),
        compiler_params=_params(("arbitrary",)),
        name="expert_ffn",
    )(block_e, n_used, src, src, src, dst, dst, h2p, w1g, w1l, w2, b1g, b1l, b2)


def _combine_kernel(x1_ref, y0_ref, y1_ref, y2_ref, y3_ref, tw_ref, nw_ref, o_ref, *, normalize):
    tm, d = x1_ref.shape
    slab = y0_ref.shape[0] // tm
    lo_sum = jnp.zeros((tm, d // 2), F32)
    hi_sum = jnp.zeros((tm, d // 2), F32)
    for k, y_ref in enumerate((y0_ref, y1_ref, y2_ref, y3_ref)):
        lo, hi = _unpack_bf16_pairs(_load_slabs(y_ref, tm, slab))
        w = tw_ref[:, k:k + 1]
        lo_sum = lo_sum + w * lo
        hi_sum = hi_sum + w * hi
    acc = x1_ref[...] + jnp.concatenate([lo_sum, hi_sum], axis=1)
    if normalize:
        acc = acc * lax.rsqrt(jnp.mean(acc * acc, axis=-1, keepdims=True) + EPS) * nw_ref[...]
    o_ref[...] = acc


def combine_norm(x1, y4p, top_w, norm_w, normalize):
    t, d = x1.shape
    tm = 256
    t_pad = padded_tokens(t)
    slab = y4p.shape[0] // (TOP_K * t_pad)
    y_specs = [pl.BlockSpec((tm * slab, LANES), functools.partial(lambda i, k: (k * (t_pad // tm) + i, 0), k=k))
               for k in range(TOP_K)]
    return pl.pallas_call(
        functools.partial(_combine_kernel, normalize=normalize),
        grid=(t // tm,),
        in_specs=[pl.BlockSpec((tm, d), lambda i: (i, 0))] + y_specs
        + [pl.BlockSpec((tm, LANES), lambda i: (i, 0)), pl.BlockSpec((1, d), lambda i: (0, 0))],
        out_specs=pl.BlockSpec((tm, d), lambda i: (i, 0)),
        out_shape=jax.ShapeDtypeStruct((t, d), F32),
        compiler_params=_params(("parallel",)),
        name="combine_norm",
    )(x1, y4p, y4p, y4p, y4p, top_w, norm_w)


def routing_layout(top_i, n_tokens):
    n_assign = n_tokens * TOP_K
    bm = EXPERT_ROWS
    n_pad = N_EXPERTS * bm
    n_blocks = (n_assign + n_pad) // bm
    flat_e = top_i[:, :TOP_K].reshape(n_assign)
    counts = jnp.sum(flat_e[:, None] == jnp.arange(N_EXPERTS, dtype=jnp.int32)[None, :], axis=0, dtype=jnp.int32)
    padded = (counts + bm - 1) // bm * bm
    pend = jnp.cumsum(padded)
    n_used = (pend[-1] // bm).astype(jnp.int32)
    block_row = jnp.arange(n_blocks, dtype=jnp.int32) * bm
    block_e = jnp.minimum(jnp.sum(pend[None, :] <= block_row[:, None], axis=1, dtype=jnp.int32), N_EXPERTS - 1)
    spare_pos = jnp.arange(bm, dtype=jnp.int32)[None, :]
    spare_key = jnp.where(spare_pos < (padded - counts)[:, None], jnp.arange(N_EXPERTS, dtype=jnp.int32)[:, None],
                          N_EXPERTS).reshape(n_pad)
    keys = jnp.concatenate([flat_e, spare_key])
    pos_bits = (n_assign + n_pad - 1).bit_length()
    assert (N_EXPERTS + 1) << pos_bits < 2 ** 31
    packed = jnp.sort((keys << pos_bits) | jnp.arange(n_assign + n_pad, dtype=jnp.int32))
    order = packed & ((1 << pos_bits) - 1)
    return order.reshape(n_blocks, 1, bm), block_e, n_used.reshape(1)


def kernel(x, w_in, rel_bias, w_branch_attn, conv_w, conv_b, dt_bias, a_log, d_skip, ssm_norm_w,
           w_branch_ssm, gate_bias, w_out, norm_mix, norm_ffn, router_w, router_b, w1, b1, w2, b2,
           norm_final):
    batch, seq, d = x.shape
    t = batch * seq
    depth = w_in.shape[0]
    n_groups = len(ATTN_GROUPS)
    attn_w = n_groups * GROUP_WIDTH
    d_inner = N_SSM_GROUPS * SSM_HEADS_PER_GROUP * SSM_HEAD_DIM
    n_heads = N_SSM_GROUPS * SSM_HEADS_PER_GROUP
    bc_w = 2 * N_SSM_GROUPS * D_STATE
    xf = x.reshape(t, d)
    for l in range(depth):
        wl = w_in[l]
        o_z = 3 * attn_w
        o_xbc = o_z + d_inner
        o_dt = o_xbc + d_inner + bc_w
        o_gate = o_dt + n_heads
        w_qkv = wl[:, :o_z].astype(BF16)
        w_rest = jnp.concatenate([wl[:, o_z:o_xbc], wl[:, o_gate:], wl[:, o_xbc:o_dt]], axis=1).astype(BF16)
        lane_of_head = (np.arange(n_heads) // SSM_HEADS_PER_GROUP) * LANES + np.arange(n_heads) % SSM_HEADS_PER_GROUP
        w_dt = jnp.zeros((d, N_SSM_GROUPS * LANES), F32).at[:, lane_of_head].set(wl[:, o_dt:o_gate]).astype(BF16)
        dt_bias_p = jnp.zeros((1, N_SSM_GROUPS * LANES), F32).at[0, lane_of_head].set(dt_bias[l].astype(F32))
        a_p = jnp.zeros((1, N_SSM_GROUPS * LANES), F32).at[0, lane_of_head].set(-jnp.exp(a_log[l].astype(F32)))
        d_skip_e = jnp.repeat(d_skip[l].astype(F32), SSM_HEAD_DIM)[None, :]

        g_mix = norm_mix[l].astype(F32)[None, :]
        rest, dt_raw, h = rms_matmul(xf, g_mix, w_rest, w_dt, tn=1792)
        qkv_groups = qkv_project(h, w_qkv, batch, seq)
        bias = jnp.stack([attention_bias(rel_bias, gi) for gi in range(n_groups)])
        attn = dilated_attention(qkv_groups, bias, batch, seq)
        y_ssm = ssd_mixer(rest, dt_raw, conv_w[l].astype(F32), conv_b[l].astype(F32)[None, :], dt_bias_p, a_p,
                          d_skip_e, ssm_norm_w[l].astype(F32)[None, :], batch, seq)
        rw = jnp.zeros((d, LANES), F32).at[:, :N_EXPERTS].set(router_w[l].astype(F32))
        rw_hi = rw.astype(BF16)
        router_w_p = jnp.concatenate([rw_hi, (rw - rw_hi.astype(F32)).astype(BF16)], axis=1)
        router_b_p = jnp.full((1, LANES), NEG_BIG, F32).at[0, :N_EXPERTS].set(router_b[l].astype(F32))
        x1, h2, top_i, top_w = merge_project(
            attn, y_ssm, rest, xf, w_branch_attn[l].astype(BF16), w_branch_ssm[l].astype(BF16),
            w_out[l].astype(BF16), gate_bias[l].astype(F32)[None, :], norm_ffn[l].astype(F32)[None, :],
            router_w_p, router_b_p)

        ids, block_e, n_used = routing_layout(top_i, t)
        w1g_t, w1l_t = split_w1(w1[l].astype(F32))
        y4p = expert_ffn(h2, ids, block_e, n_used, w1g_t, w1l_t, w2[l].astype(BF16),
                         b1[l][:, None, 0::2].astype(F32), b1[l][:, None, 1::2].astype(F32),
                         b2[l][:, None, :].astype(F32), t)
        xf = combine_norm(x1, y4p, top_w, norm_final.astype(F32)[None, :], normalize=(l == depth - 1))
    return xf.reshape(batch, seq, d)
```

```python
import functools
import math

import jax
import jax.numpy as jnp
import numpy as np
from jax import lax
from jax.experimental import pallas as pl
from jax.experimental.pallas import tpu as pltpu

F32 = jnp.float32
BF16 = jnp.bfloat16

EPS = 1e-5
NEG_BIG = -1e30

HEAD_DIM = 64
ATTN_GROUPS = ((128, 1), (512, 4), (2048, 16))
HEADS_PER_GROUP = 8
GROUP_WIDTH = HEADS_PER_GROUP * HEAD_DIM
ATTN_BLOCK = 128
NUM_BUCKETS = 32
MAX_DISTANCE = 2048
SSM_HEAD_DIM = 64
N_SSM_GROUPS = 4
SSM_HEADS_PER_GROUP = 8
D_STATE = 128
CONV_WIDTH = 4
CHUNK = 128
N_EXPERTS = 32
TOP_K = 4
SWIGLU_LIMIT = 7.0
SWIGLU_ALPHA = 1.702

LANES = 128
V7X_VMEM_BYTES = 64 * 1024 * 1024
VMEM_LIMIT = 48 * 1024 * 1024

ROW_TILE = 1024
MERGE_SUBTILES = 2
EXPERT_ROWS = 256


def padded_tokens(n_tokens):
    return n_tokens + N_EXPERTS * EXPERT_ROWS // TOP_K + 2 * EXPERT_ROWS


def _params(semantics):
    return pltpu.CompilerParams(dimension_semantics=semantics, vmem_limit_bytes=VMEM_LIMIT)


def _rms_matmul_kernel(x_ref, g_ref, w_ref, ws_ref, o_ref, os_ref, hb_ref, h_ref):
    @pl.when(pl.program_id(1) == 0)
    def _():
        x = x_ref[...]
        ms = jnp.mean(x * x, axis=-1, keepdims=True)
        h_ref[...] = (x * lax.rsqrt(ms + EPS) * g_ref[...]).astype(BF16)
        hb_ref[...] = h_ref[...]
        os_ref[...] = jnp.dot(h_ref[...], ws_ref[...], preferred_element_type=F32)

    o_ref[...] = jnp.dot(h_ref[...], w_ref[...], preferred_element_type=F32).astype(o_ref.dtype)


def rms_matmul(x, g, w, w_side, tn):
    t, d = x.shape
    n = w.shape[1]
    ns = w_side.shape[1]
    tm = ROW_TILE
    return pl.pallas_call(
        _rms_matmul_kernel,
        grid=(t // tm, n // tn),
        in_specs=[
            pl.BlockSpec((tm, d), lambda i, j: (i, 0)),
            pl.BlockSpec((1, d), lambda i, j: (0, 0)),
            pl.BlockSpec((d, tn), lambda i, j: (0, j)),
            pl.BlockSpec((d, ns), lambda i, j: (0, 0)),
        ],
        out_specs=[
            pl.BlockSpec((tm, tn), lambda i, j: (i, j)),
            pl.BlockSpec((tm, ns), lambda i, j: (i, 0)),
            pl.BlockSpec((tm, d), lambda i, j: (i, 0)),
        ],
        out_shape=[jax.ShapeDtypeStruct((t, n), BF16), jax.ShapeDtypeStruct((t, ns), F32),
                   jax.ShapeDtypeStruct((t, d), BF16)],
        scratch_shapes=[pltpu.VMEM((tm, d), BF16)],
        compiler_params=_params(("parallel", "arbitrary")),
        name="rms_matmul",
    )(x, g, w, w_side)


def _qkv_kernel(h_ref, w_ref, o0_ref, o1_ref, o2_ref, acc_ref):
    seq = h_ref.shape[0]
    slabs = GROUP_WIDTH // LANES
    n_blocks = seq // ATTN_BLOCK
    for gi, o_ref in enumerate((o0_ref, o1_ref, o2_ref)):
        dil = ATTN_GROUPS[gi][1]
        nb = n_blocks // dil
        acc = jnp.dot(h_ref[...], w_ref[:, gi * GROUP_WIDTH:(gi + 1) * GROUP_WIDTH], preferred_element_type=F32)
        if dil == 1:
            for n in range(n_blocks):
                o_ref[0, 0, n] = acc[n * ATTN_BLOCK:(n + 1) * ATTN_BLOCK, :].astype(o_ref.dtype)
            continue
        for s in range(slabs):
            acc_ref[s] = acc[:, s * LANES:(s + 1) * LANES]
        for r in range(dil):
            for n in range(nb):
                for s in range(slabs):
                    rows = pl.ds(r + n * ATTN_BLOCK * dil, ATTN_BLOCK, stride=dil)
                    o_ref[0, 0, r * nb + n, :, s * LANES:(s + 1) * LANES] = acc_ref[s, rows, :].astype(o_ref.dtype)


def qkv_project(h, w_qkv, batch, seq):
    t, d = h.shape
    n_groups = len(ATTN_GROUPS)
    tn = n_groups * GROUP_WIDTH
    out_shapes, out_specs = [], []
    for _ in ATTN_GROUPS:
        shape = (3, batch, seq // ATTN_BLOCK, ATTN_BLOCK, GROUP_WIDTH)
        out_shapes.append(jax.ShapeDtypeStruct(shape, BF16))
        out_specs.append(pl.BlockSpec((1, 1) + shape[2:], lambda w, b: (w, b, 0, 0, 0)))
    return pl.pallas_call(
        _qkv_kernel,
        grid=(3, batch),
        in_specs=[pl.BlockSpec((seq, d), lambda w, b: (b, 0)), pl.BlockSpec((d, tn), lambda w, b: (0, w))],
        out_specs=out_specs,
        out_shape=out_shapes,
        scratch_shapes=[pltpu.VMEM((GROUP_WIDTH // LANES, seq, LANES), F32)],
        compiler_params=_params(("parallel", "parallel")),
        name="qkv_project",
    )(h, w_qkv)


def _attn_kernel(q0_ref, q1_ref, q2_ref, bias_ref, o_ref, out_ref, lse_ref):
    heads = LANES // HEAD_DIM
    scale = jnp.asarray(HEAD_DIM ** -0.5, BF16)
    n_blocks = q0_ref.shape[2]
    qk = (((2,), (2,)), ((0,), (0,)))
    pv = (((2,), (1,)), ((0,), (0,)))
    blk = lax.broadcasted_iota(jnp.int32, (n_blocks, 1, 1), 0)
    lane = lax.broadcasted_iota(jnp.int32, (1, 1, LANES), 2)

    def shifted(x):
        return jnp.concatenate([x[n_blocks - 1:], x[:n_blocks - 1]], axis=0)

    for g, ref in enumerate((q0_ref, q1_ref, q2_ref)):
        dil = ATTN_GROUPS[g][1]
        nb = n_blocks // dil
        q2h = ref[0, 0] * scale
        keys, vals = ref[1, 0], ref[2, 0]
        if nb > 1:
            keys = jnp.concatenate([shifted(keys), keys], axis=1)
            vals = jnp.concatenate([shifted(vals), vals], axis=1)
        n_keys = keys.shape[1]
        vals = jnp.concatenate([vals, jnp.ones(vals.shape, BF16)], axis=2)
        pvs, dens, maxs = [], [], []
        for h in range(heads):
            in_head = (lane >= h * HEAD_DIM) & (lane < (h + 1) * HEAD_DIM)
            q = jnp.where(in_head, q2h, jnp.zeros_like(q2h))
            s = lax.dot_general(q, keys, qk, preferred_element_type=F32)
            s = s + bias_ref[g, h, :, 2 * ATTN_BLOCK - n_keys:][None]
            if nb > 1:
                key_is_prev = lax.broadcasted_iota(jnp.int32, (1, 1, n_keys), 2) < ATTN_BLOCK
                s = jnp.where((blk % nb == 0) & key_is_prev, NEG_BIG, s)
            m = jnp.max(s, axis=-1, keepdims=True)
            p = jnp.exp(s - m).astype(BF16)
            acc = lax.dot_general(p, vals, pv, preferred_element_type=F32)
            pvs.append(acc[:, :, :LANES])
            dens.append(acc[:, :, LANES:])
            maxs.append(m)
        first = lane < HEAD_DIM
        den = jnp.where(first, dens[0], dens[1])
        o2 = jnp.where(first, pvs[0], pvs[1]) / den
        l2 = jnp.where(first, maxs[0], maxs[1]) + jnp.log(den)
        for r in range(dil):
            for n in range(nb):
                start = r + n * ATTN_BLOCK * dil
                rows = pl.ds(start, ATTN_BLOCK) if dil == 1 else pl.ds(start, ATTN_BLOCK, stride=dil)
                out_ref[g, rows, :] = o2[r * nb + n]
                lse_ref[g, rows, :] = l2[r * nb + n]

    l0, l1, l2 = lse_ref[0], lse_ref[1], lse_ref[2]
    lm = jnp.maximum(jnp.maximum(l0, l1), l2)
    e0, e1, e2 = jnp.exp(l0 - lm), jnp.exp(l1 - lm), jnp.exp(l2 - lm)
    mixed = (e0 * out_ref[0] + e1 * out_ref[1] + e2 * out_ref[2]) / (e0 + e1 + e2)
    o_ref[...] = mixed.astype(o_ref.dtype)


def dilated_attention(qkv_groups, bias, batch, seq):
    heads = LANES // HEAD_DIM
    in_specs = [pl.BlockSpec((3, 1) + a.shape[2:4] + (LANES,), lambda b, hp: (0, b, 0, 0, hp)) for a in qkv_groups]
    in_specs.append(pl.BlockSpec((len(ATTN_GROUPS), heads, ATTN_BLOCK, 2 * ATTN_BLOCK), lambda b, hp: (0, hp, 0, 0)))
    return pl.pallas_call(
        _attn_kernel,
        grid=(batch, GROUP_WIDTH // LANES),
        in_specs=in_specs,
        out_specs=pl.BlockSpec((seq, LANES), lambda b, hp: (b, hp)),
        out_shape=jax.ShapeDtypeStruct((batch * seq, GROUP_WIDTH), BF16),
        scratch_shapes=[pltpu.VMEM((len(ATTN_GROUPS), seq, LANES), F32),
                        pltpu.VMEM((len(ATTN_GROUPS), seq, LANES), F32)],
        compiler_params=_params(("parallel", "parallel")),
        name="dilated_attn",
    )(*qkv_groups, bias)


def attention_bias(rel_bias, gi):
    window, dil = ATTN_GROUPS[gi]
    w_sub = window // dil
    q_idx = np.arange(ATTN_BLOCK)[:, None]
    k_idx = np.arange(2 * ATTN_BLOCK)[None, :]
    delta = q_idx + ATTN_BLOCK - k_idx
    in_band = (delta >= 0) & (delta <= w_sub)
    dist = np.clip(delta, 0, w_sub) * dil
    max_exact = NUM_BUCKETS // 2
    nf = np.maximum(dist, max_exact).astype(np.float32)
    large = max_exact + (np.log(nf / max_exact) / math.log(MAX_DISTANCE / max_exact)
                         * (NUM_BUCKETS - max_exact)).astype(np.int32)
    large = np.minimum(large, NUM_BUCKETS - 1)
    bucket = np.where(dist < max_exact, dist, large)
    table = rel_bias[:, gi * HEADS_PER_GROUP:(gi + 1) * HEADS_PER_GROUP].astype(F32)
    onehot = (bucket.reshape(-1, 1) == np.arange(NUM_BUCKETS)[None, :]).astype(np.float32)
    bias = jnp.einsum('bh,nb->hn', table, jnp.asarray(onehot), precision=lax.Precision.HIGHEST)
    bias = bias.reshape(HEADS_PER_GROUP, ATTN_BLOCK, 2 * ATTN_BLOCK)
    return jnp.where(in_band[None], bias, NEG_BIG)


def _silu(v):
    half = 0.5 * v
    return half + half * jnp.tanh(half)


SSD_CHUNKS_PER_STEP = 4
CONV_TAIL = 16


def _conv_silu(ext_ref, u_ref, w_ref, b_ref, first):
    rows = u_ref.shape[0]
    ext_rows = ext_ref.shape[0]

    if first is not None:
        @pl.when(first)
        def _():
            ext_ref[0:CONV_TAIL, :] = jnp.zeros((CONV_TAIL, ext_ref.shape[1]), ext_ref.dtype)

    u = u_ref[...]
    ext_ref[CONV_TAIL:, :] = u
    taps = CONV_WIDTH - 1
    out_row = lax.broadcasted_iota(jnp.int32, (rows, ext_rows), 0)
    src_row = lax.broadcasted_iota(jnp.int32, (rows, ext_rows), 1)
    shift_mat = jnp.concatenate([(src_row == out_row + (CONV_TAIL - k)).astype(BF16) for k in range(1, CONV_WIDTH)],
                                axis=0)
    shifted = jnp.dot(shift_mat, ext_ref[...], preferred_element_type=F32)
    acc = u.astype(F32) * w_ref[taps:CONV_WIDTH, :] + b_ref[...]
    for k in range(1, CONV_WIDTH):
        acc = acc + shifted[(k - 1) * rows:k * rows, :] * w_ref[taps - k:CONV_WIDTH - k, :]
    ext_ref[0:CONV_TAIL, :] = ext_ref[rows:, :]
    return _silu(acc)


def _ssd_kernel(x_ref, bc_ref, z_ref, dt_ref, cwx_ref, cbx_ref, cwbc_ref, cbbc_ref, dtb_ref, a_ref,
                dskip_ref, nw_ref, expand_ref, o_ref, extx_ref, extbc_ref, state_ref):
    first = pl.program_id(1) == 0
    gw = SSM_HEADS_PER_GROUP * SSM_HEAD_DIM

    @pl.when(first)
    def _():
        state_ref[...] = jnp.zeros(state_ref.shape, F32)

    row = lax.broadcasted_iota(jnp.int32, (CHUNK, CHUNK), 0)
    colm = lax.broadcasted_iota(jnp.int32, (CHUNK, CHUNK), 1)
    tril = row >= colm
    tril_b = tril.astype(BF16)
    head_rows = 8
    eye_b = (lax.broadcasted_iota(jnp.int32, (head_rows, CHUNK), 0)
             == lax.broadcasted_iota(jnp.int32, (head_rows, CHUNK), 1)).astype(BF16)
    nt = (((1,), (1,)), ((), ()))
    tn = (((0,), (0,)), ((), ()))

    def split3(v):
        p0 = v.astype(BF16)
        r1 = v - p0.astype(F32)
        p1 = r1.astype(BF16)
        p2 = (r1 - p1.astype(F32)).astype(BF16)
        return p0, p1, p2

    def select_rows(mat01, v):
        return sum(jnp.dot(mat01, p, preferred_element_type=F32) for p in split3(v))

    def select_cols(v, mat01):
        return sum(jnp.dot(p, mat01, preferred_element_type=F32) for p in split3(v)[:2])

    expand = expand_ref[...]
    for sub, g in [(sub, g) for sub in range(x_ref.shape[0] // CHUNK) for g in range(N_SSM_GROUPS)]:
        rws = pl.ds(sub * CHUNK, CHUNK)
        if g == 0:
            sub_first = first if sub == 0 else None
            xs_all = _conv_silu(extx_ref, x_ref.at[rws], cwx_ref, cbx_ref, sub_first)
            bc_all = _conv_silu(extbc_ref, bc_ref.at[rws], cwbc_ref, cbbc_ref, sub_first)
        lanes = slice(g * LANES, (g + 1) * LANES)
        ch = slice(g * gw, (g + 1) * gw)
        v = dt_ref[rws, lanes] + dtb_ref[:, lanes]
        dt = jnp.maximum(v, 0.0) + jnp.log1p(jnp.exp(-jnp.abs(v)))
        a_d = dt * a_ref[:, lanes]
        acs = select_rows(tril_b, a_d)
        acs_t = sum(lax.dot_general(eye_b, p, nt, preferred_element_type=F32) for p in split3(acs))
        last = acs[CHUNK - 1:CHUNK, :]
        dt_e = select_cols(dt, expand)
        eacs_e = select_cols(jnp.exp(acs), expand)
        edec_e = select_cols(jnp.exp(last - acs), expand)
        elast_e = eacs_e[CHUNK - 1:CHUNK, :]

        xs = xs_all[:, ch]
        bm = bc_all[:, lanes].astype(BF16)
        cm = bc_all[:, N_SSM_GROUPS * D_STATE + g * D_STATE:N_SSM_GROUPS * D_STATE + (g + 1) * D_STATE].astype(BF16)
        x_d = xs * dt_e
        cb = lax.dot_general(cm, bm, nt, preferred_element_type=F32)

        prev = state_ref[g]
        y = jnp.dot(cm, prev.astype(BF16), preferred_element_type=F32) * eacs_e
        x_d16 = x_d.astype(BF16)
        parts = []
        for j in range(SSM_HEADS_PER_GROUP):
            seg = acs[:, j:j + 1] - acs_t[j:j + 1, :]
            m_h = (cb * jnp.exp(jnp.where(tril, seg, NEG_BIG))).astype(BF16)
            parts.append(jnp.dot(m_h, x_d16[:, j * SSM_HEAD_DIM:(j + 1) * SSM_HEAD_DIM],
                                 preferred_element_type=F32))
        y = y + jnp.concatenate(parts, axis=1)

        xw = (x_d * edec_e).astype(BF16)
        state_ref[g] = prev * elast_e + lax.dot_general(bm, xw, tn, preferred_element_type=F32)

        y = y + dskip_ref[:, ch] * xs
        y = y * _silu(z_ref[rws, ch].astype(F32))
        y = y * lax.rsqrt(jnp.mean(y * y, axis=-1, keepdims=True) + EPS)
        o_ref[rws, ch] = (y * nw_ref[:, ch]).astype(o_ref.dtype)


def ssd_mixer(rest, dt_raw, conv_w, conv_b, dt_bias_p, a_p, d_skip_e, norm_w, batch, seq):
    t = rest.shape[0]
    d_inner = N_SSM_GROUPS * SSM_HEADS_PER_GROUP * SSM_HEAD_DIM
    bc_w = 2 * N_SSM_GROUPS * D_STATE
    rows = CHUNK * SSD_CHUNKS_PER_STEP
    nc = seq // rows
    gw = SSM_HEADS_PER_GROUP * SSM_HEAD_DIM
    expand = (np.arange(LANES)[:, None] == (np.arange(gw)[None, :] // SSM_HEAD_DIM)).astype(np.float32)
    rowmap = lambda b, c: (b * nc + c, 0)
    const = lambda b, c: (0, 0)
    return pl.pallas_call(
        _ssd_kernel,
        grid=(batch, nc),
        in_specs=[
            pl.BlockSpec((rows, d_inner), lambda b, c: (b * nc + c, 2)),
            pl.BlockSpec((rows, bc_w), lambda b, c: (b * nc + c, 6)),
            pl.BlockSpec((rows, d_inner), rowmap),
            pl.BlockSpec((rows, N_SSM_GROUPS * LANES), rowmap),
            pl.BlockSpec((CONV_WIDTH, d_inner), const),
            pl.BlockSpec((1, d_inner), const),
            pl.BlockSpec((CONV_WIDTH, bc_w), const),
            pl.BlockSpec((1, bc_w), const),
            pl.BlockSpec((1, N_SSM_GROUPS * LANES), const),
            pl.BlockSpec((1, N_SSM_GROUPS * LANES), const),
            pl.BlockSpec((1, d_inner), const),
            pl.BlockSpec((1, d_inner), const),
            pl.BlockSpec((LANES, gw), const),
        ],
        out_specs=pl.BlockSpec((rows, d_inner), rowmap),
        out_shape=jax.ShapeDtypeStruct((t, d_inner), BF16),
        scratch_shapes=[
            pltpu.VMEM((CONV_TAIL + CHUNK, d_inner), BF16),
            pltpu.VMEM((CONV_TAIL + CHUNK, bc_w), BF16),
            pltpu.VMEM((N_SSM_GROUPS, D_STATE, gw), F32),
        ],
        compiler_params=_params(("parallel", "arbitrary")),
        name="ssd_mixer",
    )(rest, rest, rest, dt_raw, conv_w[:, :d_inner], conv_b[:, :d_inner], conv_w[:, d_inner:],
      conv_b[:, d_inner:], dt_bias_p, a_p, d_skip_e, norm_w, jnp.asarray(expand, dtype=BF16))


def _merge_kernel(at_ref, ys_ref, gl_ref, x_ref, wa_ref, ws_ref,
                  wo_ref, gb_ref, nf_ref, rw_ref, rb_ref, x1_ref, h2_ref, ti_ref, tw_ref):
    d = x_ref.shape[1]
    sub = x_ref.shape[0] // MERGE_SUBTILES
    for part in range(MERGE_SUBTILES):
        rows = slice(part * sub, (part + 1) * sub)
        y_attn = jnp.dot(at_ref[rows, :], wa_ref[...], preferred_element_type=F32)
        y_ssm = jnp.dot(ys_ref[rows, :], ws_ref[...], preferred_element_type=F32)
        gv = gl_ref[rows, :].astype(F32) + gb_ref[...]
        gates = 1.0 / (1.0 + jnp.exp(-gv))
        merged = gates[:, :d] * y_attn + gates[:, d:] * y_ssm
        x1 = x_ref[rows, :] + jnp.dot(merged.astype(BF16), wo_ref[...], preferred_element_type=F32)
        x1_ref[rows, :] = x1
        h2 = x1 * lax.rsqrt(jnp.mean(x1 * x1, axis=-1, keepdims=True) + EPS) * nf_ref[...]
        _store_slabs(h2_ref, part * sub, sub, _pack_bf16_pairs(h2))
        h_hi = h2.astype(BF16)
        h_mid = (h2 - h_hi.astype(F32)).astype(BF16)
        both = jnp.dot(h_hi, rw_ref[...], preferred_element_type=F32)
        logits = (both[:, :LANES] + both[:, LANES:]
                  + jnp.dot(h_mid, rw_ref[:, :LANES], preferred_element_type=F32) + rb_ref[...])
        lane = lax.broadcasted_iota(jnp.int32, logits.shape, 1)
        top_i = jnp.zeros(logits.shape, jnp.int32)
        top_v = jnp.full(logits.shape, NEG_BIG, F32)
        work = logits
        for k in range(TOP_K):
            m = jnp.max(work, axis=-1, keepdims=True)
            idx = jnp.min(jnp.where(work == m, lane, LANES), axis=-1, keepdims=True)
            top_i = jnp.where(lane == k, idx, top_i)
            top_v = jnp.where(lane == k, m, top_v)
            work = jnp.where(lane == idx, NEG_BIG * 2.0, work)
        ev = jnp.exp(top_v - jnp.max(top_v, axis=-1, keepdims=True))
        ti_ref[rows, :] = top_i
        tw_ref[rows, :] = ev / jnp.sum(ev, axis=-1, keepdims=True)


def merge_project(attn, y_ssm, rest, x, wa, ws, wo, gate_bias, norm_ffn, router_w_p, router_b_p):
    t, d = x.shape
    slab = d // 2 // LANES
    tm = 256 * MERGE_SUBTILES
    d_inner = y_ssm.shape[1]
    rowmap = lambda i: (i, 0)
    const = lambda i: (0, 0)
    full = lambda a: pl.BlockSpec(a.shape, const)
    args = [attn, y_ssm, rest, x, wa, ws, wo, gate_bias, norm_ffn, router_w_p, router_b_p]
    in_specs = (
        [pl.BlockSpec((tm, GROUP_WIDTH), rowmap),
         pl.BlockSpec((tm, d_inner), rowmap),
         pl.BlockSpec((tm, 2 * d), lambda i: (i, 1)),
         pl.BlockSpec((tm, d), rowmap)]
        + [full(a) for a in args[4:]]
    )
    return pl.pallas_call(
        _merge_kernel,
        grid=(t // tm,),
        in_specs=in_specs,
        out_specs=[pl.BlockSpec((tm, d), rowmap), pl.BlockSpec((tm * slab, LANES), rowmap),
                   pl.BlockSpec((tm, LANES), rowmap), pl.BlockSpec((tm, LANES), rowmap)],
        out_shape=[jax.ShapeDtypeStruct((t, d), F32), jax.ShapeDtypeStruct((t * slab, LANES), jnp.uint32),
                   jax.ShapeDtypeStruct((t, LANES), jnp.int32), jax.ShapeDtypeStruct((t, LANES), F32)],
        compiler_params=_params(("parallel",)),
        name="merge_project",
    )(*args)


def _split_w1_kernel(w_ref, g_ref, l_ref, t_ref):
    de = g_ref.shape[1]
    for s in range(t_ref.shape[0]):
        cols = slice(s * LANES, (s + 1) * LANES)
        t_ref[s] = w_ref[0, cols, :].T
        g_ref[0, :, cols] = t_ref[s, pl.ds(0, de, stride=2), :].astype(g_ref.dtype)
        l_ref[0, :, cols] = t_ref[s, pl.ds(1, de, stride=2), :].astype(l_ref.dtype)


def split_w1(w1):
    e, d, de2 = w1.shape
    de = de2 // 2
    tk = 512
    out = jax.ShapeDtypeStruct((e, de, d), BF16)
    return pl.pallas_call(
        _split_w1_kernel,
        grid=(e, d // tk),
        in_specs=[pl.BlockSpec((1, tk, de2), lambda i, k: (i, k, 0))],
        out_specs=[pl.BlockSpec((1, de, tk), lambda i, k: (i, 0, k))] * 2,
        out_shape=[out, out],
        scratch_shapes=[pltpu.VMEM((tk // LANES, de2, LANES), F32)],
        compiler_params=_params(("parallel", "parallel")),
        name="split_w1",
    )(w1)


def _pack_bf16_pairs(v):
    w = v.shape[1] // 2
    lo = lax.bitcast_convert_type(v[:, :w].astype(BF16).astype(F32), jnp.uint32) >> 16
    hi = lax.bitcast_convert_type(v[:, w:].astype(BF16).astype(F32), jnp.uint32) & jnp.uint32(0xFFFF0000)
    return lo | hi


def _unpack_bf16_pairs(p):
    lo = lax.bitcast_convert_type(p << 16, F32)
    hi = lax.bitcast_convert_type(p & jnp.uint32(0xFFFF0000), F32)
    return lo, hi


def _store_slabs(ref, row0, rows, packed):
    slab = packed.shape[1] // LANES
    for s in range(slab):
        ref[pl.ds(row0 * slab + s, rows, stride=slab), :] = packed[:, s * LANES:(s + 1) * LANES]


def _load_slabs(ref, rows, slab):
    return jnp.concatenate([ref[pl.ds(s, rows, stride=slab), :] for s in range(slab)], axis=1)


def _expert_kernel(be_ref, nu_ref, tab_ref, h2p_ref, w1g_ref, w1l_ref, w2_ref, bias_ref, y4p_ref, xbuf, obuf, gsem, ssem,
                   *, n_tokens, t_pad):
    i = pl.program_id(0)
    n_used = nu_ref[0]
    slot = lax.rem(i, 2)
    other = 1 - slot
    gslot = lax.rem(i, 3)
    gslot1 = lax.rem(i + 1, 3)
    gslot2 = lax.rem(i + 2, 3)
    bm = EXPERT_ROWS
    slab = xbuf.shape[1] // bm
    g0, g1, g2, sp, sc = (k * bm for k in range(5))

    def gather_copy(table, j, buf):
        src = pl.multiple_of(tab_ref[0, 0, table + j], slab)
        return pltpu.make_async_copy(h2p_ref.at[pl.ds(src, slab)],
                                     xbuf.at[buf, pl.ds(j * slab, slab)], gsem.at[buf])

    def scatter_copy(table, j, buf):
        dst = pl.multiple_of(tab_ref[0, 0, table + j], slab)
        return pltpu.make_async_copy(obuf.at[buf, pl.ds(j * slab, slab)],
                                     y4p_ref.at[pl.ds(dst, slab)], ssem.at[buf])

    def wait_block(kind, buf):
        if kind == "gather":
            pltpu.make_async_copy(h2p_ref.at[pl.ds(0, bm * slab)], xbuf.at[buf], gsem.at[buf]).wait()
        else:
            pltpu.make_async_copy(obuf.at[buf], y4p_ref.at[pl.ds(0, bm * slab)], ssem.at[buf]).wait()

    @pl.when(i == 0)
    def _():
        obuf[...] = jnp.zeros(obuf.shape, obuf.dtype)
        fills = [pltpu.make_async_copy(obuf.at[1], y4p_ref.at[pl.ds((k * t_pad + n_tokens) * slab + c * bm * slab,
                                                                     bm * slab)], ssem.at[1])
                 for k in range(TOP_K) for c in range((t_pad - n_tokens) // bm)]
        for fill in fills:
            fill.start()
        for fill in fills:
            fill.wait()
        for j in range(bm):
            gather_copy(g0, j, 0).start()
        for j in range(bm):
            gather_copy(g1, j, 1).start()
        spare_row = t_pad + t_pad - 2 * bm
        pltpu.make_async_copy(obuf.at[0], y4p_ref.at[pl.ds(spare_row * slab, bm * slab)], ssem.at[0]).start()

    @pl.when(i < n_used)
    def _():
        nt = (((1,), (1,)), ((), ()))
        de = w1g_ref.shape[1]
        wait_block("gather", gslot)
        lo, hi = _unpack_bf16_pairs(_load_slabs(xbuf.at[gslot], bm, slab))
        xb = jnp.concatenate([lo, hi], axis=1).astype(BF16)
        for j in range(bm):
            gather_copy(g2, j, gslot2).start(priority=j % 2)
        for j in range(bm):
            scatter_copy(sp, j, other).start(priority=j % 2)
        glu = lax.dot_general(xb, w1g_ref[0], nt, preferred_element_type=F32) + bias_ref[0, :, :de]
        lin = lax.dot_general(xb, w1l_ref[0], nt, preferred_element_type=F32) + bias_ref[0, :, de:2 * de]
        glu = jnp.minimum(glu, SWIGLU_LIMIT)
        lin = jnp.clip(lin, -SWIGLU_LIMIT, SWIGLU_LIMIT)
        act = glu * (1.0 / (1.0 + jnp.exp(-SWIGLU_ALPHA * glu))) * (lin + 1.0)
        y = jnp.dot(act.astype(BF16), w2_ref[0], preferred_element_type=F32) + bias_ref[0, :, 2 * de:]
        wait_block("scatter", slot)
        _store_slabs(obuf.at[slot], 0, bm, _pack_bf16_pairs(y))

    @pl.when(i == n_used - 1)
    def _():
        for j in range(bm):
            scatter_copy(sc, j, slot).start()
        wait_block("scatter", other)
        wait_block("scatter", slot)
        wait_block("gather", gslot1)
        wait_block("gather", gslot2)


def expert_ffn(h2p, ids, block_e, n_used, w1g, w1l, w2, b1g, b1l, b2, n_tokens):
    n_blocks, _, bm = ids.shape
    slab = h2p.shape[0] // n_tokens
    de, d = w1g.shape[1], w1g.shape[2]
    t_pad = padded_tokens(n_tokens)
    src = jnp.minimum(ids >> 2, n_tokens - 1) * slab
    dst = ((ids & 3) * t_pad + (ids >> 2)) * slab
    first = ((t_pad - 2 * bm + jnp.arange(bm, dtype=jnp.int32)) * slab).reshape(1, 1, bm)
    ahead = lambda a, k: jnp.concatenate([a[k:]] + [a[-1:]] * k, axis=0)
    table = jnp.concatenate([src, ahead(src, 1), ahead(src, 2), jnp.concatenate([first, dst[:-1]], axis=0), dst], axis=2)
    bias = jnp.concatenate([b1g, b1l, b2], axis=2)
    wmap = lambda i, be, nu: (be[i], 0, 0)
    grid_spec = pltpu.PrefetchScalarGridSpec(
        num_scalar_prefetch=2,
        grid=(n_blocks,),
        in_specs=[
            pl.BlockSpec((1, 1, 5 * bm), lambda i, be, nu: (i, 0, 0), memory_space=pltpu.SMEM),
            pl.BlockSpec(memory_space=pl.ANY),
            pl.BlockSpec((1, de, d), wmap),
            pl.BlockSpec((1, de, d), wmap),
            pl.BlockSpec((1, de, d), wmap),
            pl.BlockSpec((1, 1, 2 * de + d), wmap),
        ],
        out_specs=pl.BlockSpec(memory_space=pl.ANY),
        scratch_shapes=[
            pltpu.VMEM((3, bm * slab, LANES), jnp.uint32),
            pltpu.VMEM((2, bm * slab, LANES), jnp.uint32),
            pltpu.SemaphoreType.DMA((3,)),
            pltpu.SemaphoreType.DMA((2,)),
        ],
    )
    return pl.pallas_call(
        functools.partial(_expert_kernel, n_tokens=n_tokens, t_pad=t_pad),
        grid_spec=grid_spec,
        out_shape=jax.ShapeDtypeStruct((TOP_K * t_pad * slab, LANES), jnp.uint32),
        compiler_params=_params(("arbitrary",)),
        name="expert_ffn",
    )(block_e, n_used, table, h2p, w1g, w1l, w2, bias)


def _combine_kernel(x1_ref, y0_ref, y1_ref, y2_ref, y3_ref, tw_ref, nw_ref, o_ref, *, normalize):
    tm, d = x1_ref.shape
    slab = y0_ref.shape[0] // tm
    lo_sum = jnp.zeros((tm, d // 2), F32)
    hi_sum = jnp.zeros((tm, d // 2), F32)
    for k, y_ref in enumerate((y0_ref, y1_ref, y2_ref, y3_ref)):
        lo, hi = _unpack_bf16_pairs(_load_slabs(y_ref, tm, slab))
        w = tw_ref[:, k:k + 1]
        lo_sum = lo_sum + w * lo
        hi_sum = hi_sum + w * hi
    acc = x1_ref[...] + jnp.concatenate([lo_sum, hi_sum], axis=1)
    if normalize:
        acc = acc * lax.rsqrt(jnp.mean(acc * acc, axis=-1, keepdims=True) + EPS) * nw_ref[...]
    o_ref[...] = acc


def combine_norm(x1, y4p, top_w, norm_w, normalize):
    t, d = x1.shape
    tm = 256
    t_pad = padded_tokens(t)
    slab = y4p.shape[0] // (TOP_K * t_pad)
    y_specs = [pl.BlockSpec((tm * slab, LANES), functools.partial(lambda i, k: (k * (t_pad // tm) + i, 0), k=k))
               for k in range(TOP_K)]
    return pl.pallas_call(
        functools.partial(_combine_kernel, normalize=normalize),
        grid=(t // tm,),
        in_specs=[pl.BlockSpec((tm, d), lambda i: (i, 0))] + y_specs
        + [pl.BlockSpec((tm, LANES), lambda i: (i, 0)), pl.BlockSpec((1, d), lambda i: (0, 0))],
        out_specs=pl.BlockSpec((tm, d), lambda i: (i, 0)),
        out_shape=jax.ShapeDtypeStruct((t, d), F32),
        compiler_params=_params(("parallel",)),
        name="combine_norm",
    )(x1, y4p, y4p, y4p, y4p, top_w, norm_w)


def routing_layout(top_i, n_tokens):
    n_assign = n_tokens * TOP_K
    bm = EXPERT_ROWS
    n_pad = N_EXPERTS * bm
    n_blocks = (n_assign + n_pad) // bm
    flat_e = top_i[:, :TOP_K].reshape(n_assign)
    counts = jnp.sum(flat_e[:, None] == jnp.arange(N_EXPERTS, dtype=jnp.int32)[None, :], axis=0, dtype=jnp.int32)
    padded = (counts + bm - 1) // bm * bm
    pend = jnp.cumsum(padded)
    n_used = (pend[-1] // bm).astype(jnp.int32)
    block_row = jnp.arange(n_blocks, dtype=jnp.int32) * bm
    block_e = jnp.minimum(jnp.sum(pend[None, :] <= block_row[:, None], axis=1, dtype=jnp.int32), N_EXPERTS - 1)
    spare_pos = jnp.arange(bm, dtype=jnp.int32)[None, :]
    spare_key = jnp.where(spare_pos < (padded - counts)[:, None], jnp.arange(N_EXPERTS, dtype=jnp.int32)[:, None],
                          N_EXPERTS).reshape(n_pad)
    keys = jnp.concatenate([flat_e, spare_key])
    pos_bits = (n_assign + n_pad - 1).bit_length()
    assert (N_EXPERTS + 1) << pos_bits < 2 ** 31
    packed = jnp.sort((keys << pos_bits) | jnp.arange(n_assign + n_pad, dtype=jnp.int32))
    order = packed & ((1 << pos_bits) - 1)
    return order.reshape(n_blocks, 1, bm), block_e, n_used.reshape(1)


def kernel(x, w_in, rel_bias, w_branch_attn, conv_w, conv_b, dt_bias, a_log, d_skip, ssm_norm_w,
           w_branch_ssm, gate_bias, w_out, norm_mix, norm_ffn, router_w, router_b, w1, b1, w2, b2,
           norm_final):
    batch, seq, d = x.shape
    t = batch * seq
    depth = w_in.shape[0]
    n_groups = len(ATTN_GROUPS)
    attn_w = n_groups * GROUP_WIDTH
    d_inner = N_SSM_GROUPS * SSM_HEADS_PER_GROUP * SSM_HEAD_DIM
    n_heads = N_SSM_GROUPS * SSM_HEADS_PER_GROUP
    bc_w = 2 * N_SSM_GROUPS * D_STATE
    xf = x.reshape(t, d)
    for l in range(depth):
        wl = w_in[l]
        o_z = 3 * attn_w
        o_xbc = o_z + d_inner
        o_dt = o_xbc + d_inner + bc_w
        o_gate = o_dt + n_heads
        w_qkv = wl[:, :o_z].astype(BF16)
        w_rest = jnp.concatenate([wl[:, o_z:o_xbc], wl[:, o_gate:], wl[:, o_xbc:o_dt]], axis=1).astype(BF16)
        lane_of_head = (np.arange(n_heads) // SSM_HEADS_PER_GROUP) * LANES + np.arange(n_heads) % SSM_HEADS_PER_GROUP
        w_dt = jnp.zeros((d, N_SSM_GROUPS * LANES), F32).at[:, lane_of_head].set(wl[:, o_dt:o_gate]).astype(BF16)
        dt_bias_p = jnp.zeros((1, N_SSM_GROUPS * LANES), F32).at[0, lane_of_head].set(dt_bias[l].astype(F32))
        a_p = jnp.zeros((1, N_SSM_GROUPS * LANES), F32).at[0, lane_of_head].set(-jnp.exp(a_log[l].astype(F32)))
        d_skip_e = jnp.repeat(d_skip[l].astype(F32), SSM_HEAD_DIM)[None, :]

        g_mix = norm_mix[l].astype(F32)[None, :]
        rest, dt_raw, h = rms_matmul(xf, g_mix, w_rest, w_dt, tn=1792)
        qkv_groups = qkv_project(h, w_qkv, batch, seq)
        bias = jnp.stack([attention_bias(rel_bias, gi) for gi in range(n_groups)])
        attn = dilated_attention(qkv_groups, bias, batch, seq)
        y_ssm = ssd_mixer(rest, dt_raw, conv_w[l].astype(F32), conv_b[l].astype(F32)[None, :], dt_bias_p, a_p,
                          d_skip_e, ssm_norm_w[l].astype(F32)[None, :], batch, seq)
        rw = jnp.zeros((d, LANES), F32).at[:, :N_EXPERTS].set(router_w[l].astype(F32))
        rw_hi = rw.astype(BF16)
        router_w_p = jnp.concatenate([rw_hi, (rw - rw_hi.astype(F32)).astype(BF16)], axis=1)
        router_b_p = jnp.full((1, LANES), NEG_BIG, F32).at[0, :N_EXPERTS].set(router_b[l].astype(F32))
        x1, h2, top_i, top_w = merge_project(
            attn, y_ssm, rest, xf, w_branch_attn[l].astype(BF16), w_branch_ssm[l].astype(BF16),
            w_out[l].astype(BF16), gate_bias[l].astype(F32)[None, :], norm_ffn[l].astype(F32)[None, :],
            router_w_p, router_b_p)

        ids, block_e, n_used = routing_layout(top_i, t)
        w1g_t, w1l_t = split_w1(w1[l].astype(F32))
        y4p = expert_ffn(h2, ids, block_e, n_used, w1g_t, w1l_t, w2[l].astype(BF16),
                         b1[l][:, None, 0::2].astype(F32), b1[l][:, None, 1::2].astype(F32),
                         b2[l][:, None, :].astype(F32), t)
        xf = combine_norm(x1, y4p, top_w, norm_final.astype(F32)[None, :], normalize=(l == depth - 1))
    return xf.reshape(batch, seq, d)
```

```python
import functools
import math

import jax
import jax.numpy as jnp
import numpy as np
from jax import lax
from jax.experimental import pallas as pl
from jax.experimental.pallas import tpu as pltpu

F32 = jnp.float32
BF16 = jnp.bfloat16

EPS = 1e-5
NEG_BIG = -1e30

HEAD_DIM = 64
ATTN_GROUPS = ((128, 1), (512, 4), (2048, 16))
HEADS_PER_GROUP = 8
GROUP_WIDTH = HEADS_PER_GROUP * HEAD_DIM
ATTN_BLOCK = 128
NUM_BUCKETS = 32
MAX_DISTANCE = 2048
SSM_HEAD_DIM = 64
N_SSM_GROUPS = 4
SSM_HEADS_PER_GROUP = 8
D_STATE = 128
CONV_WIDTH = 4
CHUNK = 128
N_EXPERTS = 32
TOP_K = 4
SWIGLU_LIMIT = 7.0
SWIGLU_ALPHA = 1.702

LANES = 128
V7X_VMEM_BYTES = 64 * 1024 * 1024
VMEM_LIMIT = 48 * 1024 * 1024

ROW_TILE = 1024
MERGE_SUBTILES = 4
EXPERT_ROWS = 256


def padded_tokens(n_tokens):
    return n_tokens + N_EXPERTS * EXPERT_ROWS // TOP_K + 2 * EXPERT_ROWS


def _params(semantics):
    return pltpu.CompilerParams(dimension_semantics=semantics, vmem_limit_bytes=VMEM_LIMIT)


def _rms_matmul_kernel(x_ref, g_ref, w_ref, ws_ref, o_ref, os_ref, hb_ref, h_ref):
    @pl.when(pl.program_id(1) == 0)
    def _():
        x = x_ref[...]
        ms = jnp.mean(x * x, axis=-1, keepdims=True)
        h_ref[...] = (x * lax.rsqrt(ms + EPS) * g_ref[...]).astype(BF16)
        hb_ref[...] = h_ref[...]
        os_ref[...] = jnp.dot(h_ref[...], ws_ref[...], preferred_element_type=F32)

    o_ref[...] = jnp.dot(h_ref[...], w_ref[...], preferred_element_type=F32).astype(o_ref.dtype)


def rms_matmul(x, g, w, w_side, tn):
    t, d = x.shape
    n = w.shape[1]
    ns = w_side.shape[1]
    tm = ROW_TILE
    return pl.pallas_call(
        _rms_matmul_kernel,
        grid=(t // tm, n // tn),
        in_specs=[
            pl.BlockSpec((tm, d), lambda i, j: (i, 0)),
            pl.BlockSpec((1, d), lambda i, j: (0, 0)),
            pl.BlockSpec((d, tn), lambda i, j: (0, j)),
            pl.BlockSpec((d, ns), lambda i, j: (0, 0)),
        ],
        out_specs=[
            pl.BlockSpec((tm, tn), lambda i, j: (i, j)),
            pl.BlockSpec((tm, ns), lambda i, j: (i, 0)),
            pl.BlockSpec((tm, d), lambda i, j: (i, 0)),
        ],
        out_shape=[jax.ShapeDtypeStruct((t, n), BF16), jax.ShapeDtypeStruct((t, ns), F32),
                   jax.ShapeDtypeStruct((t, d), BF16)],
        scratch_shapes=[pltpu.VMEM((tm, d), BF16)],
        compiler_params=_params(("parallel", "arbitrary")),
        name="rms_matmul",
    )(x, g, w, w_side)


def _qkv_kernel(h_ref, w_ref, o0_ref, o1_ref, o2_ref, acc_ref, tmp_ref):
    seq = h_ref.shape[0]
    slabs = GROUP_WIDTH // LANES
    n_blocks = seq // ATTN_BLOCK
    for gi, o_ref in enumerate((o0_ref, o1_ref, o2_ref)):
        dil = ATTN_GROUPS[gi][1]
        nb = n_blocks // dil
        acc = jnp.dot(h_ref[...], w_ref[:, gi * GROUP_WIDTH:(gi + 1) * GROUP_WIDTH], preferred_element_type=F32)
        if dil == 1:
            for n in range(n_blocks):
                o_ref[0, 0, n] = acc[n * ATTN_BLOCK:(n + 1) * ATTN_BLOCK, :].astype(o_ref.dtype)
            continue
        for s in range(slabs):
            acc_ref[s] = acc[:, s * LANES:(s + 1) * LANES]
        src_ref, step, group_rows = acc_ref, dil, seq
        if dil == 16:
            for s in range(slabs):
                for a in range(4):
                    tmp_ref[s, a * (seq // 4):(a + 1) * (seq // 4), :] = acc_ref[s, pl.ds(a, seq // 4, stride=4), :]
            src_ref, step, group_rows = tmp_ref, 4, seq // 4
        for r in range(dil):
            base = (r % (dil // step)) * group_rows + r // (dil // step)
            for n in range(nb):
                for s in range(slabs):
                    rows = pl.ds(base + n * ATTN_BLOCK * step, ATTN_BLOCK, stride=step)
                    o_ref[0, 0, r * nb + n, :, s * LANES:(s + 1) * LANES] = src_ref[s, rows, :].astype(o_ref.dtype)


def qkv_project(h, w_qkv, batch, seq):
    t, d = h.shape
    n_groups = len(ATTN_GROUPS)
    tn = n_groups * GROUP_WIDTH
    out_shapes, out_specs = [], []
    for _ in ATTN_GROUPS:
        shape = (3, batch, seq // ATTN_BLOCK, ATTN_BLOCK, GROUP_WIDTH)
        out_shapes.append(jax.ShapeDtypeStruct(shape, BF16))
        out_specs.append(pl.BlockSpec((1, 1) + shape[2:], lambda w, b: (w, b, 0, 0, 0)))
    return pl.pallas_call(
        _qkv_kernel,
        grid=(3, batch),
        in_specs=[pl.BlockSpec((seq, d), lambda w, b: (b, 0)), pl.BlockSpec((d, tn), lambda w, b: (0, w))],
        out_specs=out_specs,
        out_shape=out_shapes,
        scratch_shapes=[pltpu.VMEM((GROUP_WIDTH // LANES, seq, LANES), F32)] * 2,
        compiler_params=_params(("parallel", "parallel")),
        name="qkv_project",
    )(h, w_qkv)


def _attn_kernel(q0_ref, q1_ref, q2_ref, bias_ref, o_ref, out_ref, lse_ref):
    heads = LANES // HEAD_DIM
    scale = jnp.asarray(HEAD_DIM ** -0.5, BF16)
    n_blocks = q0_ref.shape[2]
    qk = (((2,), (2,)), ((0,), (0,)))
    pv = (((2,), (1,)), ((0,), (0,)))
    blk = lax.broadcasted_iota(jnp.int32, (n_blocks, 1, 1), 0)
    lane = lax.broadcasted_iota(jnp.int32, (1, 1, LANES), 2)

    def shifted(x):
        return jnp.concatenate([x[n_blocks - 1:], x[:n_blocks - 1]], axis=0)

    for g, ref in enumerate((q0_ref, q1_ref, q2_ref)):
        dil = ATTN_GROUPS[g][1]
        nb = n_blocks // dil
        q2h = ref[0, 0] * scale
        keys, vals = ref[1, 0], ref[2, 0]
        if nb > 1:
            keys = jnp.concatenate([shifted(keys), keys], axis=1)
            vals = jnp.concatenate([shifted(vals), vals], axis=1)
        n_keys = keys.shape[1]
        vals = jnp.concatenate([vals, jnp.ones(vals.shape, BF16)], axis=2)
        pvs, dens, maxs = [], [], []
        for h in range(heads):
            in_head = (lane >= h * HEAD_DIM) & (lane < (h + 1) * HEAD_DIM)
            q = jnp.where(in_head, q2h, jnp.zeros_like(q2h))
            s = lax.dot_general(q, keys, qk, preferred_element_type=F32)
            s = s + bias_ref[g, h, :, 2 * ATTN_BLOCK - n_keys:][None]
            if nb > 1:
                key_is_prev = lax.broadcasted_iota(jnp.int32, (1, 1, n_keys), 2) < ATTN_BLOCK
                s = jnp.where((blk % nb == 0) & key_is_prev, NEG_BIG, s)
            m = jnp.max(s, axis=-1, keepdims=True)
            p = jnp.exp(s - m).astype(BF16)
            acc = lax.dot_general(p, vals, pv, preferred_element_type=F32)
            pvs.append(acc[:, :, :LANES])
            dens.append(acc[:, :, LANES:])
            maxs.append(m)
        first = lane < HEAD_DIM
        den = jnp.where(first, dens[0], dens[1])
        o2 = jnp.where(first, pvs[0], pvs[1]) / den
        l2 = jnp.where(first, maxs[0], maxs[1]) + jnp.log(den)
        for r in range(dil):
            for n in range(nb):
                start = r + n * ATTN_BLOCK * dil
                rows = pl.ds(start, ATTN_BLOCK) if dil == 1 else pl.ds(start, ATTN_BLOCK, stride=dil)
                out_ref[g, rows, :] = o2[r * nb + n]
                lse_ref[g, rows, :] = l2[r * nb + n]

    l0, l1, l2 = lse_ref[0], lse_ref[1], lse_ref[2]
    lm = jnp.maximum(jnp.maximum(l0, l1), l2)
    e0, e1, e2 = jnp.exp(l0 - lm), jnp.exp(l1 - lm), jnp.exp(l2 - lm)
    mixed = (e0 * out_ref[0] + e1 * out_ref[1] + e2 * out_ref[2]) / (e0 + e1 + e2)
    o_ref[...] = mixed.astype(o_ref.dtype)


def dilated_attention(qkv_groups, bias, batch, seq):
    heads = LANES // HEAD_DIM
    in_specs = [pl.BlockSpec((3, 1) + a.shape[2:4] + (LANES,), lambda b, hp: (0, b, 0, 0, hp)) for a in qkv_groups]
    in_specs.append(pl.BlockSpec((len(ATTN_GROUPS), heads, ATTN_BLOCK, 2 * ATTN_BLOCK), lambda b, hp: (0, hp, 0, 0)))
    return pl.pallas_call(
        _attn_kernel,
        grid=(batch, GROUP_WIDTH // LANES),
        in_specs=in_specs,
        out_specs=pl.BlockSpec((seq, LANES), lambda b, hp: (b, hp)),
        out_shape=jax.ShapeDtypeStruct((batch * seq, GROUP_WIDTH), BF16),
        scratch_shapes=[pltpu.VMEM((len(ATTN_GROUPS), seq, LANES), F32),
                        pltpu.VMEM((len(ATTN_GROUPS), seq, LANES), F32)],
        compiler_params=_params(("parallel", "parallel")),
        name="dilated_attn",
    )(*qkv_groups, bias)


def attention_bias(rel_bias, gi):
    window, dil = ATTN_GROUPS[gi]
    w_sub = window // dil
    q_idx = np.arange(ATTN_BLOCK)[:, None]
    k_idx = np.arange(2 * ATTN_BLOCK)[None, :]
    delta = q_idx + ATTN_BLOCK - k_idx
    in_band = (delta >= 0) & (delta <= w_sub)
    dist = np.clip(delta, 0, w_sub) * dil
    max_exact = NUM_BUCKETS // 2
    nf = np.maximum(dist, max_exact).astype(np.float32)
    large = max_exact + (np.log(nf / max_exact) / math.log(MAX_DISTANCE / max_exact)
                         * (NUM_BUCKETS - max_exact)).astype(np.int32)
    large = np.minimum(large, NUM_BUCKETS - 1)
    bucket = np.where(dist < max_exact, dist, large)
    table = rel_bias[:, gi * HEADS_PER_GROUP:(gi + 1) * HEADS_PER_GROUP].astype(F32)
    onehot = (bucket.reshape(-1, 1) == np.arange(NUM_BUCKETS)[None, :]).astype(np.float32)
    bias = jnp.einsum('bh,nb->hn', table, jnp.asarray(onehot), precision=lax.Precision.HIGHEST)
    bias = bias.reshape(HEADS_PER_GROUP, ATTN_BLOCK, 2 * ATTN_BLOCK)
    return jnp.where(in_band[None], bias, NEG_BIG)


def _silu(v):
    half = 0.5 * v
    return half + half * jnp.tanh(half)


SSD_CHUNKS_PER_STEP = 4
CONV_TAIL = 16


def _conv_silu(ext_ref, u_ref, w_ref, b_ref, first):
    rows = u_ref.shape[0]
    ext_rows = ext_ref.shape[0]

    if first is not None:
        @pl.when(first)
        def _():
            ext_ref[0:CONV_TAIL, :] = jnp.zeros((CONV_TAIL, ext_ref.shape[1]), ext_ref.dtype)

    u = u_ref[...]
    ext_ref[CONV_TAIL:, :] = u
    taps = CONV_WIDTH - 1
    out_row = lax.broadcasted_iota(jnp.int32, (rows, ext_rows), 0)
    src_row = lax.broadcasted_iota(jnp.int32, (rows, ext_rows), 1)
    shift_mat = jnp.concatenate([(src_row == out_row + (CONV_TAIL - k)).astype(BF16) for k in range(1, CONV_WIDTH)],
                                axis=0)
    shifted = jnp.dot(shift_mat, ext_ref[...], preferred_element_type=F32)
    acc = u.astype(F32) * w_ref[taps:CONV_WIDTH, :] + b_ref[...]
    for k in range(1, CONV_WIDTH):
        acc = acc + shifted[(k - 1) * rows:k * rows, :] * w_ref[taps - k:CONV_WIDTH - k, :]
    ext_ref[0:CONV_TAIL, :] = ext_ref[rows:, :]
    return _silu(acc)


def _ssd_kernel(x_ref, bc_ref, z_ref, dt_ref, cwx_ref, cbx_ref, cwbc_ref, cbbc_ref, dtb_ref, a_ref,
                dskip_ref, nw_ref, expand_ref, o_ref, extx_ref, extbc_ref, state_ref):
    first = pl.program_id(1) == 0
    gw = SSM_HEADS_PER_GROUP * SSM_HEAD_DIM

    @pl.when(first)
    def _():
        state_ref[...] = jnp.zeros(state_ref.shape, F32)

    row = lax.broadcasted_iota(jnp.int32, (CHUNK, CHUNK), 0)
    colm = lax.broadcasted_iota(jnp.int32, (CHUNK, CHUNK), 1)
    tril = row >= colm
    tril_b = tril.astype(BF16)
    head_rows = 8
    eye_b = (lax.broadcasted_iota(jnp.int32, (head_rows, CHUNK), 0)
             == lax.broadcasted_iota(jnp.int32, (head_rows, CHUNK), 1)).astype(BF16)
    nt = (((1,), (1,)), ((), ()))
    tn = (((0,), (0,)), ((), ()))

    def split3(v):
        p0 = v.astype(BF16)
        r1 = v - p0.astype(F32)
        p1 = r1.astype(BF16)
        p2 = (r1 - p1.astype(F32)).astype(BF16)
        return p0, p1, p2

    def select_rows(mat01, v):
        return sum(jnp.dot(mat01, p, preferred_element_type=F32) for p in split3(v))

    def select_cols(v, mat01):
        return sum(jnp.dot(p, mat01, preferred_element_type=F32) for p in split3(v)[:2])

    expand = expand_ref[...]
    for sub, g in [(sub, g) for sub in range(x_ref.shape[0] // CHUNK) for g in range(N_SSM_GROUPS)]:
        rws = pl.ds(sub * CHUNK, CHUNK)
        if g == 0:
            sub_first = first if sub == 0 else None
            xs_all = _conv_silu(extx_ref, x_ref.at[rws], cwx_ref, cbx_ref, sub_first)
            bc_all = _conv_silu(extbc_ref, bc_ref.at[rws], cwbc_ref, cbbc_ref, sub_first)
        lanes = slice(g * LANES, (g + 1) * LANES)
        ch = slice(g * gw, (g + 1) * gw)
        v = dt_ref[rws, lanes] + dtb_ref[:, lanes]
        dt = jnp.maximum(v, 0.0) + jnp.log1p(jnp.exp(-jnp.abs(v)))
        a_d = dt * a_ref[:, lanes]
        acs = select_rows(tril_b, a_d)
        acs_t = sum(lax.dot_general(eye_b, p, nt, preferred_element_type=F32) for p in split3(acs))
        last = acs[CHUNK - 1:CHUNK, :]
        dt_e = select_cols(dt, expand)
        eacs_e = select_cols(jnp.exp(acs), expand)
        edec_e = select_cols(jnp.exp(last - acs), expand)
        elast_e = eacs_e[CHUNK - 1:CHUNK, :]

        xs = xs_all[:, ch]
        bm = bc_all[:, lanes].astype(BF16)
        cm = bc_all[:, N_SSM_GROUPS * D_STATE + g * D_STATE:N_SSM_GROUPS * D_STATE + (g + 1) * D_STATE].astype(BF16)
        x_d = xs * dt_e
        cb = lax.dot_general(cm, bm, nt, preferred_element_type=F32)

        prev = state_ref[g]
        y = jnp.dot(cm, prev.astype(BF16), preferred_element_type=F32) * eacs_e
        x_d16 = x_d.astype(BF16)
        parts = []
        for j in range(SSM_HEADS_PER_GROUP):
            seg = acs[:, j:j + 1] - acs_t[j:j + 1, :]
            m_h = (cb * jnp.exp(jnp.where(tril, seg, NEG_BIG))).astype(BF16)
            parts.append(jnp.dot(m_h, x_d16[:, j * SSM_HEAD_DIM:(j + 1) * SSM_HEAD_DIM],
                                 preferred_element_type=F32))
        y = y + jnp.concatenate(parts, axis=1)

        xw = (x_d * edec_e).astype(BF16)
        state_ref[g] = prev * elast_e + lax.dot_general(bm, xw, tn, preferred_element_type=F32)

        y = y + dskip_ref[:, ch] * xs
        y = y * _silu(z_ref[rws, ch].astype(F32))
        y = y * lax.rsqrt(jnp.mean(y * y, axis=-1, keepdims=True) + EPS)
        o_ref[rws, ch] = (y * nw_ref[:, ch]).astype(o_ref.dtype)


def ssd_mixer(rest, dt_raw, conv_w, conv_b, dt_bias_p, a_p, d_skip_e, norm_w, batch, seq):
    t = rest.shape[0]
    d_inner = N_SSM_GROUPS * SSM_HEADS_PER_GROUP * SSM_HEAD_DIM
    bc_w = 2 * N_SSM_GROUPS * D_STATE
    rows = CHUNK * SSD_CHUNKS_PER_STEP
    nc = seq // rows
    gw = SSM_HEADS_PER_GROUP * SSM_HEAD_DIM
    expand = (np.arange(LANES)[:, None] == (np.arange(gw)[None, :] // SSM_HEAD_DIM)).astype(np.float32)
    rowmap = lambda b, c: (b * nc + c, 0)
    const = lambda b, c: (0, 0)
    return pl.pallas_call(
        _ssd_kernel,
        grid=(batch, nc),
        in_specs=[
            pl.BlockSpec((rows, d_inner), lambda b, c: (b * nc + c, 2)),
            pl.BlockSpec((rows, bc_w), lambda b, c: (b * nc + c, 6)),
            pl.BlockSpec((rows, d_inner), rowmap),
            pl.BlockSpec((rows, N_SSM_GROUPS * LANES), rowmap),
            pl.BlockSpec((CONV_WIDTH, d_inner), const),
            pl.BlockSpec((1, d_inner), const),
            pl.BlockSpec((CONV_WIDTH, bc_w), const),
            pl.BlockSpec((1, bc_w), const),
            pl.BlockSpec((1, N_SSM_GROUPS * LANES), const),
            pl.BlockSpec((1, N_SSM_GROUPS * LANES), const),
            pl.BlockSpec((1, d_inner), const),
            pl.BlockSpec((1, d_inner), const),
            pl.BlockSpec((LANES, gw), const),
        ],
        out_specs=pl.BlockSpec((rows, d_inner), rowmap),
        out_shape=jax.ShapeDtypeStruct((t, d_inner), BF16),
        scratch_shapes=[
            pltpu.VMEM((CONV_TAIL + CHUNK, d_inner), BF16),
            pltpu.VMEM((CONV_TAIL + CHUNK, bc_w), BF16),
            pltpu.VMEM((N_SSM_GROUPS, D_STATE, gw), F32),
        ],
        compiler_params=_params(("parallel", "arbitrary")),
        name="ssd_mixer",
    )(rest, rest, rest, dt_raw, conv_w[:, :d_inner], conv_b[:, :d_inner], conv_w[:, d_inner:],
      conv_b[:, d_inner:], dt_bias_p, a_p, d_skip_e, norm_w, jnp.asarray(expand, dtype=BF16))


def _merge_kernel(at_ref, ys_ref, gl_ref, x_ref, wa_ref, ws_ref,
                  wo_ref, gb_ref, nf_ref, rw_ref, rb_ref, x1_ref, h2_ref, ti_ref, tw_ref):
    d = x_ref.shape[1]
    sub = x_ref.shape[0] // MERGE_SUBTILES
    for part in range(MERGE_SUBTILES):
        rows = slice(part * sub, (part + 1) * sub)
        y_attn = jnp.dot(at_ref[rows, :], wa_ref[...], preferred_element_type=F32)
        y_ssm = jnp.dot(ys_ref[rows, :], ws_ref[...], preferred_element_type=F32)
        gv = gl_ref[rows, :].astype(F32) + gb_ref[...]
        gates = 1.0 / (1.0 + jnp.exp(-gv))
        merged = gates[:, :d] * y_attn + gates[:, d:] * y_ssm
        x1 = x_ref[rows, :] + jnp.dot(merged.astype(BF16), wo_ref[...], preferred_element_type=F32)
        x1_ref[rows, :] = x1
        h2 = x1 * lax.rsqrt(jnp.mean(x1 * x1, axis=-1, keepdims=True) + EPS) * nf_ref[...]
        _store_slabs(h2_ref, part * sub, sub, _pack_bf16_pairs(h2))
        h_hi = h2.astype(BF16)
        h_mid = (h2 - h_hi.astype(F32)).astype(BF16)
        both = jnp.dot(h_hi, rw_ref[...], preferred_element_type=F32)
        logits = (both[:, :LANES] + both[:, LANES:]
                  + jnp.dot(h_mid, rw_ref[:, :LANES], preferred_element_type=F32) + rb_ref[...])
        lane = lax.broadcasted_iota(jnp.int32, logits.shape, 1)
        top_i = jnp.zeros(logits.shape, jnp.int32)
        top_v = jnp.full(logits.shape, NEG_BIG, F32)
        work = logits
        for k in range(TOP_K):
            m = jnp.max(work, axis=-1, keepdims=True)
            idx = jnp.min(jnp.where(work == m, lane, LANES), axis=-1, keepdims=True)
            top_i = jnp.where(lane == k, idx, top_i)
            top_v = jnp.where(lane == k, m, top_v)
            work = jnp.where(lane == idx, NEG_BIG * 2.0, work)
        ev = jnp.exp(top_v - jnp.max(top_v, axis=-1, keepdims=True))
        ti_ref[rows, :] = top_i
        tw_ref[rows, :] = ev / jnp.sum(ev, axis=-1, keepdims=True)


def merge_project(attn, y_ssm, rest, x, wa, ws, wo, gate_bias, norm_ffn, router_w_p, router_b_p):
    t, d = x.shape
    slab = d // 2 // LANES
    tm = 256 * MERGE_SUBTILES
    d_inner = y_ssm.shape[1]
    rowmap = lambda i: (i, 0)
    const = lambda i: (0, 0)
    full = lambda a: pl.BlockSpec(a.shape, const, pipeline_mode=pl.Buffered(1))
    args = [attn, y_ssm, rest, x, wa, ws, wo, gate_bias, norm_ffn, router_w_p, router_b_p]
    in_specs = (
        [pl.BlockSpec((tm, GROUP_WIDTH), rowmap),
         pl.BlockSpec((tm, d_inner), rowmap),
         pl.BlockSpec((tm, 2 * d), lambda i: (i, 1)),
         pl.BlockSpec((tm, d), rowmap)]
        + [full(a) for a in args[4:]]
    )
    return pl.pallas_call(
        _merge_kernel,
        grid=(t // tm,),
        in_specs=in_specs,
        out_specs=[pl.BlockSpec((tm, d), rowmap), pl.BlockSpec((tm * slab, LANES), rowmap),
                   pl.BlockSpec((tm, LANES), rowmap), pl.BlockSpec((tm, LANES), rowmap)],
        out_shape=[jax.ShapeDtypeStruct((t, d), F32), jax.ShapeDtypeStruct((t * slab, LANES), jnp.uint32),
                   jax.ShapeDtypeStruct((t, LANES), jnp.int32), jax.ShapeDtypeStruct((t, LANES), F32)],
        compiler_params=_params(("parallel",)),
        name="merge_project",
    )(*args)


def _split_w1_kernel(w_ref, g_ref, l_ref, t_ref):
    de = g_ref.shape[1]
    for s in range(t_ref.shape[0]):
        cols = slice(s * LANES, (s + 1) * LANES)
        t_ref[s] = w_ref[0, cols, :].T
        g_ref[0, :, cols] = t_ref[s, pl.ds(0, de, stride=2), :].astype(g_ref.dtype)
        l_ref[0, :, cols] = t_ref[s, pl.ds(1, de, stride=2), :].astype(l_ref.dtype)


def split_w1(w1):
    e, d, de2 = w1.shape
    de = de2 // 2
    tk = 512
    out = jax.ShapeDtypeStruct((e, de, d), BF16)
    return pl.pallas_call(
        _split_w1_kernel,
        grid=(e, d // tk),
        in_specs=[pl.BlockSpec((1, tk, de2), lambda i, k: (i, k, 0))],
        out_specs=[pl.BlockSpec((1, de, tk), lambda i, k: (i, 0, k))] * 2,
        out_shape=[out, out],
        scratch_shapes=[pltpu.VMEM((tk // LANES, de2, LANES), F32)],
        compiler_params=_params(("parallel", "parallel")),
        name="split_w1",
    )(w1)


def _pack_bf16_pairs(v):
    w = v.shape[1] // 2
    lo = lax.bitcast_convert_type(v[:, :w].astype(BF16).astype(F32), jnp.uint32) >> 16
    hi = lax.bitcast_convert_type(v[:, w:].astype(BF16).astype(F32), jnp.uint32) & jnp.uint32(0xFFFF0000)
    return lo | hi


def _unpack_bf16_pairs(p):
    lo = lax.bitcast_convert_type(p << 16, F32)
    hi = lax.bitcast_convert_type(p & jnp.uint32(0xFFFF0000), F32)
    return lo, hi


def _store_slabs(ref, row0, rows, packed):
    slab = packed.shape[1] // LANES
    for s in range(slab):
        ref[pl.ds(row0 * slab + s, rows, stride=slab), :] = packed[:, s * LANES:(s + 1) * LANES]


def _load_slabs(ref, rows, slab):
    return jnp.concatenate([ref[pl.ds(s, rows, stride=slab), :] for s in range(slab)], axis=1)


def _expert_kernel(be_ref, nu_ref, tab_ref, h2p_ref, w1g_ref, w1l_ref, w2_ref, bias_ref, y4p_ref, xbuf, obuf, gsem, ssem,
                   *, n_tokens, t_pad):
    i = pl.program_id(0)
    n_used = nu_ref[0]
    slot = lax.rem(i, 2)
    other = 1 - slot
    gslot = lax.rem(i, 3)
    gslot1 = lax.rem(i + 1, 3)
    gslot2 = lax.rem(i + 2, 3)
    bm = EXPERT_ROWS
    slab = xbuf.shape[1] // bm
    g0, g1, g2, sp, sc = (k * bm for k in range(5))

    def gather_copy(table, j, buf):
        src = pl.multiple_of(tab_ref[0, 0, table + j], slab)
        return pltpu.make_async_copy(h2p_ref.at[pl.ds(src, slab)],
                                     xbuf.at[buf, pl.ds(j * slab, slab)], gsem.at[buf])

    def scatter_copy(table, j, buf):
        dst = pl.multiple_of(tab_ref[0, 0, table + j], slab)
        return pltpu.make_async_copy(obuf.at[buf, pl.ds(j * slab, slab)],
                                     y4p_ref.at[pl.ds(dst, slab)], ssem.at[buf])

    def wait_block(kind, buf):
        if kind == "gather":
            pltpu.make_async_copy(h2p_ref.at[pl.ds(0, bm * slab)], xbuf.at[buf], gsem.at[buf]).wait()
        else:
            pltpu.make_async_copy(obuf.at[buf], y4p_ref.at[pl.ds(0, bm * slab)], ssem.at[buf]).wait()

    @pl.when(i == 0)
    def _():
        obuf[...] = jnp.zeros(obuf.shape, obuf.dtype)
        fills = [pltpu.make_async_copy(obuf.at[1], y4p_ref.at[pl.ds((k * t_pad + n_tokens) * slab + c * bm * slab,
                                                                     bm * slab)], ssem.at[1])
                 for k in range(TOP_K) for c in range((t_pad - n_tokens) // bm)]
        for fill in fills:
            fill.start()
        for fill in fills:
            fill.wait()
        for j in range(bm):
            gather_copy(g0, j, 0).start()
        for j in range(bm):
            gather_copy(g1, j, 1).start()
        spare_row = t_pad + t_pad - 2 * bm
        pltpu.make_async_copy(obuf.at[0], y4p_ref.at[pl.ds(spare_row * slab, bm * slab)], ssem.at[0]).start()

    @pl.when(i < n_used)
    def _():
        nt = (((1,), (1,)), ((), ()))
        de = w1g_ref.shape[1]
        wait_block("gather", gslot)
        lo, hi = _unpack_bf16_pairs(_load_slabs(xbuf.at[gslot], bm, slab))
        xb = jnp.concatenate([lo, hi], axis=1).astype(BF16)
        for j in range(bm):
            gather_copy(g2, j, gslot2).start(priority=j % 2)
        for j in range(bm):
            scatter_copy(sp, j, other).start(priority=j % 2)
        glu = lax.dot_general(xb, w1g_ref[0], nt, preferred_element_type=F32) + bias_ref[0, :, :de]
        lin = lax.dot_general(xb, w1l_ref[0], nt, preferred_element_type=F32) + bias_ref[0, :, de:2 * de]
        glu = jnp.minimum(glu, SWIGLU_LIMIT)
        lin = jnp.clip(lin, -SWIGLU_LIMIT, SWIGLU_LIMIT)
        act = glu * (1.0 / (1.0 + jnp.exp(-SWIGLU_ALPHA * glu))) * (lin + 1.0)
        y = jnp.dot(act.astype(BF16), w2_ref[0], preferred_element_type=F32) + bias_ref[0, :, 2 * de:]
        wait_block("scatter", slot)
        _store_slabs(obuf.at[slot], 0, bm, _pack_bf16_pairs(y))

    @pl.when(i == n_used - 1)
    def _():
        for j in range(bm):
            scatter_copy(sc, j, slot).start()
        wait_block("scatter", other)
        wait_block("scatter", slot)
        wait_block("gather", gslot1)
        wait_block("gather", gslot2)


def expert_ffn(h2p, ids, block_e, n_used, w1g, w1l, w2, b1g, b1l, b2, n_tokens):
    n_blocks, _, bm = ids.shape
    slab = h2p.shape[0] // n_tokens
    de, d = w1g.shape[1], w1g.shape[2]
    t_pad = padded_tokens(n_tokens)
    src = jnp.minimum(ids >> 2, n_tokens - 1) * slab
    dst = ((ids & 3) * t_pad + (ids >> 2)) * slab
    first = ((t_pad - 2 * bm + jnp.arange(bm, dtype=jnp.int32)) * slab).reshape(1, 1, bm)
    ahead = lambda a, k: jnp.concatenate([a[k:]] + [a[-1:]] * k, axis=0)
    table = jnp.concatenate([src, ahead(src, 1), ahead(src, 2), jnp.concatenate([first, dst[:-1]], axis=0), dst], axis=2)
    bias = jnp.concatenate([b1g, b1l, b2], axis=2)
    wmap = lambda i, be, nu: (be[i], 0, 0)
    grid_spec = pltpu.PrefetchScalarGridSpec(
        num_scalar_prefetch=2,
        grid=(n_blocks,),
        in_specs=[
            pl.BlockSpec((1, 1, 5 * bm), lambda i, be, nu: (i, 0, 0), memory_space=pltpu.SMEM),
            pl.BlockSpec(memory_space=pl.ANY),
            pl.BlockSpec((1, de, d), wmap),
            pl.BlockSpec((1, de, d), wmap),
            pl.BlockSpec((1, de, d), wmap),
            pl.BlockSpec((1, 1, 2 * de + d), wmap),
        ],
        out_specs=pl.BlockSpec(memory_space=pl.ANY),
        scratch_shapes=[
            pltpu.VMEM((3, bm * slab, LANES), jnp.uint32),
            pltpu.VMEM((2, bm * slab, LANES), jnp.uint32),
            pltpu.SemaphoreType.DMA((3,)),
            pltpu.SemaphoreType.DMA((2,)),
        ],
    )
    return pl.pallas_call(
        functools.partial(_expert_kernel, n_tokens=n_tokens, t_pad=t_pad),
        grid_spec=grid_spec,
        out_shape=jax.ShapeDtypeStruct((TOP_K * t_pad * slab, LANES), jnp.uint32),
        compiler_params=_params(("arbitrary",)),
        name="expert_ffn",
    )(block_e, n_used, table, h2p, w1g, w1l, w2, bias)


def _combine_kernel(x1_ref, y0_ref, y1_ref, y2_ref, y3_ref, tw_ref, nw_ref, o_ref, *, normalize):
    tm, d = x1_ref.shape
    slab = y0_ref.shape[0] // tm
    lo_sum = jnp.zeros((tm, d // 2), F32)
    hi_sum = jnp.zeros((tm, d // 2), F32)
    for k, y_ref in enumerate((y0_ref, y1_ref, y2_ref, y3_ref)):
        lo, hi = _unpack_bf16_pairs(_load_slabs(y_ref, tm, slab))
        w = tw_ref[:, k:k + 1]
        lo_sum = lo_sum + w * lo
        hi_sum = hi_sum + w * hi
    acc = x1_ref[...] + jnp.concatenate([lo_sum, hi_sum], axis=1)
    if normalize:
        acc = acc * lax.rsqrt(jnp.mean(acc * acc, axis=-1, keepdims=True) + EPS) * nw_ref[...]
    o_ref[...] = acc


def combine_norm(x1, y4p, top_w, norm_w, normalize):
    t, d = x1.shape
    tm = 256
    t_pad = padded_tokens(t)
    slab = y4p.shape[0] // (TOP_K * t_pad)
    y_specs = [pl.BlockSpec((tm * slab, LANES), functools.partial(lambda i, k: (k * (t_pad // tm) + i, 0), k=k))
               for k in range(TOP_K)]
    return pl.pallas_call(
        functools.partial(_combine_kernel, normalize=normalize),
        grid=(t // tm,),
        in_specs=[pl.BlockSpec((tm, d), lambda i: (i, 0))] + y_specs
        + [pl.BlockSpec((tm, LANES), lambda i: (i, 0)), pl.BlockSpec((1, d), lambda i: (0, 0))],
        out_specs=pl.BlockSpec((tm, d), lambda i: (i, 0)),
        out_shape=jax.ShapeDtypeStruct((t, d), F32),
        compiler_params=_params(("parallel",)),
        name="combine_norm",
    )(x1, y4p, y4p, y4p, y4p, top_w, norm_w)


def routing_layout(top_i, n_tokens):
    n_assign = n_tokens * TOP_K
    bm = EXPERT_ROWS
    n_pad = N_EXPERTS * bm
    n_blocks = (n_assign + n_pad) // bm
    flat_e = top_i[:, :TOP_K].reshape(n_assign)
    counts = jnp.sum(flat_e[:, None] == jnp.arange(N_EXPERTS, dtype=jnp.int32)[None, :], axis=0, dtype=jnp.int32)
    padded = (counts + bm - 1) // bm * bm
    pend = jnp.cumsum(padded)
    n_used = (pend[-1] // bm).astype(jnp.int32)
    block_row = jnp.arange(n_blocks, dtype=jnp.int32) * bm
    block_e = jnp.minimum(jnp.sum(pend[None, :] <= block_row[:, None], axis=1, dtype=jnp.int32), N_EXPERTS - 1)
    spare_pos = jnp.arange(bm, dtype=jnp.int32)[None, :]
    spare_key = jnp.where(spare_pos < (padded - counts)[:, None], jnp.arange(N_EXPERTS, dtype=jnp.int32)[:, None],
                          N_EXPERTS).reshape(n_pad)
    keys = jnp.concatenate([flat_e, spare_key])
    pos_bits = (n_assign + n_pad - 1).bit_length()
    assert (N_EXPERTS + 1) << pos_bits < 2 ** 31
    packed = jnp.sort((keys << pos_bits) | jnp.arange(n_assign + n_pad, dtype=jnp.int32))
    order = packed & ((1 << pos_bits) - 1)
    return order.reshape(n_blocks, 1, bm), block_e, n_used.reshape(1)


def kernel(x, w_in, rel_bias, w_branch_attn, conv_w, conv_b, dt_bias, a_log, d_skip, ssm_norm_w,
           w_branch_ssm, gate_bias, w_out, norm_mix, norm_ffn, router_w, router_b, w1, b1, w2, b2,
           norm_final):
    batch, seq, d = x.shape
    t = batch * seq
    depth = w_in.shape[0]
    n_groups = len(ATTN_GROUPS)
    attn_w = n_groups * GROUP_WIDTH
    d_inner = N_SSM_GROUPS * SSM_HEADS_PER_GROUP * SSM_HEAD_DIM
    n_heads = N_SSM_GROUPS * SSM_HEADS_PER_GROUP
    bc_w = 2 * N_SSM_GROUPS * D_STATE
    xf = x.reshape(t, d)
    for l in range(depth):
        wl = w_in[l]
        o_z = 3 * attn_w
        o_xbc = o_z + d_inner
        o_dt = o_xbc + d_inner + bc_w
        o_gate = o_dt + n_heads
        w_qkv = wl[:, :o_z].astype(BF16)
        w_rest = jnp.concatenate([wl[:, o_z:o_xbc], wl[:, o_gate:], wl[:, o_xbc:o_dt]], axis=1).astype(BF16)
        lane_of_head = (np.arange(n_heads) // SSM_HEADS_PER_GROUP) * LANES + np.arange(n_heads) % SSM_HEADS_PER_GROUP
        w_dt = jnp.zeros((d, N_SSM_GROUPS * LANES), F32).at[:, lane_of_head].set(wl[:, o_dt:o_gate]).astype(BF16)
        dt_bias_p = jnp.zeros((1, N_SSM_GROUPS * LANES), F32).at[0, lane_of_head].set(dt_bias[l].astype(F32))
        a_p = jnp.zeros((1, N_SSM_GROUPS * LANES), F32).at[0, lane_of_head].set(-jnp.exp(a_log[l].astype(F32)))
        d_skip_e = jnp.repeat(d_skip[l].astype(F32), SSM_HEAD_DIM)[None, :]

        g_mix = norm_mix[l].astype(F32)[None, :]
        rest, dt_raw, h = rms_matmul(xf, g_mix, w_rest, w_dt, tn=1792)
        qkv_groups = qkv_project(h, w_qkv, batch, seq)
        bias = jnp.stack([attention_bias(rel_bias, gi) for gi in range(n_groups)])
        attn = dilated_attention(qkv_groups, bias, batch, seq)
        y_ssm = ssd_mixer(rest, dt_raw, conv_w[l].astype(F32), conv_b[l].astype(F32)[None, :], dt_bias_p, a_p,
                          d_skip_e, ssm_norm_w[l].astype(F32)[None, :], batch, seq)
        rw = jnp.zeros((d, LANES), F32).at[:, :N_EXPERTS].set(router_w[l].astype(F32))
        rw_hi = rw.astype(BF16)
        router_w_p = jnp.concatenate([rw_hi, (rw - rw_hi.astype(F32)).astype(BF16)], axis=1)
        router_b_p = jnp.full((1, LANES), NEG_BIG, F32).at[0, :N_EXPERTS].set(router_b[l].astype(F32))
        x1, h2, top_i, top_w = merge_project(
            attn, y_ssm, rest, xf, w_branch_attn[l].astype(BF16), w_branch_ssm[l].astype(BF16),
            w_out[l].astype(BF16), gate_bias[l].astype(F32)[None, :], norm_ffn[l].astype(F32)[None, :],
            router_w_p, router_b_p)

        ids, block_e, n_used = routing_layout(top_i, t)
        w1g_t, w1l_t = split_w1(w1[l].astype(F32))
        y4p = expert_ffn(h2, ids, block_e, n_used, w1g_t, w1l_t, w2[l].astype(BF16),
                         b1[l][:, None, 0::2].astype(F32), b1[l][:, None, 1::2].astype(F32),
                         b2[l][:, None, :].astype(F32), t)
        xf = combine_norm(x1, y4p, top_w, norm_final.astype(F32)[None, :], normalize=(l == depth - 1))
    return xf.reshape(batch, seq, d)
```

```python
import functools
import math

import jax
import jax.numpy as jnp
import numpy as np
from jax import lax
from jax.experimental import pallas as pl
from jax.experimental.pallas import tpu as pltpu

F32 = jnp.float32
BF16 = jnp.bfloat16

EPS = 1e-5
NEG_BIG = -1e30

HEAD_DIM = 64
ATTN_GROUPS = ((128, 1), (512, 4), (2048, 16))
HEADS_PER_GROUP = 8
GROUP_WIDTH = HEADS_PER_GROUP * HEAD_DIM
ATTN_BLOCK = 128
NUM_BUCKETS = 32
MAX_DISTANCE = 2048
SSM_HEAD_DIM = 64
N_SSM_GROUPS = 4
SSM_HEADS_PER_GROUP = 8
D_STATE = 128
CONV_WIDTH = 4
CHUNK = 128
N_EXPERTS = 32
TOP_K = 4
SWIGLU_LIMIT = 7.0
SWIGLU_ALPHA = 1.702

LANES = 128
V7X_VMEM_BYTES = 64 * 1024 * 1024
VMEM_LIMIT = 48 * 1024 * 1024

ROW_TILE = 1024
MERGE_SUBTILES = 4
EXPERT_ROWS = 256


def padded_tokens(n_tokens):
    return n_tokens + N_EXPERTS * EXPERT_ROWS // TOP_K + 2 * EXPERT_ROWS


def _params(semantics):
    return pltpu.CompilerParams(dimension_semantics=semantics, vmem_limit_bytes=VMEM_LIMIT)


def _rms_matmul_kernel(x_ref, g_ref, w_ref, ws_ref, o_ref, os_ref, hb_ref, h_ref):
    @pl.when(pl.program_id(1) == 0)
    def _():
        x = x_ref[...]
        ms = jnp.mean(x * x, axis=-1, keepdims=True)
        h_ref[...] = (x * lax.rsqrt(ms + EPS) * g_ref[...]).astype(BF16)
        hb_ref[...] = h_ref[...]
        os_ref[...] = jnp.dot(h_ref[...], ws_ref[...], preferred_element_type=F32)

    o_ref[...] = jnp.dot(h_ref[...], w_ref[...], preferred_element_type=F32).astype(o_ref.dtype)


def rms_matmul(x, g, w, w_side, tn):
    t, d = x.shape
    n = w.shape[1]
    ns = w_side.shape[1]
    tm = ROW_TILE
    return pl.pallas_call(
        _rms_matmul_kernel,
        grid=(t // tm, n // tn),
        in_specs=[
            pl.BlockSpec((tm, d), lambda i, j: (i, 0)),
            pl.BlockSpec((1, d), lambda i, j: (0, 0)),
            pl.BlockSpec((d, tn), lambda i, j: (0, j)),
            pl.BlockSpec((d, ns), lambda i, j: (0, 0)),
        ],
        out_specs=[
            pl.BlockSpec((tm, tn), lambda i, j: (i, j)),
            pl.BlockSpec((tm, ns), lambda i, j: (i, 0)),
            pl.BlockSpec((tm, d), lambda i, j: (i, 0)),
        ],
        out_shape=[jax.ShapeDtypeStruct((t, n), BF16), jax.ShapeDtypeStruct((t, ns), F32),
                   jax.ShapeDtypeStruct((t, d), BF16)],
        scratch_shapes=[pltpu.VMEM((tm, d), BF16)],
        compiler_params=_params(("parallel", "arbitrary")),
        name="rms_matmul",
    )(x, g, w, w_side)


def _qkv_kernel(h_ref, w_ref, o0_ref, o1_ref, o2_ref, acc_ref, tmp_ref):
    seq = h_ref.shape[0]
    slabs = GROUP_WIDTH // LANES
    n_blocks = seq // ATTN_BLOCK
    for gi, o_ref in enumerate((o0_ref, o1_ref, o2_ref)):
        dil = ATTN_GROUPS[gi][1]
        nb = n_blocks // dil
        acc = jnp.dot(h_ref[...], w_ref[:, gi * GROUP_WIDTH:(gi + 1) * GROUP_WIDTH], preferred_element_type=F32)
        if dil == 1:
            for n in range(n_blocks):
                o_ref[0, 0, n] = acc[n * ATTN_BLOCK:(n + 1) * ATTN_BLOCK, :].astype(o_ref.dtype)
            continue
        for s in range(slabs):
            acc_ref[s] = acc[:, s * LANES:(s + 1) * LANES]
        src_ref, step, group_rows = acc_ref, dil, seq
        if dil == 16:
            for s in range(slabs):
                for a in range(4):
                    tmp_ref[s, a * (seq // 4):(a + 1) * (seq // 4), :] = acc_ref[s, pl.ds(a, seq // 4, stride=4), :]
            src_ref, step, group_rows = tmp_ref, 4, seq // 4
        for r in range(dil):
            base = (r % (dil // step)) * group_rows + r // (dil // step)
            for n in range(nb):
                for s in range(slabs):
                    rows = pl.ds(base + n * ATTN_BLOCK * step, ATTN_BLOCK, stride=step)
                    o_ref[0, 0, r * nb + n, :, s * LANES:(s + 1) * LANES] = src_ref[s, rows, :].astype(o_ref.dtype)


def qkv_project(h, w_qkv, batch, seq):
    t, d = h.shape
    n_groups = len(ATTN_GROUPS)
    tn = n_groups * GROUP_WIDTH
    out_shapes, out_specs = [], []
    for _ in ATTN_GROUPS:
        shape = (3, batch, seq // ATTN_BLOCK, ATTN_BLOCK, GROUP_WIDTH)
        out_shapes.append(jax.ShapeDtypeStruct(shape, BF16))
        out_specs.append(pl.BlockSpec((1, 1) + shape[2:], lambda w, b: (w, b, 0, 0, 0)))
    return pl.pallas_call(
        _qkv_kernel,
        grid=(3, batch),
        in_specs=[pl.BlockSpec((seq, d), lambda w, b: (b, 0)), pl.BlockSpec((d, tn), lambda w, b: (0, w))],
        out_specs=out_specs,
        out_shape=out_shapes,
        scratch_shapes=[pltpu.VMEM((GROUP_WIDTH // LANES, seq, LANES), F32)] * 2,
        compiler_params=_params(("parallel", "parallel")),
        name="qkv_project",
    )(h, w_qkv)


def _attn_kernel(q0_ref, q1_ref, q2_ref, bias_ref, o_ref, out_ref, lse_ref):
    heads = LANES // HEAD_DIM
    scale = jnp.asarray(HEAD_DIM ** -0.5, BF16)
    n_blocks = q0_ref.shape[2]
    qk = (((2,), (2,)), ((0,), (0,)))
    pv = (((2,), (1,)), ((0,), (0,)))
    blk = lax.broadcasted_iota(jnp.int32, (n_blocks, 1, 1), 0)
    lane = lax.broadcasted_iota(jnp.int32, (1, 1, LANES), 2)

    def shifted(x):
        return jnp.concatenate([x[n_blocks - 1:], x[:n_blocks - 1]], axis=0)

    for g, ref in enumerate((q0_ref, q1_ref, q2_ref)):
        dil = ATTN_GROUPS[g][1]
        nb = n_blocks // dil
        q2h = ref[0, 0] * scale
        keys, vals = ref[1, 0], ref[2, 0]
        if nb > 1:
            keys = jnp.concatenate([shifted(keys), keys], axis=1)
            vals = jnp.concatenate([shifted(vals), vals], axis=1)
        n_keys = keys.shape[1]
        vals = jnp.concatenate([vals, jnp.ones(vals.shape, BF16)], axis=2)
        pvs, dens, maxs = [], [], []
        for h in range(heads):
            in_head = (lane >= h * HEAD_DIM) & (lane < (h + 1) * HEAD_DIM)
            q = jnp.where(in_head, q2h, jnp.zeros_like(q2h))
            s = lax.dot_general(q, keys, qk, preferred_element_type=F32)
            s = s + bias_ref[g, h, :, 2 * ATTN_BLOCK - n_keys:][None]
            if nb > 1:
                key_is_prev = lax.broadcasted_iota(jnp.int32, (1, 1, n_keys), 2) < ATTN_BLOCK
                s = jnp.where((blk % nb == 0) & key_is_prev, NEG_BIG, s)
            m = jnp.max(s, axis=-1, keepdims=True)
            p = jnp.exp(s - m).astype(BF16)
            acc = lax.dot_general(p, vals, pv, preferred_element_type=F32)
            pvs.append(acc[:, :, :LANES])
            dens.append(acc[:, :, LANES:])
            maxs.append(m)
        first = lane < HEAD_DIM
        den = jnp.where(first, dens[0], dens[1])
        o2 = jnp.where(first, pvs[0], pvs[1]) / den
        l2 = jnp.where(first, maxs[0], maxs[1]) + jnp.log(den)
        for r in range(dil):
            for n in range(nb):
                start = r + n * ATTN_BLOCK * dil
                rows = pl.ds(start, ATTN_BLOCK) if dil == 1 else pl.ds(start, ATTN_BLOCK, stride=dil)
                out_ref[g, rows, :] = o2[r * nb + n]
                lse_ref[g, rows, :] = l2[r * nb + n]

    l0, l1, l2 = lse_ref[0], lse_ref[1], lse_ref[2]
    lm = jnp.maximum(jnp.maximum(l0, l1), l2)
    e0, e1, e2 = jnp.exp(l0 - lm), jnp.exp(l1 - lm), jnp.exp(l2 - lm)
    mixed = (e0 * out_ref[0] + e1 * out_ref[1] + e2 * out_ref[2]) / (e0 + e1 + e2)
    o_ref[...] = mixed.astype(o_ref.dtype)


def dilated_attention(qkv_groups, bias, batch, seq):
    heads = LANES // HEAD_DIM
    in_specs = [pl.BlockSpec((3, 1) + a.shape[2:4] + (LANES,), lambda b, hp: (0, b, 0, 0, hp)) for a in qkv_groups]
    in_specs.append(pl.BlockSpec((len(ATTN_GROUPS), heads, ATTN_BLOCK, 2 * ATTN_BLOCK), lambda b, hp: (0, hp, 0, 0)))
    return pl.pallas_call(
        _attn_kernel,
        grid=(batch, GROUP_WIDTH // LANES),
        in_specs=in_specs,
        out_specs=pl.BlockSpec((seq, LANES), lambda b, hp: (b, hp)),
        out_shape=jax.ShapeDtypeStruct((batch * seq, GROUP_WIDTH), BF16),
        scratch_shapes=[pltpu.VMEM((len(ATTN_GROUPS), seq, LANES), F32),
                        pltpu.VMEM((len(ATTN_GROUPS), seq, LANES), F32)],
        compiler_params=_params(("parallel", "parallel")),
        name="dilated_attn",
    )(*qkv_groups, bias)


def attention_bias(rel_bias, gi):
    window, dil = ATTN_GROUPS[gi]
    w_sub = window // dil
    q_idx = np.arange(ATTN_BLOCK)[:, None]
    k_idx = np.arange(2 * ATTN_BLOCK)[None, :]
    delta = q_idx + ATTN_BLOCK - k_idx
    in_band = (delta >= 0) & (delta <= w_sub)
    dist = np.clip(delta, 0, w_sub) * dil
    max_exact = NUM_BUCKETS // 2
    nf = np.maximum(dist, max_exact).astype(np.float32)
    large = max_exact + (np.log(nf / max_exact) / math.log(MAX_DISTANCE / max_exact)
                         * (NUM_BUCKETS - max_exact)).astype(np.int32)
    large = np.minimum(large, NUM_BUCKETS - 1)
    bucket = np.where(dist < max_exact, dist, large)
    table = rel_bias[:, gi * HEADS_PER_GROUP:(gi + 1) * HEADS_PER_GROUP].astype(F32)
    onehot = (bucket.reshape(-1, 1) == np.arange(NUM_BUCKETS)[None, :]).astype(np.float32)
    bias = jnp.einsum('bh,nb->hn', table, jnp.asarray(onehot), precision=lax.Precision.HIGHEST)
    bias = bias.reshape(HEADS_PER_GROUP, ATTN_BLOCK, 2 * ATTN_BLOCK)
    return jnp.where(in_band[None], bias, NEG_BIG)


def _silu(v):
    half = 0.5 * v
    return half + half * jnp.tanh(half)


SSD_CHUNKS_PER_STEP = 4
CONV_TAIL = 16


def _conv_silu(ext_ref, u_ref, w_ref, b_ref, first):
    rows = u_ref.shape[0]
    ext_rows = ext_ref.shape[0]

    if first is not None:
        @pl.when(first)
        def _():
            ext_ref[0:CONV_TAIL, :] = jnp.zeros((CONV_TAIL, ext_ref.shape[1]), ext_ref.dtype)

    u = u_ref[...]
    ext_ref[CONV_TAIL:, :] = u
    taps = CONV_WIDTH - 1
    out_row = lax.broadcasted_iota(jnp.int32, (rows, ext_rows), 0)
    src_row = lax.broadcasted_iota(jnp.int32, (rows, ext_rows), 1)
    shift_mat = jnp.concatenate([(src_row == out_row + (CONV_TAIL - k)).astype(BF16) for k in range(1, CONV_WIDTH)],
                                axis=0)
    shifted = jnp.dot(shift_mat, ext_ref[...], preferred_element_type=F32)
    acc = u.astype(F32) * w_ref[taps:CONV_WIDTH, :] + b_ref[...]
    for k in range(1, CONV_WIDTH):
        acc = acc + shifted[(k - 1) * rows:k * rows, :] * w_ref[taps - k:CONV_WIDTH - k, :]
    ext_ref[0:CONV_TAIL, :] = ext_ref[rows:, :]
    return _silu(acc)


def _ssd_kernel(x_ref, bc_ref, z_ref, dt_ref, cwx_ref, cbx_ref, cwbc_ref, cbbc_ref, dtb_ref, a_ref,
                dskip_ref, nw_ref, expand_ref, o_ref, extx_ref, extbc_ref, state_ref):
    first = pl.program_id(1) == 0
    gw = SSM_HEADS_PER_GROUP * SSM_HEAD_DIM

    @pl.when(first)
    def _():
        state_ref[...] = jnp.zeros(state_ref.shape, F32)

    row = lax.broadcasted_iota(jnp.int32, (CHUNK, CHUNK), 0)
    colm = lax.broadcasted_iota(jnp.int32, (CHUNK, CHUNK), 1)
    tril = row >= colm
    tril_b = tril.astype(BF16)
    head_rows = N_SSM_GROUPS * SSM_HEADS_PER_GROUP
    eye_b = (lax.broadcasted_iota(jnp.int32, (head_rows, CHUNK), 0)
             == lax.broadcasted_iota(jnp.int32, (head_rows, CHUNK), 1)).astype(BF16)
    nt = (((1,), (1,)), ((), ()))
    tn = (((0,), (0,)), ((), ()))

    def split3(v):
        p0 = v.astype(BF16)
        r1 = v - p0.astype(F32)
        p1 = r1.astype(BF16)
        p2 = (r1 - p1.astype(F32)).astype(BF16)
        return p0, p1, p2

    def select_rows(mat01, v):
        return sum(jnp.dot(mat01, p, preferred_element_type=F32) for p in split3(v))

    def select_cols(v, mat01):
        return sum(jnp.dot(p, mat01, preferred_element_type=F32) for p in split3(v)[:2])

    for sub, g in [(sub, g) for sub in range(x_ref.shape[0] // CHUNK) for g in range(N_SSM_GROUPS)]:
        rws = pl.ds(sub * CHUNK, CHUNK)
        if g == 0:
            sub_first = first if sub == 0 else None
            xs_all = _conv_silu(extx_ref, x_ref.at[rws], cwx_ref, cbx_ref, sub_first)
            bc_all = _conv_silu(extbc_ref, bc_ref.at[rws], cwbc_ref, cbbc_ref, sub_first)
            v = dt_ref[rws, :] + dtb_ref[...]
            dt = jnp.maximum(v, 0.0) + jnp.log1p(jnp.exp(-jnp.abs(v)))
            a_d = dt * a_ref[...]
            acs = select_rows(tril_b, a_d)
            acs_t = sum(lax.dot_general(eye_b, p, nt, preferred_element_type=F32) for p in split3(acs))
            e_acs = jnp.exp(acs)
            e_dec = jnp.exp(acs[CHUNK - 1:CHUNK, :] - acs)
        lanes = slice(g * LANES, (g + 1) * LANES)
        ch = slice(g * gw, (g + 1) * gw)
        expand = expand_ref[g]
        dt_e = select_cols(dt, expand)
        eacs_e = select_cols(e_acs, expand)
        edec_e = select_cols(e_dec, expand)
        elast_e = eacs_e[CHUNK - 1:CHUNK, :]

        xs = xs_all[:, ch]
        bm = bc_all[:, lanes].astype(BF16)
        cm = bc_all[:, N_SSM_GROUPS * D_STATE + g * D_STATE:N_SSM_GROUPS * D_STATE + (g + 1) * D_STATE].astype(BF16)
        x_d = xs * dt_e
        cb = lax.dot_general(cm, bm, nt, preferred_element_type=F32)

        prev = state_ref[g]
        y = jnp.dot(cm, prev.astype(BF16), preferred_element_type=F32) * eacs_e
        x_d16 = x_d.astype(BF16)
        parts = []
        for j in range(SSM_HEADS_PER_GROUP):
            head = g * SSM_HEADS_PER_GROUP + j
            seg = acs[:, head:head + 1] - acs_t[head:head + 1, :]
            m_h = (cb * jnp.exp(jnp.where(tril, seg, NEG_BIG))).astype(BF16)
            parts.append(jnp.dot(m_h, x_d16[:, j * SSM_HEAD_DIM:(j + 1) * SSM_HEAD_DIM],
                                 preferred_element_type=F32))
        y = y + jnp.concatenate(parts, axis=1)

        xw = (x_d * edec_e).astype(BF16)
        state_ref[g] = prev * elast_e + lax.dot_general(bm, xw, tn, preferred_element_type=F32)

        y = y + dskip_ref[:, ch] * xs
        y = y * _silu(z_ref[rws, ch].astype(F32))
        y = y * lax.rsqrt(jnp.mean(y * y, axis=-1, keepdims=True) + EPS)
        o_ref[rws, ch] = (y * nw_ref[:, ch]).astype(o_ref.dtype)


def ssd_mixer(rest, dt_raw, conv_w, conv_b, dt_bias_p, a_p, d_skip_e, norm_w, batch, seq):
    t = rest.shape[0]
    d_inner = N_SSM_GROUPS * SSM_HEADS_PER_GROUP * SSM_HEAD_DIM
    bc_w = 2 * N_SSM_GROUPS * D_STATE
    rows = CHUNK * SSD_CHUNKS_PER_STEP
    nc = seq // rows
    gw = SSM_HEADS_PER_GROUP * SSM_HEAD_DIM
    head_of_channel = np.arange(N_SSM_GROUPS)[:, None, None] * SSM_HEADS_PER_GROUP + np.arange(gw)[None, None, :] // SSM_HEAD_DIM
    expand = (np.arange(LANES)[None, :, None] == head_of_channel).astype(np.float32)
    rowmap = lambda b, c: (b * nc + c, 0)
    const = lambda b, c: (0, 0)
    return pl.pallas_call(
        _ssd_kernel,
        grid=(batch, nc),
        in_specs=[
            pl.BlockSpec((rows, d_inner), lambda b, c: (b * nc + c, 2)),
            pl.BlockSpec((rows, bc_w), lambda b, c: (b * nc + c, 6)),
            pl.BlockSpec((rows, d_inner), rowmap),
            pl.BlockSpec((rows, LANES), rowmap),
            pl.BlockSpec((CONV_WIDTH, d_inner), const),
            pl.BlockSpec((1, d_inner), const),
            pl.BlockSpec((CONV_WIDTH, bc_w), const),
            pl.BlockSpec((1, bc_w), const),
            pl.BlockSpec((1, LANES), const),
            pl.BlockSpec((1, LANES), const),
            pl.BlockSpec((1, d_inner), const),
            pl.BlockSpec((1, d_inner), const),
            pl.BlockSpec((N_SSM_GROUPS, LANES, gw), lambda b, c: (0, 0, 0)),
        ],
        out_specs=pl.BlockSpec((rows, d_inner), rowmap),
        out_shape=jax.ShapeDtypeStruct((t, d_inner), BF16),
        scratch_shapes=[
            pltpu.VMEM((CONV_TAIL + CHUNK, d_inner), BF16),
            pltpu.VMEM((CONV_TAIL + CHUNK, bc_w), BF16),
            pltpu.VMEM((N_SSM_GROUPS, D_STATE, gw), F32),
        ],
        compiler_params=_params(("parallel", "arbitrary")),
        name="ssd_mixer",
    )(rest, rest, rest, dt_raw, conv_w[:, :d_inner], conv_b[:, :d_inner], conv_w[:, d_inner:],
      conv_b[:, d_inner:], dt_bias_p, a_p, d_skip_e, norm_w, jnp.asarray(expand, dtype=BF16))


def _merge_kernel(at_ref, ys_ref, gl_ref, x_ref, wa_ref, ws_ref,
                  wo_ref, gb_ref, nf_ref, rw_ref, rb_ref, x1_ref, h2_ref, ti_ref, tw_ref):
    d = x_ref.shape[1]
    sub = x_ref.shape[0] // MERGE_SUBTILES
    for part in range(MERGE_SUBTILES):
        rows = slice(part * sub, (part + 1) * sub)
        y_attn = jnp.dot(at_ref[rows, :], wa_ref[...], preferred_element_type=F32)
        y_ssm = jnp.dot(ys_ref[rows, :], ws_ref[...], preferred_element_type=F32)
        gv = gl_ref[rows, :].astype(F32) + gb_ref[...]
        gates = 1.0 / (1.0 + jnp.exp(-gv))
        merged = gates[:, :d] * y_attn + gates[:, d:] * y_ssm
        x1 = x_ref[rows, :] + jnp.dot(merged.astype(BF16), wo_ref[...], preferred_element_type=F32)
        x1_ref[rows, :] = x1
        h2 = x1 * lax.rsqrt(jnp.mean(x1 * x1, axis=-1, keepdims=True) + EPS) * nf_ref[...]
        _store_slabs(h2_ref, part * sub, sub, _pack_bf16_pairs(h2))
        h_hi = h2.astype(BF16)
        h_mid = (h2 - h_hi.astype(F32)).astype(BF16)
        both = jnp.dot(h_hi, rw_ref[...], preferred_element_type=F32)
        logits = (both[:, :LANES] + both[:, LANES:]
                  + jnp.dot(h_mid, rw_ref[:, :LANES], preferred_element_type=F32) + rb_ref[...])
        lane = lax.broadcasted_iota(jnp.int32, logits.shape, 1)
        top_i = jnp.zeros(logits.shape, jnp.int32)
        top_v = jnp.full(logits.shape, NEG_BIG, F32)
        work = logits
        for k in range(TOP_K):
            m = jnp.max(work, axis=-1, keepdims=True)
            idx = jnp.min(jnp.where(work == m, lane, LANES), axis=-1, keepdims=True)
            top_i = jnp.where(lane == k, idx, top_i)
            top_v = jnp.where(lane == k, m, top_v)
            work = jnp.where(lane == idx, NEG_BIG * 2.0, work)
        ev = jnp.exp(top_v - jnp.max(top_v, axis=-1, keepdims=True))
        ti_ref[rows, :] = top_i
        tw_ref[rows, :] = ev / jnp.sum(ev, axis=-1, keepdims=True)


def merge_project(attn, y_ssm, rest, x, wa, ws, wo, gate_bias, norm_ffn, router_w_p, router_b_p):
    t, d = x.shape
    slab = d // 2 // LANES
    tm = 256 * MERGE_SUBTILES
    d_inner = y_ssm.shape[1]
    rowmap = lambda i: (i, 0)
    const = lambda i: (0, 0)
    full = lambda a: pl.BlockSpec(a.shape, const, pipeline_mode=pl.Buffered(1))
    args = [attn, y_ssm, rest, x, wa, ws, wo, gate_bias, norm_ffn, router_w_p, router_b_p]
    in_specs = (
        [pl.BlockSpec((tm, GROUP_WIDTH), rowmap),
         pl.BlockSpec((tm, d_inner), rowmap),
         pl.BlockSpec((tm, 2 * d), lambda i: (i, 1)),
         pl.BlockSpec((tm, d), rowmap)]
        + [full(a) for a in args[4:]]
    )
    return pl.pallas_call(
        _merge_kernel,
        grid=(t // tm,),
        in_specs=in_specs,
        out_specs=[pl.BlockSpec((tm, d), rowmap), pl.BlockSpec((tm * slab, LANES), rowmap),
                   pl.BlockSpec((tm, LANES), rowmap), pl.BlockSpec((tm, LANES), rowmap)],
        out_shape=[jax.ShapeDtypeStruct((t, d), F32), jax.ShapeDtypeStruct((t * slab, LANES), jnp.uint32),
                   jax.ShapeDtypeStruct((t, LANES), jnp.int32), jax.ShapeDtypeStruct((t, LANES), F32)],
        compiler_params=_params(("parallel",)),
        name="merge_project",
    )(*args)


def _split_w1_kernel(w_ref, g_ref, l_ref, t_ref):
    de = g_ref.shape[1]
    for s in range(t_ref.shape[0]):
        cols = slice(s * LANES, (s + 1) * LANES)
        t_ref[s] = w_ref[0, cols, :].T
        g_ref[0, :, cols] = t_ref[s, pl.ds(0, de, stride=2), :].astype(g_ref.dtype)
        l_ref[0, :, cols] = t_ref[s, pl.ds(1, de, stride=2), :].astype(l_ref.dtype)


def split_w1(w1):
    e, d, de2 = w1.shape
    de = de2 // 2
    tk = 1024
    out = jax.ShapeDtypeStruct((e, de, d), BF16)
    return pl.pallas_call(
        _split_w1_kernel,
        grid=(e, d // tk),
        in_specs=[pl.BlockSpec((1, tk, de2), lambda i, k: (i, k, 0))],
        out_specs=[pl.BlockSpec((1, de, tk), lambda i, k: (i, 0, k))] * 2,
        out_shape=[out, out],
        scratch_shapes=[pltpu.VMEM((tk // LANES, de2, LANES), F32)],
        compiler_params=_params(("parallel", "parallel")),
        name="split_w1",
    )(w1)


def _pack_bf16_pairs(v):
    w = v.shape[1] // 2
    lo = lax.bitcast_convert_type(v[:, :w].astype(BF16).astype(F32), jnp.uint32) >> 16
    hi = lax.bitcast_convert_type(v[:, w:].astype(BF16).astype(F32), jnp.uint32) & jnp.uint32(0xFFFF0000)
    return lo | hi


def _unpack_bf16_pairs(p):
    lo = lax.bitcast_convert_type(p << 16, F32)
    hi = lax.bitcast_convert_type(p & jnp.uint32(0xFFFF0000), F32)
    return lo, hi


def _store_slabs(ref, row0, rows, packed):
    slab = packed.shape[1] // LANES
    for s in range(slab):
        ref[pl.ds(row0 * slab + s, rows, stride=slab), :] = packed[:, s * LANES:(s + 1) * LANES]


def _load_slabs(ref, rows, slab):
    return jnp.concatenate([ref[pl.ds(s, rows, stride=slab), :] for s in range(slab)], axis=1)


def _expert_kernel(be_ref, nu_ref, tab_ref, h2p_ref, w1g_ref, w1l_ref, w2_ref, bias_ref, y4p_ref, xbuf, obuf, gsem, ssem,
                   *, n_tokens, t_pad):
    i = pl.program_id(0)
    n_used = nu_ref[0]
    slot = lax.rem(i, 2)
    other = 1 - slot
    gslot = lax.rem(i, 3)
    gslot1 = lax.rem(i + 1, 3)
    gslot2 = lax.rem(i + 2, 3)
    bm = EXPERT_ROWS
    slab = xbuf.shape[1] // bm
    g0, g1, g2, sp, sc = (k * bm for k in range(5))

    def gather_copy(table, j, buf):
        src = pl.multiple_of(tab_ref[0, 0, table + j], slab)
        return pltpu.make_async_copy(h2p_ref.at[pl.ds(src, slab)],
                                     xbuf.at[buf, pl.ds(j * slab, slab)], gsem.at[buf])

    def scatter_copy(table, j, buf):
        dst = pl.multiple_of(tab_ref[0, 0, table + j], slab)
        return pltpu.make_async_copy(obuf.at[buf, pl.ds(j * slab, slab)],
                                     y4p_ref.at[pl.ds(dst, slab)], ssem.at[buf])

    def wait_block(kind, buf):
        if kind == "gather":
            pltpu.make_async_copy(h2p_ref.at[pl.ds(0, bm * slab)], xbuf.at[buf], gsem.at[buf]).wait()
        else:
            pltpu.make_async_copy(obuf.at[buf], y4p_ref.at[pl.ds(0, bm * slab)], ssem.at[buf]).wait()

    @pl.when(i == 0)
    def _():
        obuf[...] = jnp.zeros(obuf.shape, obuf.dtype)
        fills = [pltpu.make_async_copy(obuf.at[1], y4p_ref.at[pl.ds((k * t_pad + n_tokens) * slab + c * bm * slab,
                                                                     bm * slab)], ssem.at[1])
                 for k in range(TOP_K) for c in range((t_pad - n_tokens) // bm)]
        for fill in fills:
            fill.start()
        for fill in fills:
            fill.wait()
        for j in range(bm):
            gather_copy(g0, j, 0).start()
        for j in range(bm):
            gather_copy(g1, j, 1).start()
        spare_row = t_pad + t_pad - 2 * bm
        pltpu.make_async_copy(obuf.at[0], y4p_ref.at[pl.ds(spare_row * slab, bm * slab)], ssem.at[0]).start()

    @pl.when(i < n_used)
    def _():
        nt = (((1,), (1,)), ((), ()))
        de = w1g_ref.shape[1]
        wait_block("gather", gslot)
        lo, hi = _unpack_bf16_pairs(_load_slabs(xbuf.at[gslot], bm, slab))
        xb = jnp.concatenate([lo, hi], axis=1).astype(BF16)
        for j in range(bm):
            gather_copy(g2, j, gslot2).start(priority=j % 2)
        for j in range(bm):
            scatter_copy(sp, j, other).start(priority=j % 2)
        glu = lax.dot_general(xb, w1g_ref[0], nt, preferred_element_type=F32) + bias_ref[0, :, :de]
        lin = lax.dot_general(xb, w1l_ref[0], nt, preferred_element_type=F32) + bias_ref[0, :, de:2 * de]
        glu = jnp.minimum(glu, SWIGLU_LIMIT)
        lin = jnp.clip(lin, -SWIGLU_LIMIT, SWIGLU_LIMIT)
        act = glu * (1.0 / (1.0 + jnp.exp(-SWIGLU_ALPHA * glu))) * (lin + 1.0)
        y = jnp.dot(act.astype(BF16), w2_ref[0], preferred_element_type=F32) + bias_ref[0, :, 2 * de:]
        wait_block("scatter", slot)
        _store_slabs(obuf.at[slot], 0, bm, _pack_bf16_pairs(y))

    @pl.when(i == n_used - 1)
    def _():
        for j in range(bm):
            scatter_copy(sc, j, slot).start()
        wait_block("scatter", other)
        wait_block("scatter", slot)
        wait_block("gather", gslot1)
        wait_block("gather", gslot2)


def expert_ffn(h2p, ids, block_e, n_used, w1g, w1l, w2, b1g, b1l, b2, n_tokens):
    n_blocks, _, bm = ids.shape
    slab = h2p.shape[0] // n_tokens
    de, d = w1g.shape[1], w1g.shape[2]
    t_pad = padded_tokens(n_tokens)
    src = jnp.minimum(ids >> 2, n_tokens - 1) * slab
    dst = ((ids & 3) * t_pad + (ids >> 2)) * slab
    first = ((t_pad - 2 * bm + jnp.arange(bm, dtype=jnp.int32)) * slab).reshape(1, 1, bm)
    ahead = lambda a, k: jnp.concatenate([a[k:]] + [a[-1:]] * k, axis=0)
    table = jnp.concatenate([src, ahead(src, 1), ahead(src, 2), jnp.concatenate([first, dst[:-1]], axis=0), dst], axis=2)
    bias = jnp.concatenate([b1g, b1l, b2], axis=2)
    wmap = lambda i, be, nu: (be[i], 0, 0)
    grid_spec = pltpu.PrefetchScalarGridSpec(
        num_scalar_prefetch=2,
        grid=(n_blocks,),
        in_specs=[
            pl.BlockSpec((1, 1, 5 * bm), lambda i, be, nu: (i, 0, 0), memory_space=pltpu.SMEM),
            pl.BlockSpec(memory_space=pl.ANY),
            pl.BlockSpec((1, de, d), wmap),
            pl.BlockSpec((1, de, d), wmap),
            pl.BlockSpec((1, de, d), wmap),
            pl.BlockSpec((1, 1, 2 * de + d), wmap),
        ],
        out_specs=pl.BlockSpec(memory_space=pl.ANY),
        scratch_shapes=[
            pltpu.VMEM((3, bm * slab, LANES), jnp.uint32),
            pltpu.VMEM((2, bm * slab, LANES), jnp.uint32),
            pltpu.SemaphoreType.DMA((3,)),
            pltpu.SemaphoreType.DMA((2,)),
        ],
    )
    return pl.pallas_call(
        functools.partial(_expert_kernel, n_tokens=n_tokens, t_pad=t_pad),
        grid_spec=grid_spec,
        out_shape=jax.ShapeDtypeStruct((TOP_K * t_pad * slab, LANES), jnp.uint32),
        compiler_params=_params(("arbitrary",)),
        name="expert_ffn",
    )(block_e, n_used, table, h2p, w1g, w1l, w2, bias)


def _combine_kernel(x1_ref, y0_ref, y1_ref, y2_ref, y3_ref, tw_ref, nw_ref, o_ref, *, normalize):
    tm, d = x1_ref.shape
    slab = y0_ref.shape[0] // tm
    lo_sum = jnp.zeros((tm, d // 2), F32)
    hi_sum = jnp.zeros((tm, d // 2), F32)
    for k, y_ref in enumerate((y0_ref, y1_ref, y2_ref, y3_ref)):
        lo, hi = _unpack_bf16_pairs(_load_slabs(y_ref, tm, slab))
        w = tw_ref[:, k:k + 1]
        lo_sum = lo_sum + w * lo
        hi_sum = hi_sum + w * hi
    acc = x1_ref[...] + jnp.concatenate([lo_sum, hi_sum], axis=1)
    if normalize:
        acc = acc * lax.rsqrt(jnp.mean(acc * acc, axis=-1, keepdims=True) + EPS) * nw_ref[...]
    o_ref[...] = acc


def combine_norm(x1, y4p, top_w, norm_w, normalize):
    t, d = x1.shape
    tm = 512
    t_pad = padded_tokens(t)
    slab = y4p.shape[0] // (TOP_K * t_pad)
    y_specs = [pl.BlockSpec((tm * slab, LANES), functools.partial(lambda i, k: (k * (t_pad // tm) + i, 0), k=k))
               for k in range(TOP_K)]
    return pl.pallas_call(
        functools.partial(_combine_kernel, normalize=normalize),
        grid=(t // tm,),
        in_specs=[pl.BlockSpec((tm, d), lambda i: (i, 0))] + y_specs
        + [pl.BlockSpec((tm, LANES), lambda i: (i, 0)), pl.BlockSpec((1, d), lambda i: (0, 0))],
        out_specs=pl.BlockSpec((tm, d), lambda i: (i, 0)),
        out_shape=jax.ShapeDtypeStruct((t, d), F32),
        compiler_params=_params(("parallel",)),
        name="combine_norm",
    )(x1, y4p, y4p, y4p, y4p, top_w, norm_w)


def routing_layout(top_i, n_tokens):
    n_assign = n_tokens * TOP_K
    bm = EXPERT_ROWS
    n_pad = N_EXPERTS * bm
    n_blocks = (n_assign + n_pad) // bm
    flat_e = top_i[:, :TOP_K].reshape(n_assign)
    counts = jnp.sum(flat_e[:, None] == jnp.arange(N_EXPERTS, dtype=jnp.int32)[None, :], axis=0, dtype=jnp.int32)
    padded = (counts + bm - 1) // bm * bm
    pend = jnp.cumsum(padded)
    n_used = (pend[-1] // bm).astype(jnp.int32)
    block_row = jnp.arange(n_blocks, dtype=jnp.int32) * bm
    block_e = jnp.minimum(jnp.sum(pend[None, :] <= block_row[:, None], axis=1, dtype=jnp.int32), N_EXPERTS - 1)
    spare_pos = jnp.arange(bm, dtype=jnp.int32)[None, :]
    spare_key = jnp.where(spare_pos < (padded - counts)[:, None], jnp.arange(N_EXPERTS, dtype=jnp.int32)[:, None],
                          N_EXPERTS).reshape(n_pad)
    keys = jnp.concatenate([flat_e, spare_key])
    pos_bits = (n_assign + n_pad - 1).bit_length()
    assert (N_EXPERTS + 1) << pos_bits < 2 ** 31
    packed = jnp.sort((keys << pos_bits) | jnp.arange(n_assign + n_pad, dtype=jnp.int32))
    order = packed & ((1 << pos_bits) - 1)
    return order.reshape(n_blocks, 1, bm), block_e, n_used.reshape(1)


def kernel(x, w_in, rel_bias, w_branch_attn, conv_w, conv_b, dt_bias, a_log, d_skip, ssm_norm_w,
           w_branch_ssm, gate_bias, w_out, norm_mix, norm_ffn, router_w, router_b, w1, b1, w2, b2,
           norm_final):
    batch, seq, d = x.shape
    t = batch * seq
    depth = w_in.shape[0]
    n_groups = len(ATTN_GROUPS)
    attn_w = n_groups * GROUP_WIDTH
    d_inner = N_SSM_GROUPS * SSM_HEADS_PER_GROUP * SSM_HEAD_DIM
    n_heads = N_SSM_GROUPS * SSM_HEADS_PER_GROUP
    bc_w = 2 * N_SSM_GROUPS * D_STATE
    xf = x.reshape(t, d)
    for l in range(depth):
        wl = w_in[l]
        o_z = 3 * attn_w
        o_xbc = o_z + d_inner
        o_dt = o_xbc + d_inner + bc_w
        o_gate = o_dt + n_heads
        w_qkv = wl[:, :o_z].astype(BF16)
        w_rest = jnp.concatenate([wl[:, o_z:o_xbc], wl[:, o_gate:], wl[:, o_xbc:o_dt]], axis=1).astype(BF16)
        w_dt = jnp.zeros((d, LANES), F32).at[:, :n_heads].set(wl[:, o_dt:o_gate]).astype(BF16)
        dt_bias_p = jnp.zeros((1, LANES), F32).at[0, :n_heads].set(dt_bias[l].astype(F32))
        a_p = jnp.zeros((1, LANES), F32).at[0, :n_heads].set(-jnp.exp(a_log[l].astype(F32)))
        d_skip_e = jnp.repeat(d_skip[l].astype(F32), SSM_HEAD_DIM)[None, :]

        g_mix = norm_mix[l].astype(F32)[None, :]
        rest, dt_raw, h = rms_matmul(xf, g_mix, w_rest, w_dt, tn=1792)
        qkv_groups = qkv_project(h, w_qkv, batch, seq)
        bias = jnp.stack([attention_bias(rel_bias, gi) for gi in range(n_groups)])
        attn = dilated_attention(qkv_groups, bias, batch, seq)
        y_ssm = ssd_mixer(rest, dt_raw, conv_w[l].astype(F32), conv_b[l].astype(F32)[None, :], dt_bias_p, a_p,
                          d_skip_e, ssm_norm_w[l].astype(F32)[None, :], batch, seq)
        rw = jnp.zeros((d, LANES), F32).at[:, :N_EXPERTS].set(router_w[l].astype(F32))
        rw_hi = rw.astype(BF16)
        router_w_p = jnp.concatenate([rw_hi, (rw - rw_hi.astype(F32)).astype(BF16)], axis=1)
        router_b_p = jnp.full((1, LANES), NEG_BIG, F32).at[0, :N_EXPERTS].set(router_b[l].astype(F32))
        x1, h2, top_i, top_w = merge_project(
            attn, y_ssm, rest, xf, w_branch_attn[l].astype(BF16), w_branch_ssm[l].astype(BF16),
            w_out[l].astype(BF16), gate_bias[l].astype(F32)[None, :], norm_ffn[l].astype(F32)[None, :],
            router_w_p, router_b_p)

        ids, block_e, n_used = routing_layout(top_i, t)
        w1g_t, w1l_t = split_w1(w1[l].astype(F32))
        y4p = expert_ffn(h2, ids, block_e, n_used, w1g_t, w1l_t, w2[l].astype(BF16),
                         b1[l][:, None, 0::2].astype(F32), b1[l][:, None, 1::2].astype(F32),
                         b2[l][:, None, :].astype(F32), t)
        xf = combine_norm(x1, y4p, top_w, norm_final.astype(F32)[None, :], normalize=(l == depth - 1))
    return xf.reshape(batch, seq, d)
```

```python
import functools
import math

import jax
import jax.numpy as jnp
import numpy as np
from jax import lax
from jax.experimental import pallas as pl
from jax.experimental.pallas import tpu as pltpu

F32 = jnp.float32
BF16 = jnp.bfloat16

EPS = 1e-5
NEG_BIG = -1e30

HEAD_DIM = 64
ATTN_GROUPS = ((128, 1), (512, 4), (2048, 16))
HEADS_PER_GROUP = 8
GROUP_WIDTH = HEADS_PER_GROUP * HEAD_DIM
ATTN_BLOCK = 128
NUM_BUCKETS = 32
MAX_DISTANCE = 2048
SSM_HEAD_DIM = 64
N_SSM_GROUPS = 4
SSM_HEADS_PER_GROUP = 8
D_STATE = 128
CONV_WIDTH = 4
CHUNK = 128
N_EXPERTS = 32
TOP_K = 4
SWIGLU_LIMIT = 7.0
SWIGLU_ALPHA = 1.702

LANES = 128
V7X_VMEM_BYTES = 64 * 1024 * 1024
VMEM_LIMIT = V7X_VMEM_BYTES * 3 // 4

ROW_TILE = 1024
REST_COLUMN_TILES = 4
MERGE_SUBTILE = 256
MERGE_SUBTILES = 4
EXPERT_ROWS = 256
COMBINE_TILE = 512
SPLIT_W1_TILE = 1024
SSD_CHUNKS_PER_STEP = 4
CONV_TAIL = 16


def padded_tokens(n_tokens):
    return n_tokens + N_EXPERTS * EXPERT_ROWS // TOP_K + 2 * EXPERT_ROWS


def _params(semantics):
    return pltpu.CompilerParams(dimension_semantics=semantics, vmem_limit_bytes=VMEM_LIMIT)


def _rms_matmul_kernel(x_ref, g_ref, w_ref, ws_ref, o_ref, os_ref, hb_ref, h_ref):
    @pl.when(pl.program_id(1) == 0)
    def _():
        x = x_ref[...]
        ms = jnp.mean(x * x, axis=-1, keepdims=True)
        h_ref[...] = (x * lax.rsqrt(ms + EPS) * g_ref[...]).astype(BF16)
        hb_ref[...] = h_ref[...]
        os_ref[...] = jnp.dot(h_ref[...], ws_ref[...], preferred_element_type=F32)

    o_ref[...] = jnp.dot(h_ref[...], w_ref[...], preferred_element_type=F32).astype(o_ref.dtype)


def rms_matmul(x, g, w, w_side):
    t, d = x.shape
    n = w.shape[1]
    ns = w_side.shape[1]
    tm = ROW_TILE
    tn = n // REST_COLUMN_TILES
    return pl.pallas_call(
        _rms_matmul_kernel,
        grid=(t // tm, n // tn),
        in_specs=[
            pl.BlockSpec((tm, d), lambda i, j: (i, 0)),
            pl.BlockSpec((1, d), lambda i, j: (0, 0)),
            pl.BlockSpec((d, tn), lambda i, j: (0, j)),
            pl.BlockSpec((d, ns), lambda i, j: (0, 0)),
        ],
        out_specs=[
            pl.BlockSpec((tm, tn), lambda i, j: (i, j)),
            pl.BlockSpec((tm, ns), lambda i, j: (i, 0)),
            pl.BlockSpec((tm, d), lambda i, j: (i, 0)),
        ],
        out_shape=[jax.ShapeDtypeStruct((t, n), BF16), jax.ShapeDtypeStruct((t, ns), F32),
                   jax.ShapeDtypeStruct((t, d), BF16)],
        scratch_shapes=[pltpu.VMEM((tm, d), BF16)],
        compiler_params=_params(("parallel", "arbitrary")),
        name="rms_matmul",
    )(x, g, w, w_side)


def _qkv_kernel(h_ref, w_ref, o0_ref, o1_ref, o2_ref, acc_ref, tmp_ref):
    seq = h_ref.shape[0]
    slabs = GROUP_WIDTH // LANES
    n_blocks = seq // ATTN_BLOCK
    for gi, o_ref in enumerate((o0_ref, o1_ref, o2_ref)):
        dil = ATTN_GROUPS[gi][1]
        nb = n_blocks // dil
        acc = jnp.dot(h_ref[...], w_ref[:, gi * GROUP_WIDTH:(gi + 1) * GROUP_WIDTH], preferred_element_type=F32)
        if dil == 1:
            for n in range(n_blocks):
                o_ref[0, 0, n] = acc[n * ATTN_BLOCK:(n + 1) * ATTN_BLOCK, :].astype(o_ref.dtype)
            continue
        for s in range(slabs):
            acc_ref[s] = acc[:, s * LANES:(s + 1) * LANES]
        src_ref, step, group_rows = acc_ref, dil, seq
        if dil == 16:
            for s in range(slabs):
                for a in range(4):
                    tmp_ref[s, a * (seq // 4):(a + 1) * (seq // 4), :] = acc_ref[s, pl.ds(a, seq // 4, stride=4), :]
            src_ref, step, group_rows = tmp_ref, 4, seq // 4
        for r in range(dil):
            base = (r % (dil // step)) * group_rows + r // (dil // step)
            for n in range(nb):
                for s in range(slabs):
                    rows = pl.ds(base + n * ATTN_BLOCK * step, ATTN_BLOCK, stride=step)
                    o_ref[0, 0, r * nb + n, :, s * LANES:(s + 1) * LANES] = src_ref[s, rows, :].astype(o_ref.dtype)


def qkv_project(h, w_qkv, batch, seq):
    t, d = h.shape
    n_groups = len(ATTN_GROUPS)
    tn = n_groups * GROUP_WIDTH
    out_shapes, out_specs = [], []
    for _ in ATTN_GROUPS:
        shape = (3, batch, seq // ATTN_BLOCK, ATTN_BLOCK, GROUP_WIDTH)
        out_shapes.append(jax.ShapeDtypeStruct(shape, BF16))
        out_specs.append(pl.BlockSpec((1, 1) + shape[2:], lambda w, b: (w, b, 0, 0, 0)))
    return pl.pallas_call(
        _qkv_kernel,
        grid=(3, batch),
        in_specs=[pl.BlockSpec((seq, d), lambda w, b: (b, 0)), pl.BlockSpec((d, tn), lambda w, b: (0, w))],
        out_specs=out_specs,
        out_shape=out_shapes,
        scratch_shapes=[pltpu.VMEM((GROUP_WIDTH // LANES, seq, LANES), F32)] * 2,
        compiler_params=_params(("parallel", "parallel")),
        name="qkv_project",
    )(h, w_qkv)


def _attn_kernel(q0_ref, q1_ref, q2_ref, bias_ref, o_ref, out_ref, lse_ref):
    heads = LANES // HEAD_DIM
    scale = jnp.asarray(HEAD_DIM ** -0.5, BF16)
    n_blocks = q0_ref.shape[2]
    qk = (((2,), (2,)), ((0,), (0,)))
    pv = (((2,), (1,)), ((0,), (0,)))
    blk = lax.broadcasted_iota(jnp.int32, (n_blocks, 1, 1), 0)
    lane = lax.broadcasted_iota(jnp.int32, (1, 1, LANES), 2)

    def shifted(x):
        return jnp.concatenate([x[n_blocks - 1:], x[:n_blocks - 1]], axis=0)

    for g, ref in enumerate((q0_ref, q1_ref, q2_ref)):
        dil = ATTN_GROUPS[g][1]
        nb = n_blocks // dil
        q2h = ref[0, 0] * scale
        keys, vals = ref[1, 0], ref[2, 0]
        if nb > 1:
            keys = jnp.concatenate([shifted(keys), keys], axis=1)
            vals = jnp.concatenate([shifted(vals), vals], axis=1)
        n_keys = keys.shape[1]
        vals = jnp.concatenate([vals, jnp.ones(vals.shape, BF16)], axis=2)
        pvs, dens, maxs = [], [], []
        for h in range(heads):
            in_head = (lane >= h * HEAD_DIM) & (lane < (h + 1) * HEAD_DIM)
            q = jnp.where(in_head, q2h, jnp.zeros_like(q2h))
            s = lax.dot_general(q, keys, qk, preferred_element_type=F32)
            s = s + bias_ref[g, h, :, 2 * ATTN_BLOCK - n_keys:][None]
            if nb > 1:
                key_is_prev = lax.broadcasted_iota(jnp.int32, (1, 1, n_keys), 2) < ATTN_BLOCK
                s = jnp.where((blk % nb == 0) & key_is_prev, NEG_BIG, s)
            m = jnp.max(s, axis=-1, keepdims=True)
            p = jnp.exp(s - m).astype(BF16)
            acc = lax.dot_general(p, vals, pv, preferred_element_type=F32)
            pvs.append(acc[:, :, :LANES])
            dens.append(acc[:, :, LANES:])
            maxs.append(m)
        first = lane < HEAD_DIM
        den = jnp.where(first, dens[0], dens[1])
        o2 = jnp.where(first, pvs[0], pvs[1]) / den
        l2 = jnp.where(first, maxs[0], maxs[1]) + jnp.log(den)
        for r in range(dil):
            for n in range(nb):
                start = r + n * ATTN_BLOCK * dil
                rows = pl.ds(start, ATTN_BLOCK) if dil == 1 else pl.ds(start, ATTN_BLOCK, stride=dil)
                out_ref[g, rows, :] = o2[r * nb + n]
                lse_ref[g, rows, :] = l2[r * nb + n]

    l0, l1, l2 = lse_ref[0], lse_ref[1], lse_ref[2]
    lm = jnp.maximum(jnp.maximum(l0, l1), l2)
    e0, e1, e2 = jnp.exp(l0 - lm), jnp.exp(l1 - lm), jnp.exp(l2 - lm)
    mixed = (e0 * out_ref[0] + e1 * out_ref[1] + e2 * out_ref[2]) / (e0 + e1 + e2)
    o_ref[...] = mixed.astype(o_ref.dtype)


def dilated_attention(qkv_groups, bias, batch, seq):
    heads = LANES // HEAD_DIM
    in_specs = [pl.BlockSpec((3, 1) + a.shape[2:4] + (LANES,), lambda b, hp: (0, b, 0, 0, hp)) for a in qkv_groups]
    in_specs.append(pl.BlockSpec((len(ATTN_GROUPS), heads, ATTN_BLOCK, 2 * ATTN_BLOCK), lambda b, hp: (0, hp, 0, 0)))
    return pl.pallas_call(
        _attn_kernel,
        grid=(batch, GROUP_WIDTH // LANES),
        in_specs=in_specs,
        out_specs=pl.BlockSpec((seq, LANES), lambda b, hp: (b, hp)),
        out_shape=jax.ShapeDtypeStruct((batch * seq, GROUP_WIDTH), BF16),
        scratch_shapes=[pltpu.VMEM((len(ATTN_GROUPS), seq, LANES), F32),
                        pltpu.VMEM((len(ATTN_GROUPS), seq, LANES), F32)],
        compiler_params=_params(("parallel", "parallel")),
        name="dilated_attn",
    )(*qkv_groups, bias)


def attention_bias(rel_bias, gi):
    window, dil = ATTN_GROUPS[gi]
    w_sub = window // dil
    q_idx = np.arange(ATTN_BLOCK)[:, None]
    k_idx = np.arange(2 * ATTN_BLOCK)[None, :]
    delta = q_idx + ATTN_BLOCK - k_idx
    in_band = (delta >= 0) & (delta <= w_sub)
    dist = np.clip(delta, 0, w_sub) * dil
    max_exact = NUM_BUCKETS // 2
    nf = np.maximum(dist, max_exact).astype(np.float32)
    large = max_exact + (np.log(nf / max_exact) / math.log(MAX_DISTANCE / max_exact)
                         * (NUM_BUCKETS - max_exact)).astype(np.int32)
    large = np.minimum(large, NUM_BUCKETS - 1)
    bucket = np.where(dist < max_exact, dist, large)
    table = rel_bias[:, gi * HEADS_PER_GROUP:(gi + 1) * HEADS_PER_GROUP].astype(F32)
    onehot = (bucket.reshape(-1, 1) == np.arange(NUM_BUCKETS)[None, :]).astype(np.float32)
    bias = jnp.einsum('bh,nb->hn', table, jnp.asarray(onehot), precision=lax.Precision.HIGHEST)
    bias = bias.reshape(HEADS_PER_GROUP, ATTN_BLOCK, 2 * ATTN_BLOCK)
    return jnp.where(in_band[None], bias, NEG_BIG)


def _silu(v):
    half = 0.5 * v
    return half + half * jnp.tanh(half)


def _conv_silu(ext_ref, u_ref, w_ref, b_ref, first):
    rows = u_ref.shape[0]
    ext_rows = ext_ref.shape[0]

    if first is not None:
        @pl.when(first)
        def _():
            ext_ref[0:CONV_TAIL, :] = jnp.zeros((CONV_TAIL, ext_ref.shape[1]), ext_ref.dtype)

    u = u_ref[...]
    ext_ref[CONV_TAIL:, :] = u
    taps = CONV_WIDTH - 1
    out_row = lax.broadcasted_iota(jnp.int32, (rows, ext_rows), 0)
    src_row = lax.broadcasted_iota(jnp.int32, (rows, ext_rows), 1)
    shift_mat = jnp.concatenate([(src_row == out_row + (CONV_TAIL - k)).astype(BF16) for k in range(1, CONV_WIDTH)],
                                axis=0)
    shifted = jnp.dot(shift_mat, ext_ref[...], preferred_element_type=F32)
    acc = u.astype(F32) * w_ref[taps:CONV_WIDTH, :] + b_ref[...]
    for k in range(1, CONV_WIDTH):
        acc = acc + shifted[(k - 1) * rows:k * rows, :] * w_ref[taps - k:CONV_WIDTH - k, :]
    ext_ref[0:CONV_TAIL, :] = ext_ref[rows:, :]
    return _silu(acc)


def _ssd_kernel(x_ref, bc_ref, z_ref, dt_ref, cwx_ref, cbx_ref, cwbc_ref, cbbc_ref, dtb_ref, a_ref,
                dskip_ref, nw_ref, expand_ref, o_ref, extx_ref, extbc_ref, state_ref):
    first = pl.program_id(1) == 0
    gw = SSM_HEADS_PER_GROUP * SSM_HEAD_DIM

    @pl.when(first)
    def _():
        state_ref[...] = jnp.zeros(state_ref.shape, F32)

    row = lax.broadcasted_iota(jnp.int32, (CHUNK, CHUNK), 0)
    colm = lax.broadcasted_iota(jnp.int32, (CHUNK, CHUNK), 1)
    tril = row >= colm
    tril_b = tril.astype(BF16)
    head_rows = N_SSM_GROUPS * SSM_HEADS_PER_GROUP
    eye_b = (lax.broadcasted_iota(jnp.int32, (head_rows, CHUNK), 0)
             == lax.broadcasted_iota(jnp.int32, (head_rows, CHUNK), 1)).astype(BF16)
    nt = (((1,), (1,)), ((), ()))
    tn = (((0,), (0,)), ((), ()))

    def split3(v):
        p0 = v.astype(BF16)
        r1 = v - p0.astype(F32)
        p1 = r1.astype(BF16)
        p2 = (r1 - p1.astype(F32)).astype(BF16)
        return p0, p1, p2

    def select_rows(mat01, v):
        return sum(jnp.dot(mat01, p, preferred_element_type=F32) for p in split3(v))

    def select_cols(v, mat01):
        return sum(jnp.dot(p, mat01, preferred_element_type=F32) for p in split3(v)[:2])

    for sub, g in [(sub, g) for sub in range(x_ref.shape[0] // CHUNK) for g in range(N_SSM_GROUPS)]:
        rws = pl.ds(sub * CHUNK, CHUNK)
        if g == 0:
            sub_first = first if sub == 0 else None
            xs_all = _conv_silu(extx_ref, x_ref.at[rws], cwx_ref, cbx_ref, sub_first)
            bc_all = _conv_silu(extbc_ref, bc_ref.at[rws], cwbc_ref, cbbc_ref, sub_first)
            v = dt_ref[rws, :] + dtb_ref[...]
            dt = jnp.maximum(v, 0.0) + jnp.log1p(jnp.exp(-jnp.abs(v)))
            a_d = dt * a_ref[...]
            acs = select_rows(tril_b, a_d)
            acs_t = sum(lax.dot_general(eye_b, p, nt, preferred_element_type=F32) for p in split3(acs))
            e_acs = jnp.exp(acs)
            e_dec = jnp.exp(acs[CHUNK - 1:CHUNK, :] - acs)
        lanes = slice(g * LANES, (g + 1) * LANES)
        ch = slice(g * gw, (g + 1) * gw)
        expand = expand_ref[g]
        dt_e = select_cols(dt, expand)
        eacs_e = select_cols(e_acs, expand)
        edec_e = select_cols(e_dec, expand)
        elast_e = eacs_e[CHUNK - 1:CHUNK, :]

        xs = xs_all[:, ch]
        bm = bc_all[:, lanes].astype(BF16)
        cm = bc_all[:, N_SSM_GROUPS * D_STATE + g * D_STATE:N_SSM_GROUPS * D_STATE + (g + 1) * D_STATE].astype(BF16)
        x_d = xs * dt_e
        cb = lax.dot_general(cm, bm, nt, preferred_element_type=F32)

        prev = state_ref[g]
        y = jnp.dot(cm, prev.astype(BF16), preferred_element_type=F32) * eacs_e
        x_d16 = x_d.astype(BF16)
        parts = []
        for j in range(SSM_HEADS_PER_GROUP):
            head = g * SSM_HEADS_PER_GROUP + j
            seg = acs[:, head:head + 1] - acs_t[head:head + 1, :]
            m_h = (cb * jnp.exp(jnp.where(tril, seg, NEG_BIG))).astype(BF16)
            parts.append(jnp.dot(m_h, x_d16[:, j * SSM_HEAD_DIM:(j + 1) * SSM_HEAD_DIM],
                                 preferred_element_type=F32))
        y = y + jnp.concatenate(parts, axis=1)

        xw = (x_d * edec_e).astype(BF16)
        state_ref[g] = prev * elast_e + lax.dot_general(bm, xw, tn, preferred_element_type=F32)

        y = y + dskip_ref[:, ch] * xs
        y = y * _silu(z_ref[rws, ch].astype(F32))
        y = y * lax.rsqrt(jnp.mean(y * y, axis=-1, keepdims=True) + EPS)
        o_ref[rws, ch] = (y * nw_ref[:, ch]).astype(o_ref.dtype)


def ssd_mixer(rest, dt_raw, conv_w, conv_b, dt_bias_p, a_p, d_skip_e, norm_w, batch, seq):
    t = rest.shape[0]
    d_inner = N_SSM_GROUPS * SSM_HEADS_PER_GROUP * SSM_HEAD_DIM
    bc_w = 2 * N_SSM_GROUPS * D_STATE
    rows = CHUNK * SSD_CHUNKS_PER_STEP
    nc = seq // rows
    gw = SSM_HEADS_PER_GROUP * SSM_HEAD_DIM
    head_of_channel = np.arange(N_SSM_GROUPS)[:, None, None] * SSM_HEADS_PER_GROUP + np.arange(gw)[None, None, :] // SSM_HEAD_DIM
    expand = (np.arange(LANES)[None, :, None] == head_of_channel).astype(np.float32)
    rowmap = lambda b, c: (b * nc + c, 0)
    const = lambda b, c: (0, 0)
    return pl.pallas_call(
        _ssd_kernel,
        grid=(batch, nc),
        in_specs=[
            pl.BlockSpec((rows, d_inner), lambda b, c: (b * nc + c, 2)),
            pl.BlockSpec((rows, bc_w), lambda b, c: (b * nc + c, 6)),
            pl.BlockSpec((rows, d_inner), rowmap),
            pl.BlockSpec((rows, LANES), rowmap),
            pl.BlockSpec((CONV_WIDTH, d_inner), const),
            pl.BlockSpec((1, d_inner), const),
            pl.BlockSpec((CONV_WIDTH, bc_w), const),
            pl.BlockSpec((1, bc_w), const),
            pl.BlockSpec((1, LANES), const),
            pl.BlockSpec((1, LANES), const),
            pl.BlockSpec((1, d_inner), const),
            pl.BlockSpec((1, d_inner), const),
            pl.BlockSpec((N_SSM_GROUPS, LANES, gw), lambda b, c: (0, 0, 0)),
        ],
        out_specs=pl.BlockSpec((rows, d_inner), rowmap),
        out_shape=jax.ShapeDtypeStruct((t, d_inner), BF16),
        scratch_shapes=[
            pltpu.VMEM((CONV_TAIL + CHUNK, d_inner), BF16),
            pltpu.VMEM((CONV_TAIL + CHUNK, bc_w), BF16),
            pltpu.VMEM((N_SSM_GROUPS, D_STATE, gw), F32),
        ],
        compiler_params=_params(("parallel", "arbitrary")),
        name="ssd_mixer",
    )(rest, rest, rest, dt_raw, conv_w[:, :d_inner], conv_b[:, :d_inner], conv_w[:, d_inner:],
      conv_b[:, d_inner:], dt_bias_p, a_p, d_skip_e, norm_w, jnp.asarray(expand, dtype=BF16))


def _merge_kernel(at_ref, ys_ref, gl_ref, x_ref, wa_ref, ws_ref,
                  wo_ref, gb_ref, nf_ref, rw_ref, rb_ref, x1_ref, h2_ref, ti_ref, tw_ref):
    d = x_ref.shape[1]
    sub = x_ref.shape[0] // MERGE_SUBTILES
    for part in range(MERGE_SUBTILES):
        rows = slice(part * sub, (part + 1) * sub)
        y_attn = jnp.dot(at_ref[rows, :], wa_ref[...], preferred_element_type=F32)
        y_ssm = jnp.dot(ys_ref[rows, :], ws_ref[...], preferred_element_type=F32)
        gv = gl_ref[rows, :].astype(F32) + gb_ref[...]
        gates = 1.0 / (1.0 + jnp.exp(-gv))
        merged = gates[:, :d] * y_attn + gates[:, d:] * y_ssm
        x1 = x_ref[rows, :] + jnp.dot(merged.astype(BF16), wo_ref[...], preferred_element_type=F32)
        x1_ref[rows, :] = x1
        h2 = x1 * lax.rsqrt(jnp.mean(x1 * x1, axis=-1, keepdims=True) + EPS) * nf_ref[...]
        _store_slabs(h2_ref, part * sub, sub, _pack_bf16_pairs(h2))
        h_hi = h2.astype(BF16)
        h_mid = (h2 - h_hi.astype(F32)).astype(BF16)
        both = jnp.dot(h_hi, rw_ref[...], preferred_element_type=F32)
        logits = (both[:, :LANES] + both[:, LANES:]
                  + jnp.dot(h_mid, rw_ref[:, :LANES], preferred_element_type=F32) + rb_ref[...])
        lane = lax.broadcasted_iota(jnp.int32, logits.shape, 1)
        top_i = jnp.zeros(logits.shape, jnp.int32)
        top_v = jnp.full(logits.shape, NEG_BIG, F32)
        work = logits
        for k in range(TOP_K):
            m = jnp.max(work, axis=-1, keepdims=True)
            idx = jnp.min(jnp.where(work == m, lane, LANES), axis=-1, keepdims=True)
            top_i = jnp.where(lane == k, idx, top_i)
            top_v = jnp.where(lane == k, m, top_v)
            work = jnp.where(lane == idx, NEG_BIG * 2.0, work)
        ev = jnp.exp(top_v - jnp.max(top_v, axis=-1, keepdims=True))
        ti_ref[rows, :] = top_i
        tw_ref[rows, :] = ev / jnp.sum(ev, axis=-1, keepdims=True)


def merge_project(attn, y_ssm, rest, x, wa, ws, wo, gate_bias, norm_ffn, router_w_p, router_b_p):
    t, d = x.shape
    slab = d // 2 // LANES
    tm = MERGE_SUBTILE * MERGE_SUBTILES
    d_inner = y_ssm.shape[1]
    rowmap = lambda i: (i, 0)
    const = lambda i: (0, 0)
    full = lambda a: pl.BlockSpec(a.shape, const, pipeline_mode=pl.Buffered(1))
    args = [attn, y_ssm, rest, x, wa, ws, wo, gate_bias, norm_ffn, router_w_p, router_b_p]
    in_specs = (
        [pl.BlockSpec((tm, GROUP_WIDTH), rowmap),
         pl.BlockSpec((tm, d_inner), rowmap),
         pl.BlockSpec((tm, 2 * d), lambda i: (i, 1)),
         pl.BlockSpec((tm, d), rowmap)]
        + [full(a) for a in args[4:]]
    )
    return pl.pallas_call(
        _merge_kernel,
        grid=(t // tm,),
        in_specs=in_specs,
        out_specs=[pl.BlockSpec((tm, d), rowmap), pl.BlockSpec((tm * slab, LANES), rowmap),
                   pl.BlockSpec((tm, LANES), rowmap), pl.BlockSpec((tm, LANES), rowmap)],
        out_shape=[jax.ShapeDtypeStruct((t, d), F32), jax.ShapeDtypeStruct((t * slab, LANES), jnp.uint32),
                   jax.ShapeDtypeStruct((t, LANES), jnp.int32), jax.ShapeDtypeStruct((t, LANES), F32)],
        compiler_params=_params(("parallel",)),
        name="merge_project",
    )(*args)


def _split_w1_kernel(w_ref, g_ref, l_ref, t_ref):
    de = g_ref.shape[1]
    for s in range(t_ref.shape[0]):
        cols = slice(s * LANES, (s + 1) * LANES)
        t_ref[s] = w_ref[0, cols, :].T
        g_ref[0, :, cols] = t_ref[s, pl.ds(0, de, stride=2), :].astype(g_ref.dtype)
        l_ref[0, :, cols] = t_ref[s, pl.ds(1, de, stride=2), :].astype(l_ref.dtype)


def split_w1(w1):
    e, d, de2 = w1.shape
    de = de2 // 2
    tk = SPLIT_W1_TILE
    out = jax.ShapeDtypeStruct((e, de, d), BF16)
    return pl.pallas_call(
        _split_w1_kernel,
        grid=(e, d // tk),
        in_specs=[pl.BlockSpec((1, tk, de2), lambda i, k: (i, k, 0))],
        out_specs=[pl.BlockSpec((1, de, tk), lambda i, k: (i, 0, k))] * 2,
        out_shape=[out, out],
        scratch_shapes=[pltpu.VMEM((tk // LANES, de2, LANES), F32)],
        compiler_params=_params(("parallel", "parallel")),
        name="split_w1",
    )(w1)


def _pack_bf16_pairs(v):
    w = v.shape[1] // 2
    lo = lax.bitcast_convert_type(v[:, :w].astype(BF16).astype(F32), jnp.uint32) >> 16
    hi = lax.bitcast_convert_type(v[:, w:].astype(BF16).astype(F32), jnp.uint32) & jnp.uint32(0xFFFF0000)
    return lo | hi


def _unpack_bf16_pairs(p):
    lo = lax.bitcast_convert_type(p << 16, F32)
    hi = lax.bitcast_convert_type(p & jnp.uint32(0xFFFF0000), F32)
    return lo, hi


def _store_slabs(ref, row0, rows, packed):
    slab = packed.shape[1] // LANES
    for s in range(slab):
        ref[pl.ds(row0 * slab + s, rows, stride=slab), :] = packed[:, s * LANES:(s + 1) * LANES]


def _load_slabs(ref, rows, slab):
    return jnp.concatenate([ref[pl.ds(s, rows, stride=slab), :] for s in range(slab)], axis=1)


def _expert_kernel(be_ref, nu_ref, tab_ref, h2p_ref, w1g_ref, w1l_ref, w2_ref, bias_ref, y4p_ref, xbuf, obuf, gsem, ssem,
                   *, n_tokens, t_pad):
    i = pl.program_id(0)
    n_used = nu_ref[0]
    slot = lax.rem(i, 2)
    other = 1 - slot
    gslot = lax.rem(i, 3)
    gslot1 = lax.rem(i + 1, 3)
    gslot2 = lax.rem(i + 2, 3)
    bm = EXPERT_ROWS
    slab = xbuf.shape[1] // bm
    g0, g1, g2, sp, sc = (k * bm for k in range(5))

    def gather_copy(table, j, buf):
        src = pl.multiple_of(tab_ref[0, 0, table + j], slab)
        return pltpu.make_async_copy(h2p_ref.at[pl.ds(src, slab)],
                                     xbuf.at[buf, pl.ds(j * slab, slab)], gsem.at[buf])

    def scatter_copy(table, j, buf):
        dst = pl.multiple_of(tab_ref[0, 0, table + j], slab)
        return pltpu.make_async_copy(obuf.at[buf, pl.ds(j * slab, slab)],
                                     y4p_ref.at[pl.ds(dst, slab)], ssem.at[buf])

    def wait_block(kind, buf):
        if kind == "gather":
            pltpu.make_async_copy(h2p_ref.at[pl.ds(0, bm * slab)], xbuf.at[buf], gsem.at[buf]).wait()
        else:
            pltpu.make_async_copy(obuf.at[buf], y4p_ref.at[pl.ds(0, bm * slab)], ssem.at[buf]).wait()

    @pl.when(i == 0)
    def _():
        obuf[...] = jnp.zeros(obuf.shape, obuf.dtype)
        fills = [pltpu.make_async_copy(obuf.at[1], y4p_ref.at[pl.ds((k * t_pad + n_tokens) * slab + c * bm * slab,
                                                                     bm * slab)], ssem.at[1])
                 for k in range(TOP_K) for c in range((t_pad - n_tokens) // bm)]
        for fill in fills:
            fill.start()
        for fill in fills:
            fill.wait()
        for j in range(bm):
            gather_copy(g0, j, 0).start()
        for j in range(bm):
            gather_copy(g1, j, 1).start()
        spare_row = t_pad + t_pad - 2 * bm
        pltpu.make_async_copy(obuf.at[0], y4p_ref.at[pl.ds(spare_row * slab, bm * slab)], ssem.at[0]).start()

    @pl.when(i < n_used)
    def _():
        nt = (((1,), (1,)), ((), ()))
        de = w1g_ref.shape[1]
        wait_block("gather", gslot)
        lo, hi = _unpack_bf16_pairs(_load_slabs(xbuf.at[gslot], bm, slab))
        xb = jnp.concatenate([lo, hi], axis=1).astype(BF16)
        for j in range(bm):
            gather_copy(g2, j, gslot2).start(priority=j % 2)
        for j in range(bm):
            scatter_copy(sp, j, other).start(priority=j % 2)
        glu = lax.dot_general(xb, w1g_ref[0], nt, preferred_element_type=F32) + bias_ref[0, :, :de]
        lin = lax.dot_general(xb, w1l_ref[0], nt, preferred_element_type=F32) + bias_ref[0, :, de:2 * de]
        glu = jnp.minimum(glu, SWIGLU_LIMIT)
        lin = jnp.clip(lin, -SWIGLU_LIMIT, SWIGLU_LIMIT)
        act = glu * (1.0 / (1.0 + jnp.exp(-SWIGLU_ALPHA * glu))) * (lin + 1.0)
        y = jnp.dot(act.astype(BF16), w2_ref[0], preferred_element_type=F32) + bias_ref[0, :, 2 * de:]
        wait_block("scatter", slot)
        _store_slabs(obuf.at[slot], 0, bm, _pack_bf16_pairs(y))

    @pl.when(i == n_used - 1)
    def _():
        for j in range(bm):
            scatter_copy(sc, j, slot).start()
        wait_block("scatter", other)
        wait_block("scatter", slot)
        wait_block("gather", gslot1)
        wait_block("gather", gslot2)


def expert_ffn(h2p, ids, block_e, n_used, w1g, w1l, w2, b1g, b1l, b2, n_tokens):
    n_blocks, _, bm = ids.shape
    slab = h2p.shape[0] // n_tokens
    de, d = w1g.shape[1], w1g.shape[2]
    t_pad = padded_tokens(n_tokens)
    src = jnp.minimum(ids >> 2, n_tokens - 1) * slab
    dst = ((ids & 3) * t_pad + (ids >> 2)) * slab
    first = ((t_pad - 2 * bm + jnp.arange(bm, dtype=jnp.int32)) * slab).reshape(1, 1, bm)
    ahead = lambda a, k: jnp.concatenate([a[k:]] + [a[-1:]] * k, axis=0)
    table = jnp.concatenate([src, ahead(src, 1), ahead(src, 2), jnp.concatenate([first, dst[:-1]], axis=0), dst], axis=2)
    bias = jnp.concatenate([b1g, b1l, b2], axis=2)
    wmap = lambda i, be, nu: (be[i], 0, 0)
    grid_spec = pltpu.PrefetchScalarGridSpec(
        num_scalar_prefetch=2,
        grid=(n_blocks,),
        in_specs=[
            pl.BlockSpec((1, 1, 5 * bm), lambda i, be, nu: (i, 0, 0), memory_space=pltpu.SMEM),
            pl.BlockSpec(memory_space=pl.ANY),
            pl.BlockSpec((1, de, d), wmap),
            pl.BlockSpec((1, de, d), wmap),
            pl.BlockSpec((1, de, d), wmap),
            pl.BlockSpec((1, 1, 2 * de + d), wmap),
        ],
        out_specs=pl.BlockSpec(memory_space=pl.ANY),
        scratch_shapes=[
            pltpu.VMEM((3, bm * slab, LANES), jnp.uint32),
            pltpu.VMEM((2, bm * slab, LANES), jnp.uint32),
            pltpu.SemaphoreType.DMA((3,)),
            pltpu.SemaphoreType.DMA((2,)),
        ],
    )
    return pl.pallas_call(
        functools.partial(_expert_kernel, n_tokens=n_tokens, t_pad=t_pad),
        grid_spec=grid_spec,
        out_shape=jax.ShapeDtypeStruct((TOP_K * t_pad * slab, LANES), jnp.uint32),
        compiler_params=_params(("arbitrary",)),
        name="expert_ffn",
    )(block_e, n_used, table, h2p, w1g, w1l, w2, bias)


def _combine_kernel(x1_ref, y0_ref, y1_ref, y2_ref, y3_ref, tw_ref, nw_ref, o_ref, *, normalize):
    tm, d = x1_ref.shape
    slab = y0_ref.shape[0] // tm
    lo_sum = jnp.zeros((tm, d // 2), F32)
    hi_sum = jnp.zeros((tm, d // 2), F32)
    for k, y_ref in enumerate((y0_ref, y1_ref, y2_ref, y3_ref)):
        lo, hi = _unpack_bf16_pairs(_load_slabs(y_ref, tm, slab))
        w = tw_ref[:, k:k + 1]
        lo_sum = lo_sum + w * lo
        hi_sum = hi_sum + w * hi
    acc = x1_ref[...] + jnp.concatenate([lo_sum, hi_sum], axis=1)
    if normalize:
        acc = acc * lax.rsqrt(jnp.mean(acc * acc, axis=-1, keepdims=True) + EPS) * nw_ref[...]
    o_ref[...] = acc


def combine_norm(x1, y4p, top_w, norm_w, normalize):
    t, d = x1.shape
    tm = COMBINE_TILE
    t_pad = padded_tokens(t)
    slab = y4p.shape[0] // (TOP_K * t_pad)
    y_specs = [pl.BlockSpec((tm * slab, LANES), functools.partial(lambda i, k: (k * (t_pad // tm) + i, 0), k=k))
               for k in range(TOP_K)]
    return pl.pallas_call(
        functools.partial(_combine_kernel, normalize=normalize),
        grid=(t // tm,),
        in_specs=[pl.BlockSpec((tm, d), lambda i: (i, 0))] + y_specs
        + [pl.BlockSpec((tm, LANES), lambda i: (i, 0)), pl.BlockSpec((1, d), lambda i: (0, 0))],
        out_specs=pl.BlockSpec((tm, d), lambda i: (i, 0)),
        out_shape=jax.ShapeDtypeStruct((t, d), F32),
        compiler_params=_params(("parallel",)),
        name="combine_norm",
    )(x1, y4p, y4p, y4p, y4p, top_w, norm_w)


def routing_layout(top_i, n_tokens):
    n_assign = n_tokens * TOP_K
    bm = EXPERT_ROWS
    n_pad = N_EXPERTS * bm
    n_blocks = (n_assign + n_pad) // bm
    flat_e = top_i[:, :TOP_K].reshape(n_assign)
    counts = jnp.sum(flat_e[:, None] == jnp.arange(N_EXPERTS, dtype=jnp.int32)[None, :], axis=0, dtype=jnp.int32)
    padded = (counts + bm - 1) // bm * bm
    pend = jnp.cumsum(padded)
    n_used = (pend[-1] // bm).astype(jnp.int32)
    block_row = jnp.arange(n_blocks, dtype=jnp.int32) * bm
    block_e = jnp.minimum(jnp.sum(pend[None, :] <= block_row[:, None], axis=1, dtype=jnp.int32), N_EXPERTS - 1)
    spare_pos = jnp.arange(bm, dtype=jnp.int32)[None, :]
    spare_key = jnp.where(spare_pos < (padded - counts)[:, None], jnp.arange(N_EXPERTS, dtype=jnp.int32)[:, None],
                          N_EXPERTS).reshape(n_pad)
    keys = jnp.concatenate([flat_e, spare_key])
    pos_bits = (n_assign + n_pad - 1).bit_length()
    assert (N_EXPERTS + 1) << pos_bits < 2 ** 31
    packed = jnp.sort((keys << pos_bits) | jnp.arange(n_assign + n_pad, dtype=jnp.int32))
    order = packed & ((1 << pos_bits) - 1)
    return order.reshape(n_blocks, 1, bm), block_e, n_used.reshape(1)


def kernel(x, w_in, rel_bias, w_branch_attn, conv_w, conv_b, dt_bias, a_log, d_skip, ssm_norm_w,
           w_branch_ssm, gate_bias, w_out, norm_mix, norm_ffn, router_w, router_b, w1, b1, w2, b2,
           norm_final):
    batch, seq, d = x.shape
    t = batch * seq
    depth = w_in.shape[0]
    n_groups = len(ATTN_GROUPS)
    attn_w = n_groups * GROUP_WIDTH
    d_inner = N_SSM_GROUPS * SSM_HEADS_PER_GROUP * SSM_HEAD_DIM
    n_heads = N_SSM_GROUPS * SSM_HEADS_PER_GROUP
    bc_w = 2 * N_SSM_GROUPS * D_STATE
    xf = x.reshape(t, d)
    for l in range(depth):
        wl = w_in[l]
        o_z = 3 * attn_w
        o_xbc = o_z + d_inner
        o_dt = o_xbc + d_inner + bc_w
        o_gate = o_dt + n_heads
        w_qkv = wl[:, :o_z].astype(BF16)
        w_rest = jnp.concatenate([wl[:, o_z:o_xbc], wl[:, o_gate:], wl[:, o_xbc:o_dt]], axis=1).astype(BF16)
        w_dt = jnp.zeros((d, LANES), F32).at[:, :n_heads].set(wl[:, o_dt:o_gate]).astype(BF16)
        dt_bias_p = jnp.zeros((1, LANES), F32).at[0, :n_heads].set(dt_bias[l].astype(F32))
        a_p = jnp.zeros((1, LANES), F32).at[0, :n_heads].set(-jnp.exp(a_log[l].astype(F32)))
        d_skip_e = jnp.repeat(d_skip[l].astype(F32), SSM_HEAD_DIM)[None, :]

        g_mix = norm_mix[l].astype(F32)[None, :]
        rest, dt_raw, h = rms_matmul(xf, g_mix, w_rest, w_dt)
        qkv_groups = qkv_project(h, w_qkv, batch, seq)
        bias = jnp.stack([attention_bias(rel_bias, gi) for gi in range(n_groups)])
        attn = dilated_attention(qkv_groups, bias, batch, seq)
        y_ssm = ssd_mixer(rest, dt_raw, conv_w[l].astype(F32), conv_b[l].astype(F32)[None, :], dt_bias_p, a_p,
                          d_skip_e, ssm_norm_w[l].astype(F32)[None, :], batch, seq)
        rw = jnp.zeros((d, LANES), F32).at[:, :N_EXPERTS].set(router_w[l].astype(F32))
        rw_hi = rw.astype(BF16)
        router_w_p = jnp.concatenate([rw_hi, (rw - rw_hi.astype(F32)).astype(BF16)], axis=1)
        router_b_p = jnp.full((1, LANES), NEG_BIG, F32).at[0, :N_EXPERTS].set(router_b[l].astype(F32))
        x1, h2, top_i, top_w = merge_project(
            attn, y_ssm, rest, xf, w_branch_attn[l].astype(BF16), w_branch_ssm[l].astype(BF16),
            w_out[l].astype(BF16), gate_bias[l].astype(F32)[None, :], norm_ffn[l].astype(F32)[None, :],
            router_w_p, router_b_p)

        ids, block_e, n_used = routing_layout(top_i, t)
        w1g_t, w1l_t = split_w1(w1[l].astype(F32))
        y4p = expert_ffn(h2, ids, block_e, n_used, w1g_t, w1l_t, w2[l].astype(BF16),
                         b1[l][:, None, 0::2].astype(F32), b1[l][:, None, 1::2].astype(F32),
                         b2[l][:, None, :].astype(F32), t)
        xf = combine_norm(x1, y4p, top_w, norm_final.astype(F32)[None, :], normalize=(l == depth - 1))
    return xf.reshape(batch, seq, d)
```

```python
import functools
import math

import jax
import jax.numpy as jnp
import numpy as np
from jax import lax
from jax.experimental import pallas as pl
from jax.experimental.pallas import tpu as pltpu

F32 = jnp.float32
BF16 = jnp.bfloat16

EPS = 1e-5
NEG_BIG = -1e30

HEAD_DIM = 64
ATTN_GROUPS = ((128, 1), (512, 4), (2048, 16))
HEADS_PER_GROUP = 8
GROUP_WIDTH = HEADS_PER_GROUP * HEAD_DIM
ATTN_BLOCK = 128
NUM_BUCKETS = 32
MAX_DISTANCE = 2048
SSM_HEAD_DIM = 64
N_SSM_GROUPS = 4
SSM_HEADS_PER_GROUP = 8
D_STATE = 128
CONV_WIDTH = 4
CHUNK = 128
N_EXPERTS = 32
TOP_K = 4
SWIGLU_LIMIT = 7.0
SWIGLU_ALPHA = 1.702

LANES = 128
V7X_VMEM_BYTES = 64 * 1024 * 1024
VMEM_LIMIT = V7X_VMEM_BYTES * 3 // 4

ROW_TILE = 1024
REST_COLUMN_TILES = 4
MERGE_SUBTILE = 256
MERGE_SUBTILES = 4
EXPERT_ROWS = 256
COMBINE_TILE = 512
SPLIT_W1_TILE = 1024
SSD_CHUNKS_PER_STEP = 4
CONV_TAIL = 16


def padded_tokens(n_tokens):
    return n_tokens + N_EXPERTS * EXPERT_ROWS // TOP_K + 2 * EXPERT_ROWS


def _params(semantics):
    return pltpu.CompilerParams(dimension_semantics=semantics, vmem_limit_bytes=VMEM_LIMIT)


def _rms_matmul_kernel(x_ref, g_ref, w_ref, ws_ref, o_ref, os_ref, hb_ref, h_ref):
    @pl.when(pl.program_id(1) == 0)
    def _():
        x = x_ref[...]
        ms = jnp.mean(x * x, axis=-1, keepdims=True)
        h_ref[...] = (x * lax.rsqrt(ms + EPS) * g_ref[...]).astype(BF16)
        hb_ref[...] = h_ref[...]
        os_ref[...] = jnp.dot(h_ref[...], ws_ref[...], preferred_element_type=F32)

    o_ref[...] = jnp.dot(h_ref[...], w_ref[...], preferred_element_type=F32).astype(o_ref.dtype)


def rms_matmul(x, g, w, w_side):
    t, d = x.shape
    n = w.shape[1]
    ns = w_side.shape[1]
    tm = ROW_TILE
    tn = n // REST_COLUMN_TILES
    return pl.pallas_call(
        _rms_matmul_kernel,
        grid=(t // tm, n // tn),
        in_specs=[
            pl.BlockSpec((tm, d), lambda i, j: (i, 0)),
            pl.BlockSpec((1, d), lambda i, j: (0, 0)),
            pl.BlockSpec((d, tn), lambda i, j: (0, j)),
            pl.BlockSpec((d, ns), lambda i, j: (0, 0)),
        ],
        out_specs=[
            pl.BlockSpec((tm, tn), lambda i, j: (i, j)),
            pl.BlockSpec((tm, ns), lambda i, j: (i, 0)),
            pl.BlockSpec((tm, d), lambda i, j: (i, 0)),
        ],
        out_shape=[jax.ShapeDtypeStruct((t, n), BF16), jax.ShapeDtypeStruct((t, ns), F32),
                   jax.ShapeDtypeStruct((t, d), BF16)],
        scratch_shapes=[pltpu.VMEM((tm, d), BF16)],
        compiler_params=_params(("parallel", "arbitrary")),
        name="rms_matmul",
    )(x, g, w, w_side)


def _qkv_kernel(h_ref, w_ref, o0_ref, o1_ref, o2_ref, acc_ref, tmp_ref):
    seq = h_ref.shape[0]
    slabs = GROUP_WIDTH // LANES
    n_blocks = seq // ATTN_BLOCK
    for gi, o_ref in enumerate((o0_ref, o1_ref, o2_ref)):
        dil = ATTN_GROUPS[gi][1]
        nb = n_blocks // dil
        acc = jnp.dot(h_ref[...], w_ref[:, gi * GROUP_WIDTH:(gi + 1) * GROUP_WIDTH], preferred_element_type=F32)
        if dil == 1:
            for n in range(n_blocks):
                o_ref[0, 0, n] = acc[n * ATTN_BLOCK:(n + 1) * ATTN_BLOCK, :].astype(o_ref.dtype)
            continue
        for s in range(slabs):
            acc_ref[s] = acc[:, s * LANES:(s + 1) * LANES]
        src_ref, step, group_rows = acc_ref, dil, seq
        if dil == 16:
            for s in range(slabs):
                for a in range(4):
                    tmp_ref[s, a * (seq // 4):(a + 1) * (seq // 4), :] = acc_ref[s, pl.ds(a, seq // 4, stride=4), :]
            src_ref, step, group_rows = tmp_ref, 4, seq // 4
        for r in range(dil):
            base = (r % (dil // step)) * group_rows + r // (dil // step)
            for n in range(nb):
                for s in range(slabs):
                    rows = pl.ds(base + n * ATTN_BLOCK * step, ATTN_BLOCK, stride=step)
                    o_ref[0, 0, r * nb + n, :, s * LANES:(s + 1) * LANES] = src_ref[s, rows, :].astype(o_ref.dtype)


def qkv_project(h, w_qkv, batch, seq):
    t, d = h.shape
    n_groups = len(ATTN_GROUPS)
    tn = n_groups * GROUP_WIDTH
    out_shapes, out_specs = [], []
    for _ in ATTN_GROUPS:
        shape = (3, batch, seq // ATTN_BLOCK, ATTN_BLOCK, GROUP_WIDTH)
        out_shapes.append(jax.ShapeDtypeStruct(shape, BF16))
        out_specs.append(pl.BlockSpec((1, 1) + shape[2:], lambda w, b: (w, b, 0, 0, 0)))
    return pl.pallas_call(
        _qkv_kernel,
        grid=(3, batch),
        in_specs=[pl.BlockSpec((seq, d), lambda w, b: (b, 0)), pl.BlockSpec((d, tn), lambda w, b: (0, w))],
        out_specs=out_specs,
        out_shape=out_shapes,
        scratch_shapes=[pltpu.VMEM((GROUP_WIDTH // LANES, seq, LANES), F32)] * 2,
        compiler_params=_params(("parallel", "parallel")),
        name="qkv_project",
    )(h, w_qkv)


def _attn_kernel(q0_ref, q1_ref, q2_ref, bias_ref, o_ref, out_ref, lse_ref):
    heads = LANES // HEAD_DIM
    scale = jnp.asarray(HEAD_DIM ** -0.5, BF16)
    n_blocks = q0_ref.shape[2]
    qk = (((2,), (2,)), ((0,), (0,)))
    pv = (((2,), (1,)), ((0,), (0,)))
    blk = lax.broadcasted_iota(jnp.int32, (n_blocks, 1, 1), 0)
    lane = lax.broadcasted_iota(jnp.int32, (1, 1, LANES), 2)

    def shifted(x):
        return jnp.concatenate([x[n_blocks - 1:], x[:n_blocks - 1]], axis=0)

    for g, ref in enumerate((q0_ref, q1_ref, q2_ref)):
        dil = ATTN_GROUPS[g][1]
        nb = n_blocks // dil
        q2h = ref[0, 0] * scale
        keys, vals = ref[1, 0], ref[2, 0]
        if nb > 1:
            keys = jnp.concatenate([shifted(keys), keys], axis=1)
            vals = jnp.concatenate([shifted(vals), vals], axis=1)
        n_keys = keys.shape[1]
        vals = jnp.concatenate([vals, jnp.ones(vals.shape, BF16)], axis=2)
        pvs, dens, maxs = [], [], []
        for h in range(heads):
            in_head = (lane >= h * HEAD_DIM) & (lane < (h + 1) * HEAD_DIM)
            q = jnp.where(in_head, q2h, jnp.zeros_like(q2h))
            s = lax.dot_general(q, keys, qk, preferred_element_type=F32)
            s = s + bias_ref[g, h, :, 2 * ATTN_BLOCK - n_keys:][None]
            if nb > 1:
                key_is_prev = lax.broadcasted_iota(jnp.int32, (1, 1, n_keys), 2) < ATTN_BLOCK
                s = jnp.where((blk % nb == 0) & key_is_prev, NEG_BIG, s)
            m = jnp.max(s, axis=-1, keepdims=True)
            p = jnp.exp(s - m).astype(BF16)
            acc = lax.dot_general(p, vals, pv, preferred_element_type=F32)
            pvs.append(acc[:, :, :LANES])
            dens.append(acc[:, :, LANES:])
            maxs.append(m)
        first = lane < HEAD_DIM
        den = jnp.where(first, dens[0], dens[1])
        o2 = jnp.where(first, pvs[0], pvs[1]) / den
        l2 = jnp.where(first, maxs[0], maxs[1]) + jnp.log(den)
        for r in range(dil):
            for n in range(nb):
                start = r + n * ATTN_BLOCK * dil
                rows = pl.ds(start, ATTN_BLOCK) if dil == 1 else pl.ds(start, ATTN_BLOCK, stride=dil)
                out_ref[g, rows, :] = o2[r * nb + n]
                lse_ref[g, rows, :] = l2[r * nb + n]

    l0, l1, l2 = lse_ref[0], lse_ref[1], lse_ref[2]
    lm = jnp.maximum(jnp.maximum(l0, l1), l2)
    e0, e1, e2 = jnp.exp(l0 - lm), jnp.exp(l1 - lm), jnp.exp(l2 - lm)
    mixed = (e0 * out_ref[0] + e1 * out_ref[1] + e2 * out_ref[2]) / (e0 + e1 + e2)
    o_ref[...] = mixed.astype(o_ref.dtype)


def dilated_attention(qkv_groups, bias, batch, seq):
    heads = LANES // HEAD_DIM
    in_specs = [pl.BlockSpec((3, 1) + a.shape[2:4] + (LANES,), lambda b, hp: (0, b, 0, 0, hp)) for a in qkv_groups]
    in_specs.append(pl.BlockSpec((len(ATTN_GROUPS), heads, ATTN_BLOCK, 2 * ATTN_BLOCK), lambda b, hp: (0, hp, 0, 0)))
    return pl.pallas_call(
        _attn_kernel,
        grid=(batch, GROUP_WIDTH // LANES),
        in_specs=in_specs,
        out_specs=pl.BlockSpec((seq, LANES), lambda b, hp: (b, hp)),
        out_shape=jax.ShapeDtypeStruct((batch * seq, GROUP_WIDTH), BF16),
        scratch_shapes=[pltpu.VMEM((len(ATTN_GROUPS), seq, LANES), F32),
                        pltpu.VMEM((len(ATTN_GROUPS), seq, LANES), F32)],
        compiler_params=_params(("parallel", "parallel")),
        name="dilated_attn",
    )(*qkv_groups, bias)


def attention_bias(rel_bias, gi):
    window, dil = ATTN_GROUPS[gi]
    w_sub = window // dil
    q_idx = np.arange(ATTN_BLOCK)[:, None]
    k_idx = np.arange(2 * ATTN_BLOCK)[None, :]
    delta = q_idx + ATTN_BLOCK - k_idx
    in_band = (delta >= 0) & (delta <= w_sub)
    dist = np.clip(delta, 0, w_sub) * dil
    max_exact = NUM_BUCKETS // 2
    nf = np.maximum(dist, max_exact).astype(np.float32)
    large = max_exact + (np.log(nf / max_exact) / math.log(MAX_DISTANCE / max_exact)
                         * (NUM_BUCKETS - max_exact)).astype(np.int32)
    large = np.minimum(large, NUM_BUCKETS - 1)
    bucket = np.where(dist < max_exact, dist, large)
    table = rel_bias[:, gi * HEADS_PER_GROUP:(gi + 1) * HEADS_PER_GROUP].astype(F32)
    onehot = (bucket.reshape(-1, 1) == np.arange(NUM_BUCKETS)[None, :]).astype(np.float32)
    bias = jnp.einsum('bh,nb->hn', table, jnp.asarray(onehot), precision=lax.Precision.HIGHEST)
    bias = bias.reshape(HEADS_PER_GROUP, ATTN_BLOCK, 2 * ATTN_BLOCK)
    return jnp.where(in_band[None], bias, NEG_BIG)


def _silu(v):
    half = 0.5 * v
    return half + half * jnp.tanh(half)


def _conv_silu(ext_ref, u_ref, w_ref, b_ref, first):
    rows = u_ref.shape[0]
    ext_rows = ext_ref.shape[0]

    if first is not None:
        @pl.when(first)
        def _():
            ext_ref[0:CONV_TAIL, :] = jnp.zeros((CONV_TAIL, ext_ref.shape[1]), ext_ref.dtype)

    u = u_ref[...]
    ext_ref[CONV_TAIL:, :] = u
    taps = CONV_WIDTH - 1
    out_row = lax.broadcasted_iota(jnp.int32, (rows, ext_rows), 0)
    src_row = lax.broadcasted_iota(jnp.int32, (rows, ext_rows), 1)
    shift_mat = jnp.concatenate([(src_row == out_row + (CONV_TAIL - k)).astype(BF16) for k in range(1, CONV_WIDTH)],
                                axis=0)
    shifted = jnp.dot(shift_mat, ext_ref[...], preferred_element_type=F32)
    acc = u.astype(F32) * w_ref[taps:CONV_WIDTH, :] + b_ref[...]
    for k in range(1, CONV_WIDTH):
        acc = acc + shifted[(k - 1) * rows:k * rows, :] * w_ref[taps - k:CONV_WIDTH - k, :]
    ext_ref[0:CONV_TAIL, :] = ext_ref[rows:, :]
    return _silu(acc)


def _ssd_kernel(x_ref, bc_ref, z_ref, dt_ref, cwx_ref, cbx_ref, cwbc_ref, cbbc_ref, dtb_ref, a_ref,
                dskip_ref, nw_ref, expand_ref, o_ref, extx_ref, extbc_ref, state_ref):
    first = pl.program_id(1) == 0
    gw = SSM_HEADS_PER_GROUP * SSM_HEAD_DIM

    @pl.when(first)
    def _():
        state_ref[...] = jnp.zeros(state_ref.shape, F32)

    row = lax.broadcasted_iota(jnp.int32, (CHUNK, CHUNK), 0)
    colm = lax.broadcasted_iota(jnp.int32, (CHUNK, CHUNK), 1)
    tril = row >= colm
    tril_b = tril.astype(BF16)
    head_rows = N_SSM_GROUPS * SSM_HEADS_PER_GROUP
    eye_b = (lax.broadcasted_iota(jnp.int32, (head_rows, CHUNK), 0)
             == lax.broadcasted_iota(jnp.int32, (head_rows, CHUNK), 1)).astype(BF16)
    nt = (((1,), (1,)), ((), ()))
    tn = (((0,), (0,)), ((), ()))

    def split3(v):
        p0 = v.astype(BF16)
        r1 = v - p0.astype(F32)
        p1 = r1.astype(BF16)
        p2 = (r1 - p1.astype(F32)).astype(BF16)
        return p0, p1, p2

    def select_rows(mat01, v):
        return sum(jnp.dot(mat01, p, preferred_element_type=F32) for p in split3(v))

    def select_cols(v, mat01, parts):
        return sum(jnp.dot(p, mat01, preferred_element_type=F32) for p in split3(v)[:parts])

    for sub, g in [(sub, g) for sub in range(x_ref.shape[0] // CHUNK) for g in range(N_SSM_GROUPS)]:
        rws = pl.ds(sub * CHUNK, CHUNK)
        if g == 0:
            sub_first = first if sub == 0 else None
            xs_all = _conv_silu(extx_ref, x_ref.at[rws], cwx_ref, cbx_ref, sub_first)
            bc_all = _conv_silu(extbc_ref, bc_ref.at[rws], cwbc_ref, cbbc_ref, sub_first)
            v = dt_ref[rws, :] + dtb_ref[...]
            dt = jnp.maximum(v, 0.0) + jnp.log1p(jnp.exp(-jnp.abs(v)))
            a_d = dt * a_ref[...]
            acs = select_rows(tril_b, a_d)
            acs_t = sum(lax.dot_general(eye_b, p, nt, preferred_element_type=F32) for p in split3(acs))
            e_acs = jnp.exp(acs)
            e_dec = jnp.exp(acs[CHUNK - 1:CHUNK, :] - acs)
        lanes = slice(g * LANES, (g + 1) * LANES)
        ch = slice(g * gw, (g + 1) * gw)
        expand = expand_ref[g]
        dt_e = select_cols(dt, expand, 1)
        eacs_e = select_cols(e_acs, expand, 2)
        edec_e = select_cols(e_dec, expand, 1)
        elast_e = eacs_e[CHUNK - 1:CHUNK, :]

        xs = xs_all[:, ch]
        bm = bc_all[:, lanes].astype(BF16)
        cm = bc_all[:, N_SSM_GROUPS * D_STATE + g * D_STATE:N_SSM_GROUPS * D_STATE + (g + 1) * D_STATE].astype(BF16)
        x_d = xs * dt_e
        cb = lax.dot_general(cm, bm, nt, preferred_element_type=F32)

        prev = state_ref[g]
        y = jnp.dot(cm, prev.astype(BF16), preferred_element_type=F32) * eacs_e
        x_d16 = x_d.astype(BF16)
        parts = []
        for j in range(SSM_HEADS_PER_GROUP):
            head = g * SSM_HEADS_PER_GROUP + j
            seg = acs[:, head:head + 1] - acs_t[head:head + 1, :]
            m_h = (cb * jnp.exp(jnp.where(tril, seg, NEG_BIG))).astype(BF16)
            parts.append(jnp.dot(m_h, x_d16[:, j * SSM_HEAD_DIM:(j + 1) * SSM_HEAD_DIM],
                                 preferred_element_type=F32))
        y = y + jnp.concatenate(parts, axis=1)

        xw = (x_d * edec_e).astype(BF16)
        state_ref[g] = prev * elast_e + lax.dot_general(bm, xw, tn, preferred_element_type=F32)

        y = y + dskip_ref[:, ch] * xs
        y = y * _silu(z_ref[rws, ch].astype(F32))
        y = y * lax.rsqrt(jnp.mean(y * y, axis=-1, keepdims=True) + EPS)
        o_ref[rws, ch] = (y * nw_ref[:, ch]).astype(o_ref.dtype)


def ssd_mixer(rest, dt_raw, conv_w, conv_b, dt_bias_p, a_p, d_skip_e, norm_w, batch, seq):
    t = rest.shape[0]
    d_inner = N_SSM_GROUPS * SSM_HEADS_PER_GROUP * SSM_HEAD_DIM
    bc_w = 2 * N_SSM_GROUPS * D_STATE
    rows = CHUNK * SSD_CHUNKS_PER_STEP
    nc = seq // rows
    gw = SSM_HEADS_PER_GROUP * SSM_HEAD_DIM
    head_of_channel = np.arange(N_SSM_GROUPS)[:, None, None] * SSM_HEADS_PER_GROUP + np.arange(gw)[None, None, :] // SSM_HEAD_DIM
    expand = (np.arange(LANES)[None, :, None] == head_of_channel).astype(np.float32)
    rowmap = lambda b, c: (b * nc + c, 0)
    const = lambda b, c: (0, 0)
    return pl.pallas_call(
        _ssd_kernel,
        grid=(batch, nc),
        in_specs=[
            pl.BlockSpec((rows, d_inner), lambda b, c: (b * nc + c, 2)),
            pl.BlockSpec((rows, bc_w), lambda b, c: (b * nc + c, 6)),
            pl.BlockSpec((rows, d_inner), rowmap),
            pl.BlockSpec((rows, LANES), rowmap),
            pl.BlockSpec((CONV_WIDTH, d_inner), const),
            pl.BlockSpec((1, d_inner), const),
            pl.BlockSpec((CONV_WIDTH, bc_w), const),
            pl.BlockSpec((1, bc_w), const),
            pl.BlockSpec((1, LANES), const),
            pl.BlockSpec((1, LANES), const),
            pl.BlockSpec((1, d_inner), const),
            pl.BlockSpec((1, d_inner), const),
            pl.BlockSpec((N_SSM_GROUPS, LANES, gw), lambda b, c: (0, 0, 0)),
        ],
        out_specs=pl.BlockSpec((rows, d_inner), rowmap),
        out_shape=jax.ShapeDtypeStruct((t, d_inner), BF16),
        scratch_shapes=[
            pltpu.VMEM((CONV_TAIL + CHUNK, d_inner), BF16),
            pltpu.VMEM((CONV_TAIL + CHUNK, bc_w), BF16),
            pltpu.VMEM((N_SSM_GROUPS, D_STATE, gw), F32),
        ],
        compiler_params=_params(("parallel", "arbitrary")),
        name="ssd_mixer",
    )(rest, rest, rest, dt_raw, conv_w[:, :d_inner], conv_b[:, :d_inner], conv_w[:, d_inner:],
      conv_b[:, d_inner:], dt_bias_p, a_p, d_skip_e, norm_w, jnp.asarray(expand, dtype=BF16))


def _merge_kernel(at_ref, ys_ref, gl_ref, x_ref, wa_ref, ws_ref,
                  wo_ref, gb_ref, nf_ref, rw_ref, rb_ref, x1_ref, h2_ref, ti_ref, tw_ref):
    d = x_ref.shape[1]
    sub = x_ref.shape[0] // MERGE_SUBTILES
    for part in range(MERGE_SUBTILES):
        rows = slice(part * sub, (part + 1) * sub)
        y_attn = jnp.dot(at_ref[rows, :], wa_ref[...], preferred_element_type=F32)
        y_ssm = jnp.dot(ys_ref[rows, :], ws_ref[...], preferred_element_type=F32)
        gv = gl_ref[rows, :].astype(F32) + gb_ref[...]
        gates = 1.0 / (1.0 + jnp.exp(-gv))
        merged = gates[:, :d] * y_attn + gates[:, d:] * y_ssm
        x1 = x_ref[rows, :] + jnp.dot(merged.astype(BF16), wo_ref[...], preferred_element_type=F32)
        x1_ref[rows, :] = x1
        h2 = x1 * lax.rsqrt(jnp.mean(x1 * x1, axis=-1, keepdims=True) + EPS) * nf_ref[...]
        _store_slabs(h2_ref, part * sub, sub, _pack_bf16_pairs(h2))
        h_hi = h2.astype(BF16)
        h_mid = (h2 - h_hi.astype(F32)).astype(BF16)
        both = jnp.dot(h_hi, rw_ref[...], preferred_element_type=F32)
        logits = (both[:, :LANES] + both[:, LANES:]
                  + jnp.dot(h_mid, rw_ref[:, :LANES], preferred_element_type=F32) + rb_ref[...])
        lane = lax.broadcasted_iota(jnp.int32, logits.shape, 1)
        top_i = jnp.zeros(logits.shape, jnp.int32)
        top_v = jnp.full(logits.shape, NEG_BIG, F32)
        work = logits
        for k in range(TOP_K):
            m = jnp.max(work, axis=-1, keepdims=True)
            idx = jnp.min(jnp.where(work == m, lane, LANES), axis=-1, keepdims=True)
            top_i = jnp.where(lane == k, idx, top_i)
            top_v = jnp.where(lane == k, m, top_v)
            work = jnp.where(lane == idx, NEG_BIG * 2.0, work)
        ev = jnp.exp(top_v - jnp.max(top_v, axis=-1, keepdims=True))
        ti_ref[rows, :] = top_i
        tw_ref[rows, :] = ev / jnp.sum(ev, axis=-1, keepdims=True)


def merge_project(attn, y_ssm, rest, x, wa, ws, wo, gate_bias, norm_ffn, router_w_p, router_b_p):
    t, d = x.shape
    slab = d // 2 // LANES
    tm = MERGE_SUBTILE * MERGE_SUBTILES
    d_inner = y_ssm.shape[1]
    rowmap = lambda i: (i, 0)
    const = lambda i: (0, 0)
    full = lambda a: pl.BlockSpec(a.shape, const, pipeline_mode=pl.Buffered(1))
    args = [attn, y_ssm, rest, x, wa, ws, wo, gate_bias, norm_ffn, router_w_p, router_b_p]
    in_specs = (
        [pl.BlockSpec((tm, GROUP_WIDTH), rowmap),
         pl.BlockSpec((tm, d_inner), rowmap),
         pl.BlockSpec((tm, 2 * d), lambda i: (i, 1)),
         pl.BlockSpec((tm, d), rowmap)]
        + [full(a) for a in args[4:]]
    )
    return pl.pallas_call(
        _merge_kernel,
        grid=(t // tm,),
        in_specs=in_specs,
        out_specs=[pl.BlockSpec((tm, d), rowmap), pl.BlockSpec((tm * slab, LANES), rowmap),
                   pl.BlockSpec((tm, LANES), rowmap), pl.BlockSpec((tm, LANES), rowmap)],
        out_shape=[jax.ShapeDtypeStruct((t, d), F32), jax.ShapeDtypeStruct((t * slab, LANES), jnp.uint32),
                   jax.ShapeDtypeStruct((t, LANES), jnp.int32), jax.ShapeDtypeStruct((t, LANES), F32)],
        compiler_params=_params(("parallel",)),
        name="merge_project",
    )(*args)


def _split_w1_kernel(w_ref, g_ref, l_ref, t_ref):
    de = g_ref.shape[1]
    for s in range(t_ref.shape[0]):
        cols = slice(s * LANES, (s + 1) * LANES)
        t_ref[s] = w_ref[0, cols, :].T
        g_ref[0, :, cols] = t_ref[s, pl.ds(0, de, stride=2), :].astype(g_ref.dtype)
        l_ref[0, :, cols] = t_ref[s, pl.ds(1, de, stride=2), :].astype(l_ref.dtype)


def split_w1(w1):
    e, d, de2 = w1.shape
    de = de2 // 2
    tk = SPLIT_W1_TILE
    out = jax.ShapeDtypeStruct((e, de, d), BF16)
    return pl.pallas_call(
        _split_w1_kernel,
        grid=(e, d // tk),
        in_specs=[pl.BlockSpec((1, tk, de2), lambda i, k: (i, k, 0))],
        out_specs=[pl.BlockSpec((1, de, tk), lambda i, k: (i, 0, k))] * 2,
        out_shape=[out, out],
        scratch_shapes=[pltpu.VMEM((tk // LANES, de2, LANES), F32)],
        compiler_params=_params(("parallel", "parallel")),
        name="split_w1",
    )(w1)


def _pack_bf16_pairs(v):
    w = v.shape[1] // 2
    lo = lax.bitcast_convert_type(v[:, :w].astype(BF16).astype(F32), jnp.uint32) >> 16
    hi = lax.bitcast_convert_type(v[:, w:].astype(BF16).astype(F32), jnp.uint32) & jnp.uint32(0xFFFF0000)
    return lo | hi


def _unpack_bf16_pairs(p):
    lo = lax.bitcast_convert_type(p << 16, F32)
    hi = lax.bitcast_convert_type(p & jnp.uint32(0xFFFF0000), F32)
    return lo, hi


def _store_slabs(ref, row0, rows, packed):
    slab = packed.shape[1] // LANES
    for s in range(slab):
        ref[pl.ds(row0 * slab + s, rows, stride=slab), :] = packed[:, s * LANES:(s + 1) * LANES]


def _load_slabs(ref, rows, slab):
    return jnp.concatenate([ref[pl.ds(s, rows, stride=slab), :] for s in range(slab)], axis=1)


def _expert_kernel(be_ref, nu_ref, tab_ref, h2p_ref, w1g_ref, w1l_ref, w2_ref, bias_ref, y4p_ref, xbuf, obuf, gsem, ssem,
                   *, n_tokens, t_pad):
    i = pl.program_id(0)
    n_used = nu_ref[0]
    slot = lax.rem(i, 2)
    other = 1 - slot
    gslot = lax.rem(i, 3)
    gslot1 = lax.rem(i + 1, 3)
    gslot2 = lax.rem(i + 2, 3)
    bm = EXPERT_ROWS
    slab = xbuf.shape[1] // bm
    g0, g1, g2, sp, sc = (k * bm for k in range(5))

    def gather_copy(table, j, buf):
        src = pl.multiple_of(tab_ref[0, 0, table + j], slab)
        return pltpu.make_async_copy(h2p_ref.at[pl.ds(src, slab)],
                                     xbuf.at[buf, pl.ds(j * slab, slab)], gsem.at[buf])

    def scatter_copy(table, j, buf):
        dst = pl.multiple_of(tab_ref[0, 0, table + j], slab)
        return pltpu.make_async_copy(obuf.at[buf, pl.ds(j * slab, slab)],
                                     y4p_ref.at[pl.ds(dst, slab)], ssem.at[buf])

    def wait_block(kind, buf):
        if kind == "gather":
            pltpu.make_async_copy(h2p_ref.at[pl.ds(0, bm * slab)], xbuf.at[buf], gsem.at[buf]).wait()
        else:
            pltpu.make_async_copy(obuf.at[buf], y4p_ref.at[pl.ds(0, bm * slab)], ssem.at[buf]).wait()

    @pl.when(i == 0)
    def _():
        obuf[...] = jnp.zeros(obuf.shape, obuf.dtype)
        fills = [pltpu.make_async_copy(obuf.at[1], y4p_ref.at[pl.ds((k * t_pad + n_tokens) * slab + c * bm * slab,
                                                                     bm * slab)], ssem.at[1])
                 for k in range(TOP_K) for c in range((t_pad - n_tokens) // bm)]
        for fill in fills:
            fill.start()
        for fill in fills:
            fill.wait()
        for j in range(bm):
            gather_copy(g0, j, 0).start()
        for j in range(bm):
            gather_copy(g1, j, 1).start()
        spare_row = t_pad + t_pad - 2 * bm
        pltpu.make_async_copy(obuf.at[0], y4p_ref.at[pl.ds(spare_row * slab, bm * slab)], ssem.at[0]).start()

    @pl.when(i < n_used)
    def _():
        nt = (((1,), (1,)), ((), ()))
        de = w1g_ref.shape[1]
        wait_block("gather", gslot)
        lo, hi = _unpack_bf16_pairs(_load_slabs(xbuf.at[gslot], bm, slab))
        xb = jnp.concatenate([lo, hi], axis=1).astype(BF16)
        for j in range(bm):
            gather_copy(g2, j, gslot2).start(priority=j % 2)
        for j in range(bm):
            scatter_copy(sp, j, other).start(priority=j % 2)
        glu = lax.dot_general(xb, w1g_ref[0], nt, preferred_element_type=F32) + bias_ref[0, :, :de]
        lin = lax.dot_general(xb, w1l_ref[0], nt, preferred_element_type=F32) + bias_ref[0, :, de:2 * de]
        glu = jnp.minimum(glu, SWIGLU_LIMIT)
        lin = jnp.clip(lin, -SWIGLU_LIMIT, SWIGLU_LIMIT)
        act = glu * (1.0 / (1.0 + jnp.exp(-SWIGLU_ALPHA * glu))) * (lin + 1.0)
        y = jnp.dot(act.astype(BF16), w2_ref[0], preferred_element_type=F32) + bias_ref[0, :, 2 * de:]
        wait_block("scatter", slot)
        _store_slabs(obuf.at[slot], 0, bm, _pack_bf16_pairs(y))

    @pl.when(i == n_used - 1)
    def _():
        for j in range(bm):
            scatter_copy(sc, j, slot).start()
        wait_block("scatter", other)
        wait_block("scatter", slot)
        wait_block("gather", gslot1)
        wait_block("gather", gslot2)


def expert_ffn(h2p, ids, block_e, n_used, w1g, w1l, w2, b1g, b1l, b2, n_tokens):
    n_blocks, _, bm = ids.shape
    slab = h2p.shape[0] // n_tokens
    de, d = w1g.shape[1], w1g.shape[2]
    t_pad = padded_tokens(n_tokens)
    src = jnp.minimum(ids >> 2, n_tokens - 1) * slab
    dst = ((ids & 3) * t_pad + (ids >> 2)) * slab
    first = ((t_pad - 2 * bm + jnp.arange(bm, dtype=jnp.int32)) * slab).reshape(1, 1, bm)
    ahead = lambda a, k: jnp.concatenate([a[k:]] + [a[-1:]] * k, axis=0)
    table = jnp.concatenate([src, ahead(src, 1), ahead(src, 2), jnp.concatenate([first, dst[:-1]], axis=0), dst], axis=2)
    bias = jnp.concatenate([b1g, b1l, b2], axis=2)
    wmap = lambda i, be, nu: (be[i], 0, 0)
    grid_spec = pltpu.PrefetchScalarGridSpec(
        num_scalar_prefetch=2,
        grid=(n_blocks,),
        in_specs=[
            pl.BlockSpec((1, 1, 5 * bm), lambda i, be, nu: (i, 0, 0), memory_space=pltpu.SMEM),
            pl.BlockSpec(memory_space=pl.ANY),
            pl.BlockSpec((1, de, d), wmap),
            pl.BlockSpec((1, de, d), wmap),
            pl.BlockSpec((1, de, d), wmap),
            pl.BlockSpec((1, 1, 2 * de + d), wmap),
        ],
        out_specs=pl.BlockSpec(memory_space=pl.ANY),
        scratch_shapes=[
            pltpu.VMEM((3, bm * slab, LANES), jnp.uint32),
            pltpu.VMEM((2, bm * slab, LANES), jnp.uint32),
            pltpu.SemaphoreType.DMA((3,)),
            pltpu.SemaphoreType.DMA((2,)),
        ],
    )
    return pl.pallas_call(
        functools.partial(_expert_kernel, n_tokens=n_tokens, t_pad=t_pad),
        grid_spec=grid_spec,
        out_shape=jax.ShapeDtypeStruct((TOP_K * t_pad * slab, LANES), jnp.uint32),
        compiler_params=_params(("arbitrary",)),
        name="expert_ffn",
    )(block_e, n_used, table, h2p, w1g, w1l, w2, bias)


def _combine_kernel(x1_ref, y0_ref, y1_ref, y2_ref, y3_ref, tw_ref, nw_ref, o_ref, *, normalize):
    tm, d = x1_ref.shape
    slab = y0_ref.shape[0] // tm
    lo_sum = jnp.zeros((tm, d // 2), F32)
    hi_sum = jnp.zeros((tm, d // 2), F32)
    for k, y_ref in enumerate((y0_ref, y1_ref, y2_ref, y3_ref)):
        lo, hi = _unpack_bf16_pairs(_load_slabs(y_ref, tm, slab))
        w = tw_ref[:, k:k + 1]
        lo_sum = lo_sum + w * lo
        hi_sum = hi_sum + w * hi
    acc = x1_ref[...] + jnp.concatenate([lo_sum, hi_sum], axis=1)
    if normalize:
        acc = acc * lax.rsqrt(jnp.mean(acc * acc, axis=-1, keepdims=True) + EPS) * nw_ref[...]
    o_ref[...] = acc


def combine_norm(x1, y4p, top_w, norm_w, normalize):
    t, d = x1.shape
    tm = COMBINE_TILE
    t_pad = padded_tokens(t)
    slab = y4p.shape[0] // (TOP_K * t_pad)
    y_specs = [pl.BlockSpec((tm * slab, LANES), functools.partial(lambda i, k: (k * (t_pad // tm) + i, 0), k=k))
               for k in range(TOP_K)]
    return pl.pallas_call(
        functools.partial(_combine_kernel, normalize=normalize),
        grid=(t // tm,),
        in_specs=[pl.BlockSpec((tm, d), lambda i: (i, 0))] + y_specs
        + [pl.BlockSpec((tm, LANES), lambda i: (i, 0)), pl.BlockSpec((1, d), lambda i: (0, 0))],
        out_specs=pl.BlockSpec((tm, d), lambda i: (i, 0)),
        out_shape=jax.ShapeDtypeStruct((t, d), F32),
        compiler_params=_params(("parallel",)),
        name="combine_norm",
    )(x1, y4p, y4p, y4p, y4p, top_w, norm_w)


def routing_layout(top_i, n_tokens):
    n_assign = n_tokens * TOP_K
    bm = EXPERT_ROWS
    n_pad = N_EXPERTS * bm
    n_blocks = (n_assign + n_pad) // bm
    flat_e = top_i[:, :TOP_K].reshape(n_assign)
    counts = jnp.sum(flat_e[:, None] == jnp.arange(N_EXPERTS, dtype=jnp.int32)[None, :], axis=0, dtype=jnp.int32)
    padded = (counts + bm - 1) // bm * bm
    pend = jnp.cumsum(padded)
    n_used = (pend[-1] // bm).astype(jnp.int32)
    block_row = jnp.arange(n_blocks, dtype=jnp.int32) * bm
    block_e = jnp.minimum(jnp.sum(pend[None, :] <= block_row[:, None], axis=1, dtype=jnp.int32), N_EXPERTS - 1)
    spare_pos = jnp.arange(bm, dtype=jnp.int32)[None, :]
    spare_key = jnp.where(spare_pos < (padded - counts)[:, None], jnp.arange(N_EXPERTS, dtype=jnp.int32)[:, None],
                          N_EXPERTS).reshape(n_pad)
    keys = jnp.concatenate([flat_e, spare_key])
    pos_bits = (n_assign + n_pad - 1).bit_length()
    assert (N_EXPERTS + 1) << pos_bits < 2 ** 31
    packed = jnp.sort((keys << pos_bits) | jnp.arange(n_assign + n_pad, dtype=jnp.int32))
    order = packed & ((1 << pos_bits) - 1)
    return order.reshape(n_blocks, 1, bm), block_e, n_used.reshape(1)


def kernel(x, w_in, rel_bias, w_branch_attn, conv_w, conv_b, dt_bias, a_log, d_skip, ssm_norm_w,
           w_branch_ssm, gate_bias, w_out, norm_mix, norm_ffn, router_w, router_b, w1, b1, w2, b2,
           norm_final):
    batch, seq, d = x.shape
    t = batch * seq
    depth = w_in.shape[0]
    n_groups = len(ATTN_GROUPS)
    attn_w = n_groups * GROUP_WIDTH
    d_inner = N_SSM_GROUPS * SSM_HEADS_PER_GROUP * SSM_HEAD_DIM
    n_heads = N_SSM_GROUPS * SSM_HEADS_PER_GROUP
    bc_w = 2 * N_SSM_GROUPS * D_STATE
    xf = x.reshape(t, d)
    for l in range(depth):
        wl = w_in[l]
        o_z = 3 * attn_w
        o_xbc = o_z + d_inner
        o_dt = o_xbc + d_inner + bc_w
        o_gate = o_dt + n_heads
        w_qkv = wl[:, :o_z].astype(BF16)
        w_rest = jnp.concatenate([wl[:, o_z:o_xbc], wl[:, o_gate:], wl[:, o_xbc:o_dt]], axis=1).astype(BF16)
        w_dt = jnp.zeros((d, LANES), F32).at[:, :n_heads].set(wl[:, o_dt:o_gate]).astype(BF16)
        dt_bias_p = jnp.zeros((1, LANES), F32).at[0, :n_heads].set(dt_bias[l].astype(F32))
        a_p = jnp.zeros((1, LANES), F32).at[0, :n_heads].set(-jnp.exp(a_log[l].astype(F32)))
        d_skip_e = jnp.repeat(d_skip[l].astype(F32), SSM_HEAD_DIM)[None, :]

        g_mix = norm_mix[l].astype(F32)[None, :]
        rest, dt_raw, h = rms_matmul(xf, g_mix, w_rest, w_dt)
        qkv_groups = qkv_project(h, w_qkv, batch, seq)
        bias = jnp.stack([attention_bias(rel_bias, gi) for gi in range(n_groups)])
        attn = dilated_attention(qkv_groups, bias, batch, seq)
        y_ssm = ssd_mixer(rest, dt_raw, conv_w[l].astype(F32), conv_b[l].astype(F32)[None, :], dt_bias_p, a_p,
                          d_skip_e, ssm_norm_w[l].astype(F32)[None, :], batch, seq)
        rw = jnp.zeros((d, LANES), F32).at[:, :N_EXPERTS].set(router_w[l].astype(F32))
        rw_hi = rw.astype(BF16)
        router_w_p = jnp.concatenate([rw_hi, (rw - rw_hi.astype(F32)).astype(BF16)], axis=1)
        router_b_p = jnp.full((1, LANES), NEG_BIG, F32).at[0, :N_EXPERTS].set(router_b[l].astype(F32))
        x1, h2, top_i, top_w = merge_project(
            attn, y_ssm, rest, xf, w_branch_attn[l].astype(BF16), w_branch_ssm[l].astype(BF16),
            w_out[l].astype(BF16), gate_bias[l].astype(F32)[None, :], norm_ffn[l].astype(F32)[None, :],
            router_w_p, router_b_p)

        ids, block_e, n_used = routing_layout(top_i, t)
        w1g_t, w1l_t = split_w1(w1[l].astype(F32))
        y4p = expert_ffn(h2, ids, block_e, n_used, w1g_t, w1l_t, w2[l].astype(BF16),
                         b1[l][:, None, 0::2].astype(F32), b1[l][:, None, 1::2].astype(F32),
                         b2[l][:, None, :].astype(F32), t)
        xf = combine_norm(x1, y4p, top_w, norm_final.astype(F32)[None, :], normalize=(l == depth - 1))
    return xf.reshape(batch, seq, d)
```

```python
import functools
import math

import jax
import jax.numpy as jnp
import numpy as np
from jax import lax
from jax.experimental import pallas as pl
from jax.experimental.pallas import tpu as pltpu

F32 = jnp.float32
BF16 = jnp.bfloat16

EPS = 1e-5
NEG_BIG = -1e30

HEAD_DIM = 64
ATTN_GROUPS = ((128, 1), (512, 4), (2048, 16))
HEADS_PER_GROUP = 8
GROUP_WIDTH = HEADS_PER_GROUP * HEAD_DIM
ATTN_BLOCK = 128
NUM_BUCKETS = 32
MAX_DISTANCE = 2048
SSM_HEAD_DIM = 64
N_SSM_GROUPS = 4
SSM_HEADS_PER_GROUP = 8
D_STATE = 128
CONV_WIDTH = 4
CHUNK = 128
N_EXPERTS = 32
TOP_K = 4
SWIGLU_LIMIT = 7.0
SWIGLU_ALPHA = 1.702

LANES = 128
V7X_VMEM_BYTES = 64 * 1024 * 1024
VMEM_LIMIT = V7X_VMEM_BYTES * 3 // 4

ROW_TILE = 512
REST_COLUMN_TILES = 4
MERGE_SUBTILE = 256
MERGE_SUBTILES = 4
EXPERT_ROWS = 256
COMBINE_TILE = 512
SPLIT_W1_TILE = 1024
SSD_CHUNKS_PER_STEP = 4
CONV_TAIL = 16


def padded_tokens(n_tokens):
    return n_tokens + N_EXPERTS * EXPERT_ROWS // TOP_K + 2 * EXPERT_ROWS


def _params(semantics):
    return pltpu.CompilerParams(dimension_semantics=semantics, vmem_limit_bytes=VMEM_LIMIT)


def _rms_matmul_kernel(x_ref, g_ref, w_ref, ws_ref, o_ref, os_ref, hb_ref):
    x = x_ref[...]
    ms = jnp.mean(x * x, axis=-1, keepdims=True)
    h = (x * lax.rsqrt(ms + EPS) * g_ref[...]).astype(BF16)
    hb_ref[...] = h
    os_ref[...] = jnp.dot(h, ws_ref[...], preferred_element_type=F32)
    tn = w_ref.shape[1] // REST_COLUMN_TILES
    for c in range(REST_COLUMN_TILES):
        cols = slice(c * tn, (c + 1) * tn)
        o_ref[:, cols] = jnp.dot(h, w_ref[:, cols], preferred_element_type=F32).astype(o_ref.dtype)


def rms_matmul(x, g, w, w_side):
    t, d = x.shape
    n = w.shape[1]
    ns = w_side.shape[1]
    tm = ROW_TILE
    resident = lambda shape: pl.BlockSpec(shape, lambda i: (0, 0), pipeline_mode=pl.Buffered(1))
    return pl.pallas_call(
        _rms_matmul_kernel,
        grid=(t // tm,),
        in_specs=[
            pl.BlockSpec((tm, d), lambda i: (i, 0)),
            pl.BlockSpec((1, d), lambda i: (0, 0)),
            resident((d, n)),
            resident((d, ns)),
        ],
        out_specs=[
            pl.BlockSpec((tm, n), lambda i: (i, 0)),
            pl.BlockSpec((tm, ns), lambda i: (i, 0)),
            pl.BlockSpec((tm, d), lambda i: (i, 0)),
        ],
        out_shape=[jax.ShapeDtypeStruct((t, n), BF16), jax.ShapeDtypeStruct((t, ns), F32),
                   jax.ShapeDtypeStruct((t, d), BF16)],
        compiler_params=_params(("parallel",)),
        name="rms_matmul",
    )(x, g, w, w_side)


def _qkv_kernel(h_ref, w_ref, o0_ref, o1_ref, o2_ref, acc_ref, tmp_ref):
    seq = h_ref.shape[0]
    slabs = GROUP_WIDTH // LANES
    n_blocks = seq // ATTN_BLOCK
    for gi, o_ref in enumerate((o0_ref, o1_ref, o2_ref)):
        dil = ATTN_GROUPS[gi][1]
        nb = n_blocks // dil
        acc = jnp.dot(h_ref[...], w_ref[:, gi * GROUP_WIDTH:(gi + 1) * GROUP_WIDTH], preferred_element_type=F32)
        if dil == 1:
            for n in range(n_blocks):
                o_ref[0, 0, n] = acc[n * ATTN_BLOCK:(n + 1) * ATTN_BLOCK, :].astype(o_ref.dtype)
            continue
        for s in range(slabs):
            acc_ref[s] = acc[:, s * LANES:(s + 1) * LANES]
        src_ref, step, group_rows = acc_ref, dil, seq
        if dil == 16:
            for s in range(slabs):
                for a in range(4):
                    tmp_ref[s, a * (seq // 4):(a + 1) * (seq // 4), :] = acc_ref[s, pl.ds(a, seq // 4, stride=4), :]
            src_ref, step, group_rows = tmp_ref, 4, seq // 4
        for r in range(dil):
            base = (r % (dil // step)) * group_rows + r // (dil // step)
            for n in range(nb):
                for s in range(slabs):
                    rows = pl.ds(base + n * ATTN_BLOCK * step, ATTN_BLOCK, stride=step)
                    o_ref[0, 0, r * nb + n, :, s * LANES:(s + 1) * LANES] = src_ref[s, rows, :].astype(o_ref.dtype)


def qkv_project(h, w_qkv, batch, seq):
    t, d = h.shape
    n_groups = len(ATTN_GROUPS)
    tn = n_groups * GROUP_WIDTH
    out_shapes, out_specs = [], []
    for _ in ATTN_GROUPS:
        shape = (3, batch, seq // ATTN_BLOCK, ATTN_BLOCK, GROUP_WIDTH)
        out_shapes.append(jax.ShapeDtypeStruct(shape, BF16))
        out_specs.append(pl.BlockSpec((1, 1) + shape[2:], lambda w, b: (w, b, 0, 0, 0)))
    return pl.pallas_call(
        _qkv_kernel,
        grid=(3, batch),
        in_specs=[pl.BlockSpec((seq, d), lambda w, b: (b, 0)), pl.BlockSpec((d, tn), lambda w, b: (0, w))],
        out_specs=out_specs,
        out_shape=out_shapes,
        scratch_shapes=[pltpu.VMEM((GROUP_WIDTH // LANES, seq, LANES), F32)] * 2,
        compiler_params=_params(("parallel", "parallel")),
        name="qkv_project",
    )(h, w_qkv)


def _attn_kernel(q0_ref, q1_ref, q2_ref, bias_ref, o_ref, out_ref, lse_ref):
    heads = LANES // HEAD_DIM
    scale = jnp.asarray(HEAD_DIM ** -0.5, BF16)
    n_blocks = q0_ref.shape[2]
    qk = (((2,), (2,)), ((0,), (0,)))
    pv = (((2,), (1,)), ((0,), (0,)))
    blk = lax.broadcasted_iota(jnp.int32, (n_blocks, 1, 1), 0)
    lane = lax.broadcasted_iota(jnp.int32, (1, 1, LANES), 2)

    def shifted(x):
        return jnp.concatenate([x[n_blocks - 1:], x[:n_blocks - 1]], axis=0)

    for g, ref in enumerate((q0_ref, q1_ref, q2_ref)):
        dil = ATTN_GROUPS[g][1]
        nb = n_blocks // dil
        q2h = ref[0, 0] * scale
        keys, vals = ref[1, 0], ref[2, 0]
        if nb > 1:
            keys = jnp.concatenate([shifted(keys), keys], axis=1)
            vals = jnp.concatenate([shifted(vals), vals], axis=1)
        n_keys = keys.shape[1]
        vals = jnp.concatenate([vals, jnp.ones(vals.shape, BF16)], axis=2)
        pvs, dens, maxs = [], [], []
        for h in range(heads):
            in_head = (lane >= h * HEAD_DIM) & (lane < (h + 1) * HEAD_DIM)
            q = jnp.where(in_head, q2h, jnp.zeros_like(q2h))
            s = lax.dot_general(q, keys, qk, preferred_element_type=F32)
            s = s + bias_ref[g, h, :, 2 * ATTN_BLOCK - n_keys:][None]
            if nb > 1:
                key_is_prev = lax.broadcasted_iota(jnp.int32, (1, 1, n_keys), 2) < ATTN_BLOCK
                s = jnp.where((blk % nb == 0) & key_is_prev, NEG_BIG, s)
            m = jnp.max(s, axis=-1, keepdims=True)
            p = jnp.exp(s - m).astype(BF16)
            acc = lax.dot_general(p, vals, pv, preferred_element_type=F32)
            pvs.append(acc[:, :, :LANES])
            dens.append(acc[:, :, LANES:])
            maxs.append(m)
        first = lane < HEAD_DIM
        den = jnp.where(first, dens[0], dens[1])
        o2 = jnp.where(first, pvs[0], pvs[1]) / den
        l2 = jnp.where(first, maxs[0], maxs[1]) + jnp.log(den)
        for r in range(dil):
            for n in range(nb):
                start = r + n * ATTN_BLOCK * dil
                rows = pl.ds(start, ATTN_BLOCK) if dil == 1 else pl.ds(start, ATTN_BLOCK, stride=dil)
                out_ref[g, rows, :] = o2[r * nb + n]
                lse_ref[g, rows, :] = l2[r * nb + n]

    l0, l1, l2 = lse_ref[0], lse_ref[1], lse_ref[2]
    lm = jnp.maximum(jnp.maximum(l0, l1), l2)
    e0, e1, e2 = jnp.exp(l0 - lm), jnp.exp(l1 - lm), jnp.exp(l2 - lm)
    mixed = (e0 * out_ref[0] + e1 * out_ref[1] + e2 * out_ref[2]) / (e0 + e1 + e2)
    o_ref[...] = mixed.astype(o_ref.dtype)


def dilated_attention(qkv_groups, bias, batch, seq):
    heads = LANES // HEAD_DIM
    in_specs = [pl.BlockSpec((3, 1) + a.shape[2:4] + (LANES,), lambda b, hp: (0, b, 0, 0, hp)) for a in qkv_groups]
    in_specs.append(pl.BlockSpec((len(ATTN_GROUPS), heads, ATTN_BLOCK, 2 * ATTN_BLOCK), lambda b, hp: (0, hp, 0, 0)))
    return pl.pallas_call(
        _attn_kernel,
        grid=(batch, GROUP_WIDTH // LANES),
        in_specs=in_specs,
        out_specs=pl.BlockSpec((seq, LANES), lambda b, hp: (b, hp)),
        out_shape=jax.ShapeDtypeStruct((batch * seq, GROUP_WIDTH), BF16),
        scratch_shapes=[pltpu.VMEM((len(ATTN_GROUPS), seq, LANES), F32),
                        pltpu.VMEM((len(ATTN_GROUPS), seq, LANES), F32)],
        compiler_params=_params(("parallel", "parallel")),
        name="dilated_attn",
    )(*qkv_groups, bias)


def attention_bias(rel_bias, gi):
    window, dil = ATTN_GROUPS[gi]
    w_sub = window // dil
    q_idx = np.arange(ATTN_BLOCK)[:, None]
    k_idx = np.arange(2 * ATTN_BLOCK)[None, :]
    delta = q_idx + ATTN_BLOCK - k_idx
    in_band = (delta >= 0) & (delta <= w_sub)
    dist = np.clip(delta, 0, w_sub) * dil
    max_exact = NUM_BUCKETS // 2
    nf = np.maximum(dist, max_exact).astype(np.float32)
    large = max_exact + (np.log(nf / max_exact) / math.log(MAX_DISTANCE / max_exact)
                         * (NUM_BUCKETS - max_exact)).astype(np.int32)
    large = np.minimum(large, NUM_BUCKETS - 1)
    bucket = np.where(dist < max_exact, dist, large)
    table = rel_bias[:, gi * HEADS_PER_GROUP:(gi + 1) * HEADS_PER_GROUP].astype(F32)
    onehot = (bucket.reshape(-1, 1) == np.arange(NUM_BUCKETS)[None, :]).astype(np.float32)
    bias = jnp.einsum('bh,nb->hn', table, jnp.asarray(onehot), precision=lax.Precision.HIGHEST)
    bias = bias.reshape(HEADS_PER_GROUP, ATTN_BLOCK, 2 * ATTN_BLOCK)
    return jnp.where(in_band[None], bias, NEG_BIG)


def _silu(v):
    half = 0.5 * v
    return half + half * jnp.tanh(half)


def _conv_silu(ext_ref, u_ref, w_ref, b_ref, first):
    rows = u_ref.shape[0]
    ext_rows = ext_ref.shape[0]

    if first is not None:
        @pl.when(first)
        def _():
            ext_ref[0:CONV_TAIL, :] = jnp.zeros((CONV_TAIL, ext_ref.shape[1]), ext_ref.dtype)

    u = u_ref[...]
    ext_ref[CONV_TAIL:, :] = u
    taps = CONV_WIDTH - 1
    out_row = lax.broadcasted_iota(jnp.int32, (rows, ext_rows), 0)
    src_row = lax.broadcasted_iota(jnp.int32, (rows, ext_rows), 1)
    shift_mat = jnp.concatenate([(src_row == out_row + (CONV_TAIL - k)).astype(BF16) for k in range(1, CONV_WIDTH)],
                                axis=0)
    shifted = jnp.dot(shift_mat, ext_ref[...], preferred_element_type=F32)
    acc = u.astype(F32) * w_ref[taps:CONV_WIDTH, :] + b_ref[...]
    for k in range(1, CONV_WIDTH):
        acc = acc + shifted[(k - 1) * rows:k * rows, :] * w_ref[taps - k:CONV_WIDTH - k, :]
    ext_ref[0:CONV_TAIL, :] = ext_ref[rows:, :]
    return _silu(acc)


def _ssd_kernel(x_ref, bc_ref, z_ref, dt_ref, cwx_ref, cbx_ref, cwbc_ref, cbbc_ref, dtb_ref, a_ref,
                dskip_ref, nw_ref, expand_ref, o_ref, extx_ref, extbc_ref, state_ref):
    first = pl.program_id(1) == 0
    gw = SSM_HEADS_PER_GROUP * SSM_HEAD_DIM

    @pl.when(first)
    def _():
        state_ref[...] = jnp.zeros(state_ref.shape, F32)

    row = lax.broadcasted_iota(jnp.int32, (CHUNK, CHUNK), 0)
    colm = lax.broadcasted_iota(jnp.int32, (CHUNK, CHUNK), 1)
    tril = row >= colm
    tril_b = tril.astype(BF16)
    head_rows = N_SSM_GROUPS * SSM_HEADS_PER_GROUP
    eye_b = (lax.broadcasted_iota(jnp.int32, (head_rows, CHUNK), 0)
             == lax.broadcasted_iota(jnp.int32, (head_rows, CHUNK), 1)).astype(BF16)
    nt = (((1,), (1,)), ((), ()))
    tn = (((0,), (0,)), ((), ()))

    def split3(v):
        p0 = v.astype(BF16)
        r1 = v - p0.astype(F32)
        p1 = r1.astype(BF16)
        p2 = (r1 - p1.astype(F32)).astype(BF16)
        return p0, p1, p2

    def select_rows(mat01, v):
        return sum(jnp.dot(mat01, p, preferred_element_type=F32) for p in split3(v))

    def select_cols(v, mat01, parts):
        return sum(jnp.dot(p, mat01, preferred_element_type=F32) for p in split3(v)[:parts])

    for sub, g in [(sub, g) for sub in range(x_ref.shape[0] // CHUNK) for g in range(N_SSM_GROUPS)]:
        rws = pl.ds(sub * CHUNK, CHUNK)
        if g == 0:
            sub_first = first if sub == 0 else None
            xs_all = _conv_silu(extx_ref, x_ref.at[rws], cwx_ref, cbx_ref, sub_first)
            bc_all = _conv_silu(extbc_ref, bc_ref.at[rws], cwbc_ref, cbbc_ref, sub_first)
            v = dt_ref[rws, :] + dtb_ref[...]
            dt = jnp.maximum(v, 0.0) + jnp.log1p(jnp.exp(-jnp.abs(v)))
            a_d = dt * a_ref[...]
            acs = select_rows(tril_b, a_d)
            acs_t = sum(lax.dot_general(eye_b, p, nt, preferred_element_type=F32) for p in split3(acs))
            e_acs = jnp.exp(acs)
            e_dec = jnp.exp(acs[CHUNK - 1:CHUNK, :] - acs)
        lanes = slice(g * LANES, (g + 1) * LANES)
        ch = slice(g * gw, (g + 1) * gw)
        expand = expand_ref[g]
        dt_e = select_cols(dt, expand, 1)
        eacs_e = select_cols(e_acs, expand, 2)
        edec_e = select_cols(e_dec, expand, 1)
        elast_e = eacs_e[CHUNK - 1:CHUNK, :]

        xs = xs_all[:, ch]
        bm = bc_all[:, lanes].astype(BF16)
        cm = bc_all[:, N_SSM_GROUPS * D_STATE + g * D_STATE:N_SSM_GROUPS * D_STATE + (g + 1) * D_STATE].astype(BF16)
        x_d = xs * dt_e
        cb = lax.dot_general(cm, bm, nt, preferred_element_type=F32)

        prev = state_ref[g]
        y = jnp.dot(cm, prev.astype(BF16), preferred_element_type=F32) * eacs_e
        x_d16 = x_d.astype(BF16)
        parts = []
        for j in range(SSM_HEADS_PER_GROUP):
            head = g * SSM_HEADS_PER_GROUP + j
            seg = acs[:, head:head + 1] - acs_t[head:head + 1, :]
            m_h = (cb * jnp.exp(jnp.where(tril, seg, NEG_BIG))).astype(BF16)
            parts.append(jnp.dot(m_h, x_d16[:, j * SSM_HEAD_DIM:(j + 1) * SSM_HEAD_DIM],
                                 preferred_element_type=F32))
        y = y + jnp.concatenate(parts, axis=1)

        xw = (x_d * edec_e).astype(BF16)
        state_ref[g] = prev * elast_e + lax.dot_general(bm, xw, tn, preferred_element_type=F32)

        y = y + dskip_ref[:, ch] * xs
        y = y * _silu(z_ref[rws, ch].astype(F32))
        y = y * lax.rsqrt(jnp.mean(y * y, axis=-1, keepdims=True) + EPS)
        o_ref[rws, ch] = (y * nw_ref[:, ch]).astype(o_ref.dtype)


def ssd_mixer(rest, dt_raw, conv_w, conv_b, dt_bias_p, a_p, d_skip_e, norm_w, batch, seq):
    t = rest.shape[0]
    d_inner = N_SSM_GROUPS * SSM_HEADS_PER_GROUP * SSM_HEAD_DIM
    bc_w = 2 * N_SSM_GROUPS * D_STATE
    rows = CHUNK * SSD_CHUNKS_PER_STEP
    nc = seq // rows
    gw = SSM_HEADS_PER_GROUP * SSM_HEAD_DIM
    head_of_channel = np.arange(N_SSM_GROUPS)[:, None, None] * SSM_HEADS_PER_GROUP + np.arange(gw)[None, None, :] // SSM_HEAD_DIM
    expand = (np.arange(LANES)[None, :, None] == head_of_channel).astype(np.float32)
    rowmap = lambda b, c: (b * nc + c, 0)
    const = lambda b, c: (0, 0)
    return pl.pallas_call(
        _ssd_kernel,
        grid=(batch, nc),
        in_specs=[
            pl.BlockSpec((rows, d_inner), lambda b, c: (b * nc + c, 2)),
            pl.BlockSpec((rows, bc_w), lambda b, c: (b * nc + c, 6)),
            pl.BlockSpec((rows, d_inner), rowmap),
            pl.BlockSpec((rows, LANES), rowmap),
            pl.BlockSpec((CONV_WIDTH, d_inner), const),
            pl.BlockSpec((1, d_inner), const),
            pl.BlockSpec((CONV_WIDTH, bc_w), const),
            pl.BlockSpec((1, bc_w), const),
            pl.BlockSpec((1, LANES), const),
            pl.BlockSpec((1, LANES), const),
            pl.BlockSpec((1, d_inner), const),
            pl.BlockSpec((1, d_inner), const),
            pl.BlockSpec((N_SSM_GROUPS, LANES, gw), lambda b, c: (0, 0, 0)),
        ],
        out_specs=pl.BlockSpec((rows, d_inner), rowmap),
        out_shape=jax.ShapeDtypeStruct((t, d_inner), BF16),
        scratch_shapes=[
            pltpu.VMEM((CONV_TAIL + CHUNK, d_inner), BF16),
            pltpu.VMEM((CONV_TAIL + CHUNK, bc_w), BF16),
            pltpu.VMEM((N_SSM_GROUPS, D_STATE, gw), F32),
        ],
        compiler_params=_params(("parallel", "arbitrary")),
        name="ssd_mixer",
    )(rest, rest, rest, dt_raw, conv_w[:, :d_inner], conv_b[:, :d_inner], conv_w[:, d_inner:],
      conv_b[:, d_inner:], dt_bias_p, a_p, d_skip_e, norm_w, jnp.asarray(expand, dtype=BF16))


def _merge_kernel(at_ref, ys_ref, gl_ref, x_ref, wa_ref, ws_ref,
                  wo_ref, gb_ref, nf_ref, rw_ref, rb_ref, x1_ref, h2_ref, ti_ref, tw_ref):
    d = x_ref.shape[1]
    sub = x_ref.shape[0] // MERGE_SUBTILES
    for part in range(MERGE_SUBTILES):
        rows = slice(part * sub, (part + 1) * sub)
        y_attn = jnp.dot(at_ref[rows, :], wa_ref[...], preferred_element_type=F32)
        y_ssm = jnp.dot(ys_ref[rows, :], ws_ref[...], preferred_element_type=F32)
        gv = gl_ref[rows, :].astype(F32) + gb_ref[...]
        gates = 1.0 / (1.0 + jnp.exp(-gv))
        merged = gates[:, :d] * y_attn + gates[:, d:] * y_ssm
        x1 = x_ref[rows, :] + jnp.dot(merged.astype(BF16), wo_ref[...], preferred_element_type=F32)
        x1_ref[rows, :] = x1
        h2 = x1 * lax.rsqrt(jnp.mean(x1 * x1, axis=-1, keepdims=True) + EPS) * nf_ref[...]
        _store_slabs(h2_ref, part * sub, sub, _pack_bf16_pairs(h2))
        h_hi = h2.astype(BF16)
        h_mid = (h2 - h_hi.astype(F32)).astype(BF16)
        both = jnp.dot(h_hi, rw_ref[...], preferred_element_type=F32)
        logits = (both[:, :LANES] + both[:, LANES:]
                  + jnp.dot(h_mid, rw_ref[:, :LANES], preferred_element_type=F32) + rb_ref[...])
        lane = lax.broadcasted_iota(jnp.int32, logits.shape, 1)
        top_i = jnp.zeros(logits.shape, jnp.int32)
        top_v = jnp.full(logits.shape, NEG_BIG, F32)
        work = logits
        for k in range(TOP_K):
            m = jnp.max(work, axis=-1, keepdims=True)
            idx = jnp.min(jnp.where(work == m, lane, LANES), axis=-1, keepdims=True)
            top_i = jnp.where(lane == k, idx, top_i)
            top_v = jnp.where(lane == k, m, top_v)
            work = jnp.where(lane == idx, NEG_BIG * 2.0, work)
        ev = jnp.exp(top_v - jnp.max(top_v, axis=-1, keepdims=True))
        ti_ref[rows, :] = top_i
        tw_ref[rows, :] = ev / jnp.sum(ev, axis=-1, keepdims=True)


def merge_project(attn, y_ssm, rest, x, wa, ws, wo, gate_bias, norm_ffn, router_w_p, router_b_p):
    t, d = x.shape
    slab = d // 2 // LANES
    tm = MERGE_SUBTILE * MERGE_SUBTILES
    d_inner = y_ssm.shape[1]
    rowmap = lambda i: (i, 0)
    const = lambda i: (0, 0)
    full = lambda a: pl.BlockSpec(a.shape, const, pipeline_mode=pl.Buffered(1))
    args = [attn, y_ssm, rest, x, wa, ws, wo, gate_bias, norm_ffn, router_w_p, router_b_p]
    in_specs = (
        [pl.BlockSpec((tm, GROUP_WIDTH), rowmap),
         pl.BlockSpec((tm, d_inner), rowmap),
         pl.BlockSpec((tm, 2 * d), lambda i: (i, 1)),
         pl.BlockSpec((tm, d), rowmap)]
        + [full(a) for a in args[4:]]
    )
    return pl.pallas_call(
        _merge_kernel,
        grid=(t // tm,),
        in_specs=in_specs,
        out_specs=[pl.BlockSpec((tm, d), rowmap), pl.BlockSpec((tm * slab, LANES), rowmap),
                   pl.BlockSpec((tm, LANES), rowmap), pl.BlockSpec((tm, LANES), rowmap)],
        out_shape=[jax.ShapeDtypeStruct((t, d), F32), jax.ShapeDtypeStruct((t * slab, LANES), jnp.uint32),
                   jax.ShapeDtypeStruct((t, LANES), jnp.int32), jax.ShapeDtypeStruct((t, LANES), F32)],
        compiler_params=_params(("parallel",)),
        name="merge_project",
    )(*args)


def _split_w1_kernel(w_ref, g_ref, l_ref, t_ref):
    de = g_ref.shape[1]
    for s in range(t_ref.shape[0]):
        cols = slice(s * LANES, (s + 1) * LANES)
        t_ref[s] = w_ref[0, cols, :].T
        g_ref[0, :, cols] = t_ref[s, pl.ds(0, de, stride=2), :].astype(g_ref.dtype)
        l_ref[0, :, cols] = t_ref[s, pl.ds(1, de, stride=2), :].astype(l_ref.dtype)


def split_w1(w1):
    e, d, de2 = w1.shape
    de = de2 // 2
    tk = SPLIT_W1_TILE
    out = jax.ShapeDtypeStruct((e, de, d), BF16)
    return pl.pallas_call(
        _split_w1_kernel,
        grid=(e, d // tk),
        in_specs=[pl.BlockSpec((1, tk, de2), lambda i, k: (i, k, 0))],
        out_specs=[pl.BlockSpec((1, de, tk), lambda i, k: (i, 0, k))] * 2,
        out_shape=[out, out],
        scratch_shapes=[pltpu.VMEM((tk // LANES, de2, LANES), F32)],
        compiler_params=_params(("parallel", "parallel")),
        name="split_w1",
    )(w1)


def _pack_bf16_pairs(v):
    w = v.shape[1] // 2
    lo = lax.bitcast_convert_type(v[:, :w].astype(BF16).astype(F32), jnp.uint32) >> 16
    hi = lax.bitcast_convert_type(v[:, w:].astype(BF16).astype(F32), jnp.uint32) & jnp.uint32(0xFFFF0000)
    return lo | hi


def _unpack_bf16_pairs(p):
    lo = lax.bitcast_convert_type(p << 16, F32)
    hi = lax.bitcast_convert_type(p & jnp.uint32(0xFFFF0000), F32)
    return lo, hi


def _store_slabs(ref, row0, rows, packed):
    slab = packed.shape[1] // LANES
    for s in range(slab):
        ref[pl.ds(row0 * slab + s, rows, stride=slab), :] = packed[:, s * LANES:(s + 1) * LANES]


def _load_slabs(ref, rows, slab):
    return jnp.concatenate([ref[pl.ds(s, rows, stride=slab), :] for s in range(slab)], axis=1)


def _expert_kernel(be_ref, nu_ref, tab_ref, h2p_ref, w1g_ref, w1l_ref, w2_ref, bias_ref, y4p_ref, xbuf, obuf, gsem, ssem,
                   *, n_tokens, t_pad):
    i = pl.program_id(0)
    n_used = nu_ref[0]
    slot = lax.rem(i, 2)
    other = 1 - slot
    gslot = lax.rem(i, 3)
    gslot1 = lax.rem(i + 1, 3)
    gslot2 = lax.rem(i + 2, 3)
    bm = EXPERT_ROWS
    slab = xbuf.shape[1] // bm
    g0, g1, g2, sp, sc = (k * bm for k in range(5))

    def gather_copy(table, j, buf):
        src = pl.multiple_of(tab_ref[0, 0, table + j], slab)
        return pltpu.make_async_copy(h2p_ref.at[pl.ds(src, slab)],
                                     xbuf.at[buf, pl.ds(j * slab, slab)], gsem.at[buf])

    def scatter_copy(table, j, buf):
        dst = pl.multiple_of(tab_ref[0, 0, table + j], slab)
        return pltpu.make_async_copy(obuf.at[buf, pl.ds(j * slab, slab)],
                                     y4p_ref.at[pl.ds(dst, slab)], ssem.at[buf])

    def wait_block(kind, buf):
        if kind == "gather":
            pltpu.make_async_copy(h2p_ref.at[pl.ds(0, bm * slab)], xbuf.at[buf], gsem.at[buf]).wait()
        else:
            pltpu.make_async_copy(obuf.at[buf], y4p_ref.at[pl.ds(0, bm * slab)], ssem.at[buf]).wait()

    @pl.when(i == 0)
    def _():
        obuf[...] = jnp.zeros(obuf.shape, obuf.dtype)
        fills = [pltpu.make_async_copy(obuf.at[1], y4p_ref.at[pl.ds((k * t_pad + n_tokens) * slab + c * bm * slab,
                                                                     bm * slab)], ssem.at[1])
                 for k in range(TOP_K) for c in range((t_pad - n_tokens) // bm)]
        for fill in fills:
            fill.start()
        for fill in fills:
            fill.wait()
        for j in range(bm):
            gather_copy(g0, j, 0).start()
        for j in range(bm):
            gather_copy(g1, j, 1).start()
        spare_row = t_pad + t_pad - 2 * bm
        pltpu.make_async_copy(obuf.at[0], y4p_ref.at[pl.ds(spare_row * slab, bm * slab)], ssem.at[0]).start()

    @pl.when(i < n_used)
    def _():
        nt = (((1,), (1,)), ((), ()))
        de = w1g_ref.shape[1]
        wait_block("gather", gslot)
        lo, hi = _unpack_bf16_pairs(_load_slabs(xbuf.at[gslot], bm, slab))
        xb = jnp.concatenate([lo, hi], axis=1).astype(BF16)
        for j in range(bm):
            gather_copy(g2, j, gslot2).start(priority=j % 2)
        for j in range(bm):
            scatter_copy(sp, j, other).start(priority=j % 2)
        glu = lax.dot_general(xb, w1g_ref[0], nt, preferred_element_type=F32) + bias_ref[0, :, :de]
        lin = lax.dot_general(xb, w1l_ref[0], nt, preferred_element_type=F32) + bias_ref[0, :, de:2 * de]
        glu = jnp.minimum(glu, SWIGLU_LIMIT)
        lin = jnp.clip(lin, -SWIGLU_LIMIT, SWIGLU_LIMIT)
        act = glu * (1.0 / (1.0 + jnp.exp(-SWIGLU_ALPHA * glu))) * (lin + 1.0)
        y = jnp.dot(act.astype(BF16), w2_ref[0], preferred_element_type=F32) + bias_ref[0, :, 2 * de:]
        wait_block("scatter", slot)
        _store_slabs(obuf.at[slot], 0, bm, _pack_bf16_pairs(y))

    @pl.when(i == n_used - 1)
    def _():
        for j in range(bm):
            scatter_copy(sc, j, slot).start()
        wait_block("scatter", other)
        wait_block("scatter", slot)
        wait_block("gather", gslot1)
        wait_block("gather", gslot2)


def expert_ffn(h2p, ids, block_e, n_used, w1g, w1l, w2, b1g, b1l, b2, n_tokens):
    n_blocks, _, bm = ids.shape
    slab = h2p.shape[0] // n_tokens
    de, d = w1g.shape[1], w1g.shape[2]
    t_pad = padded_tokens(n_tokens)
    src = jnp.minimum(ids >> 2, n_tokens - 1) * slab
    dst = ((ids & 3) * t_pad + (ids >> 2)) * slab
    first = ((t_pad - 2 * bm + jnp.arange(bm, dtype=jnp.int32)) * slab).reshape(1, 1, bm)
    ahead = lambda a, k: jnp.concatenate([a[k:]] + [a[-1:]] * k, axis=0)
    table = jnp.concatenate([src, ahead(src, 1), ahead(src, 2), jnp.concatenate([first, dst[:-1]], axis=0), dst], axis=2)
    bias = jnp.concatenate([b1g, b1l, b2], axis=2)
    wmap = lambda i, be, nu: (be[i], 0, 0)
    grid_spec = pltpu.PrefetchScalarGridSpec(
        num_scalar_prefetch=2,
        grid=(n_blocks,),
        in_specs=[
            pl.BlockSpec((1, 1, 5 * bm), lambda i, be, nu: (i, 0, 0), memory_space=pltpu.SMEM),
            pl.BlockSpec(memory_space=pl.ANY),
            pl.BlockSpec((1, de, d), wmap),
            pl.BlockSpec((1, de, d), wmap),
            pl.BlockSpec((1, de, d), wmap),
            pl.BlockSpec((1, 1, 2 * de + d), wmap),
        ],
        out_specs=pl.BlockSpec(memory_space=pl.ANY),
        scratch_shapes=[
            pltpu.VMEM((3, bm * slab, LANES), jnp.uint32),
            pltpu.VMEM((2, bm * slab, LANES), jnp.uint32),
            pltpu.SemaphoreType.DMA((3,)),
            pltpu.SemaphoreType.DMA((2,)),
        ],
    )
    return pl.pallas_call(
        functools.partial(_expert_kernel, n_tokens=n_tokens, t_pad=t_pad),
        grid_spec=grid_spec,
        out_shape=jax.ShapeDtypeStruct((TOP_K * t_pad * slab, LANES), jnp.uint32),
        compiler_params=_params(("arbitrary",)),
        name="expert_ffn",
    )(block_e, n_used, table, h2p, w1g, w1l, w2, bias)


def _combine_kernel(x1_ref, y0_ref, y1_ref, y2_ref, y3_ref, tw_ref, nw_ref, o_ref, *, normalize):
    tm, d = x1_ref.shape
    slab = y0_ref.shape[0] // tm
    lo_sum = jnp.zeros((tm, d // 2), F32)
    hi_sum = jnp.zeros((tm, d // 2), F32)
    for k, y_ref in enumerate((y0_ref, y1_ref, y2_ref, y3_ref)):
        lo, hi = _unpack_bf16_pairs(_load_slabs(y_ref, tm, slab))
        w = tw_ref[:, k:k + 1]
        lo_sum = lo_sum + w * lo
        hi_sum = hi_sum + w * hi
    acc = x1_ref[...] + jnp.concatenate([lo_sum, hi_sum], axis=1)
    if normalize:
        acc = acc * lax.rsqrt(jnp.mean(acc * acc, axis=-1, keepdims=True) + EPS) * nw_ref[...]
    o_ref[...] = acc


def combine_norm(x1, y4p, top_w, norm_w, normalize):
    t, d = x1.shape
    tm = COMBINE_TILE
    t_pad = padded_tokens(t)
    slab = y4p.shape[0] // (TOP_K * t_pad)
    y_specs = [pl.BlockSpec((tm * slab, LANES), functools.partial(lambda i, k: (k * (t_pad // tm) + i, 0), k=k))
               for k in range(TOP_K)]
    return pl.pallas_call(
        functools.partial(_combine_kernel, normalize=normalize),
        grid=(t // tm,),
        in_specs=[pl.BlockSpec((tm, d), lambda i: (i, 0))] + y_specs
        + [pl.BlockSpec((tm, LANES), lambda i: (i, 0)), pl.BlockSpec((1, d), lambda i: (0, 0))],
        out_specs=pl.BlockSpec((tm, d), lambda i: (i, 0)),
        out_shape=jax.ShapeDtypeStruct((t, d), F32),
        compiler_params=_params(("parallel",)),
        name="combine_norm",
    )(x1, y4p, y4p, y4p, y4p, top_w, norm_w)


def routing_layout(top_i, n_tokens):
    n_assign = n_tokens * TOP_K
    bm = EXPERT_ROWS
    n_pad = N_EXPERTS * bm
    n_blocks = (n_assign + n_pad) // bm
    flat_e = top_i[:, :TOP_K].reshape(n_assign)
    counts = jnp.sum(flat_e[:, None] == jnp.arange(N_EXPERTS, dtype=jnp.int32)[None, :], axis=0, dtype=jnp.int32)
    padded = (counts + bm - 1) // bm * bm
    pend = jnp.cumsum(padded)
    n_used = (pend[-1] // bm).astype(jnp.int32)
    block_row = jnp.arange(n_blocks, dtype=jnp.int32) * bm
    block_e = jnp.minimum(jnp.sum(pend[None, :] <= block_row[:, None], axis=1, dtype=jnp.int32), N_EXPERTS - 1)
    spare_pos = jnp.arange(bm, dtype=jnp.int32)[None, :]
    spare_key = jnp.where(spare_pos < (padded - counts)[:, None], jnp.arange(N_EXPERTS, dtype=jnp.int32)[:, None],
                          N_EXPERTS).reshape(n_pad)
    keys = jnp.concatenate([flat_e, spare_key])
    pos_bits = (n_assign + n_pad - 1).bit_length()
    assert (N_EXPERTS + 1) << pos_bits < 2 ** 31
    packed = jnp.sort((keys << pos_bits) | jnp.arange(n_assign + n_pad, dtype=jnp.int32))
    order = packed & ((1 << pos_bits) - 1)
    return order.reshape(n_blocks, 1, bm), block_e, n_used.reshape(1)


def kernel(x, w_in, rel_bias, w_branch_attn, conv_w, conv_b, dt_bias, a_log, d_skip, ssm_norm_w,
           w_branch_ssm, gate_bias, w_out, norm_mix, norm_ffn, router_w, router_b, w1, b1, w2, b2,
           norm_final):
    batch, seq, d = x.shape
    t = batch * seq
    depth = w_in.shape[0]
    n_groups = len(ATTN_GROUPS)
    attn_w = n_groups * GROUP_WIDTH
    d_inner = N_SSM_GROUPS * SSM_HEADS_PER_GROUP * SSM_HEAD_DIM
    n_heads = N_SSM_GROUPS * SSM_HEADS_PER_GROUP
    bc_w = 2 * N_SSM_GROUPS * D_STATE
    xf = x.reshape(t, d)
    for l in range(depth):
        wl = w_in[l]
        o_z = 3 * attn_w
        o_xbc = o_z + d_inner
        o_dt = o_xbc + d_inner + bc_w
        o_gate = o_dt + n_heads
        w_qkv = wl[:, :o_z].astype(BF16)
        w_rest = jnp.concatenate([wl[:, o_z:o_xbc], wl[:, o_gate:], wl[:, o_xbc:o_dt]], axis=1).astype(BF16)
        w_dt = jnp.zeros((d, LANES), F32).at[:, :n_heads].set(wl[:, o_dt:o_gate]).astype(BF16)
        dt_bias_p = jnp.zeros((1, LANES), F32).at[0, :n_heads].set(dt_bias[l].astype(F32))
        a_p = jnp.zeros((1, LANES), F32).at[0, :n_heads].set(-jnp.exp(a_log[l].astype(F32)))
        d_skip_e = jnp.repeat(d_skip[l].astype(F32), SSM_HEAD_DIM)[None, :]

        g_mix = norm_mix[l].astype(F32)[None, :]
        rest, dt_raw, h = rms_matmul(xf, g_mix, w_rest, w_dt)
        qkv_groups = qkv_project(h, w_qkv, batch, seq)
        bias = jnp.stack([attention_bias(rel_bias, gi) for gi in range(n_groups)])
        attn = dilated_attention(qkv_groups, bias, batch, seq)
        y_ssm = ssd_mixer(rest, dt_raw, conv_w[l].astype(F32), conv_b[l].astype(F32)[None, :], dt_bias_p, a_p,
                          d_skip_e, ssm_norm_w[l].astype(F32)[None, :], batch, seq)
        rw = jnp.zeros((d, LANES), F32).at[:, :N_EXPERTS].set(router_w[l].astype(F32))
        rw_hi = rw.astype(BF16)
        router_w_p = jnp.concatenate([rw_hi, (rw - rw_hi.astype(F32)).astype(BF16)], axis=1)
        router_b_p = jnp.full((1, LANES), NEG_BIG, F32).at[0, :N_EXPERTS].set(router_b[l].astype(F32))
        x1, h2, top_i, top_w = merge_project(
            attn, y_ssm, rest, xf, w_branch_attn[l].astype(BF16), w_branch_ssm[l].astype(BF16),
            w_out[l].astype(BF16), gate_bias[l].astype(F32)[None, :], norm_ffn[l].astype(F32)[None, :],
            router_w_p, router_b_p)

        ids, block_e, n_used = routing_layout(top_i, t)
        w1g_t, w1l_t = split_w1(w1[l].astype(F32))
        y4p = expert_ffn(h2, ids, block_e, n_used, w1g_t, w1l_t, w2[l].astype(BF16),
                         b1[l][:, None, 0::2].astype(F32), b1[l][:, None, 1::2].astype(F32),
                         b2[l][:, None, :].astype(F32), t)
        xf = combine_norm(x1, y4p, top_w, norm_final.astype(F32)[None, :], normalize=(l == depth - 1))
    return xf.reshape(batch, seq, d)
```

```python
import functools
import math

import jax
import jax.numpy as jnp
import numpy as np
from jax import lax
from jax.experimental import pallas as pl
from jax.experimental.pallas import tpu as pltpu

F32 = jnp.float32
BF16 = jnp.bfloat16

EPS = 1e-5
NEG_BIG = -1e30

HEAD_DIM = 64
ATTN_GROUPS = ((128, 1), (512, 4), (2048, 16))
HEADS_PER_GROUP = 8
GROUP_WIDTH = HEADS_PER_GROUP * HEAD_DIM
ATTN_BLOCK = 128
NUM_BUCKETS = 32
MAX_DISTANCE = 2048
SSM_HEAD_DIM = 64
N_SSM_GROUPS = 4
SSM_HEADS_PER_GROUP = 8
D_STATE = 128
CONV_WIDTH = 4
CHUNK = 128
N_EXPERTS = 32
TOP_K = 4
SWIGLU_LIMIT = 7.0
SWIGLU_ALPHA = 1.702

LANES = 128
V7X_VMEM_BYTES = 64 * 1024 * 1024
VMEM_LIMIT = V7X_VMEM_BYTES * 3 // 4
VMEM_LIMIT_LARGE = V7X_VMEM_BYTES * 7 // 8

ROW_TILE = 512
REST_COLUMN_TILES = 4
MERGE_SUBTILE = 256
MERGE_SUBTILES = 4
EXPERT_ROWS = 256
COMBINE_TILE = 512
SPLIT_W1_TILE = 1024
SSD_CHUNKS_PER_STEP = 4
CONV_TAIL = 16


def padded_tokens(n_tokens):
    return n_tokens + N_EXPERTS * EXPERT_ROWS // TOP_K + 2 * EXPERT_ROWS


def _params(semantics):
    return pltpu.CompilerParams(dimension_semantics=semantics, vmem_limit_bytes=VMEM_LIMIT)


def _rms_matmul_kernel(x_ref, g_ref, w_ref, ws_ref, w1_ref, o_ref, os_ref, hb_ref, w1g_ref, w1l_ref, t_ref):
    _split_w1_kernel(w1_ref, w1g_ref, w1l_ref, t_ref)
    x = x_ref[...]
    ms = jnp.mean(x * x, axis=-1, keepdims=True)
    h = (x * lax.rsqrt(ms + EPS) * g_ref[...]).astype(BF16)
    hb_ref[...] = h
    os_ref[...] = jnp.dot(h, ws_ref[...], preferred_element_type=F32)
    tn = w_ref.shape[1] // REST_COLUMN_TILES
    for c in range(REST_COLUMN_TILES):
        cols = slice(c * tn, (c + 1) * tn)
        o_ref[:, cols] = jnp.dot(h, w_ref[:, cols], preferred_element_type=F32).astype(o_ref.dtype)


def rms_matmul(x, g, w, w_side, w1):
    t, d = x.shape
    n = w.shape[1]
    ns = w_side.shape[1]
    tm = ROW_TILE
    e, d1, de2 = w1.shape
    steps = t // tm
    tiles_per_expert = steps // e
    tk = d1 // tiles_per_expert
    assert steps == e * tiles_per_expert and d1 == tk * tiles_per_expert and tk % LANES == 0
    resident = lambda shape: pl.BlockSpec(shape, lambda i: (0, 0), pipeline_mode=pl.Buffered(1))
    w1_out = jax.ShapeDtypeStruct((e, de2 // 2, d1), BF16)
    w1_out_spec = pl.BlockSpec((1, de2 // 2, tk), lambda i: (i // tiles_per_expert, 0, i % tiles_per_expert))
    return pl.pallas_call(
        _rms_matmul_kernel,
        grid=(steps,),
        in_specs=[
            pl.BlockSpec((tm, d), lambda i: (i, 0)),
            pl.BlockSpec((1, d), lambda i: (0, 0)),
            resident((d, n)),
            resident((d, ns)),
            pl.BlockSpec((1, tk, de2), lambda i: (i // tiles_per_expert, i % tiles_per_expert, 0)),
        ],
        out_specs=[
            pl.BlockSpec((tm, n), lambda i: (i, 0)),
            pl.BlockSpec((tm, ns), lambda i: (i, 0)),
            pl.BlockSpec((tm, d), lambda i: (i, 0)),
            w1_out_spec,
            w1_out_spec,
        ],
        out_shape=[jax.ShapeDtypeStruct((t, n), BF16), jax.ShapeDtypeStruct((t, ns), F32),
                   jax.ShapeDtypeStruct((t, d), BF16), w1_out, w1_out],
        scratch_shapes=[pltpu.VMEM((tk // LANES, de2, LANES), F32)],
        compiler_params=pltpu.CompilerParams(dimension_semantics=("parallel",), vmem_limit_bytes=VMEM_LIMIT_LARGE),
        name="rms_matmul",
    )(x, g, w, w_side, w1)


def _qkv_kernel(h_ref, w_ref, o0_ref, o1_ref, o2_ref, acc_ref, tmp_ref):
    seq = h_ref.shape[0]
    slabs = GROUP_WIDTH // LANES
    n_blocks = seq // ATTN_BLOCK
    for gi, o_ref in enumerate((o0_ref, o1_ref, o2_ref)):
        dil = ATTN_GROUPS[gi][1]
        nb = n_blocks // dil
        acc = jnp.dot(h_ref[...], w_ref[:, gi * GROUP_WIDTH:(gi + 1) * GROUP_WIDTH], preferred_element_type=F32)
        if dil == 1:
            for n in range(n_blocks):
                o_ref[0, 0, n] = acc[n * ATTN_BLOCK:(n + 1) * ATTN_BLOCK, :].astype(o_ref.dtype)
            continue
        for s in range(slabs):
            acc_ref[s] = acc[:, s * LANES:(s + 1) * LANES]
        src_ref, step, group_rows = acc_ref, dil, seq
        if dil == 16:
            for s in range(slabs):
                for a in range(4):
                    tmp_ref[s, a * (seq // 4):(a + 1) * (seq // 4), :] = acc_ref[s, pl.ds(a, seq // 4, stride=4), :]
            src_ref, step, group_rows = tmp_ref, 4, seq // 4
        for r in range(dil):
            base = (r % (dil // step)) * group_rows + r // (dil // step)
            for n in range(nb):
                for s in range(slabs):
                    rows = pl.ds(base + n * ATTN_BLOCK * step, ATTN_BLOCK, stride=step)
                    o_ref[0, 0, r * nb + n, :, s * LANES:(s + 1) * LANES] = src_ref[s, rows, :].astype(o_ref.dtype)


def qkv_project(h, w_qkv, batch, seq):
    t, d = h.shape
    n_groups = len(ATTN_GROUPS)
    tn = n_groups * GROUP_WIDTH
    out_shapes, out_specs = [], []
    for _ in ATTN_GROUPS:
        shape = (3, batch, seq // ATTN_BLOCK, ATTN_BLOCK, GROUP_WIDTH)
        out_shapes.append(jax.ShapeDtypeStruct(shape, BF16))
        out_specs.append(pl.BlockSpec((1, 1) + shape[2:], lambda w, b: (w, b, 0, 0, 0)))
    return pl.pallas_call(
        _qkv_kernel,
        grid=(3, batch),
        in_specs=[pl.BlockSpec((seq, d), lambda w, b: (b, 0)), pl.BlockSpec((d, tn), lambda w, b: (0, w))],
        out_specs=out_specs,
        out_shape=out_shapes,
        scratch_shapes=[pltpu.VMEM((GROUP_WIDTH // LANES, seq, LANES), F32)] * 2,
        compiler_params=_params(("parallel", "parallel")),
        name="qkv_project",
    )(h, w_qkv)


def _attn_kernel(q0_ref, q1_ref, q2_ref, bias_ref, o_ref, out_ref, lse_ref):
    heads = LANES // HEAD_DIM
    scale = jnp.asarray(HEAD_DIM ** -0.5, BF16)
    n_blocks = q0_ref.shape[2]
    qk = (((2,), (2,)), ((0,), (0,)))
    pv = (((2,), (1,)), ((0,), (0,)))
    blk = lax.broadcasted_iota(jnp.int32, (n_blocks, 1, 1), 0)
    lane = lax.broadcasted_iota(jnp.int32, (1, 1, LANES), 2)

    def shifted(x):
        return jnp.concatenate([x[n_blocks - 1:], x[:n_blocks - 1]], axis=0)

    for g, ref in enumerate((q0_ref, q1_ref, q2_ref)):
        dil = ATTN_GROUPS[g][1]
        nb = n_blocks // dil
        q2h = ref[0, 0] * scale
        keys, vals = ref[1, 0], ref[2, 0]
        if nb > 1:
            keys = jnp.concatenate([shifted(keys), keys], axis=1)
            vals = jnp.concatenate([shifted(vals), vals], axis=1)
        n_keys = keys.shape[1]
        vals = jnp.concatenate([vals, jnp.ones(vals.shape, BF16)], axis=2)
        pvs, dens, maxs = [], [], []
        for h in range(heads):
            in_head = (lane >= h * HEAD_DIM) & (lane < (h + 1) * HEAD_DIM)
            q = jnp.where(in_head, q2h, jnp.zeros_like(q2h))
            s = lax.dot_general(q, keys, qk, preferred_element_type=F32)
            s = s + bias_ref[g, h, :, 2 * ATTN_BLOCK - n_keys:][None]
            if nb > 1:
                key_is_prev = lax.broadcasted_iota(jnp.int32, (1, 1, n_keys), 2) < ATTN_BLOCK
                s = jnp.where((blk % nb == 0) & key_is_prev, NEG_BIG, s)
            m = jnp.max(s, axis=-1, keepdims=True)
            p = jnp.exp(s - m).astype(BF16)
            acc = lax.dot_general(p, vals, pv, preferred_element_type=F32)
            pvs.append(acc[:, :, :LANES])
            dens.append(acc[:, :, LANES:])
            maxs.append(m)
        first = lane < HEAD_DIM
        den = jnp.where(first, dens[0], dens[1])
        o2 = jnp.where(first, pvs[0], pvs[1]) / den
        l2 = jnp.where(first, maxs[0], maxs[1]) + jnp.log(den)
        for r in range(dil):
            for n in range(nb):
                start = r + n * ATTN_BLOCK * dil
                rows = pl.ds(start, ATTN_BLOCK) if dil == 1 else pl.ds(start, ATTN_BLOCK, stride=dil)
                out_ref[g, rows, :] = o2[r * nb + n]
                lse_ref[g, rows, :] = l2[r * nb + n]

    l0, l1, l2 = lse_ref[0], lse_ref[1], lse_ref[2]
    lm = jnp.maximum(jnp.maximum(l0, l1), l2)
    e0, e1, e2 = jnp.exp(l0 - lm), jnp.exp(l1 - lm), jnp.exp(l2 - lm)
    mixed = (e0 * out_ref[0] + e1 * out_ref[1] + e2 * out_ref[2]) / (e0 + e1 + e2)
    o_ref[...] = mixed.astype(o_ref.dtype)


def dilated_attention(qkv_groups, bias, batch, seq):
    heads = LANES // HEAD_DIM
    in_specs = [pl.BlockSpec((3, 1) + a.shape[2:4] + (LANES,), lambda b, hp: (0, b, 0, 0, hp)) for a in qkv_groups]
    in_specs.append(pl.BlockSpec((len(ATTN_GROUPS), heads, ATTN_BLOCK, 2 * ATTN_BLOCK), lambda b, hp: (0, hp, 0, 0)))
    return pl.pallas_call(
        _attn_kernel,
        grid=(batch, GROUP_WIDTH // LANES),
        in_specs=in_specs,
        out_specs=pl.BlockSpec((seq, LANES), lambda b, hp: (b, hp)),
        out_shape=jax.ShapeDtypeStruct((batch * seq, GROUP_WIDTH), BF16),
        scratch_shapes=[pltpu.VMEM((len(ATTN_GROUPS), seq, LANES), F32),
                        pltpu.VMEM((len(ATTN_GROUPS), seq, LANES), F32)],
        compiler_params=_params(("parallel", "parallel")),
        name="dilated_attn",
    )(*qkv_groups, bias)


def attention_bias(rel_bias, gi):
    window, dil = ATTN_GROUPS[gi]
    w_sub = window // dil
    q_idx = np.arange(ATTN_BLOCK)[:, None]
    k_idx = np.arange(2 * ATTN_BLOCK)[None, :]
    delta = q_idx + ATTN_BLOCK - k_idx
    in_band = (delta >= 0) & (delta <= w_sub)
    dist = np.clip(delta, 0, w_sub) * dil
    max_exact = NUM_BUCKETS // 2
    nf = np.maximum(dist, max_exact).astype(np.float32)
    large = max_exact + (np.log(nf / max_exact) / math.log(MAX_DISTANCE / max_exact)
                         * (NUM_BUCKETS - max_exact)).astype(np.int32)
    large = np.minimum(large, NUM_BUCKETS - 1)
    bucket = np.where(dist < max_exact, dist, large)
    table = rel_bias[:, gi * HEADS_PER_GROUP:(gi + 1) * HEADS_PER_GROUP].astype(F32)
    onehot = (bucket.reshape(-1, 1) == np.arange(NUM_BUCKETS)[None, :]).astype(np.float32)
    bias = jnp.einsum('bh,nb->hn', table, jnp.asarray(onehot), precision=lax.Precision.HIGHEST)
    bias = bias.reshape(HEADS_PER_GROUP, ATTN_BLOCK, 2 * ATTN_BLOCK)
    return jnp.where(in_band[None], bias, NEG_BIG)


def _silu(v):
    half = 0.5 * v
    return half + half * jnp.tanh(half)


def _conv_silu(ext_ref, u_ref, w_ref, b_ref, first):
    rows = u_ref.shape[0]
    ext_rows = ext_ref.shape[0]

    if first is not None:
        @pl.when(first)
        def _():
            ext_ref[0:CONV_TAIL, :] = jnp.zeros((CONV_TAIL, ext_ref.shape[1]), ext_ref.dtype)

    u = u_ref[...]
    ext_ref[CONV_TAIL:, :] = u
    taps = CONV_WIDTH - 1
    out_row = lax.broadcasted_iota(jnp.int32, (rows, ext_rows), 0)
    src_row = lax.broadcasted_iota(jnp.int32, (rows, ext_rows), 1)
    shift_mat = jnp.concatenate([(src_row == out_row + (CONV_TAIL - k)).astype(BF16) for k in range(1, CONV_WIDTH)],
                                axis=0)
    shifted = jnp.dot(shift_mat, ext_ref[...], preferred_element_type=F32)
    acc = u.astype(F32) * w_ref[taps:CONV_WIDTH, :] + b_ref[...]
    for k in range(1, CONV_WIDTH):
        acc = acc + shifted[(k - 1) * rows:k * rows, :] * w_ref[taps - k:CONV_WIDTH - k, :]
    ext_ref[0:CONV_TAIL, :] = ext_ref[rows:, :]
    return _silu(acc)


def _ssd_kernel(x_ref, bc_ref, z_ref, dt_ref, cwx_ref, cbx_ref, cwbc_ref, cbbc_ref, dtb_ref, a_ref,
                dskip_ref, nw_ref, expand_ref, o_ref, extx_ref, extbc_ref, state_ref):
    first = pl.program_id(1) == 0
    gw = SSM_HEADS_PER_GROUP * SSM_HEAD_DIM

    @pl.when(first)
    def _():
        state_ref[...] = jnp.zeros(state_ref.shape, F32)

    row = lax.broadcasted_iota(jnp.int32, (CHUNK, CHUNK), 0)
    colm = lax.broadcasted_iota(jnp.int32, (CHUNK, CHUNK), 1)
    tril = row >= colm
    tril_b = tril.astype(BF16)
    head_rows = N_SSM_GROUPS * SSM_HEADS_PER_GROUP
    eye_b = (lax.broadcasted_iota(jnp.int32, (head_rows, CHUNK), 0)
             == lax.broadcasted_iota(jnp.int32, (head_rows, CHUNK), 1)).astype(BF16)
    nt = (((1,), (1,)), ((), ()))
    tn = (((0,), (0,)), ((), ()))

    def split3(v):
        p0 = v.astype(BF16)
        r1 = v - p0.astype(F32)
        p1 = r1.astype(BF16)
        p2 = (r1 - p1.astype(F32)).astype(BF16)
        return p0, p1, p2

    def select_rows(mat01, v):
        return sum(jnp.dot(mat01, p, preferred_element_type=F32) for p in split3(v))

    def select_cols(v, mat01, parts):
        return sum(jnp.dot(p, mat01, preferred_element_type=F32) for p in split3(v)[:parts])

    for sub, g in [(sub, g) for sub in range(x_ref.shape[0] // CHUNK) for g in range(N_SSM_GROUPS)]:
        rws = pl.ds(sub * CHUNK, CHUNK)
        if g == 0:
            sub_first = first if sub == 0 else None
            xs_all = _conv_silu(extx_ref, x_ref.at[rws], cwx_ref, cbx_ref, sub_first)
            bc_all = _conv_silu(extbc_ref, bc_ref.at[rws], cwbc_ref, cbbc_ref, sub_first)
            v = dt_ref[rws, :] + dtb_ref[...]
            dt = jnp.maximum(v, 0.0) + jnp.log1p(jnp.exp(-jnp.abs(v)))
            a_d = dt * a_ref[...]
            acs = select_rows(tril_b, a_d)
            acs_t = sum(lax.dot_general(eye_b, p, nt, preferred_element_type=F32) for p in split3(acs))
            e_acs = jnp.exp(acs)
            e_dec = jnp.exp(acs[CHUNK - 1:CHUNK, :] - acs)
        lanes = slice(g * LANES, (g + 1) * LANES)
        ch = slice(g * gw, (g + 1) * gw)
        expand = expand_ref[g]
        dt_e = select_cols(dt, expand, 1)
        eacs_e = select_cols(e_acs, expand, 2)
        edec_e = select_cols(e_dec, expand, 1)
        elast_e = eacs_e[CHUNK - 1:CHUNK, :]

        xs = xs_all[:, ch]
        bm = bc_all[:, lanes].astype(BF16)
        cm = bc_all[:, N_SSM_GROUPS * D_STATE + g * D_STATE:N_SSM_GROUPS * D_STATE + (g + 1) * D_STATE].astype(BF16)
        x_d = xs * dt_e
        cb = lax.dot_general(cm, bm, nt, preferred_element_type=F32)

        prev = state_ref[g]
        y = jnp.dot(cm, prev.astype(BF16), preferred_element_type=F32) * eacs_e
        x_d16 = x_d.astype(BF16)
        parts = []
        for j in range(SSM_HEADS_PER_GROUP):
            head = g * SSM_HEADS_PER_GROUP + j
            seg = acs[:, head:head + 1] - acs_t[head:head + 1, :]
            m_h = (cb * jnp.exp(jnp.where(tril, seg, NEG_BIG))).astype(BF16)
            parts.append(jnp.dot(m_h, x_d16[:, j * SSM_HEAD_DIM:(j + 1) * SSM_HEAD_DIM],
                                 preferred_element_type=F32))
        y = y + jnp.concatenate(parts, axis=1)

        xw = (x_d * edec_e).astype(BF16)
        state_ref[g] = prev * elast_e + lax.dot_general(bm, xw, tn, preferred_element_type=F32)

        y = y + dskip_ref[:, ch] * xs
        y = y * _silu(z_ref[rws, ch].astype(F32))
        y = y * lax.rsqrt(jnp.mean(y * y, axis=-1, keepdims=True) + EPS)
        o_ref[rws, ch] = (y * nw_ref[:, ch]).astype(o_ref.dtype)


def ssd_mixer(rest, dt_raw, conv_w, conv_b, dt_bias_p, a_p, d_skip_e, norm_w, batch, seq):
    t = rest.shape[0]
    d_inner = N_SSM_GROUPS * SSM_HEADS_PER_GROUP * SSM_HEAD_DIM
    bc_w = 2 * N_SSM_GROUPS * D_STATE
    rows = CHUNK * SSD_CHUNKS_PER_STEP
    nc = seq // rows
    gw = SSM_HEADS_PER_GROUP * SSM_HEAD_DIM
    head_of_channel = np.arange(N_SSM_GROUPS)[:, None, None] * SSM_HEADS_PER_GROUP + np.arange(gw)[None, None, :] // SSM_HEAD_DIM
    expand = (np.arange(LANES)[None, :, None] == head_of_channel).astype(np.float32)
    rowmap = lambda b, c: (b * nc + c, 0)
    const = lambda b, c: (0, 0)
    return pl.pallas_call(
        _ssd_kernel,
        grid=(batch, nc),
        in_specs=[
            pl.BlockSpec((rows, d_inner), lambda b, c: (b * nc + c, 2)),
            pl.BlockSpec((rows, bc_w), lambda b, c: (b * nc + c, 6)),
            pl.BlockSpec((rows, d_inner), rowmap),
            pl.BlockSpec((rows, LANES), rowmap),
            pl.BlockSpec((CONV_WIDTH, d_inner), const),
            pl.BlockSpec((1, d_inner), const),
            pl.BlockSpec((CONV_WIDTH, bc_w), const),
            pl.BlockSpec((1, bc_w), const),
            pl.BlockSpec((1, LANES), const),
            pl.BlockSpec((1, LANES), const),
            pl.BlockSpec((1, d_inner), const),
            pl.BlockSpec((1, d_inner), const),
            pl.BlockSpec((N_SSM_GROUPS, LANES, gw), lambda b, c: (0, 0, 0)),
        ],
        out_specs=pl.BlockSpec((rows, d_inner), rowmap),
        out_shape=jax.ShapeDtypeStruct((t, d_inner), BF16),
        scratch_shapes=[
            pltpu.VMEM((CONV_TAIL + CHUNK, d_inner), BF16),
            pltpu.VMEM((CONV_TAIL + CHUNK, bc_w), BF16),
            pltpu.VMEM((N_SSM_GROUPS, D_STATE, gw), F32),
        ],
        compiler_params=_params(("parallel", "arbitrary")),
        name="ssd_mixer",
    )(rest, rest, rest, dt_raw, conv_w[:, :d_inner], conv_b[:, :d_inner], conv_w[:, d_inner:],
      conv_b[:, d_inner:], dt_bias_p, a_p, d_skip_e, norm_w, jnp.asarray(expand, dtype=BF16))


def _merge_kernel(at_ref, ys_ref, gl_ref, x_ref, wa_ref, ws_ref,
                  wo_ref, gb_ref, nf_ref, rw_ref, rb_ref, x1_ref, h2_ref, ti_ref, tw_ref):
    d = x_ref.shape[1]
    sub = x_ref.shape[0] // MERGE_SUBTILES
    for part in range(MERGE_SUBTILES):
        rows = slice(part * sub, (part + 1) * sub)
        y_attn = jnp.dot(at_ref[rows, :], wa_ref[...], preferred_element_type=F32)
        y_ssm = jnp.dot(ys_ref[rows, :], ws_ref[...], preferred_element_type=F32)
        gv = gl_ref[rows, :].astype(F32) + gb_ref[...]
        gates = 1.0 / (1.0 + jnp.exp(-gv))
        merged = gates[:, :d] * y_attn + gates[:, d:] * y_ssm
        x1 = x_ref[rows, :] + jnp.dot(merged.astype(BF16), wo_ref[...], preferred_element_type=F32)
        x1_ref[rows, :] = x1
        h2 = x1 * lax.rsqrt(jnp.mean(x1 * x1, axis=-1, keepdims=True) + EPS) * nf_ref[...]
        _store_slabs(h2_ref, part * sub, sub, _pack_bf16_pairs(h2))
        h_hi = h2.astype(BF16)
        h_mid = (h2 - h_hi.astype(F32)).astype(BF16)
        both = jnp.dot(h_hi, rw_ref[...], preferred_element_type=F32)
        logits = (both[:, :LANES] + both[:, LANES:]
                  + jnp.dot(h_mid, rw_ref[:, :LANES], preferred_element_type=F32) + rb_ref[...])
        lane = lax.broadcasted_iota(jnp.int32, logits.shape, 1)
        top_i = jnp.zeros(logits.shape, jnp.int32)
        top_v = jnp.full(logits.shape, NEG_BIG, F32)
        work = logits
        for k in range(TOP_K):
            m = jnp.max(work, axis=-1, keepdims=True)
            idx = jnp.min(jnp.where(work == m, lane, LANES), axis=-1, keepdims=True)
            top_i = jnp.where(lane == k, idx, top_i)
            top_v = jnp.where(lane == k, m, top_v)
            work = jnp.where(lane == idx, NEG_BIG * 2.0, work)
        ev = jnp.exp(top_v - jnp.max(top_v, axis=-1, keepdims=True))
        ti_ref[rows, :] = top_i
        tw_ref[rows, :] = ev / jnp.sum(ev, axis=-1, keepdims=True)


def merge_project(attn, y_ssm, rest, x, wa, ws, wo, gate_bias, norm_ffn, router_w_p, router_b_p):
    t, d = x.shape
    slab = d // 2 // LANES
    tm = MERGE_SUBTILE * MERGE_SUBTILES
    d_inner = y_ssm.shape[1]
    rowmap = lambda i: (i, 0)
    const = lambda i: (0, 0)
    full = lambda a: pl.BlockSpec(a.shape, const, pipeline_mode=pl.Buffered(1))
    args = [attn, y_ssm, rest, x, wa, ws, wo, gate_bias, norm_ffn, router_w_p, router_b_p]
    in_specs = (
        [pl.BlockSpec((tm, GROUP_WIDTH), rowmap),
         pl.BlockSpec((tm, d_inner), rowmap),
         pl.BlockSpec((tm, 2 * d), lambda i: (i, 1)),
         pl.BlockSpec((tm, d), rowmap)]
        + [full(a) for a in args[4:]]
    )
    return pl.pallas_call(
        _merge_kernel,
        grid=(t // tm,),
        in_specs=in_specs,
        out_specs=[pl.BlockSpec((tm, d), rowmap), pl.BlockSpec((tm * slab, LANES), rowmap),
                   pl.BlockSpec((tm, LANES), rowmap), pl.BlockSpec((tm, LANES), rowmap)],
        out_shape=[jax.ShapeDtypeStruct((t, d), F32), jax.ShapeDtypeStruct((t * slab, LANES), jnp.uint32),
                   jax.ShapeDtypeStruct((t, LANES), jnp.int32), jax.ShapeDtypeStruct((t, LANES), F32)],
        compiler_params=_params(("parallel",)),
        name="merge_project",
    )(*args)


def _split_w1_kernel(w_ref, g_ref, l_ref, t_ref):
    de = g_ref.shape[1]
    for s in range(t_ref.shape[0]):
        cols = slice(s * LANES, (s + 1) * LANES)
        t_ref[s] = w_ref[0, cols, :].T
        g_ref[0, :, cols] = t_ref[s, pl.ds(0, de, stride=2), :].astype(g_ref.dtype)
        l_ref[0, :, cols] = t_ref[s, pl.ds(1, de, stride=2), :].astype(l_ref.dtype)


def split_w1(w1):
    e, d, de2 = w1.shape
    de = de2 // 2
    tk = SPLIT_W1_TILE
    out = jax.ShapeDtypeStruct((e, de, d), BF16)
    return pl.pallas_call(
        _split_w1_kernel,
        grid=(e, d // tk),
        in_specs=[pl.BlockSpec((1, tk, de2), lambda i, k: (i, k, 0))],
        out_specs=[pl.BlockSpec((1, de, tk), lambda i, k: (i, 0, k))] * 2,
        out_shape=[out, out],
        scratch_shapes=[pltpu.VMEM((tk // LANES, de2, LANES), F32)],
        compiler_params=_params(("parallel", "parallel")),
        name="split_w1",
    )(w1)


def _pack_bf16_pairs(v):
    w = v.shape[1] // 2
    lo = lax.bitcast_convert_type(v[:, :w].astype(BF16).astype(F32), jnp.uint32) >> 16
    hi = lax.bitcast_convert_type(v[:, w:].astype(BF16).astype(F32), jnp.uint32) & jnp.uint32(0xFFFF0000)
    return lo | hi


def _unpack_bf16_pairs(p):
    lo = lax.bitcast_convert_type(p << 16, F32)
    hi = lax.bitcast_convert_type(p & jnp.uint32(0xFFFF0000), F32)
    return lo, hi


def _store_slabs(ref, row0, rows, packed):
    slab = packed.shape[1] // LANES
    for s in range(slab):
        ref[pl.ds(row0 * slab + s, rows, stride=slab), :] = packed[:, s * LANES:(s + 1) * LANES]


def _load_slabs(ref, rows, slab):
    return jnp.concatenate([ref[pl.ds(s, rows, stride=slab), :] for s in range(slab)], axis=1)


def _expert_kernel(be_ref, nu_ref, tab_ref, h2p_ref, w1g_ref, w1l_ref, w2_ref, bias_ref, y4p_ref, xbuf, obuf, gsem, ssem,
                   *, n_tokens, t_pad):
    i = pl.program_id(0)
    n_used = nu_ref[0]
    slot = lax.rem(i, 2)
    other = 1 - slot
    gslot = lax.rem(i, 3)
    gslot1 = lax.rem(i + 1, 3)
    gslot2 = lax.rem(i + 2, 3)
    bm = EXPERT_ROWS
    slab = xbuf.shape[1] // bm
    g0, g1, g2, sp, sc = (k * bm for k in range(5))

    def gather_copy(table, j, buf):
        src = pl.multiple_of(tab_ref[0, 0, table + j], slab)
        return pltpu.make_async_copy(h2p_ref.at[pl.ds(src, slab)],
                                     xbuf.at[buf, pl.ds(j * slab, slab)], gsem.at[buf])

    def scatter_copy(table, j, buf):
        dst = pl.multiple_of(tab_ref[0, 0, table + j], slab)
        return pltpu.make_async_copy(obuf.at[buf, pl.ds(j * slab, slab)],
                                     y4p_ref.at[pl.ds(dst, slab)], ssem.at[buf])

    def wait_block(kind, buf):
        if kind == "gather":
            pltpu.make_async_copy(h2p_ref.at[pl.ds(0, bm * slab)], xbuf.at[buf], gsem.at[buf]).wait()
        else:
            pltpu.make_async_copy(obuf.at[buf], y4p_ref.at[pl.ds(0, bm * slab)], ssem.at[buf]).wait()

    @pl.when(i == 0)
    def _():
        obuf[...] = jnp.zeros(obuf.shape, obuf.dtype)
        fills = [pltpu.make_async_copy(obuf.at[1], y4p_ref.at[pl.ds((k * t_pad + n_tokens) * slab + c * bm * slab,
                                                                     bm * slab)], ssem.at[1])
                 for k in range(TOP_K) for c in range((t_pad - n_tokens) // bm)]
        for fill in fills:
            fill.start()
        for fill in fills:
            fill.wait()
        for j in range(bm):
            gather_copy(g0, j, 0).start()
        for j in range(bm):
            gather_copy(g1, j, 1).start()
        spare_row = t_pad + t_pad - 2 * bm
        pltpu.make_async_copy(obuf.at[0], y4p_ref.at[pl.ds(spare_row * slab, bm * slab)], ssem.at[0]).start()

    @pl.when(i < n_used)
    def _():
        nt = (((1,), (1,)), ((), ()))
        de = w1g_ref.shape[1]
        wait_block("gather", gslot)
        lo, hi = _unpack_bf16_pairs(_load_slabs(xbuf.at[gslot], bm, slab))
        xb = jnp.concatenate([lo, hi], axis=1).astype(BF16)
        for j in range(bm):
            gather_copy(g2, j, gslot2).start(priority=j % 2)
        for j in range(bm):
            scatter_copy(sp, j, other).start(priority=j % 2)
        glu = lax.dot_general(xb, w1g_ref[0], nt, preferred_element_type=F32) + bias_ref[0, :, :de]
        lin = lax.dot_general(xb, w1l_ref[0], nt, preferred_element_type=F32) + bias_ref[0, :, de:2 * de]
        glu = jnp.minimum(glu, SWIGLU_LIMIT)
        lin = jnp.clip(lin, -SWIGLU_LIMIT, SWIGLU_LIMIT)
        act = glu * (1.0 / (1.0 + jnp.exp(-SWIGLU_ALPHA * glu))) * (lin + 1.0)
        y = jnp.dot(act.astype(BF16), w2_ref[0], preferred_element_type=F32) + bias_ref[0, :, 2 * de:]
        wait_block("scatter", slot)
        _store_slabs(obuf.at[slot], 0, bm, _pack_bf16_pairs(y))

    @pl.when(i == n_used - 1)
    def _():
        for j in range(bm):
            scatter_copy(sc, j, slot).start()
        wait_block("scatter", other)
        wait_block("scatter", slot)
        wait_block("gather", gslot1)
        wait_block("gather", gslot2)


def expert_ffn(h2p, ids, block_e, n_used, w1g, w1l, w2, b1g, b1l, b2, n_tokens):
    n_blocks, _, bm = ids.shape
    slab = h2p.shape[0] // n_tokens
    de, d = w1g.shape[1], w1g.shape[2]
    t_pad = padded_tokens(n_tokens)
    src = jnp.minimum(ids >> 2, n_tokens - 1) * slab
    dst = ((ids & 3) * t_pad + (ids >> 2)) * slab
    first = ((t_pad - 2 * bm + jnp.arange(bm, dtype=jnp.int32)) * slab).reshape(1, 1, bm)
    ahead = lambda a, k: jnp.concatenate([a[k:]] + [a[-1:]] * k, axis=0)
    table = jnp.concatenate([src, ahead(src, 1), ahead(src, 2), jnp.concatenate([first, dst[:-1]], axis=0), dst], axis=2)
    bias = jnp.concatenate([b1g, b1l, b2], axis=2)
    wmap = lambda i, be, nu: (be[i], 0, 0)
    grid_spec = pltpu.PrefetchScalarGridSpec(
        num_scalar_prefetch=2,
        grid=(n_blocks,),
        in_specs=[
            pl.BlockSpec((1, 1, 5 * bm), lambda i, be, nu: (i, 0, 0), memory_space=pltpu.SMEM),
            pl.BlockSpec(memory_space=pl.ANY),
            pl.BlockSpec((1, de, d), wmap),
            pl.BlockSpec((1, de, d), wmap),
            pl.BlockSpec((1, de, d), wmap),
            pl.BlockSpec((1, 1, 2 * de + d), wmap),
        ],
        out_specs=pl.BlockSpec(memory_space=pl.ANY),
        scratch_shapes=[
            pltpu.VMEM((3, bm * slab, LANES), jnp.uint32),
            pltpu.VMEM((2, bm * slab, LANES), jnp.uint32),
            pltpu.SemaphoreType.DMA((3,)),
            pltpu.SemaphoreType.DMA((2,)),
        ],
    )
    return pl.pallas_call(
        functools.partial(_expert_kernel, n_tokens=n_tokens, t_pad=t_pad),
        grid_spec=grid_spec,
        out_shape=jax.ShapeDtypeStruct((TOP_K * t_pad * slab, LANES), jnp.uint32),
        compiler_params=_params(("arbitrary",)),
        name="expert_ffn",
    )(block_e, n_used, table, h2p, w1g, w1l, w2, bias)


def _combine_kernel(x1_ref, y0_ref, y1_ref, y2_ref, y3_ref, tw_ref, nw_ref, o_ref, *, normalize):
    tm, d = x1_ref.shape
    slab = y0_ref.shape[0] // tm
    lo_sum = jnp.zeros((tm, d // 2), F32)
    hi_sum = jnp.zeros((tm, d // 2), F32)
    for k, y_ref in enumerate((y0_ref, y1_ref, y2_ref, y3_ref)):
        lo, hi = _unpack_bf16_pairs(_load_slabs(y_ref, tm, slab))
        w = tw_ref[:, k:k + 1]
        lo_sum = lo_sum + w * lo
        hi_sum = hi_sum + w * hi
    acc = x1_ref[...] + jnp.concatenate([lo_sum, hi_sum], axis=1)
    if normalize:
        acc = acc * lax.rsqrt(jnp.mean(acc * acc, axis=-1, keepdims=True) + EPS) * nw_ref[...]
    o_ref[...] = acc


def combine_norm(x1, y4p, top_w, norm_w, normalize):
    t, d = x1.shape
    tm = COMBINE_TILE
    t_pad = padded_tokens(t)
    slab = y4p.shape[0] // (TOP_K * t_pad)
    y_specs = [pl.BlockSpec((tm * slab, LANES), functools.partial(lambda i, k: (k * (t_pad // tm) + i, 0), k=k))
               for k in range(TOP_K)]
    return pl.pallas_call(
        functools.partial(_combine_kernel, normalize=normalize),
        grid=(t // tm,),
        in_specs=[pl.BlockSpec((tm, d), lambda i: (i, 0))] + y_specs
        + [pl.BlockSpec((tm, LANES), lambda i: (i, 0)), pl.BlockSpec((1, d), lambda i: (0, 0))],
        out_specs=pl.BlockSpec((tm, d), lambda i: (i, 0)),
        out_shape=jax.ShapeDtypeStruct((t, d), F32),
        compiler_params=_params(("parallel",)),
        name="combine_norm",
    )(x1, y4p, y4p, y4p, y4p, top_w, norm_w)


def routing_layout(top_i, n_tokens):
    n_assign = n_tokens * TOP_K
    bm = EXPERT_ROWS
    n_pad = N_EXPERTS * bm
    n_blocks = (n_assign + n_pad) // bm
    flat_e = top_i[:, :TOP_K].reshape(n_assign)
    counts = jnp.sum(flat_e[:, None] == jnp.arange(N_EXPERTS, dtype=jnp.int32)[None, :], axis=0, dtype=jnp.int32)
    padded = (counts + bm - 1) // bm * bm
    pend = jnp.cumsum(padded)
    n_used = (pend[-1] // bm).astype(jnp.int32)
    block_row = jnp.arange(n_blocks, dtype=jnp.int32) * bm
    block_e = jnp.minimum(jnp.sum(pend[None, :] <= block_row[:, None], axis=1, dtype=jnp.int32), N_EXPERTS - 1)
    spare_pos = jnp.arange(bm, dtype=jnp.int32)[None, :]
    spare_key = jnp.where(spare_pos < (padded - counts)[:, None], jnp.arange(N_EXPERTS, dtype=jnp.int32)[:, None],
                          N_EXPERTS).reshape(n_pad)
    keys = jnp.concatenate([flat_e, spare_key])
    pos_bits = (n_assign + n_pad - 1).bit_length()
    assert (N_EXPERTS + 1) << pos_bits < 2 ** 31
    packed = jnp.sort((keys << pos_bits) | jnp.arange(n_assign + n_pad, dtype=jnp.int32))
    order = packed & ((1 << pos_bits) - 1)
    return order.reshape(n_blocks, 1, bm), block_e, n_used.reshape(1)


def kernel(x, w_in, rel_bias, w_branch_attn, conv_w, conv_b, dt_bias, a_log, d_skip, ssm_norm_w,
           w_branch_ssm, gate_bias, w_out, norm_mix, norm_ffn, router_w, router_b, w1, b1, w2, b2,
           norm_final):
    batch, seq, d = x.shape
    t = batch * seq
    depth = w_in.shape[0]
    n_groups = len(ATTN_GROUPS)
    attn_w = n_groups * GROUP_WIDTH
    d_inner = N_SSM_GROUPS * SSM_HEADS_PER_GROUP * SSM_HEAD_DIM
    n_heads = N_SSM_GROUPS * SSM_HEADS_PER_GROUP
    bc_w = 2 * N_SSM_GROUPS * D_STATE
    xf = x.reshape(t, d)
    for l in range(depth):
        wl = w_in[l]
        o_z = 3 * attn_w
        o_xbc = o_z + d_inner
        o_dt = o_xbc + d_inner + bc_w
        o_gate = o_dt + n_heads
        w_qkv = wl[:, :o_z].astype(BF16)
        w_rest = jnp.concatenate([wl[:, o_z:o_xbc], wl[:, o_gate:], wl[:, o_xbc:o_dt]], axis=1).astype(BF16)
        w_dt = jnp.zeros((d, LANES), F32).at[:, :n_heads].set(wl[:, o_dt:o_gate]).astype(BF16)
        dt_bias_p = jnp.zeros((1, LANES), F32).at[0, :n_heads].set(dt_bias[l].astype(F32))
        a_p = jnp.zeros((1, LANES), F32).at[0, :n_heads].set(-jnp.exp(a_log[l].astype(F32)))
        d_skip_e = jnp.repeat(d_skip[l].astype(F32), SSM_HEAD_DIM)[None, :]

        g_mix = norm_mix[l].astype(F32)[None, :]
        rest, dt_raw, h, w1g_t, w1l_t = rms_matmul(xf, g_mix, w_rest, w_dt, w1[l].astype(F32))
        qkv_groups = qkv_project(h, w_qkv, batch, seq)
        bias = jnp.stack([attention_bias(rel_bias, gi) for gi in range(n_groups)])
        attn = dilated_attention(qkv_groups, bias, batch, seq)
        y_ssm = ssd_mixer(rest, dt_raw, conv_w[l].astype(F32), conv_b[l].astype(F32)[None, :], dt_bias_p, a_p,
                          d_skip_e, ssm_norm_w[l].astype(F32)[None, :], batch, seq)
        rw = jnp.zeros((d, LANES), F32).at[:, :N_EXPERTS].set(router_w[l].astype(F32))
        rw_hi = rw.astype(BF16)
        router_w_p = jnp.concatenate([rw_hi, (rw - rw_hi.astype(F32)).astype(BF16)], axis=1)
        router_b_p = jnp.full((1, LANES), NEG_BIG, F32).at[0, :N_EXPERTS].set(router_b[l].astype(F32))
        x1, h2, top_i, top_w = merge_project(
            attn, y_ssm, rest, xf, w_branch_attn[l].astype(BF16), w_branch_ssm[l].astype(BF16),
            w_out[l].astype(BF16), gate_bias[l].astype(F32)[None, :], norm_ffn[l].astype(F32)[None, :],
            router_w_p, router_b_p)

        ids, block_e, n_used = routing_layout(top_i, t)
        y4p = expert_ffn(h2, ids, block_e, n_used, w1g_t, w1l_t, w2[l].astype(BF16),
                         b1[l][:, None, 0::2].astype(F32), b1[l][:, None, 1::2].astype(F32),
                         b2[l][:, None, :].astype(F32), t)
        xf = combine_norm(x1, y4p, top_w, norm_final.astype(F32)[None, :], normalize=(l == depth - 1))
    return xf.reshape(batch, seq, d)
```

```python
import functools
import math

import jax
import jax.numpy as jnp
import numpy as np
from jax import lax
from jax.experimental import pallas as pl
from jax.experimental.pallas import tpu as pltpu

F32 = jnp.float32
BF16 = jnp.bfloat16

EPS = 1e-5
NEG_BIG = -1e30

HEAD_DIM = 64
ATTN_GROUPS = ((128, 1), (512, 4), (2048, 16))
HEADS_PER_GROUP = 8
GROUP_WIDTH = HEADS_PER_GROUP * HEAD_DIM
ATTN_BLOCK = 128
NUM_BUCKETS = 32
MAX_DISTANCE = 2048
SSM_HEAD_DIM = 64
N_SSM_GROUPS = 4
SSM_HEADS_PER_GROUP = 8
D_STATE = 128
CONV_WIDTH = 4
CHUNK = 128
N_EXPERTS = 32
TOP_K = 4
SWIGLU_LIMIT = 7.0
SWIGLU_ALPHA = 1.702

LANES = 128
V7X_VMEM_BYTES = 64 * 1024 * 1024
VMEM_LIMIT = V7X_VMEM_BYTES * 3 // 4
VMEM_LIMIT_LARGE = V7X_VMEM_BYTES * 7 // 8

ROW_TILE = 512
REST_COLUMN_TILES = 4
MERGE_SUBTILE = 256
MERGE_SUBTILES = 4
EXPERT_ROWS = 256
COMBINE_TILE = 512
SPLIT_W1_TILE = 1024
SSD_CHUNKS_PER_STEP = 4
CONV_TAIL = 16


def padded_tokens(n_tokens):
    return n_tokens + N_EXPERTS * EXPERT_ROWS // TOP_K + 2 * EXPERT_ROWS


def _params(semantics):
    return pltpu.CompilerParams(dimension_semantics=semantics, vmem_limit_bytes=VMEM_LIMIT)


def _rms_matmul_kernel(x_ref, g_ref, w_ref, ws_ref, w1_ref, o_ref, os_ref, hb_ref, w1g_ref, w1l_ref, t_ref):
    _split_w1_kernel(w1_ref, w1g_ref, w1l_ref, t_ref)
    x = x_ref[...]
    ms = jnp.mean(x * x, axis=-1, keepdims=True)
    h = (x * lax.rsqrt(ms + EPS) * g_ref[...]).astype(BF16)
    hb_ref[...] = h
    os_ref[...] = jnp.dot(h, ws_ref[...], preferred_element_type=F32)
    tn = w_ref.shape[1] // REST_COLUMN_TILES
    for c in range(REST_COLUMN_TILES):
        cols = slice(c * tn, (c + 1) * tn)
        o_ref[:, cols] = jnp.dot(h, w_ref[:, cols], preferred_element_type=F32).astype(o_ref.dtype)


def rms_matmul(x, g, w, w_side, w1):
    t, d = x.shape
    n = w.shape[1]
    ns = w_side.shape[1]
    tm = ROW_TILE
    e, d1, de2 = w1.shape
    steps = t // tm
    tiles_per_expert = steps // e
    tk = d1 // tiles_per_expert
    assert steps == e * tiles_per_expert and d1 == tk * tiles_per_expert and tk % LANES == 0
    resident = lambda shape: pl.BlockSpec(shape, lambda i: (0, 0), pipeline_mode=pl.Buffered(1))
    w1_out = jax.ShapeDtypeStruct((e, de2 // 2, d1), BF16)
    w1_out_spec = pl.BlockSpec((1, de2 // 2, tk), lambda i: (i // tiles_per_expert, 0, i % tiles_per_expert))
    return pl.pallas_call(
        _rms_matmul_kernel,
        grid=(steps,),
        in_specs=[
            pl.BlockSpec((tm, d), lambda i: (i, 0)),
            pl.BlockSpec((1, d), lambda i: (0, 0)),
            resident((d, n)),
            resident((d, ns)),
            pl.BlockSpec((1, tk, de2), lambda i: (i // tiles_per_expert, i % tiles_per_expert, 0)),
        ],
        out_specs=[
            pl.BlockSpec((tm, n), lambda i: (i, 0)),
            pl.BlockSpec((tm, ns), lambda i: (i, 0)),
            pl.BlockSpec((tm, d), lambda i: (i, 0)),
            w1_out_spec,
            w1_out_spec,
        ],
        out_shape=[jax.ShapeDtypeStruct((t, n), BF16), jax.ShapeDtypeStruct((t, ns), F32),
                   jax.ShapeDtypeStruct((t, d), BF16), w1_out, w1_out],
        scratch_shapes=[pltpu.VMEM((tk // LANES, de2, LANES), F32)],
        compiler_params=pltpu.CompilerParams(dimension_semantics=("parallel",), vmem_limit_bytes=VMEM_LIMIT_LARGE),
        name="rms_matmul",
    )(x, g, w, w_side, w1)


def _qkv_kernel(h_ref, w_ref, o0_ref, o1_ref, o2_ref, acc_ref, tmp_ref):
    seq = h_ref.shape[0]
    slabs = GROUP_WIDTH // LANES
    n_blocks = seq // ATTN_BLOCK
    for gi, o_ref in enumerate((o0_ref, o1_ref, o2_ref)):
        dil = ATTN_GROUPS[gi][1]
        nb = n_blocks // dil
        acc = jnp.dot(h_ref[...], w_ref[:, gi * GROUP_WIDTH:(gi + 1) * GROUP_WIDTH], preferred_element_type=F32)
        if dil == 1:
            for n in range(n_blocks):
                o_ref[0, 0, n] = acc[n * ATTN_BLOCK:(n + 1) * ATTN_BLOCK, :].astype(o_ref.dtype)
            continue
        for s in range(slabs):
            acc_ref[s] = acc[:, s * LANES:(s + 1) * LANES]
        src_ref, step, group_rows = acc_ref, dil, seq
        if dil == 16:
            for s in range(slabs):
                for a in range(4):
                    tmp_ref[s, a * (seq // 4):(a + 1) * (seq // 4), :] = acc_ref[s, pl.ds(a, seq // 4, stride=4), :]
            src_ref, step, group_rows = tmp_ref, 4, seq // 4
        for r in range(dil):
            base = (r % (dil // step)) * group_rows + r // (dil // step)
            for n in range(nb):
                for s in range(slabs):
                    rows = pl.ds(base + n * ATTN_BLOCK * step, ATTN_BLOCK, stride=step)
                    o_ref[0, 0, r * nb + n, :, s * LANES:(s + 1) * LANES] = src_ref[s, rows, :].astype(o_ref.dtype)


def qkv_project(h, w_qkv, batch, seq):
    t, d = h.shape
    n_groups = len(ATTN_GROUPS)
    tn = n_groups * GROUP_WIDTH
    out_shapes, out_specs = [], []
    for _ in ATTN_GROUPS:
        shape = (3, batch, seq // ATTN_BLOCK, ATTN_BLOCK, GROUP_WIDTH)
        out_shapes.append(jax.ShapeDtypeStruct(shape, BF16))
        out_specs.append(pl.BlockSpec((1, 1) + shape[2:], lambda w, b: (w, b, 0, 0, 0)))
    return pl.pallas_call(
        _qkv_kernel,
        grid=(3, batch),
        in_specs=[pl.BlockSpec((seq, d), lambda w, b: (b, 0)), pl.BlockSpec((d, tn), lambda w, b: (0, w))],
        out_specs=out_specs,
        out_shape=out_shapes,
        scratch_shapes=[pltpu.VMEM((GROUP_WIDTH // LANES, seq, LANES), F32)] * 2,
        compiler_params=_params(("parallel", "parallel")),
        name="qkv_project",
    )(h, w_qkv)


def _attn_kernel(q0_ref, q1_ref, q2_ref, bias_ref, w2_ref, o_ref, w2b_ref, out_ref, lse_ref):
    w2b_ref[...] = w2_ref[...].astype(w2b_ref.dtype)
    heads = LANES // HEAD_DIM
    scale = jnp.asarray(HEAD_DIM ** -0.5, BF16)
    n_blocks = q0_ref.shape[2]
    qk = (((2,), (2,)), ((0,), (0,)))
    pv = (((2,), (1,)), ((0,), (0,)))
    blk = lax.broadcasted_iota(jnp.int32, (n_blocks, 1, 1), 0)
    lane = lax.broadcasted_iota(jnp.int32, (1, 1, LANES), 2)

    def shifted(x):
        return jnp.concatenate([x[n_blocks - 1:], x[:n_blocks - 1]], axis=0)

    for g, ref in enumerate((q0_ref, q1_ref, q2_ref)):
        dil = ATTN_GROUPS[g][1]
        nb = n_blocks // dil
        q2h = ref[0, 0] * scale
        keys, vals = ref[1, 0], ref[2, 0]
        if nb > 1:
            keys = jnp.concatenate([shifted(keys), keys], axis=1)
            vals = jnp.concatenate([shifted(vals), vals], axis=1)
        n_keys = keys.shape[1]
        vals = jnp.concatenate([vals, jnp.ones(vals.shape, BF16)], axis=2)
        pvs, dens, maxs = [], [], []
        for h in range(heads):
            in_head = (lane >= h * HEAD_DIM) & (lane < (h + 1) * HEAD_DIM)
            q = jnp.where(in_head, q2h, jnp.zeros_like(q2h))
            s = lax.dot_general(q, keys, qk, preferred_element_type=F32)
            s = s + bias_ref[g, h, :, 2 * ATTN_BLOCK - n_keys:][None]
            if nb > 1:
                key_is_prev = lax.broadcasted_iota(jnp.int32, (1, 1, n_keys), 2) < ATTN_BLOCK
                s = jnp.where((blk % nb == 0) & key_is_prev, NEG_BIG, s)
            m = jnp.max(s, axis=-1, keepdims=True)
            p = jnp.exp(s - m).astype(BF16)
            acc = lax.dot_general(p, vals, pv, preferred_element_type=F32)
            pvs.append(acc[:, :, :LANES])
            dens.append(acc[:, :, LANES:])
            maxs.append(m)
        first = lane < HEAD_DIM
        den = jnp.where(first, dens[0], dens[1])
        o2 = jnp.where(first, pvs[0], pvs[1]) / den
        l2 = jnp.where(first, maxs[0], maxs[1]) + jnp.log(den)
        for r in range(dil):
            for n in range(nb):
                start = r + n * ATTN_BLOCK * dil
                rows = pl.ds(start, ATTN_BLOCK) if dil == 1 else pl.ds(start, ATTN_BLOCK, stride=dil)
                out_ref[g, rows, :] = o2[r * nb + n]
                lse_ref[g, rows, :] = l2[r * nb + n]

    l0, l1, l2 = lse_ref[0], lse_ref[1], lse_ref[2]
    lm = jnp.maximum(jnp.maximum(l0, l1), l2)
    e0, e1, e2 = jnp.exp(l0 - lm), jnp.exp(l1 - lm), jnp.exp(l2 - lm)
    mixed = (e0 * out_ref[0] + e1 * out_ref[1] + e2 * out_ref[2]) / (e0 + e1 + e2)
    o_ref[...] = mixed.astype(o_ref.dtype)


def dilated_attention(qkv_groups, bias, w2, batch, seq):
    heads = LANES // HEAD_DIM
    pairs = GROUP_WIDTH // LANES
    e, de, d = w2.shape
    tiles_per_expert = batch * pairs // e
    tr = de // tiles_per_expert
    assert batch * pairs == e * tiles_per_expert and de == tr * tiles_per_expert and tr % 16 == 0
    w2_map = lambda b, hp: ((b * pairs + hp) // tiles_per_expert, (b * pairs + hp) % tiles_per_expert, 0)
    in_specs = [pl.BlockSpec((3, 1) + a.shape[2:4] + (LANES,), lambda b, hp: (0, b, 0, 0, hp)) for a in qkv_groups]
    in_specs.append(pl.BlockSpec((len(ATTN_GROUPS), heads, ATTN_BLOCK, 2 * ATTN_BLOCK), lambda b, hp: (0, hp, 0, 0)))
    in_specs.append(pl.BlockSpec((1, tr, d), w2_map))
    return pl.pallas_call(
        _attn_kernel,
        grid=(batch, pairs),
        in_specs=in_specs,
        out_specs=[pl.BlockSpec((seq, LANES), lambda b, hp: (b, hp)), pl.BlockSpec((1, tr, d), w2_map)],
        out_shape=[jax.ShapeDtypeStruct((batch * seq, GROUP_WIDTH), BF16), jax.ShapeDtypeStruct((e, de, d), BF16)],
        scratch_shapes=[pltpu.VMEM((len(ATTN_GROUPS), seq, LANES), F32),
                        pltpu.VMEM((len(ATTN_GROUPS), seq, LANES), F32)],
        compiler_params=_params(("parallel", "parallel")),
        name="dilated_attn",
    )(*qkv_groups, bias, w2)


def attention_bias(rel_bias, gi):
    window, dil = ATTN_GROUPS[gi]
    w_sub = window // dil
    q_idx = np.arange(ATTN_BLOCK)[:, None]
    k_idx = np.arange(2 * ATTN_BLOCK)[None, :]
    delta = q_idx + ATTN_BLOCK - k_idx
    in_band = (delta >= 0) & (delta <= w_sub)
    dist = np.clip(delta, 0, w_sub) * dil
    max_exact = NUM_BUCKETS // 2
    nf = np.maximum(dist, max_exact).astype(np.float32)
    large = max_exact + (np.log(nf / max_exact) / math.log(MAX_DISTANCE / max_exact)
                         * (NUM_BUCKETS - max_exact)).astype(np.int32)
    large = np.minimum(large, NUM_BUCKETS - 1)
    bucket = np.where(dist < max_exact, dist, large)
    table = rel_bias[:, gi * HEADS_PER_GROUP:(gi + 1) * HEADS_PER_GROUP].astype(F32)
    onehot = (bucket.reshape(-1, 1) == np.arange(NUM_BUCKETS)[None, :]).astype(np.float32)
    bias = jnp.einsum('bh,nb->hn', table, jnp.asarray(onehot), precision=lax.Precision.HIGHEST)
    bias = bias.reshape(HEADS_PER_GROUP, ATTN_BLOCK, 2 * ATTN_BLOCK)
    return jnp.where(in_band[None], bias, NEG_BIG)


def _silu(v):
    half = 0.5 * v
    return half + half * jnp.tanh(half)


def _conv_silu(ext_ref, u_ref, w_ref, b_ref, first):
    rows = u_ref.shape[0]
    ext_rows = ext_ref.shape[0]

    if first is not None:
        @pl.when(first)
        def _():
            ext_ref[0:CONV_TAIL, :] = jnp.zeros((CONV_TAIL, ext_ref.shape[1]), ext_ref.dtype)

    u = u_ref[...]
    ext_ref[CONV_TAIL:, :] = u
    taps = CONV_WIDTH - 1
    out_row = lax.broadcasted_iota(jnp.int32, (rows, ext_rows), 0)
    src_row = lax.broadcasted_iota(jnp.int32, (rows, ext_rows), 1)
    shift_mat = jnp.concatenate([(src_row == out_row + (CONV_TAIL - k)).astype(BF16) for k in range(1, CONV_WIDTH)],
                                axis=0)
    shifted = jnp.dot(shift_mat, ext_ref[...], preferred_element_type=F32)
    acc = u.astype(F32) * w_ref[taps:CONV_WIDTH, :] + b_ref[...]
    for k in range(1, CONV_WIDTH):
        acc = acc + shifted[(k - 1) * rows:k * rows, :] * w_ref[taps - k:CONV_WIDTH - k, :]
    ext_ref[0:CONV_TAIL, :] = ext_ref[rows:, :]
    return _silu(acc)


def _ssd_kernel(x_ref, bc_ref, z_ref, dt_ref, cwx_ref, cbx_ref, cwbc_ref, cbbc_ref, dtb_ref, a_ref,
                dskip_ref, nw_ref, expand_ref, o_ref, extx_ref, extbc_ref, state_ref):
    first = pl.program_id(1) == 0
    gw = SSM_HEADS_PER_GROUP * SSM_HEAD_DIM

    @pl.when(first)
    def _():
        state_ref[...] = jnp.zeros(state_ref.shape, F32)

    row = lax.broadcasted_iota(jnp.int32, (CHUNK, CHUNK), 0)
    colm = lax.broadcasted_iota(jnp.int32, (CHUNK, CHUNK), 1)
    tril = row >= colm
    tril_b = tril.astype(BF16)
    head_rows = N_SSM_GROUPS * SSM_HEADS_PER_GROUP
    eye_b = (lax.broadcasted_iota(jnp.int32, (head_rows, CHUNK), 0)
             == lax.broadcasted_iota(jnp.int32, (head_rows, CHUNK), 1)).astype(BF16)
    nt = (((1,), (1,)), ((), ()))
    tn = (((0,), (0,)), ((), ()))

    def split3(v):
        p0 = v.astype(BF16)
        r1 = v - p0.astype(F32)
        p1 = r1.astype(BF16)
        p2 = (r1 - p1.astype(F32)).astype(BF16)
        return p0, p1, p2

    def select_rows(mat01, v):
        return sum(jnp.dot(mat01, p, preferred_element_type=F32) for p in split3(v))

    def select_cols(v, mat01, parts):
        return sum(jnp.dot(p, mat01, preferred_element_type=F32) for p in split3(v)[:parts])

    for sub, g in [(sub, g) for sub in range(x_ref.shape[0] // CHUNK) for g in range(N_SSM_GROUPS)]:
        rws = pl.ds(sub * CHUNK, CHUNK)
        if g == 0:
            sub_first = first if sub == 0 else None
            xs_all = _conv_silu(extx_ref, x_ref.at[rws], cwx_ref, cbx_ref, sub_first)
            bc_all = _conv_silu(extbc_ref, bc_ref.at[rws], cwbc_ref, cbbc_ref, sub_first)
            v = dt_ref[rws, :] + dtb_ref[...]
            dt = jnp.maximum(v, 0.0) + jnp.log1p(jnp.exp(-jnp.abs(v)))
            a_d = dt * a_ref[...]
            acs = select_rows(tril_b, a_d)
            acs_t = sum(lax.dot_general(eye_b, p, nt, preferred_element_type=F32) for p in split3(acs))
            e_acs = jnp.exp(acs)
            e_dec = jnp.exp(acs[CHUNK - 1:CHUNK, :] - acs)
        lanes = slice(g * LANES, (g + 1) * LANES)
        ch = slice(g * gw, (g + 1) * gw)
        expand = expand_ref[g]
        dt_e = select_cols(dt, expand, 1)
        eacs_e = select_cols(e_acs, expand, 2)
        edec_e = select_cols(e_dec, expand, 1)
        elast_e = eacs_e[CHUNK - 1:CHUNK, :]

        xs = xs_all[:, ch]
        bm = bc_all[:, lanes].astype(BF16)
        cm = bc_all[:, N_SSM_GROUPS * D_STATE + g * D_STATE:N_SSM_GROUPS * D_STATE + (g + 1) * D_STATE].astype(BF16)
        x_d = xs * dt_e
        cb = lax.dot_general(cm, bm, nt, preferred_element_type=F32)

        prev = state_ref[g]
        y = jnp.dot(cm, prev.astype(BF16), preferred_element_type=F32) * eacs_e
        x_d16 = x_d.astype(BF16)
        parts = []
        for j in range(SSM_HEADS_PER_GROUP):
            head = g * SSM_HEADS_PER_GROUP + j
            seg = acs[:, head:head + 1] - acs_t[head:head + 1, :]
            m_h = (cb * jnp.exp(jnp.where(tril, seg, NEG_BIG))).astype(BF16)
            parts.append(jnp.dot(m_h, x_d16[:, j * SSM_HEAD_DIM:(j + 1) * SSM_HEAD_DIM],
                                 preferred_element_type=F32))
        y = y + jnp.concatenate(parts, axis=1)

        xw = (x_d * edec_e).astype(BF16)
        state_ref[g] = prev * elast_e + lax.dot_general(bm, xw, tn, preferred_element_type=F32)

        y = y + dskip_ref[:, ch] * xs
        y = y * _silu(z_ref[rws, ch].astype(F32))
        y = y * lax.rsqrt(jnp.mean(y * y, axis=-1, keepdims=True) + EPS)
        o_ref[rws, ch] = (y * nw_ref[:, ch]).astype(o_ref.dtype)


def ssd_mixer(rest, dt_raw, conv_w, conv_b, dt_bias_p, a_p, d_skip_e, norm_w, batch, seq):
    t = rest.shape[0]
    d_inner = N_SSM_GROUPS * SSM_HEADS_PER_GROUP * SSM_HEAD_DIM
    bc_w = 2 * N_SSM_GROUPS * D_STATE
    rows = CHUNK * SSD_CHUNKS_PER_STEP
    nc = seq // rows
    gw = SSM_HEADS_PER_GROUP * SSM_HEAD_DIM
    head_of_channel = np.arange(N_SSM_GROUPS)[:, None, None] * SSM_HEADS_PER_GROUP + np.arange(gw)[None, None, :] // SSM_HEAD_DIM
    expand = (np.arange(LANES)[None, :, None] == head_of_channel).astype(np.float32)
    rowmap = lambda b, c: (b * nc + c, 0)
    const = lambda b, c: (0, 0)
    return pl.pallas_call(
        _ssd_kernel,
        grid=(batch, nc),
        in_specs=[
            pl.BlockSpec((rows, d_inner), lambda b, c: (b * nc + c, 2)),
            pl.BlockSpec((rows, bc_w), lambda b, c: (b * nc + c, 6)),
            pl.BlockSpec((rows, d_inner), rowmap),
            pl.BlockSpec((rows, LANES), rowmap),
            pl.BlockSpec((CONV_WIDTH, d_inner), const),
            pl.BlockSpec((1, d_inner), const),
            pl.BlockSpec((CONV_WIDTH, bc_w), const),
            pl.BlockSpec((1, bc_w), const),
            pl.BlockSpec((1, LANES), const),
            pl.BlockSpec((1, LANES), const),
            pl.BlockSpec((1, d_inner), const),
            pl.BlockSpec((1, d_inner), const),
            pl.BlockSpec((N_SSM_GROUPS, LANES, gw), lambda b, c: (0, 0, 0)),
        ],
        out_specs=pl.BlockSpec((rows, d_inner), rowmap),
        out_shape=jax.ShapeDtypeStruct((t, d_inner), BF16),
        scratch_shapes=[
            pltpu.VMEM((CONV_TAIL + CHUNK, d_inner), BF16),
            pltpu.VMEM((CONV_TAIL + CHUNK, bc_w), BF16),
            pltpu.VMEM((N_SSM_GROUPS, D_STATE, gw), F32),
        ],
        compiler_params=_params(("parallel", "arbitrary")),
        name="ssd_mixer",
    )(rest, rest, rest, dt_raw, conv_w[:, :d_inner], conv_b[:, :d_inner], conv_w[:, d_inner:],
      conv_b[:, d_inner:], dt_bias_p, a_p, d_skip_e, norm_w, jnp.asarray(expand, dtype=BF16))


def _merge_kernel(at_ref, ys_ref, gl_ref, x_ref, wa_ref, ws_ref,
                  wo_ref, gb_ref, nf_ref, rw_ref, rb_ref, x1_ref, h2_ref, ti_ref, tw_ref):
    d = x_ref.shape[1]
    sub = x_ref.shape[0] // MERGE_SUBTILES
    for part in range(MERGE_SUBTILES):
        rows = slice(part * sub, (part + 1) * sub)
        y_attn = jnp.dot(at_ref[rows, :], wa_ref[...], preferred_element_type=F32)
        y_ssm = jnp.dot(ys_ref[rows, :], ws_ref[...], preferred_element_type=F32)
        gv = gl_ref[rows, :].astype(F32) + gb_ref[...]
        gates = 1.0 / (1.0 + jnp.exp(-gv))
        merged = gates[:, :d] * y_attn + gates[:, d:] * y_ssm
        x1 = x_ref[rows, :] + jnp.dot(merged.astype(BF16), wo_ref[...], preferred_element_type=F32)
        x1_ref[rows, :] = x1
        h2 = x1 * lax.rsqrt(jnp.mean(x1 * x1, axis=-1, keepdims=True) + EPS) * nf_ref[...]
        _store_slabs(h2_ref, part * sub, sub, _pack_bf16_pairs(h2))
        h_hi = h2.astype(BF16)
        h_mid = (h2 - h_hi.astype(F32)).astype(BF16)
        both = jnp.dot(h_hi, rw_ref[...], preferred_element_type=F32)
        logits = (both[:, :LANES] + both[:, LANES:]
                  + jnp.dot(h_mid, rw_ref[:, :LANES], preferred_element_type=F32) + rb_ref[...])
        lane = lax.broadcasted_iota(jnp.int32, logits.shape, 1)
        top_i = jnp.zeros(logits.shape, jnp.int32)
        top_v = jnp.full(logits.shape, NEG_BIG, F32)
        work = logits
        for k in range(TOP_K):
            m = jnp.max(work, axis=-1, keepdims=True)
            idx = jnp.min(jnp.where(work == m, lane, LANES), axis=-1, keepdims=True)
            top_i = jnp.where(lane == k, idx, top_i)
            top_v = jnp.where(lane == k, m, top_v)
            work = jnp.where(lane == idx, NEG_BIG * 2.0, work)
        ev = jnp.exp(top_v - jnp.max(top_v, axis=-1, keepdims=True))
        ti_ref[rows, :] = top_i
        tw_ref[rows, :] = ev / jnp.sum(ev, axis=-1, keepdims=True)


def merge_project(attn, y_ssm, rest, x, wa, ws, wo, gate_bias, norm_ffn, router_w_p, router_b_p):
    t, d = x.shape
    slab = d // 2 // LANES
    tm = MERGE_SUBTILE * MERGE_SUBTILES
    d_inner = y_ssm.shape[1]
    rowmap = lambda i: (i, 0)
    const = lambda i: (0, 0)
    full = lambda a: pl.BlockSpec(a.shape, const, pipeline_mode=pl.Buffered(1))
    args = [attn, y_ssm, rest, x, wa, ws, wo, gate_bias, norm_ffn, router_w_p, router_b_p]
    in_specs = (
        [pl.BlockSpec((tm, GROUP_WIDTH), rowmap),
         pl.BlockSpec((tm, d_inner), rowmap),
         pl.BlockSpec((tm, 2 * d), lambda i: (i, 1)),
         pl.BlockSpec((tm, d), rowmap)]
        + [full(a) for a in args[4:]]
    )
    return pl.pallas_call(
        _merge_kernel,
        grid=(t // tm,),
        in_specs=in_specs,
        out_specs=[pl.BlockSpec((tm, d), rowmap), pl.BlockSpec((tm * slab, LANES), rowmap),
                   pl.BlockSpec((tm, LANES), rowmap), pl.BlockSpec((tm, LANES), rowmap)],
        out_shape=[jax.ShapeDtypeStruct((t, d), F32), jax.ShapeDtypeStruct((t * slab, LANES), jnp.uint32),
                   jax.ShapeDtypeStruct((t, LANES), jnp.int32), jax.ShapeDtypeStruct((t, LANES), F32)],
        compiler_params=_params(("parallel",)),
        name="merge_project",
    )(*args)


def _split_w1_kernel(w_ref, g_ref, l_ref, t_ref):
    de = g_ref.shape[1]
    for s in range(t_ref.shape[0]):
        cols = slice(s * LANES, (s + 1) * LANES)
        t_ref[s] = w_ref[0, cols, :].T
        g_ref[0, :, cols] = t_ref[s, pl.ds(0, de, stride=2), :].astype(g_ref.dtype)
        l_ref[0, :, cols] = t_ref[s, pl.ds(1, de, stride=2), :].astype(l_ref.dtype)


def split_w1(w1):
    e, d, de2 = w1.shape
    de = de2 // 2
    tk = SPLIT_W1_TILE
    out = jax.ShapeDtypeStruct((e, de, d), BF16)
    return pl.pallas_call(
        _split_w1_kernel,
        grid=(e, d // tk),
        in_specs=[pl.BlockSpec((1, tk, de2), lambda i, k: (i, k, 0))],
        out_specs=[pl.BlockSpec((1, de, tk), lambda i, k: (i, 0, k))] * 2,
        out_shape=[out, out],
        scratch_shapes=[pltpu.VMEM((tk // LANES, de2, LANES), F32)],
        compiler_params=_params(("parallel", "parallel")),
        name="split_w1",
    )(w1)


def _pack_bf16_pairs(v):
    w = v.shape[1] // 2
    lo = lax.bitcast_convert_type(v[:, :w].astype(BF16).astype(F32), jnp.uint32) >> 16
    hi = lax.bitcast_convert_type(v[:, w:].astype(BF16).astype(F32), jnp.uint32) & jnp.uint32(0xFFFF0000)
    return lo | hi


def _unpack_bf16_pairs(p):
    lo = lax.bitcast_convert_type(p << 16, F32)
    hi = lax.bitcast_convert_type(p & jnp.uint32(0xFFFF0000), F32)
    return lo, hi


def _store_slabs(ref, row0, rows, packed):
    slab = packed.shape[1] // LANES
    for s in range(slab):
        ref[pl.ds(row0 * slab + s, rows, stride=slab), :] = packed[:, s * LANES:(s + 1) * LANES]


def _load_slabs(ref, rows, slab):
    return jnp.concatenate([ref[pl.ds(s, rows, stride=slab), :] for s in range(slab)], axis=1)


def _expert_kernel(be_ref, nu_ref, tab_ref, h2p_ref, w1g_ref, w1l_ref, w2_ref, bias_ref, y4p_ref, xbuf, obuf, gsem, ssem,
                   *, n_tokens, t_pad):
    i = pl.program_id(0)
    n_used = nu_ref[0]
    slot = lax.rem(i, 2)
    other = 1 - slot
    gslot = lax.rem(i, 3)
    gslot1 = lax.rem(i + 1, 3)
    gslot2 = lax.rem(i + 2, 3)
    bm = EXPERT_ROWS
    slab = xbuf.shape[1] // bm
    g0, g1, g2, sp, sc = (k * bm for k in range(5))

    def gather_copy(table, j, buf):
        src = pl.multiple_of(tab_ref[0, 0, table + j], slab)
        return pltpu.make_async_copy(h2p_ref.at[pl.ds(src, slab)],
                                     xbuf.at[buf, pl.ds(j * slab, slab)], gsem.at[buf])

    def scatter_copy(table, j, buf):
        dst = pl.multiple_of(tab_ref[0, 0, table + j], slab)
        return pltpu.make_async_copy(obuf.at[buf, pl.ds(j * slab, slab)],
                                     y4p_ref.at[pl.ds(dst, slab)], ssem.at[buf])

    def wait_block(kind, buf):
        if kind == "gather":
            pltpu.make_async_copy(h2p_ref.at[pl.ds(0, bm * slab)], xbuf.at[buf], gsem.at[buf]).wait()
        else:
            pltpu.make_async_copy(obuf.at[buf], y4p_ref.at[pl.ds(0, bm * slab)], ssem.at[buf]).wait()

    @pl.when(i == 0)
    def _():
        obuf[...] = jnp.zeros(obuf.shape, obuf.dtype)
        fills = [pltpu.make_async_copy(obuf.at[1], y4p_ref.at[pl.ds((k * t_pad + n_tokens) * slab + c * bm * slab,
                                                                     bm * slab)], ssem.at[1])
                 for k in range(TOP_K) for c in range((t_pad - n_tokens) // bm)]
        for fill in fills:
            fill.start()
        for fill in fills:
            fill.wait()
        for j in range(bm):
            gather_copy(g0, j, 0).start()
        for j in range(bm):
            gather_copy(g1, j, 1).start()
        spare_row = t_pad + t_pad - 2 * bm
        pltpu.make_async_copy(obuf.at[0], y4p_ref.at[pl.ds(spare_row * slab, bm * slab)], ssem.at[0]).start()

    @pl.when(i < n_used)
    def _():
        nt = (((1,), (1,)), ((), ()))
        de = w1g_ref.shape[1]
        wait_block("gather", gslot)
        lo, hi = _unpack_bf16_pairs(_load_slabs(xbuf.at[gslot], bm, slab))
        xb = jnp.concatenate([lo, hi], axis=1).astype(BF16)
        for j in range(bm):
            gather_copy(g2, j, gslot2).start(priority=j % 2)
        for j in range(bm):
            scatter_copy(sp, j, other).start(priority=j % 2)
        glu = lax.dot_general(xb, w1g_ref[0], nt, preferred_element_type=F32) + bias_ref[0, :, :de]
        lin = lax.dot_general(xb, w1l_ref[0], nt, preferred_element_type=F32) + bias_ref[0, :, de:2 * de]
        glu = jnp.minimum(glu, SWIGLU_LIMIT)
        lin = jnp.clip(lin, -SWIGLU_LIMIT, SWIGLU_LIMIT)
        act = glu * (1.0 / (1.0 + jnp.exp(-SWIGLU_ALPHA * glu))) * (lin + 1.0)
        y = jnp.dot(act.astype(BF16), w2_ref[0], preferred_element_type=F32) + bias_ref[0, :, 2 * de:]
        wait_block("scatter", slot)
        _store_slabs(obuf.at[slot], 0, bm, _pack_bf16_pairs(y))

    @pl.when(i == n_used - 1)
    def _():
        for j in range(bm):
            scatter_copy(sc, j, slot).start()
        wait_block("scatter", other)
        wait_block("scatter", slot)
        wait_block("gather", gslot1)
        wait_block("gather", gslot2)


def expert_ffn(h2p, ids, block_e, n_used, w1g, w1l, w2, b1g, b1l, b2, n_tokens):
    n_blocks, _, bm = ids.shape
    slab = h2p.shape[0] // n_tokens
    de, d = w1g.shape[1], w1g.shape[2]
    t_pad = padded_tokens(n_tokens)
    src = jnp.minimum(ids >> 2, n_tokens - 1) * slab
    dst = ((ids & 3) * t_pad + (ids >> 2)) * slab
    first = ((t_pad - 2 * bm + jnp.arange(bm, dtype=jnp.int32)) * slab).reshape(1, 1, bm)
    ahead = lambda a, k: jnp.concatenate([a[k:]] + [a[-1:]] * k, axis=0)
    table = jnp.concatenate([src, ahead(src, 1), ahead(src, 2), jnp.concatenate([first, dst[:-1]], axis=0), dst], axis=2)
    bias = jnp.concatenate([b1g, b1l, b2], axis=2)
    wmap = lambda i, be, nu: (be[i], 0, 0)
    grid_spec = pltpu.PrefetchScalarGridSpec(
        num_scalar_prefetch=2,
        grid=(n_blocks,),
        in_specs=[
            pl.BlockSpec((1, 1, 5 * bm), lambda i, be, nu: (i, 0, 0), memory_space=pltpu.SMEM),
            pl.BlockSpec(memory_space=pl.ANY),
            pl.BlockSpec((1, de, d), wmap),
            pl.BlockSpec((1, de, d), wmap),
            pl.BlockSpec((1, de, d), wmap),
            pl.BlockSpec((1, 1, 2 * de + d), wmap),
        ],
        out_specs=pl.BlockSpec(memory_space=pl.ANY),
        scratch_shapes=[
            pltpu.VMEM((3, bm * slab, LANES), jnp.uint32),
            pltpu.VMEM((2, bm * slab, LANES), jnp.uint32),
            pltpu.SemaphoreType.DMA((3,)),
            pltpu.SemaphoreType.DMA((2,)),
        ],
    )
    return pl.pallas_call(
        functools.partial(_expert_kernel, n_tokens=n_tokens, t_pad=t_pad),
        grid_spec=grid_spec,
        out_shape=jax.ShapeDtypeStruct((TOP_K * t_pad * slab, LANES), jnp.uint32),
        compiler_params=_params(("arbitrary",)),
        name="expert_ffn",
    )(block_e, n_used, table, h2p, w1g, w1l, w2, bias)


def _combine_kernel(x1_ref, y0_ref, y1_ref, y2_ref, y3_ref, tw_ref, nw_ref, o_ref, *, normalize):
    tm, d = x1_ref.shape
    slab = y0_ref.shape[0] // tm
    lo_sum = jnp.zeros((tm, d // 2), F32)
    hi_sum = jnp.zeros((tm, d // 2), F32)
    for k, y_ref in enumerate((y0_ref, y1_ref, y2_ref, y3_ref)):
        lo, hi = _unpack_bf16_pairs(_load_slabs(y_ref, tm, slab))
        w = tw_ref[:, k:k + 1]
        lo_sum = lo_sum + w * lo
        hi_sum = hi_sum + w * hi
    acc = x1_ref[...] + jnp.concatenate([lo_sum, hi_sum], axis=1)
    if normalize:
        acc = acc * lax.rsqrt(jnp.mean(acc * acc, axis=-1, keepdims=True) + EPS) * nw_ref[...]
    o_ref[...] = acc


def combine_norm(x1, y4p, top_w, norm_w, normalize):
    t, d = x1.shape
    tm = COMBINE_TILE
    t_pad = padded_tokens(t)
    slab = y4p.shape[0] // (TOP_K * t_pad)
    y_specs = [pl.BlockSpec((tm * slab, LANES), functools.partial(lambda i, k: (k * (t_pad // tm) + i, 0), k=k))
               for k in range(TOP_K)]
    return pl.pallas_call(
        functools.partial(_combine_kernel, normalize=normalize),
        grid=(t // tm,),
        in_specs=[pl.BlockSpec((tm, d), lambda i: (i, 0))] + y_specs
        + [pl.BlockSpec((tm, LANES), lambda i: (i, 0)), pl.BlockSpec((1, d), lambda i: (0, 0))],
        out_specs=pl.BlockSpec((tm, d), lambda i: (i, 0)),
        out_shape=jax.ShapeDtypeStruct((t, d), F32),
        compiler_params=_params(("parallel",)),
        name="combine_norm",
    )(x1, y4p, y4p, y4p, y4p, top_w, norm_w)


def routing_layout(top_i, n_tokens):
    n_assign = n_tokens * TOP_K
    bm = EXPERT_ROWS
    n_pad = N_EXPERTS * bm
    n_blocks = (n_assign + n_pad) // bm
    flat_e = top_i[:, :TOP_K].reshape(n_assign)
    counts = jnp.sum(flat_e[:, None] == jnp.arange(N_EXPERTS, dtype=jnp.int32)[None, :], axis=0, dtype=jnp.int32)
    padded = (counts + bm - 1) // bm * bm
    pend = jnp.cumsum(padded)
    n_used = (pend[-1] // bm).astype(jnp.int32)
    block_row = jnp.arange(n_blocks, dtype=jnp.int32) * bm
    block_e = jnp.minimum(jnp.sum(pend[None, :] <= block_row[:, None], axis=1, dtype=jnp.int32), N_EXPERTS - 1)
    spare_pos = jnp.arange(bm, dtype=jnp.int32)[None, :]
    spare_key = jnp.where(spare_pos < (padded - counts)[:, None], jnp.arange(N_EXPERTS, dtype=jnp.int32)[:, None],
                          N_EXPERTS).reshape(n_pad)
    keys = jnp.concatenate([flat_e, spare_key])
    pos_bits = (n_assign + n_pad - 1).bit_length()
    assert (N_EXPERTS + 1) << pos_bits < 2 ** 31
    packed = jnp.sort((keys << pos_bits) | jnp.arange(n_assign + n_pad, dtype=jnp.int32))
    order = packed & ((1 << pos_bits) - 1)
    return order.reshape(n_blocks, 1, bm), block_e, n_used.reshape(1)


def kernel(x, w_in, rel_bias, w_branch_attn, conv_w, conv_b, dt_bias, a_log, d_skip, ssm_norm_w,
           w_branch_ssm, gate_bias, w_out, norm_mix, norm_ffn, router_w, router_b, w1, b1, w2, b2,
           norm_final):
    batch, seq, d = x.shape
    t = batch * seq
    depth = w_in.shape[0]
    n_groups = len(ATTN_GROUPS)
    attn_w = n_groups * GROUP_WIDTH
    d_inner = N_SSM_GROUPS * SSM_HEADS_PER_GROUP * SSM_HEAD_DIM
    n_heads = N_SSM_GROUPS * SSM_HEADS_PER_GROUP
    bc_w = 2 * N_SSM_GROUPS * D_STATE
    xf = x.reshape(t, d)
    for l in range(depth):
        wl = w_in[l]
        o_z = 3 * attn_w
        o_xbc = o_z + d_inner
        o_dt = o_xbc + d_inner + bc_w
        o_gate = o_dt + n_heads
        w_qkv = wl[:, :o_z].astype(BF16)
        w_rest = jnp.concatenate([wl[:, o_z:o_xbc], wl[:, o_gate:], wl[:, o_xbc:o_dt]], axis=1).astype(BF16)
        w_dt = jnp.zeros((d, LANES), F32).at[:, :n_heads].set(wl[:, o_dt:o_gate]).astype(BF16)
        dt_bias_p = jnp.zeros((1, LANES), F32).at[0, :n_heads].set(dt_bias[l].astype(F32))
        a_p = jnp.zeros((1, LANES), F32).at[0, :n_heads].set(-jnp.exp(a_log[l].astype(F32)))
        d_skip_e = jnp.repeat(d_skip[l].astype(F32), SSM_HEAD_DIM)[None, :]

        g_mix = norm_mix[l].astype(F32)[None, :]
        rest, dt_raw, h, w1g_t, w1l_t = rms_matmul(xf, g_mix, w_rest, w_dt, w1[l].astype(F32))
        qkv_groups = qkv_project(h, w_qkv, batch, seq)
        bias = jnp.stack([attention_bias(rel_bias, gi) for gi in range(n_groups)])
        attn, w2_b = dilated_attention(qkv_groups, bias, w2[l].astype(F32), batch, seq)
        y_ssm = ssd_mixer(rest, dt_raw, conv_w[l].astype(F32), conv_b[l].astype(F32)[None, :], dt_bias_p, a_p,
                          d_skip_e, ssm_norm_w[l].astype(F32)[None, :], batch, seq)
        rw = jnp.zeros((d, LANES), F32).at[:, :N_EXPERTS].set(router_w[l].astype(F32))
        rw_hi = rw.astype(BF16)
        router_w_p = jnp.concatenate([rw_hi, (rw - rw_hi.astype(F32)).astype(BF16)], axis=1)
        router_b_p = jnp.full((1, LANES), NEG_BIG, F32).at[0, :N_EXPERTS].set(router_b[l].astype(F32))
        x1, h2, top_i, top_w = merge_project(
            attn, y_ssm, rest, xf, w_branch_attn[l].astype(BF16), w_branch_ssm[l].astype(BF16),
            w_out[l].astype(BF16), gate_bias[l].astype(F32)[None, :], norm_ffn[l].astype(F32)[None, :],
            router_w_p, router_b_p)

        ids, block_e, n_used = routing_layout(top_i, t)
        y4p = expert_ffn(h2, ids, block_e, n_used, w1g_t, w1l_t, w2_b,
                         b1[l][:, None, 0::2].astype(F32), b1[l][:, None, 1::2].astype(F32),
                         b2[l][:, None, :].astype(F32), t)
        xf = combine_norm(x1, y4p, top_w, norm_final.astype(F32)[None, :], normalize=(l == depth - 1))
    return xf.reshape(batch, seq, d)
```
